```python
import math
import jax, jax.numpy as jnp
from jax import lax
import numpy as np

D_MODEL = 2048
BATCH = 8
SEQ = 8192
DEPTH = 4

HEAD_DIM = 64
N_MIX_HEADS = D_MODEL // HEAD_DIM
A_HEADS = N_MIX_HEADS // 4
A_KV_HEADS = A_HEADS // 4
B_HEADS = N_MIX_HEADS // 4
C_HEADS = N_MIX_HEADS - A_HEADS - B_HEADS
C_KV_HEADS = C_HEADS // 4
A_W = A_HEADS * HEAD_DIM
A_KV_W = A_KV_HEADS * HEAD_DIM
B_W = B_HEADS * HEAD_DIM
C_W = C_HEADS * HEAD_DIM
C_KV_W = C_KV_HEADS * HEAD_DIM
MIX_WIDTH = A_W + B_W + C_W
IN_SIZES = [A_W, A_KV_W, A_KV_W, B_W, B_W, B_W, C_W, C_KV_W, C_KV_W]
IN_WIDTH = sum(IN_SIZES)
IN_SPLITS = [int(v) for v in np.cumsum(IN_SIZES)[:-1]]

WINDOW = 128
A_BLOCK = 128
T5_BUCKETS = 32
T5_MAX_DIST = 128
GRID_W = 64
NA_ROWS_MAX = 8
NA_COLS = 16
NA_QCOLS = 16
C_BLOCK = 128
ROPE_THETA = 10000.0
D_FF = 4 * D_MODEL
EPS = 1e-6
MASK_VALUE = -1e30

kernel_name = "hymba_style_hybrid_encoder"


def rmsnorm(x, g):
    xf = x.astype(jnp.float32)
    y = xf * lax.rsqrt(jnp.mean(xf * xf, axis=-1, keepdims=True) + EPS)
    return (y * g.astype(jnp.float32)).astype(x.dtype)


def t5_bucket(rel):
    nb = T5_BUCKETS // 2
    max_exact = nb // 2
    base = jnp.where(rel > 0, nb, 0)
    n = jnp.abs(rel)
    nf = jnp.maximum(n, 1).astype(jnp.float32)
    large = max_exact + (jnp.log(nf / max_exact) / math.log(T5_MAX_DIST / max_exact)
                         * (nb - max_exact)).astype(jnp.int32)
    large = jnp.minimum(large, nb - 1)
    return base + jnp.where(n < max_exact, n, large)


def window_attention(q, k, v, sink, t5_table):
    bsz, s_len = q.shape[0], q.shape[1]
    nb = s_len // A_BLOCK
    grp = A_HEADS // A_KV_HEADS
    qb = q.reshape(bsz, nb, A_BLOCK, A_KV_HEADS, grp, HEAD_DIM)

    def kv_blocks(t):
        tp = jnp.pad(t, ((0, 0), (A_BLOCK, A_BLOCK), (0, 0), (0, 0)))
        parts = [tp[:, o:o + s_len].reshape(bsz, nb, A_BLOCK, A_KV_HEADS, HEAD_DIM)
                 for o in (0, A_BLOCK, 2 * A_BLOCK)]
        return jnp.concatenate(parts, axis=2)

    kb, vb = kv_blocks(k), kv_blocks(v)
    qi = jnp.arange(A_BLOCK)[:, None]
    kj = jnp.arange(3 * A_BLOCK)[None, :]
    rel = kj - A_BLOCK - qi
    bias = t5_table[t5_bucket(rel)]
    bias = bias.transpose(2, 0, 1).reshape(A_KV_HEADS, grp, A_BLOCK, 3 * A_BLOCK).astype(jnp.float32)
    key_pos = jnp.arange(nb)[:, None] * A_BLOCK - A_BLOCK + jnp.arange(3 * A_BLOCK)[None, :]
    valid = ((jnp.abs(rel) <= WINDOW)[None]
             & ((key_pos >= 0) & (key_pos < s_len))[:, None, :])
    scale = HEAD_DIM ** -0.5
    s = jnp.einsum('bnqgrd,bnkgd->bngrqk', qb, kb).astype(jnp.float32) * scale + bias
    s = jnp.where(valid[None, :, None, None], s, MASK_VALUE)
    sink_l = jnp.broadcast_to(sink.astype(jnp.float32).reshape(1, 1, A_KV_HEADS, grp, 1, 1),
                              s.shape[:-1] + (1,))
    p = jax.nn.softmax(jnp.concatenate([s, sink_l], axis=-1), axis=-1)[..., :-1]
    o = jnp.einsum('bngrqk,bnkgd->bnqgrd', p.astype(v.dtype), vb)
    return o.reshape(bsz, s_len, A_W)


def neighborhood_attention(q, k, v, rpb):
    bsz, s_len = q.shape[0], q.shape[1]
    rows = s_len // GRID_W
    kr = min(NA_ROWS_MAX, rows)
    ncb = GRID_W // NA_QCOLS
    kbw = min(GRID_W, NA_COLS + NA_QCOLS)
    q5 = q.reshape(bsz, rows, GRID_W, B_HEADS, HEAD_DIM)
    k5 = k.reshape(bsz, rows, GRID_W, B_HEADS, HEAD_DIM)
    v5 = v.reshape(bsz, rows, GRID_W, B_HEADS, HEAD_DIM)
    r = jnp.arange(rows)
    row_start = jnp.clip(r - kr // 2, 0, rows - kr)
    row_idx = row_start[:, None] + jnp.arange(kr)[None, :]
    dr_idx = row_idx - r[:, None] + (NA_ROWS_MAX - 1)
    c = jnp.arange(GRID_W).reshape(ncb, NA_QCOLS)
    col_start = jnp.clip(c - NA_COLS // 2, 0, GRID_W - NA_COLS)
    kblk_start = jnp.clip(jnp.arange(ncb) * NA_QCOLS - NA_COLS // 2, 0, GRID_W - kbw)
    kcols = kblk_start[:, None] + jnp.arange(kbw)[None, :]
    kc = kcols[:, None, :]
    cs = col_start[:, :, None]
    col_valid = (kc >= cs) & (kc < cs + NA_COLS)
    dc_idx = jnp.clip(kc - c[:, :, None] + NA_COLS - 1, 0, 2 * NA_COLS - 2)
    scale = HEAD_DIM ** -0.5

    def row_block(args):
        q_r, ridx, dridx = args
        kg = k5[:, ridx][:, :, kcols]
        vg = v5[:, ridx][:, :, kcols]
        qg = q_r.reshape(bsz, ncb, NA_QCOLS, B_HEADS, HEAD_DIM)
        s = jnp.einsum('bmqhd,bimjhd->bmhqij', qg, kg).astype(jnp.float32) * scale
        bias = rpb[:, dridx][:, :, dc_idx]
        s = s + bias.transpose(2, 0, 3, 1, 4).astype(jnp.float32)[None]
        s = jnp.where(col_valid[None, :, None, :, None, :], s, MASK_VALUE)
        p = jax.nn.softmax(s.reshape(bsz, ncb, B_HEADS, NA_QCOLS, kr * kbw), axis=-1)
        p = p.reshape(s.shape).astype(v.dtype)
        o = jnp.einsum('bmhqij,bimjhd->bmqhd', p, vg)
        return o.reshape(bsz, GRID_W, B_HEADS, HEAD_DIM)

    o = lax.map(row_block, (q5.transpose(1, 0, 2, 3, 4), row_idx, dr_idx))
    return o.transpose(1, 0, 2, 3, 4).reshape(bsz, s_len, B_W)


def rope_axis(x, ang):
    x1, x2 = jnp.split(x, 2, axis=-1)
    cos = jnp.cos(ang)[None, :, None, :]
    sin = jnp.sin(ang)[None, :, None, :]
    return jnp.concatenate([x1 * cos - x2 * sin, x2 * cos + x1 * sin], axis=-1).astype(x.dtype)


def axial_rope(x, ang_row, ang_col):
    x_row, x_col = jnp.split(x, 2, axis=-1)
    return jnp.concatenate([rope_axis(x_row, ang_row), rope_axis(x_col, ang_col)], axis=-1)


def axial_global_attention(q, k, v, q_gain, k_gain):
    bsz, s_len = q.shape[0], q.shape[1]
    grp = C_HEADS // C_KV_HEADS
    t = jnp.arange(s_len)
    row = (t // GRID_W).astype(jnp.float32)
    col = (t % GRID_W).astype(jnp.float32)
    axis_dim = HEAD_DIM // 2
    freqs = ROPE_THETA ** (-jnp.arange(0, axis_dim, 2, dtype=jnp.float32) / axis_dim)
    ang_row = row[:, None] * freqs[None, :]
    ang_col = col[:, None] * freqs[None, :]
    q = axial_rope(rmsnorm(q, q_gain), ang_row, ang_col)
    k = axial_rope(rmsnorm(k, k_gain), ang_row, ang_col)
    nb = s_len // C_BLOCK
    qb = q.reshape(bsz, nb, C_BLOCK, C_KV_HEADS, grp, HEAD_DIM).transpose(1, 0, 2, 3, 4, 5)
    scale = HEAD_DIM ** -0.5

    def block(q_blk):
        s = jnp.einsum('bqgrd,bkgd->bgrqk', q_blk, k).astype(jnp.float32) * scale
        p = jax.nn.softmax(s, axis=-1).astype(v.dtype)
        return jnp.einsum('bgrqk,bkgd->bqgrd', p, v)

    o = lax.map(block, qb)
    return o.transpose(1, 0, 2, 3, 4, 5).reshape(bsz, s_len, C_W)


def _fwd_setup_inputs(seed: int = 0) -> dict:
    key = jax.random.key(seed)
    ks = jax.random.split(key, 16)
    f32 = jnp.float32

    def nrm(k, shape, scale):
        return jax.random.normal(k, shape, f32) * scale

    return {
        "x": nrm(ks[0], (BATCH, SEQ, D_MODEL), 1.0),
        "norm_mix": 1.0 + nrm(ks[1], (DEPTH, D_MODEL), 0.02),
        "w_in": nrm(ks[2], (DEPTH, D_MODEL, IN_WIDTH), D_MODEL ** -0.5),
        "a_sink": nrm(ks[3], (DEPTH, A_HEADS), 0.5),
        "t5_table": nrm(ks[4], (T5_BUCKETS, A_HEADS), 0.5),
        "b_rpb": nrm(ks[5], (DEPTH, B_HEADS, 2 * NA_ROWS_MAX - 1, 2 * NA_COLS - 1), 0.5),
        "c_q_gain": 1.0 + nrm(ks[6], (DEPTH, HEAD_DIM), 0.02),
        "c_k_gain": 1.0 + nrm(ks[7], (DEPTH, HEAD_DIM), 0.02),
        "out_gain_a": 1.0 + nrm(ks[8], (DEPTH, A_W), 0.02),
        "out_gain_b": 1.0 + nrm(ks[9], (DEPTH, B_W), 0.02),
        "out_gain_c": 1.0 + nrm(ks[10], (DEPTH, C_W), 0.02),
        "w_o": nrm(ks[11], (DEPTH, MIX_WIDTH, D_MODEL), MIX_WIDTH ** -0.5),
        "norm_mlp": 1.0 + nrm(ks[12], (DEPTH, D_MODEL), 0.02),
        "w_up": nrm(ks[13], (DEPTH, D_MODEL, D_FF), D_MODEL ** -0.5),
        "w_down": nrm(ks[14], (DEPTH, D_FF, D_MODEL), D_FF ** -0.5),
        "norm_final": 1.0 + nrm(ks[15], (D_MODEL,), 0.02),
    }


def _fwd_reference(x, norm_mix, w_in, a_sink, t5_table, b_rpb, c_q_gain, c_k_gain,
              out_gain_a, out_gain_b, out_gain_c, w_o, norm_mlp, w_up, w_down, norm_final):
    bsz, s_len = x.shape[0], x.shape[1]
    for l in range(DEPTH):
        h = rmsnorm(x, norm_mix[l])
        proj = jnp.einsum('bsd,de->bse', h, w_in[l])
        qa, ka, va, qb, kb, vb, qc, kc, vc = jnp.split(proj, IN_SPLITS, axis=-1)
        oa = window_attention(qa.reshape(bsz, s_len, A_HEADS, HEAD_DIM),
                              ka.reshape(bsz, s_len, A_KV_HEADS, HEAD_DIM),
                              va.reshape(bsz, s_len, A_KV_HEADS, HEAD_DIM),
                              a_sink[l], t5_table)
        ob = neighborhood_attention(qb.reshape(bsz, s_len, B_HEADS, HEAD_DIM),
                                    kb.reshape(bsz, s_len, B_HEADS, HEAD_DIM),
                                    vb.reshape(bsz, s_len, B_HEADS, HEAD_DIM),
                                    b_rpb[l])
        oc = axial_global_attention(qc.reshape(bsz, s_len, C_HEADS, HEAD_DIM),
                                    kc.reshape(bsz, s_len, C_KV_HEADS, HEAD_DIM),
                                    vc.reshape(bsz, s_len, C_KV_HEADS, HEAD_DIM),
                                    c_q_gain[l], c_k_gain[l])
        mix = jnp.concatenate([rmsnorm(oa, out_gain_a[l]),
                               rmsnorm(ob, out_gain_b[l]),
                               rmsnorm(oc, out_gain_c[l])], axis=-1)
        x = x + jnp.einsum('bse,ed->bsd', mix, w_o[l])
        h = rmsnorm(x, norm_mlp[l])
        u = jax.nn.relu(jnp.einsum('bsd,df->bsf', h, w_up[l]))
        x = x + jnp.einsum('bsf,fd->bsd', u * u, w_down[l])
    return rmsnorm(x, norm_final)


import jax as _jax
import jax.numpy as _jnp

TWIN_FORMAT = 'train_step'
FWD_PARAMS = ['x', 'norm_mix', 'w_in', 'a_sink', 't5_table', 'b_rpb', 'c_q_gain', 'c_k_gain', 'out_gain_a', 'out_gain_b', 'out_gain_c', 'w_o', 'norm_mlp', 'w_up', 'w_down', 'norm_final']
TWIN_WEIGHTS = ['norm_mix', 'w_in', 'a_sink', 't5_table', 'b_rpb', 'c_q_gain', 'c_k_gain', 'out_gain_a', 'out_gain_b', 'out_gain_c', 'w_o', 'norm_mlp', 'w_up', 'w_down', 'norm_final']
TWIN_DIFF_INPUT = 'x'
TWIN_INPUTS = ['x', 'norm_mix', 'w_in', 'a_sink', 't5_table', 'b_rpb', 'c_q_gain', 'c_k_gain', 'out_gain_a', 'out_gain_b', 'out_gain_c', 'w_o', 'norm_mlp', 'w_up', 'w_down', 'norm_final', 'loss_target', 'm_norm_mix', 'm_w_in', 'm_a_sink', 'm_t5_table', 'm_b_rpb', 'm_c_q_gain', 'm_c_k_gain', 'm_out_gain_a', 'm_out_gain_b', 'm_out_gain_c', 'm_w_o', 'm_norm_mlp', 'm_w_up', 'm_w_down', 'm_norm_final', 'v_norm_mix', 'v_w_in', 'v_a_sink', 'v_t5_table', 'v_b_rpb', 'v_c_q_gain', 'v_c_k_gain', 'v_out_gain_a', 'v_out_gain_b', 'v_out_gain_c', 'v_w_o', 'v_norm_mlp', 'v_w_up', 'v_w_down', 'v_norm_final']
TWIN_OUTPUTS = ['loss', 'grad_x', 'grad_norm_mix', 'grad_w_in', 'grad_a_sink', 'grad_t5_table', 'grad_b_rpb', 'grad_c_q_gain', 'grad_c_k_gain', 'grad_out_gain_a', 'grad_out_gain_b', 'grad_out_gain_c', 'grad_w_o', 'grad_norm_mlp', 'grad_w_up', 'grad_w_down', 'grad_norm_final', 'delta_norm_mix', 'delta_w_in', 'delta_a_sink', 'delta_t5_table', 'delta_b_rpb', 'delta_c_q_gain', 'delta_c_k_gain', 'delta_out_gain_a', 'delta_out_gain_b', 'delta_out_gain_c', 'delta_w_o', 'delta_norm_mlp', 'delta_w_up', 'delta_w_down', 'delta_norm_final', 'new_m_norm_mix', 'new_m_w_in', 'new_m_a_sink', 'new_m_t5_table', 'new_m_b_rpb', 'new_m_c_q_gain', 'new_m_c_k_gain', 'new_m_out_gain_a', 'new_m_out_gain_b', 'new_m_out_gain_c', 'new_m_w_o', 'new_m_norm_mlp', 'new_m_w_up', 'new_m_w_down', 'new_m_norm_final', 'new_v_norm_mix', 'new_v_w_in', 'new_v_a_sink', 'new_v_t5_table', 'new_v_b_rpb', 'new_v_c_q_gain', 'new_v_c_k_gain', 'new_v_out_gain_a', 'new_v_out_gain_b', 'new_v_out_gain_c', 'new_v_w_o', 'new_v_norm_mlp', 'new_v_w_up', 'new_v_w_down', 'new_v_norm_final']
TWIN_LEAF_KINDS = {'loss': 'loss', 'grad_x': 'grad_x', 'grad_norm_mix': 'grad_w', 'grad_w_in': 'grad_w', 'grad_a_sink': 'grad_w', 'grad_t5_table': 'grad_w', 'grad_b_rpb': 'grad_w', 'grad_c_q_gain': 'grad_w', 'grad_c_k_gain': 'grad_w', 'grad_out_gain_a': 'grad_w', 'grad_out_gain_b': 'grad_w', 'grad_out_gain_c': 'grad_w', 'grad_w_o': 'grad_w', 'grad_norm_mlp': 'grad_w', 'grad_w_up': 'grad_w', 'grad_w_down': 'grad_w', 'grad_norm_final': 'grad_w', 'delta_norm_mix': 'delta_w', 'delta_w_in': 'delta_w', 'delta_a_sink': 'delta_w', 'delta_t5_table': 'delta_w', 'delta_b_rpb': 'delta_w', 'delta_c_q_gain': 'delta_w', 'delta_c_k_gain': 'delta_w', 'delta_out_gain_a': 'delta_w', 'delta_out_gain_b': 'delta_w', 'delta_out_gain_c': 'delta_w', 'delta_w_o': 'delta_w', 'delta_norm_mlp': 'delta_w', 'delta_w_up': 'delta_w', 'delta_w_down': 'delta_w', 'delta_norm_final': 'delta_w', 'new_m_norm_mix': 'new_m', 'new_m_w_in': 'new_m', 'new_m_a_sink': 'new_m', 'new_m_t5_table': 'new_m', 'new_m_b_rpb': 'new_m', 'new_m_c_q_gain': 'new_m', 'new_m_c_k_gain': 'new_m', 'new_m_out_gain_a': 'new_m', 'new_m_out_gain_b': 'new_m', 'new_m_out_gain_c': 'new_m', 'new_m_w_o': 'new_m', 'new_m_norm_mlp': 'new_m', 'new_m_w_up': 'new_m', 'new_m_w_down': 'new_m', 'new_m_norm_final': 'new_m', 'new_v_norm_mix': 'new_v', 'new_v_w_in': 'new_v', 'new_v_a_sink': 'new_v', 'new_v_t5_table': 'new_v', 'new_v_b_rpb': 'new_v', 'new_v_c_q_gain': 'new_v', 'new_v_c_k_gain': 'new_v', 'new_v_out_gain_a': 'new_v', 'new_v_out_gain_b': 'new_v', 'new_v_out_gain_c': 'new_v', 'new_v_w_o': 'new_v', 'new_v_norm_mlp': 'new_v', 'new_v_w_up': 'new_v', 'new_v_w_down': 'new_v', 'new_v_norm_final': 'new_v'}


def _forward(args):
    return _fwd_reference(*[args[k] for k in FWD_PARAMS])


def _output_shape():
    def fwd():
        inp = _fwd_setup_inputs(0)
        return _fwd_reference(*[inp[k] for k in FWD_PARAMS])
    out = _jax.eval_shape(fwd)
    return out.shape, out.dtype

N_MICROBATCH = 1
ADAM_LR = 0.001
ADAM_B1 = 0.9
ADAM_B2 = 0.999
ADAM_EPS = 1e-08
ADAM_WD = 0.01
ADAM_STEP = 10
PER_EXAMPLE_BATCH_AXIS = {'x': 0, 'loss_target': 0}
SHARED_INPUTS = []
_WEIGHT_DTYPES = {'norm_mix': _jnp.float32, 'w_in': _jnp.float32, 'a_sink': _jnp.float32, 't5_table': _jnp.float32, 'b_rpb': _jnp.float32, 'c_q_gain': _jnp.float32, 'c_k_gain': _jnp.float32, 'out_gain_a': _jnp.float32, 'out_gain_b': _jnp.float32, 'out_gain_c': _jnp.float32, 'w_o': _jnp.float32, 'norm_mlp': _jnp.float32, 'w_up': _jnp.float32, 'w_down': _jnp.float32, 'norm_final': _jnp.float32}
MOMENT_SCALE = {'norm_mix': 1.500371e-01, 'w_in': 1.083909e-01, 'a_sink': 3.823677e-03, 't5_table': 1.249941e-01, 'b_rpb': 1.611466e-02, 'c_q_gain': 1.878850e-01, 'c_k_gain': 2.313947e-01, 'out_gain_a': 1.249647e-01, 'out_gain_b': 1.291647e-01, 'out_gain_c': 1.578985e-01, 'w_o': 1.405026e-01, 'norm_mlp': 9.320912e-02, 'w_up': 4.572459e-02, 'w_down': 1.583886e-01, 'norm_final': 3.439919e+01}


def _to_microbatches(a, axis):
    t = _jnp.moveaxis(a, axis, 0)
    t = t.reshape((N_MICROBATCH, t.shape[0] // N_MICROBATCH) + t.shape[1:])
    return _jnp.moveaxis(t, 1, axis + 1)


def setup_inputs(seed: int = 0) -> dict:
    inp = _fwd_setup_inputs(seed)
    key = _jax.random.fold_in(_jax.random.key(seed), 7919)
    shape, _ = _output_shape()
    out = dict(inp)
    out["loss_target"] = _jax.random.normal(_jax.random.fold_in(key, 0), shape, _jnp.float32)
    for i, name in enumerate(TWIN_WEIGHTS):
        w = inp[name].astype(_jnp.float32)
        if MOMENT_SCALE is None:
            s = _jnp.sqrt(_jnp.mean(_jnp.square(w)) + 1e-30)
        else:
            s = MOMENT_SCALE[name]
        km, kv = _jax.random.split(_jax.random.fold_in(key, i + 1))
        out[name] = w
        out["m_" + name] = s * _jax.random.normal(km, w.shape, _jnp.float32)
        out["v_" + name] = (s * s) * _jax.random.uniform(kv, w.shape, _jnp.float32, 0.5, 1.5)
    if N_MICROBATCH > 1:
        for name, axis in PER_EXAMPLE_BATCH_AXIS.items():
            out[name] = _to_microbatches(out[name], axis)
    return {'x': out['x'], 'norm_mix': out['norm_mix'], 'w_in': out['w_in'], 'a_sink': out['a_sink'], 't5_table': out['t5_table'], 'b_rpb': out['b_rpb'], 'c_q_gain': out['c_q_gain'], 'c_k_gain': out['c_k_gain'], 'out_gain_a': out['out_gain_a'], 'out_gain_b': out['out_gain_b'], 'out_gain_c': out['out_gain_c'], 'w_o': out['w_o'], 'norm_mlp': out['norm_mlp'], 'w_up': out['w_up'], 'w_down': out['w_down'], 'norm_final': out['norm_final'], 'loss_target': out['loss_target'], 'm_norm_mix': out['m_norm_mix'], 'm_w_in': out['m_w_in'], 'm_a_sink': out['m_a_sink'], 'm_t5_table': out['m_t5_table'], 'm_b_rpb': out['m_b_rpb'], 'm_c_q_gain': out['m_c_q_gain'], 'm_c_k_gain': out['m_c_k_gain'], 'm_out_gain_a': out['m_out_gain_a'], 'm_out_gain_b': out['m_out_gain_b'], 'm_out_gain_c': out['m_out_gain_c'], 'm_w_o': out['m_w_o'], 'm_norm_mlp': out['m_norm_mlp'], 'm_w_up': out['m_w_up'], 'm_w_down': out['m_w_down'], 'm_norm_final': out['m_norm_final'], 'v_norm_mix': out['v_norm_mix'], 'v_w_in': out['v_w_in'], 'v_a_sink': out['v_a_sink'], 'v_t5_table': out['v_t5_table'], 'v_b_rpb': out['v_b_rpb'], 'v_c_q_gain': out['v_c_q_gain'], 'v_c_k_gain': out['v_c_k_gain'], 'v_out_gain_a': out['v_out_gain_a'], 'v_out_gain_b': out['v_out_gain_b'], 'v_out_gain_c': out['v_out_gain_c'], 'v_w_o': out['v_w_o'], 'v_norm_mlp': out['v_norm_mlp'], 'v_w_up': out['v_w_up'], 'v_w_down': out['v_w_down'], 'v_norm_final': out['v_norm_final']}


def _loss(weights, diff, rest, loss_target):
    with _jax.named_scope("forward"):
        args = {**rest, TWIN_DIFF_INPUT: diff, **{k: w.astype(_WEIGHT_DTYPES[k]) for k, w in weights.items()}}
        y = _forward(args)
    with _jax.named_scope("loss_head"):
        err = _jnp.square(y.astype(_jnp.float32) - loss_target)
        return 0.5 * _jnp.sum(_jnp.mean(err, axis=-1)) if err.ndim else 0.5 * err


def _adamw(w, g, m, v):
    m = ADAM_B1 * m + (1.0 - ADAM_B1) * g
    v = ADAM_B2 * v + (1.0 - ADAM_B2) * _jnp.square(g)
    m_hat = m / (1.0 - ADAM_B1 ** ADAM_STEP)
    v_hat = v / (1.0 - ADAM_B2 ** ADAM_STEP)
    delta = -ADAM_LR * (m_hat / (_jnp.sqrt(v_hat) + ADAM_EPS) + ADAM_WD * w)
    return delta, m, v


def reference(x, norm_mix, w_in, a_sink, t5_table, b_rpb, c_q_gain, c_k_gain, out_gain_a, out_gain_b, out_gain_c, w_o, norm_mlp, w_up, w_down, norm_final, loss_target, m_norm_mix, m_w_in, m_a_sink, m_t5_table, m_b_rpb, m_c_q_gain, m_c_k_gain, m_out_gain_a, m_out_gain_b, m_out_gain_c, m_w_o, m_norm_mlp, m_w_up, m_w_down, m_norm_final, v_norm_mix, v_w_in, v_a_sink, v_t5_table, v_b_rpb, v_c_q_gain, v_c_k_gain, v_out_gain_a, v_out_gain_b, v_out_gain_c, v_w_o, v_norm_mlp, v_w_up, v_w_down, v_norm_final):
    given = dict(x=x, norm_mix=norm_mix, w_in=w_in, a_sink=a_sink, t5_table=t5_table, b_rpb=b_rpb, c_q_gain=c_q_gain, c_k_gain=c_k_gain, out_gain_a=out_gain_a, out_gain_b=out_gain_b, out_gain_c=out_gain_c, w_o=w_o, norm_mlp=norm_mlp, w_up=w_up, w_down=w_down, norm_final=norm_final, loss_target=loss_target, m_norm_mix=m_norm_mix, m_w_in=m_w_in, m_a_sink=m_a_sink, m_t5_table=m_t5_table, m_b_rpb=m_b_rpb, m_c_q_gain=m_c_q_gain, m_c_k_gain=m_c_k_gain, m_out_gain_a=m_out_gain_a, m_out_gain_b=m_out_gain_b, m_out_gain_c=m_out_gain_c, m_w_o=m_w_o, m_norm_mlp=m_norm_mlp, m_w_up=m_w_up, m_w_down=m_w_down, m_norm_final=m_norm_final, v_norm_mix=v_norm_mix, v_w_in=v_w_in, v_a_sink=v_a_sink, v_t5_table=v_t5_table, v_b_rpb=v_b_rpb, v_c_q_gain=v_c_q_gain, v_c_k_gain=v_c_k_gain, v_out_gain_a=v_out_gain_a, v_out_gain_b=v_out_gain_b, v_out_gain_c=v_out_gain_c, v_w_o=v_w_o, v_norm_mlp=v_norm_mlp, v_w_up=v_w_up, v_w_down=v_w_down, v_norm_final=v_norm_final)
    weights = {n: given[n] for n in TWIN_WEIGHTS}
    shared = {n: given[n] for n in SHARED_INPUTS}
    per_example = {n: given[n] for n in ['x']}
    grad_fn = _jax.value_and_grad(_loss, argnums=(0, 1))

    def one_microbatch(ex, loss_target):
        ex = dict(ex)
        diff = ex.pop(TWIN_DIFF_INPUT)
        return grad_fn(weights, diff, {**shared, **ex}, loss_target)

    if N_MICROBATCH == 1:
        loss, (grad_w, grad_x) = one_microbatch(per_example, given["loss_target"])
    else:
        def body(carry, xs):
            loss_sum, grad_sum = carry
            l_k, (gw_k, gx_k) = one_microbatch(xs[0], xs[1])
            with _jax.named_scope("update"):
                return (loss_sum + l_k, _jax.tree.map(_jnp.add, grad_sum, gw_k)), gx_k

        init = (_jnp.zeros((), _jnp.float32), _jax.tree.map(_jnp.zeros_like, weights))
        (loss, grad_w), grad_x = _jax.lax.scan(body, init, (per_example, given["loss_target"]))
    with _jax.named_scope("update"):
        delta_w, new_m, new_v = {}, {}, {}
        for n in TWIN_WEIGHTS:
            delta_w[n], new_m[n], new_v[n] = _adamw(weights[n], grad_w[n], given["m_" + n], given["v_" + n])
    return (loss, grad_x, *[grad_w[n] for n in TWIN_WEIGHTS], *[delta_w[n] for n in TWIN_WEIGHTS],
            *[new_m[n] for n in TWIN_WEIGHTS], *[new_v[n] for n in TWIN_WEIGHTS])
```

```python
import functools
import math

import jax
import jax.numpy as jnp
from jax import lax
from jax.experimental import pallas as pl
from jax.experimental.pallas import tpu as pltpu

F32 = jnp.float32
MXU_DT = jnp.bfloat16
HIGHEST = lax.Precision.HIGHEST

HEAD_DIM = 64
LANES = 128
EPS = 1e-6
MASK_VALUE = -1e30
GRID_W = 64
NA_ROWS = 8
T5_BUCKETS = 32
T5_MAX_DIST = 128
ROPE_THETA = 10000.0
ADAM_LR, ADAM_B1, ADAM_B2, ADAM_EPS, ADAM_WD, ADAM_STEP = 0.001, 0.9, 0.999, 1e-08, 0.01, 10
VMEM_LIMIT = 56 * 1024 * 1024

MESH_ID = pl.DeviceIdType.MESH
ANY = pl.BlockSpec(memory_space=pl.ANY)

NT_DIMS = (((1,), (1,)), ((), ()))
TN_DIMS = (((0,), (0,)), ((), ()))
NN_DIMS = (((1,), (0,)), ((), ()))


def _dot(a, b, dims=NN_DIMS):
    return lax.dot_general(a, b, dims, preferred_element_type=F32)


def _call(body, *, name, out_shape, grid=(), in_specs=None, out_specs=None, scratch=(), sem=None,
          prefetch=0, aliases=None):
    params = {"vmem_limit_bytes": VMEM_LIMIT}
    if sem is not None:
        params["dimension_semantics"] = sem
    kwargs = {}
    if aliases:
        kwargs["input_output_aliases"] = aliases
    if prefetch:
        spec = pltpu.PrefetchScalarGridSpec(num_scalar_prefetch=prefetch, grid=grid, in_specs=in_specs,
                                            out_specs=out_specs, scratch_shapes=list(scratch))
        return pl.pallas_call(body, grid_spec=spec, out_shape=out_shape, name=name,
                              compiler_params=pltpu.CompilerParams(**params), **kwargs)
    return pl.pallas_call(body, grid=grid, in_specs=in_specs, out_specs=out_specs, out_shape=out_shape,
                          scratch_shapes=list(scratch), name=name,
                          compiler_params=pltpu.CompilerParams(**params), **kwargs)


def _sds(shape, dtype=F32):
    return jax.ShapeDtypeStruct(tuple(shape), dtype)


def _matmul(a, b, *, mode, name, tm, tn, tk, epi="plain", extra=(), out_dtypes=(F32,), mkn=None,
            b_spec=None, out_spec=None, out_shape=None):
    if mkn is None:
        if mode == "nn":
            (m, k), n = a.shape, b.shape[1]
        elif mode == "nt":
            (m, k), n = a.shape, b.shape[0]
        else:
            (k, m), n = a.shape, b.shape[1]
    else:
        m, k, n = mkn
    tm, tn, tk = min(tm, m), min(tn, n), min(tk, k)
    assert m % tm == 0 and n % tn == 0 and k % tk == 0, (name, m, n, k, tm, tn, tk)
    nk = k // tk
    dims = {"nn": NN_DIMS, "nt": NT_DIMS, "tn": TN_DIMS}[mode]
    n_extra, n_out = len(extra), len(out_dtypes)

    def body(a_ref, b_ref, *rest):
        extra_refs = rest[:n_extra]
        out_refs = rest[n_extra:n_extra + n_out]
        acc_ref = rest[n_extra + n_out]
        kk = pl.program_id(2)

        @pl.when(kk == 0)
        def _():
            acc_ref[...] = jnp.zeros_like(acc_ref)

        acc_ref[...] += _dot(a_ref[...].astype(MXU_DT), b_ref[...].astype(MXU_DT), dims)

        @pl.when(kk == nk - 1)
        def _():
            acc = acc_ref[...]
            if epi == "plain":
                out_refs[0][...] = acc.astype(out_refs[0].dtype)
            elif epi == "res":
                out_refs[0][...] = (extra_refs[0][...] + acc).astype(out_refs[0].dtype)
            elif epi == "relu2":
                u = jnp.maximum(acc, 0.0)
                out_refs[0][...] = u.astype(out_refs[0].dtype)
                out_refs[1][...] = (u * u).astype(out_refs[1].dtype)
            elif epi == "mul2u":
                out_refs[0][...] = (2.0 * extra_refs[0][...] * acc).astype(out_refs[0].dtype)
            else:
                raise ValueError(epi)

    if mode == "tn":
        a_spec = pl.BlockSpec((tk, tm), lambda i, j, kk: (kk, i))
    else:
        a_spec = pl.BlockSpec((tm, tk), lambda i, j, kk: (i, kk))
    if b_spec is None:
        if mode == "nt":
            b_spec = pl.BlockSpec((tn, tk), lambda i, j, kk: (j, kk))
        else:
            b_spec = pl.BlockSpec((tk, tn), lambda i, j, kk: (kk, j))
    mn_spec = pl.BlockSpec((tm, tn), lambda i, j, kk: (i, j))
    if out_spec is None:
        out_spec = mn_spec
    if out_shape is None:
        out_shape = (m, n)
    res = _call(body, name=name, grid=(m // tm, n // tn, nk),
                in_specs=[a_spec, b_spec] + [mn_spec] * n_extra,
                out_specs=[out_spec] * n_out,
                out_shape=[_sds(out_shape, d) for d in out_dtypes],
                scratch=[pltpu.VMEM((tm, tn), F32)],
                sem=("parallel", "parallel", "arbitrary"))(a, b, *extra)
    return res if n_out > 1 else res[0]


def _row_tile(s):
    return min(512, s)


def _rms_fwd(x, g, name):
    s, d = x.shape
    tm = _row_tile(s)

    def body(x_ref, g_ref, h_ref):
        xv = x_ref[...]
        r = lax.rsqrt(jnp.mean(xv * xv, axis=-1, keepdims=True) + EPS)
        h_ref[...] = ((xv * r) * g_ref[...]).astype(h_ref.dtype)

    return _call(body, name=name, grid=(s // tm,),
                 in_specs=[pl.BlockSpec((tm, d), lambda i: (i, 0)), pl.BlockSpec((1, d), lambda i: (0, 0))],
                 out_specs=pl.BlockSpec((tm, d), lambda i: (i, 0)),
                 out_shape=_sds((s, d), MXU_DT), sem=("parallel",))(x, g.reshape(1, d))


def _rms_bwd(x, g, dh, dres, name):
    s, d = x.shape
    tm = _row_tile(s)

    def body(x_ref, g_ref, dh_ref, dres_ref, dx_ref, dxb_ref, dg_ref):
        i = pl.program_id(0)
        xv = x_ref[...]
        r = lax.rsqrt(jnp.mean(xv * xv, axis=-1, keepdims=True) + EPS)
        xh = xv * r
        dhv = dh_ref[...]
        gd = dhv * g_ref[...]
        c = jnp.mean(gd * xh, axis=-1, keepdims=True)
        dx = dres_ref[...] + r * (gd - xh * c)
        dx_ref[...] = dx
        dxb_ref[...] = dx.astype(dxb_ref.dtype)
        part = jnp.sum(dhv * xh, axis=0, keepdims=True)

        @pl.when(i == 0)
        def _():
            dg_ref[...] = part

        @pl.when(i > 0)
        def _():
            dg_ref[...] += part

    row = pl.BlockSpec((tm, d), lambda i: (i, 0))
    vec = pl.BlockSpec((1, d), lambda i: (0, 0))
    return _call(body, name=name, grid=(s // tm,), in_specs=[row, vec, row, row],
                 out_specs=[row, row, vec],
                 out_shape=[_sds((s, d)), _sds((s, d), MXU_DT), _sds((1, d))],
                 sem=("arbitrary",))(x, g.reshape(1, d), dh, dres)


def _final_loss(x, g, target, name):
    s, d = x.shape
    tm = _row_tile(s)

    def body(x_ref, g_ref, t_ref, loss_ref, dx_ref, dxb_ref, dg_ref):
        i = pl.program_id(0)
        xv = x_ref[...]
        gv = g_ref[...]
        r = lax.rsqrt(jnp.mean(xv * xv, axis=-1, keepdims=True) + EPS)
        xh = xv * r
        err = xh * gv - t_ref[...]
        part_loss = 0.5 * jnp.sum(jnp.mean(err * err, axis=-1, keepdims=True), axis=0, keepdims=True)
        dy = err * (1.0 / d)
        gd = dy * gv
        c = jnp.mean(gd * xh, axis=-1, keepdims=True)
        dx = r * (gd - xh * c)
        dx_ref[...] = dx
        dxb_ref[...] = dx.astype(dxb_ref.dtype)
        part_g = jnp.sum(dy * xh, axis=0, keepdims=True)
        part_l = jnp.broadcast_to(part_loss, (1, LANES))

        @pl.when(i == 0)
        def _():
            dg_ref[...] = part_g
            loss_ref[...] = part_l

        @pl.when(i > 0)
        def _():
            dg_ref[...] += part_g
            loss_ref[...] += part_l

    row = pl.BlockSpec((tm, d), lambda i: (i, 0))
    vec = pl.BlockSpec((1, d), lambda i: (0, 0))
    return _call(body, name=name, grid=(s // tm,), in_specs=[row, vec, row],
                 out_specs=[pl.BlockSpec((1, LANES), lambda i: (0, 0)), row, row, vec],
                 out_shape=[_sds((1, LANES)), _sds((s, d)), _sds((s, d), MXU_DT), _sds((1, d))],
                 sem=("arbitrary",))(x, g.reshape(1, d), target)


def _lane_iota(shape):
    return lax.broadcasted_iota(jnp.int32, shape, len(shape) - 1)


def _swap_halves(x):
    return pltpu.roll(x, HEAD_DIM, 1)


def _segsum64(x, ones_ref):
    ones = ones_ref[...]
    outs = []
    for c in range(x.shape[1] // LANES):
        xc = x[:, c * LANES:(c + 1) * LANES]
        hi = xc.astype(MXU_DT)
        r1 = xc - hi.astype(F32)
        mid = r1.astype(MXU_DT)
        lo = (r1 - mid.astype(F32)).astype(MXU_DT)
        outs.append(_dot(hi, ones) + _dot(mid, ones) + _dot(lo, ones))
    return outs[0] if len(outs) == 1 else jnp.concatenate(outs, axis=1)


def _pair_ones():
    i = jnp.arange(LANES)
    return (i[:, None] // HEAD_DIM == i[None, :] // HEAD_DIM).astype(MXU_DT)


def _col(x, lane):
    return jnp.sum(jnp.where(_lane_iota(x.shape) == lane, x, 0.0), axis=-1, keepdims=True)


class _LocalCfg:
    def __init__(self, *, groups, qb, kw, qw, qcol, kcol, vcol, kvhalf, kstart, variant):
        self.groups, self.qb, self.kw, self.qw = groups, qb, kw, qw
        self.qcol, self.kcol, self.vcol = qcol, kcol, vcol
        self.kvhalf = kvhalf
        self.kstart, self.variant = kstart, variant
        self.pairs = qw // LANES


def _cfg_a(s):
    nb = s // 128
    return _LocalCfg(groups=1, qb=128, kw=384, qw=512, qcol=lambda g: 0, kcol=lambda g: 4, vcol=lambda g: 5,
                     kvhalf=lambda t, e: t // 2,
                     kstart=lambda n: 128 * jnp.clip(n - 1, 0, nb - 3),
                     variant=lambda n: jnp.where(n <= 0, 0, jnp.where(n == nb - 1, 2, 1)))


def _cfg_b(s):
    rows = s // GRID_W
    return _LocalCfg(groups=4, qb=64, kw=512, qw=128, qcol=lambda g: 6 + g, kcol=lambda g: 10 + g,
                     vcol=lambda g: 14 + g, kvhalf=lambda t, e: e,
                     kstart=lambda n: GRID_W * jnp.clip(n - NA_ROWS // 2, 0, rows - NA_ROWS),
                     variant=lambda n: jnp.where(n < 4, jnp.maximum(n, 0),
                                                 jnp.where(n > rows - 4, n - (rows - 8), 4)))


def _local_head(cfg, t, e, qp, qp_sw, kb, bias, sink_row, left_q):
    kvh = cfg.kvhalf(t, e)
    qsrc = qp if e == kvh else qp_sw
    keep = left_q if kvh == 0 else jnp.logical_not(left_q)
    qm = jnp.where(keep, qsrc, 0.0).astype(MXU_DT)
    sc = _dot(qm, kb, NT_DIMS) + bias
    snk = _col(sink_row, 2 * t + e)
    m = jnp.maximum(jnp.max(sc, axis=-1, keepdims=True), snk)
    p = jnp.exp(sc - m)
    l = jnp.sum(p, axis=-1, keepdims=True) + jnp.exp(snk - m)
    p = p / l
    return qm, keep, p, m, l, snk


def _local_attn_fwd(proj, bias, sink, cfg, name):
    s = proj.shape[0]
    qb, kw, qw, g_n = cfg.qb, cfg.kw, cfg.qw, cfg.groups
    hq = 2 * cfg.pairs

    def body(q_ref, k_ref, v_ref, b_ref, s_ref, o_ref):
        n = pl.program_id(1)
        ks = pl.multiple_of(cfg.kstart(n), 64)
        kf = k_ref[pl.ds(ks, kw), :]
        vf = v_ref[pl.ds(ks, kw), :]
        kb = kf.astype(MXU_DT)
        vf_sw = _swap_halves(vf)
        left_q = _lane_iota((qb, LANES)) < HEAD_DIM
        left_k = _lane_iota((kw, LANES)) < HEAD_DIM
        sink_row = s_ref[...]
        for t in range(cfg.pairs):
            qp = q_ref[:, t * LANES:(t + 1) * LANES] * 0.125
            qp_sw = _swap_halves(qp)
            acc = jnp.zeros((qb, LANES), F32)
            for e in range(2):
                _, _, p, _, _, _ = _local_head(cfg, t, e, qp, qp_sw, kb, b_ref[0, 2 * t + e], sink_row, left_q)
                vsrc = vf if e == cfg.kvhalf(t, e) else vf_sw
                vsel = jnp.where(left_k if e == 0 else jnp.logical_not(left_k), vsrc, 0.0).astype(MXU_DT)
                acc = acc + _dot(p.astype(MXU_DT), vsel)
            o_ref[:, t * LANES:(t + 1) * LANES] = acc

    return _call(
        body, name=name, grid=(g_n, s // qb),
        in_specs=[pl.BlockSpec((qb, qw), lambda g, n: (n, cfg.qcol(g))),
                  pl.BlockSpec((s, LANES), lambda g, n: (0, cfg.kcol(g))),
                  pl.BlockSpec((s, LANES), lambda g, n: (0, cfg.vcol(g))),
                  pl.BlockSpec((1, hq, qb, kw), lambda g, n: (cfg.variant(n), g, 0, 0)),
                  pl.BlockSpec((None, 1, LANES), lambda g, n: (g, 0, 0))],
        out_specs=pl.BlockSpec((qb, qw), lambda g, n: (n, g)),
        out_shape=_sds((s, g_n * qw)), sem=("parallel", "arbitrary"))(proj, proj, proj, bias, sink)


def _local_attn_bwd(proj, bias, sink, do, cfg, name):
    s = proj.shape[0]
    qb, kw, qw, g_n = cfg.qb, cfg.kw, cfg.qw, cfg.groups
    hq = 2 * cfg.pairs

    def body(q_ref, k_ref, v_ref, b_ref, s_ref, do_ref, dq_ref, dk_ref, dv_ref, db_ref, dsk_ref):
        n = pl.program_id(1)
        ks = pl.multiple_of(cfg.kstart(n), 64)
        first = jnp.logical_or(n == 0, cfg.variant(n) != cfg.variant(n - 1))

        @pl.when(n == 0)
        def _():
            dk_ref[...] = jnp.zeros_like(dk_ref)
            dv_ref[...] = jnp.zeros_like(dv_ref)
            dsk_ref[...] = jnp.zeros_like(dsk_ref)

        @pl.when(first)
        def _():
            db_ref[...] = jnp.zeros_like(db_ref)

        kf = k_ref[pl.ds(ks, kw), :]
        vf = v_ref[pl.ds(ks, kw), :]
        kb = kf.astype(MXU_DT)
        vb = vf.astype(MXU_DT)
        kf_sw = _swap_halves(kf)
        left_q = _lane_iota((qb, LANES)) < HEAD_DIM
        left_k = _lane_iota((kw, LANES)) < HEAD_DIM
        sink_row = s_ref[...]
        row0 = lax.broadcasted_iota(jnp.int32, (8, LANES), 0) == 0
        lane8 = _lane_iota((8, LANES))
        dk_acc = jnp.zeros((kw, LANES), F32)
        dv_acc = jnp.zeros((kw, LANES), F32)
        dsk_acc = jnp.zeros((8, LANES), F32)
        for t in range(cfg.pairs):
            qp = q_ref[:, t * LANES:(t + 1) * LANES] * 0.125
            qp_sw = _swap_halves(qp)
            dop = do_ref[:, t * LANES:(t + 1) * LANES]
            dop_sw = _swap_halves(dop)
            dq_t = jnp.zeros((qb, LANES), F32)
            for e in range(2):
                h = 2 * t + e
                qm, keep, p, m, l, snk = _local_head(cfg, t, e, qp, qp_sw, kb, b_ref[0, h], sink_row, left_q)
                kvh = cfg.kvhalf(t, e)
                dom = jnp.where(keep, dop if e == kvh else dop_sw, 0.0).astype(MXU_DT)
                dp = _dot(dom, vb, NT_DIMS)
                dd = jnp.sum(p * dp, axis=-1, keepdims=True)
                ds = p * (dp - dd)
                p_sink = jnp.exp(snk - m) / l
                dsink = jnp.sum(-p_sink * dd, axis=0, keepdims=True)
                dsk_acc = dsk_acc + jnp.where(jnp.logical_and(row0, lane8 == h), dsink, 0.0)
                dsb = ds.astype(MXU_DT)
                dv_acc = dv_acc + _dot(p.astype(MXU_DT), dom, TN_DIMS)
                dk_acc = dk_acc + _dot(dsb, qm, TN_DIMS)
                ksrc = kf if e == kvh else kf_sw
                ksel = jnp.where(left_k if e == 0 else jnp.logical_not(left_k), ksrc, 0.0).astype(MXU_DT)
                dq_t = dq_t + _dot(dsb, ksel)
                db_ref[0, h] += ds
            dq_ref[:, t * LANES:(t + 1) * LANES] = dq_t * 0.125
        dk_ref[pl.ds(ks, kw), :] += dk_acc
        dv_ref[pl.ds(ks, kw), :] += dv_acc
        dsk_ref[...] += dsk_acc

    n_var = bias.shape[0]
    return _call(
        body, name=name, grid=(g_n, s // qb),
        in_specs=[pl.BlockSpec((qb, qw), lambda g, n: (n, cfg.qcol(g))),
                  pl.BlockSpec((s, LANES), lambda g, n: (0, cfg.kcol(g))),
                  pl.BlockSpec((s, LANES), lambda g, n: (0, cfg.vcol(g))),
                  pl.BlockSpec((1, hq, qb, kw), lambda g, n: (cfg.variant(n), g, 0, 0)),
                  pl.BlockSpec((None, 1, LANES), lambda g, n: (g, 0, 0)),
                  pl.BlockSpec((qb, qw), lambda g, n: (n, g))],
        out_specs=[pl.BlockSpec((qb, qw), lambda g, n: (n, g)),
                   pl.BlockSpec((s, LANES), lambda g, n: (0, g)),
                   pl.BlockSpec((s, LANES), lambda g, n: (0, g)),
                   pl.BlockSpec((1, hq, qb, kw), lambda g, n: (cfg.variant(n), g, 0, 0)),
                   pl.BlockSpec((None, 8, LANES), lambda g, n: (g, 0, 0))],
        out_shape=[_sds((s, g_n * qw)), _sds((s, g_n * LANES)), _sds((s, g_n * LANES)),
                   _sds((n_var, g_n * hq, qb, kw)), _sds((g_n, 8, LANES))],
        sem=("parallel", "arbitrary"))(proj, proj, proj, bias, sink, do)


QC_COL, KC_COL, VC_COL = 9, 13, 14
CW = 256


def _swap16(x):
    w = x.shape[1]
    lane = _lane_iota(x.shape)
    return jnp.where(lane % 32 < 16, pltpu.roll(x, w - 16, 1), pltpu.roll(x, 16, 1))


def _dup_halves(x):
    left = _lane_iota(x.shape) < HEAD_DIM
    sw = _swap_halves(x)
    return jnp.where(left, x, sw), jnp.where(left, sw, x)


def _normrope(x, gain, cos, sin, ones_ref):
    ms = _segsum64(x * x, ones_ref) * (1.0 / HEAD_DIM)
    r = lax.rsqrt(ms + EPS)
    y = (x * r) * gain
    return y * cos + _swap16(y) * sin, r


def _cprep_fwd(proj, gq, gk, cos, sin, ones, name):
    s = proj.shape[0]
    tm = _row_tile(s)

    def body(q0, q1, q2, q3, k_ref, v_ref, gq_ref, gk_ref, cos_ref, sin_ref, ones_ref, qh_ref, kd_ref, vd_ref):
        cos_v, sin_v = cos_ref[...], sin_ref[...]
        for c, q_ref in enumerate((q0, q1, q2, q3)):
            y, _ = _normrope(q_ref[...], gq_ref[...], cos_v, sin_v, ones_ref)
            qh_ref[:, c * CW:(c + 1) * CW] = (y * 0.125).astype(qh_ref.dtype)
        yk, _ = _normrope(k_ref[...], gk_ref[...], cos_v, sin_v, ones_ref)
        vv = v_ref[...]
        for p in range(2):
            ka, kb_ = _dup_halves(yk[:, p * LANES:(p + 1) * LANES])
            va, vb_ = _dup_halves(vv[:, p * LANES:(p + 1) * LANES])
            kd_ref[:, (2 * p) * LANES:(2 * p + 1) * LANES] = ka.astype(kd_ref.dtype)
            kd_ref[:, (2 * p + 1) * LANES:(2 * p + 2) * LANES] = kb_.astype(kd_ref.dtype)
            vd_ref[:, (2 * p) * LANES:(2 * p + 1) * LANES] = va.astype(vd_ref.dtype)
            vd_ref[:, (2 * p + 1) * LANES:(2 * p + 2) * LANES] = vb_.astype(vd_ref.dtype)

    def chunk(col):
        return pl.BlockSpec((tm, CW), lambda i: (i, col))

    vec = pl.BlockSpec((1, CW), lambda i: (0, 0))
    tab = pl.BlockSpec((tm, CW), lambda i: (i, 0))
    return _call(body, name=name, grid=(s // tm,),
                 in_specs=[chunk(QC_COL), chunk(QC_COL + 1), chunk(QC_COL + 2), chunk(QC_COL + 3),
                           chunk(KC_COL), chunk(VC_COL), vec, vec, tab, tab,
                           pl.BlockSpec((LANES, LANES), lambda i: (0, 0))],
                 out_specs=[pl.BlockSpec((tm, 4 * CW), lambda i: (i, 0)),
                            pl.BlockSpec((tm, 2 * CW), lambda i: (i, 0)),
                            pl.BlockSpec((tm, 2 * CW), lambda i: (i, 0))],
                 out_shape=[_sds((s, 4 * CW), MXU_DT), _sds((s, 2 * CW), MXU_DT), _sds((s, 2 * CW), MXU_DT)],
                 sem=("parallel",))(proj, proj, proj, proj, proj, proj, gq, gk, cos, sin, ones)


def _cprep_bwd(proj, gq, gk, cos, sin, ones, dqh, dkd, dvd, name):
    s = proj.shape[0]
    tm = _row_tile(s)

    def fold(ref, p):
        a = ref[:, (2 * p) * LANES:(2 * p + 1) * LANES]
        b = ref[:, (2 * p + 1) * LANES:(2 * p + 2) * LANES]
        ta = a + _swap_halves(a)
        tb = b + _swap_halves(b)
        return jnp.where(_lane_iota(a.shape) < HEAD_DIM, ta, tb)

    def norm_bwd(x, gain, dyr, cos_v, sin_v, ones_ref):
        dy = dyr * cos_v + _swap16(dyr * sin_v)
        ms = _segsum64(x * x, ones_ref) * (1.0 / HEAD_DIM)
        r = lax.rsqrt(ms + EPS)
        xh = x * r
        gd = dy * gain
        c = _segsum64(gd * xh, ones_ref) * (1.0 / HEAD_DIM)
        return r * (gd - xh * c), jnp.sum(dy * xh, axis=0, keepdims=True)

    def body(q0, q1, q2, q3, k_ref, gq_ref, gk_ref, cos_ref, sin_ref, ones_ref, dqh_ref, dkd_ref, dvd_ref,
             dq_ref, dk_ref, dv_ref, dgq_ref, dgk_ref):
        i = pl.program_id(0)
        cos_v, sin_v = cos_ref[...], sin_ref[...]
        gq_part = jnp.zeros((1, CW), F32)
        for c, q_ref in enumerate((q0, q1, q2, q3)):
            dx, dg = norm_bwd(q_ref[...], gq_ref[...], dqh_ref[:, c * CW:(c + 1) * CW] * 0.125, cos_v, sin_v,
                              ones_ref)
            dq_ref[:, c * CW:(c + 1) * CW] = dx
            gq_part = gq_part + dg
        dkr = jnp.concatenate([fold(dkd_ref, 0), fold(dkd_ref, 1)], axis=1)
        dxk, gk_part = norm_bwd(k_ref[...], gk_ref[...], dkr, cos_v, sin_v, ones_ref)
        dk_ref[...] = dxk
        dv_ref[...] = jnp.concatenate([fold(dvd_ref, 0), fold(dvd_ref, 1)], axis=1)

        @pl.when(i == 0)
        def _():
            dgq_ref[...] = gq_part
            dgk_ref[...] = gk_part

        @pl.when(i > 0)
        def _():
            dgq_ref[...] += gq_part
            dgk_ref[...] += gk_part

    def chunk(col):
        return pl.BlockSpec((tm, CW), lambda i: (i, col))

    vec = pl.BlockSpec((1, CW), lambda i: (0, 0))
    tab = pl.BlockSpec((tm, CW), lambda i: (i, 0))
    return _call(body, name=name, grid=(s // tm,),
                 in_specs=[chunk(QC_COL), chunk(QC_COL + 1), chunk(QC_COL + 2), chunk(QC_COL + 3), chunk(KC_COL),
                           vec, vec, tab, tab, pl.BlockSpec((LANES, LANES), lambda i: (0, 0)),
                           pl.BlockSpec((tm, 4 * CW), lambda i: (i, 0)),
                           pl.BlockSpec((tm, 2 * CW), lambda i: (i, 0)),
                           pl.BlockSpec((tm, 2 * CW), lambda i: (i, 0))],
                 out_specs=[pl.BlockSpec((tm, 4 * CW), lambda i: (i, 0)), tab, tab, vec, vec],
                 out_shape=[_sds((s, 4 * CW)), _sds((s, CW)), _sds((s, CW)), _sds((1, CW)), _sds((1, CW))],
                 sem=("arbitrary",))(proj, proj, proj, proj, proj, gq, gk, cos, sin, ones, dqh, dkd, dvd)


def _flash_tiles(s):
    return min(512, s), min(512, s)


def _flash_fwd(qh, kd, vd, name):
    s = qh.shape[0]
    tq, tk = _flash_tiles(s)
    nk = s // tk

    def body(q_ref, k_ref, v_ref, o_ref, lse_ref, qm_ref, m_ref, l_ref, acc_ref):
        j = pl.program_id(2)
        left_q = _lane_iota((tq, LANES)) < HEAD_DIM

        @pl.when(j == 0)
        def _():
            m_ref[...] = jnp.full(m_ref.shape, MASK_VALUE, F32)
            l_ref[...] = jnp.zeros_like(l_ref)
            acc_ref[...] = jnp.zeros_like(acc_ref)
            for t in range(2):
                qp = q_ref[:, t * LANES:(t + 1) * LANES]
                qm_ref[2 * t] = jnp.where(left_q, qp, jnp.zeros_like(qp))
                qm_ref[2 * t + 1] = jnp.where(left_q, jnp.zeros_like(qp), qp)

        kb = k_ref[...]
        vv = v_ref[...]
        left_k = _lane_iota((tk, LANES)) < HEAD_DIM
        vsel = (jnp.where(left_k, vv, jnp.zeros_like(vv)), jnp.where(left_k, jnp.zeros_like(vv), vv))
        for t in range(2):
            pv, alpha = [], []
            for e in range(2):
                h = 2 * t + e
                sc = _dot(qm_ref[h], kb, NT_DIMS)
                m_prev = m_ref[h]
                m_new = jnp.maximum(m_prev, jnp.max(sc, axis=-1, keepdims=True))
                a = jnp.exp(m_prev - m_new)
                p = jnp.exp(sc - m_new)
                l_ref[h] = a * l_ref[h] + jnp.sum(p, axis=-1, keepdims=True)
                m_ref[h] = m_new
                pv.append(_dot(p.astype(MXU_DT), vsel[e]))
                alpha.append(a)
            acc_ref[t] = acc_ref[t] * jnp.where(left_q, alpha[0], alpha[1]) + pv[0] + pv[1]

        @pl.when(j == nk - 1)
        def _():
            for t in range(2):
                l0, l1 = l_ref[2 * t], l_ref[2 * t + 1]
                o_ref[:, t * LANES:(t + 1) * LANES] = acc_ref[t] / jnp.where(left_q, l0, l1)
                lse_ref[:, t * LANES:(t + 1) * LANES] = jnp.where(left_q, m_ref[2 * t] + jnp.log(l0),
                                                                 m_ref[2 * t + 1] + jnp.log(l1))

    qspec = pl.BlockSpec((tq, CW), lambda g, i, j: (i, g))
    kspec = pl.BlockSpec((tk, LANES), lambda g, i, j: (j, g))
    return _call(body, name=name, grid=(4, s // tq, nk), in_specs=[qspec, kspec, kspec],
                 out_specs=[qspec, qspec], out_shape=[_sds((s, 4 * CW)), _sds((s, 4 * CW))],
                 scratch=[pltpu.VMEM((4, tq, LANES), MXU_DT), pltpu.VMEM((4, tq, 1), F32),
                          pltpu.VMEM((4, tq, 1), F32), pltpu.VMEM((2, tq, LANES), F32)],
                 sem=("parallel", "parallel", "arbitrary"))(qh, kd, vd)


def _flash_bwd(qh, kd, vd, do, lse, dd, name):
    s = qh.shape[0]
    tq, tk = _flash_tiles(s)
    ni = s // tq

    def body(q_ref, k_ref, v_ref, do_ref, lse_ref, dd_ref, dq_ref, dk_ref, dv_ref, dk_acc, dv_acc):
        j = pl.program_id(1)
        i = pl.program_id(2)

        @pl.when(i == 0)
        def _():
            dk_acc[...] = jnp.zeros_like(dk_acc)
            dv_acc[...] = jnp.zeros_like(dv_acc)

        kb = k_ref[...]
        vb = v_ref[...]
        left_q = _lane_iota((tq, LANES)) < HEAD_DIM
        left_k = _lane_iota((tk, LANES)) < HEAD_DIM
        rows = pl.ds(pl.multiple_of(i * tq, tq), tq)
        for t in range(2):
            qp = q_ref[:, t * LANES:(t + 1) * LANES]
            dop = do_ref[:, t * LANES:(t + 1) * LANES]
            lsep = lse_ref[:, t * LANES:(t + 1) * LANES]
            ddp = dd_ref[:, t * LANES:(t + 1) * LANES]
            dq_t = jnp.zeros((tq, LANES), F32)
            for e in range(2):
                keep_q = left_q if e == 0 else jnp.logical_not(left_q)
                keep_k = left_k if e == 0 else jnp.logical_not(left_k)
                qm = jnp.where(keep_q, qp, jnp.zeros_like(qp))
                dom = jnp.where(keep_q, dop, jnp.zeros_like(dop))
                lse_h = _col(lsep, e * HEAD_DIM)
                dd_h = _col(ddp, e * HEAD_DIM)
                p = jnp.exp(_dot(qm, kb, NT_DIMS) - lse_h)
                dp = _dot(dom, vb, NT_DIMS)
                dsb = (p * (dp - dd_h)).astype(MXU_DT)
                dv_acc[...] += _dot(p.astype(MXU_DT), dom, TN_DIMS)
                dk_acc[...] += _dot(dsb, qm, TN_DIMS)
                dq_t = dq_t + _dot(dsb, jnp.where(keep_k, kb, jnp.zeros_like(kb)))

            @pl.when(j == 0)
            def _():
                dq_ref[rows, t * LANES:(t + 1) * LANES] = dq_t

            @pl.when(j > 0)
            def _():
                dq_ref[rows, t * LANES:(t + 1) * LANES] += dq_t

        @pl.when(i == ni - 1)
        def _():
            dk_ref[...] = dk_acc[...]
            dv_ref[...] = dv_acc[...]

    qspec = pl.BlockSpec((tq, CW), lambda g, j, i: (i, g))
    kspec = pl.BlockSpec((tk, LANES), lambda g, j, i: (j, g))
    return _call(body, name=name, grid=(4, s // tk, ni),
                 in_specs=[qspec, kspec, kspec, qspec, qspec, qspec],
                 out_specs=[pl.BlockSpec((s, CW), lambda g, j, i: (0, g)), kspec, kspec],
                 out_shape=[_sds((s, 4 * CW)), _sds((s, 2 * CW)), _sds((s, 2 * CW))],
                 scratch=[pltpu.VMEM((tk, LANES), F32), pltpu.VMEM((tk, LANES), F32)],
                 sem=("parallel", "arbitrary", "arbitrary"))(qh, kd, vd, do, lse, dd)


def _groupnorm_fwd(oa, ob, oc, ga, gb, gc, name):
    s = oa.shape[0]
    tm = _row_tile(s)
    wa, wb, wc = oa.shape[1], ob.shape[1], oc.shape[1]

    def body(oa_ref, ob_ref, oc_ref, ga_ref, gb_ref, gc_ref, mix_ref):
        off = 0
        for o_ref, g_ref, w in ((oa_ref, ga_ref, wa), (ob_ref, gb_ref, wb), (oc_ref, gc_ref, wc)):
            xv = o_ref[...]
            r = lax.rsqrt(jnp.mean(xv * xv, axis=-1, keepdims=True) + EPS)
            mix_ref[:, off:off + w] = ((xv * r) * g_ref[...]).astype(mix_ref.dtype)
            off += w

    def row(w):
        return pl.BlockSpec((tm, w), lambda i: (i, 0))

    def vec(w):
        return pl.BlockSpec((1, w), lambda i: (0, 0))

    return _call(body, name=name, grid=(s // tm,),
                 in_specs=[row(wa), row(wb), row(wc), vec(wa), vec(wb), vec(wc)],
                 out_specs=row(wa + wb + wc), out_shape=_sds((s, wa + wb + wc), MXU_DT),
                 sem=("parallel",))(oa, ob, oc, ga.reshape(1, wa), gb.reshape(1, wb), gc.reshape(1, wc))


def _groupnorm_bwd(dmix, oa, ob, oc, ga, gb, gc, ones, name):
    s = oa.shape[0]
    tm = _row_tile(s)
    wa, wb, wc = oa.shape[1], ob.shape[1], oc.shape[1]

    def body(dm_ref, oa_ref, ob_ref, oc_ref, ga_ref, gb_ref, gc_ref, ones_ref,
             doa_ref, dob_ref, doc_ref, docb_ref, dd_ref, dga_ref, dgb_ref, dgc_ref):
        i = pl.program_id(0)
        off = 0
        parts = []
        for o_ref, g_ref, do_ref, w in ((oa_ref, ga_ref, doa_ref, wa), (ob_ref, gb_ref, dob_ref, wb),
                                        (oc_ref, gc_ref, doc_ref, wc)):
            xv = o_ref[...]
            dh = dm_ref[:, off:off + w]
            r = lax.rsqrt(jnp.mean(xv * xv, axis=-1, keepdims=True) + EPS)
            xh = xv * r
            gd = dh * g_ref[...]
            c = jnp.mean(gd * xh, axis=-1, keepdims=True)
            dx = r * (gd - xh * c)
            do_ref[...] = dx
            parts.append(jnp.sum(dh * xh, axis=0, keepdims=True))
            if o_ref is oc_ref:
                docb_ref[...] = dx.astype(docb_ref.dtype)
                dd_ref[...] = _segsum64(dx * xv, ones_ref)
            off += w

        @pl.when(i == 0)
        def _():
            dga_ref[...], dgb_ref[...], dgc_ref[...] = parts

        @pl.when(i > 0)
        def _():
            dga_ref[...] += parts[0]
            dgb_ref[...] += parts[1]
            dgc_ref[...] += parts[2]

    def row(w):
        return pl.BlockSpec((tm, w), lambda i: (i, 0))

    def vec(w):
        return pl.BlockSpec((1, w), lambda i: (0, 0))

    return _call(body, name=name, grid=(s // tm,),
                 in_specs=[row(wa + wb + wc), row(wa), row(wb), row(wc), vec(wa), vec(wb), vec(wc),
                           pl.BlockSpec((LANES, LANES), lambda i: (0, 0))],
                 out_specs=[row(wa), row(wb), row(wc), row(wc), row(wc), vec(wa), vec(wb), vec(wc)],
                 out_shape=[_sds((s, wa)), _sds((s, wb)), _sds((s, wc)), _sds((s, wc), MXU_DT), _sds((s, wc)),
                            _sds((1, wa)), _sds((1, wb)), _sds((1, wc))],
                 sem=("arbitrary",))(dmix, oa, ob, oc, ga.reshape(1, wa), gb.reshape(1, wb), gc.reshape(1, wc), ones)


def _adam_math(w, g, m, v):
    m = ADAM_B1 * m + (1.0 - ADAM_B1) * g
    v = ADAM_B2 * v + (1.0 - ADAM_B2) * jnp.square(g)
    m_hat = m / (1.0 - ADAM_B1 ** ADAM_STEP)
    v_hat = v / (1.0 - ADAM_B2 ** ADAM_STEP)
    delta = -ADAM_LR * (m_hat / (jnp.sqrt(v_hat) + ADAM_EPS) + ADAM_WD * w)
    return delta, m, v


def _adamw(w, g, m, v, name):
    rows, cols = w.shape
    tr = min(256, rows)

    def body(w_ref, g_ref, m_ref, v_ref, d_ref, mo_ref, vo_ref):
        d_ref[...], mo_ref[...], vo_ref[...] = _adam_math(w_ref[...], g_ref[...], m_ref[...], v_ref[...])

    spec = pl.BlockSpec((tr, cols), lambda i: (i, 0))
    return _call(body, name=name, grid=(rows // tr,), in_specs=[spec] * 4, out_specs=[spec] * 3,
                 out_shape=[_sds((rows, cols))] * 3, sem=("parallel",))(w, g, m, v)


def _mesh_pos():
    return lax.axis_index("x"), lax.axis_index("y"), lax.axis_index("c")


def _peer_chips(x, y):
    return [(1 - x, y), (x, 1 - y), (1 - x, 1 - y)]


def _allgather_weights(shards):
    n = len(shards)

    def body(*refs):
        sh, out = refs[:n], refs[n:2 * n]
        send_sems, recv_sems, local_sems = refs[2 * n:]
        x, y, c = _mesh_pos()
        me = 2 * x + y
        sibling = (x, y, 1 - c)
        chips = _peer_chips(x, y)

        def half(t, chip, hc):
            h = sh[t].shape[1] // 2
            return out[t].at[:, chip, pl.ds(hc * h, h), :]

        def rcopy(t, k, src, dst, to):
            return pltpu.make_async_remote_copy(src_ref=src, dst_ref=dst, send_sem=send_sems.at[6 * t + k],
                                                recv_sem=recv_sems.at[6 * t + k], device_id=to,
                                                device_id_type=MESH_ID)

        local, first, passed = [], [], []
        for t in range(n):
            cp = pltpu.make_async_copy(sh[t], out[t].at[:, me], local_sems.at[t])
            cp.start()
            local.append(cp)
            h = sh[t].shape[1] // 2
            for k, (px, py) in enumerate(chips):
                cp = rcopy(t, k, sh[t].at[:, pl.ds(c * h, h), :], half(t, me, c), (px, py, c))
                cp.start()
                first.append(cp)
        for t in range(n):
            for k, (px, py) in enumerate(chips):
                blk = half(t, 2 * px + py, c)
                rcopy(t, k, blk, blk, sibling).wait_recv()
                cp = rcopy(t, 3 + k, blk, blk, sibling)
                cp.start()
                passed.append(cp)
        for t in range(n):
            for k, (px, py) in enumerate(chips):
                blk = half(t, 2 * px + py, 1 - c)
                rcopy(t, 3 + k, blk, blk, sibling).wait_recv()
        for cp in first + passed:
            cp.wait_send()
        for cp in local:
            cp.wait()

    return _call(body, name="allgather_weights", in_specs=[ANY] * n, out_specs=[ANY] * n,
                 out_shape=[_sds((w.shape[0], 4) + w.shape[1:], w.dtype) for w in shards],
                 scratch=[pltpu.SemaphoreType.DMA((6 * n,)), pltpu.SemaphoreType.DMA((6 * n,)),
                          pltpu.SemaphoreType.DMA((n,))])(*shards)


def _exchange_core_halves(grads):
    n = len(grads)

    def body(*refs):
        src, dst = refs[:n], refs[n:2 * n]
        send_sems, recv_sems = refs[2 * n:]
        x, y, c = _mesh_pos()
        copies = []
        for t in range(n):
            h = src[t].shape[1] // 2
            cp = pltpu.make_async_remote_copy(src_ref=src[t].at[:, pl.ds((1 - c) * h, h), :], dst_ref=dst[t],
                                              send_sem=send_sems.at[t], recv_sem=recv_sems.at[t],
                                              device_id=(x, y, 1 - c), device_id_type=MESH_ID)
            cp.start()
            copies.append(cp)
        for cp in copies:
            cp.wait()

    return _call(body, name="exchange_core_halves", in_specs=[ANY] * n, out_specs=[ANY] * n,
                 out_shape=[_sds((4, g.shape[1] // 2, g.shape[2])) for g in grads],
                 scratch=[pltpu.SemaphoreType.DMA((n,)), pltpu.SemaphoreType.DMA((n,))])(*grads)


def _add_half(g, r, c_idx, name):
    _, rows, cols = g.shape
    h = rows // 2
    tr = min(256, h)
    nb = h // tr

    def body(c_ref, g_ref, r_ref, o_ref):
        o_ref[...] = g_ref[...] + r_ref[...]

    return _call(body, name=name, grid=(4, nb), prefetch=1,
                 in_specs=[pl.BlockSpec((None, tr, cols), lambda s, i, c: (s, c[0] * nb + i, 0)),
                           pl.BlockSpec((None, tr, cols), lambda s, i, c: (s, i, 0))],
                 out_specs=pl.BlockSpec((None, tr, cols), lambda s, i, c: (s, i, 0)),
                 out_shape=_sds((4, h, cols)), sem=("parallel", "parallel"))(c_idx, g, r)


def _exchange_chips(parts):
    n = len(parts)

    def body(*refs):
        src, dst = refs[:n], refs[n:2 * n]
        send_sems, recv_sems = refs[2 * n:]
        x, y, c = _mesh_pos()
        copies = []
        for t in range(n):
            for k, (px, py) in enumerate(_peer_chips(x, y)):
                cp = pltpu.make_async_remote_copy(src_ref=src[t].at[2 * px + py], dst_ref=dst[t].at[k],
                                                  send_sem=send_sems.at[3 * t + k], recv_sem=recv_sems.at[3 * t + k],
                                                  device_id=(px, py, c), device_id_type=MESH_ID)
                cp.start()
                copies.append(cp)
        for cp in copies:
            cp.wait()

    return _call(body, name="exchange_chips", in_specs=[ANY] * n, out_specs=[ANY] * n,
                 out_shape=[_sds((3,) + p.shape[1:]) for p in parts],
                 scratch=[pltpu.SemaphoreType.DMA((3 * n,)), pltpu.SemaphoreType.DMA((3 * n,))])(*parts)


def _sum_chips(part, recv, me_idx, name):
    _, h, cols = part.shape
    tr = min(256, h)

    def body(me_ref, p_ref, r0_ref, r1_ref, r2_ref, o_ref):
        o_ref[...] = ((p_ref[...] + r0_ref[...]) + r1_ref[...]) + r2_ref[...]

    def slot(k):
        return pl.BlockSpec((None, tr, cols), lambda i, me: (k, i, 0))

    return _call(body, name=name, grid=(h // tr,), prefetch=1,
                 in_specs=[pl.BlockSpec((None, tr, cols), lambda i, me: (me[0], i, 0)), slot(0), slot(1), slot(2)],
                 out_specs=pl.BlockSpec((tr, cols), lambda i, me: (i, 0)),
                 out_shape=_sds((h, cols)), sem=("parallel",))(me_idx, part, recv, recv, recv)


def _share_halves(halves, n_layers):
    n = len(halves)
    n_types = n // n_layers

    def body(*refs):
        src, out = refs[:n], refs[n:n + n_types]
        send_sems, recv_sems, local_sems = refs[n + n_types:]
        x, y, c = _mesh_pos()
        copies = []
        for t in range(n):
            ty, layer = divmod(t, n_layers)
            h = src[t].shape[0]
            dst = out[ty].at[layer, pl.ds(c * h, h), :]
            lc = pltpu.make_async_copy(src[t], dst, local_sems.at[t])
            lc.start()
            rc = pltpu.make_async_remote_copy(src_ref=src[t], dst_ref=dst, send_sem=send_sems.at[t],
                                              recv_sem=recv_sems.at[t], device_id=(x, y, 1 - c),
                                              device_id_type=MESH_ID)
            rc.start()
            copies.append((lc, rc))
        for lc, rc in copies:
            lc.wait()
            rc.wait()

    out_shape = [_sds((n_layers, 2 * halves[ty * n_layers].shape[0], halves[ty * n_layers].shape[1]))
                 for ty in range(n_types)]
    return _call(body, name="share_halves", in_specs=[ANY] * n, out_specs=[ANY] * n_types, out_shape=out_shape,
                 scratch=[pltpu.SemaphoreType.DMA((n,)), pltpu.SemaphoreType.DMA((n,)),
                          pltpu.SemaphoreType.DMA((n,))])(*halves)


def _allreduce_small_adam(g, w, m, v):
    rows = g.shape[0]

    def body(g_ref, w_ref, m_ref, v_ref, gs_ref, d_ref, mo_ref, vo_ref, buf, send_sems, recv_sems):
        x, y, c = _mesh_pos()
        me = 4 * x + 2 * y + c
        buf[me] = g_ref[...]
        copies = []
        for k in range(1, 8):
            px = 1 - x if (k >> 2) & 1 else x
            py = 1 - y if (k >> 1) & 1 else y
            pc = 1 - c if k & 1 else c
            cp = pltpu.make_async_remote_copy(src_ref=g_ref, dst_ref=buf.at[me], send_sem=send_sems.at[k - 1],
                                              recv_sem=recv_sems.at[k - 1], device_id=(px, py, pc),
                                              device_id_type=MESH_ID)
            cp.start()
            copies.append(cp)
        for cp in copies:
            cp.wait()
        total = buf[0]
        for d in range(1, 8):
            total = total + buf[d]
        gs_ref[...] = total
        d_ref[...], mo_ref[...], vo_ref[...] = _adam_math(w_ref[...], total, m_ref[...], v_ref[...])

    vm = pl.BlockSpec(memory_space=pltpu.VMEM)
    return _call(body, name="allreduce_small_adam", in_specs=[vm] * 4, out_specs=[vm] * 4,
                 out_shape=[_sds((rows, LANES))] * 4,
                 scratch=[pltpu.VMEM((8, rows, LANES), F32), pltpu.SemaphoreType.DMA((7,)),
                          pltpu.SemaphoreType.DMA((7,))])(g, w, m, v)


def _t5_bucket(rel):
    nb = T5_BUCKETS // 2
    max_exact = nb // 2
    base = jnp.where(rel > 0, nb, 0)
    n = jnp.abs(rel)
    nf = jnp.maximum(n, 1).astype(F32)
    large = max_exact + (jnp.log(nf / max_exact) / math.log(T5_MAX_DIST / max_exact)
                         * (nb - max_exact)).astype(jnp.int32)
    large = jnp.minimum(large, nb - 1)
    return base + jnp.where(n < max_exact, n, large)


def _a_bias_maps():
    v = jnp.arange(3)[:, None, None]
    q = jnp.arange(128)[None, :, None]
    k = jnp.arange(384)[None, None, :]
    rel = k - 128 * v - q
    valid = jnp.abs(rel) <= 128
    onehot = (_t5_bucket(rel)[..., None] == jnp.arange(T5_BUCKETS)).astype(F32)
    return onehot * valid[..., None].astype(F32), valid


def _b_bias_maps():
    v = jnp.arange(8)[:, None]
    i = jnp.arange(NA_ROWS)[None, :]
    dr = jnp.where(v == 4, i + 3, i - v + 7)
    row_oh = (dr[..., None] == jnp.arange(2 * NA_ROWS - 1)).astype(F32)
    q = jnp.arange(GRID_W)[:, None]
    kc = jnp.arange(GRID_W)[None, :]
    cs = jnp.clip(q - 8, 0, GRID_W - 16)
    valid = (kc >= cs) & (kc < cs + 16)
    col_oh = ((kc - q + 15)[..., None] == jnp.arange(31)).astype(F32) * valid[..., None].astype(F32)
    return row_oh, col_oh, valid


def _rope_tables(s):
    t = jnp.arange(s)
    row = (t // GRID_W).astype(F32)
    col = (t % GRID_W).astype(F32)
    axis_dim = HEAD_DIM // 2
    freqs = ROPE_THETA ** (-jnp.arange(0, axis_dim, 2, dtype=F32) / axis_dim)
    ang_row = row[:, None] * freqs[None, :]
    ang_col = col[:, None] * freqs[None, :]
    cos = jnp.concatenate([jnp.cos(ang_row)] * 2 + [jnp.cos(ang_col)] * 2, axis=1)
    sin = jnp.concatenate([-jnp.sin(ang_row), jnp.sin(ang_row), -jnp.sin(ang_col), jnp.sin(ang_col)], axis=1)
    return jnp.tile(cos, (1, CW // HEAD_DIM)), jnp.tile(sin, (1, CW // HEAD_DIM))


def _pack(parts, rows):
    flat = jnp.concatenate([p.reshape(-1).astype(F32) for p in parts])
    return jnp.pad(flat, (0, rows * LANES - flat.shape[0])).reshape(rows, LANES)


def _unpack(buf, shapes):
    flat = buf.reshape(-1)
    out, off = [], 0
    for shp in shapes:
        size = math.prod(shp)
        out.append(flat[off:off + size].reshape(shp))
        off += size
    return out


def kernel(x, norm_mix, w_in, a_sink, t5_table, b_rpb, c_q_gain, c_k_gain, out_gain_a, out_gain_b, out_gain_c, w_o, norm_mlp, w_up, w_down, norm_final, loss_target, m_norm_mix, m_w_in, m_a_sink, m_t5_table, m_b_rpb, m_c_q_gain, m_c_k_gain, m_out_gain_a, m_out_gain_b, m_out_gain_c, m_w_o, m_norm_mlp, m_w_up, m_w_down, m_norm_final, v_norm_mix, v_w_in, v_a_sink, v_t5_table, v_b_rpb, v_c_q_gain, v_c_k_gain, v_out_gain_a, v_out_gain_b, v_out_gain_c, v_w_o, v_norm_mlp, v_w_up, v_w_down, v_norm_final):
    n_layers = w_in.shape[0]
    s, d = x.shape[1], x.shape[2]
    d_ff = 4 * w_up.shape[2]
    in_w = 4 * w_in.shape[2]
    xs = x.reshape(s, d)
    target = loss_target.reshape(s, d)
    cfg_a, cfg_b = _cfg_a(s), _cfg_b(s)

    wg_in, wg_o, wg_up, wg_down = _allgather_weights(
        [w_in.astype(MXU_DT), w_o.astype(MXU_DT), w_up.astype(MXU_DT), w_down.astype(MXU_DT)])
    wf_in = wg_in.transpose(0, 2, 1, 3).reshape(n_layers, d, in_w)
    wf_o = wg_o.reshape(n_layers, d, d)
    wf_down = wg_down.reshape(n_layers, d_ff, d)
    ff_shard = w_up.shape[2]

    ones = _pair_ones()
    cos_t, sin_t = _rope_tables(s)
    a_onehot, a_valid = _a_bias_maps()
    bias_a = jnp.where(a_valid[:, None], jnp.einsum("vqkb,bh->vhqk", a_onehot, t5_table, precision=HIGHEST),
                       MASK_VALUE)
    row_oh, col_oh, b_valid = _b_bias_maps()
    sink_b = jnp.full((4, 1, LANES), MASK_VALUE, F32)

    def b_bias(rpb):
        t = jnp.einsum("hrz,vir->vhiz", rpb, row_oh, precision=HIGHEST)
        t = jnp.einsum("vhiz,qcz->vhqic", t, col_oh, precision=HIGHEST)
        t = jnp.where(b_valid[None, None, :, None, :], t, MASK_VALUE)
        return t.reshape(8, 8, GRID_W, NA_ROWS * GRID_W)

    def tile_gain(gvec):
        return jnp.tile(gvec, CW // HEAD_DIM).reshape(1, CW)

    def pad_sink(svec):
        return jnp.pad(svec, (0, LANES - svec.shape[0])).reshape(1, 1, LANES)

    saved = []
    xc = xs
    for l in range(n_layers):
        h1 = _rms_fwd(xc, norm_mix[l], "rms_mix")
        proj = _matmul(h1, wf_in[l], mode="nn", name="proj_in", tm=1024, tn=768, tk=2048)
        bias_b = b_bias(b_rpb[l])
        oa = _local_attn_fwd(proj, bias_a, pad_sink(a_sink[l]), cfg_a, "attn_a_fwd")
        ob = _local_attn_fwd(proj, bias_b, sink_b, cfg_b, "attn_b_fwd")
        gq, gk = tile_gain(c_q_gain[l]), tile_gain(c_k_gain[l])
        qh, kd, vd = _cprep_fwd(proj, gq, gk, cos_t, sin_t, ones, "cprep_fwd")
        oc, lse = _flash_fwd(qh, kd, vd, "attn_c_fwd")
        mix = _groupnorm_fwd(oa, ob, oc, out_gain_a[l], out_gain_b[l], out_gain_c[l], "groupnorm_fwd")
        x_mid = _matmul(mix, wf_o[l], mode="nn", name="proj_out", tm=1024, tn=1024, tk=2048, epi="res",
                        extra=(xc,))
        h2 = _rms_fwd(x_mid, norm_mlp[l], "rms_mlp")
        nb_up = ff_shard // 1024
        u, uu = _matmul(h2, wg_up[l], mode="nn", name="mlp_up", tm=1024, tn=1024, tk=2048, epi="relu2",
                        out_dtypes=(F32, MXU_DT), mkn=(s, d, d_ff),
                        b_spec=pl.BlockSpec((None, 2048, 1024), lambda i, j, kk: (j // nb_up, kk, j % nb_up)))
        x_out = _matmul(uu, wf_down[l], mode="nn", name="mlp_down", tm=1024, tn=1024, tk=2048, epi="res",
                        extra=(x_mid,))
        saved.append((xc, h1, proj, bias_b, oa, ob, qh, kd, vd, oc, lse, mix, x_mid, h2, u, uu))
        xc = x_out

    loss_part, dx, dxb, dg_final = _final_loss(xc, norm_final, target, "final_loss")

    g_in, g_o, g_up, g_down = [], [], [], []
    small = {k: [] for k in ("norm_mix", "a_sink", "b_rpb", "cq", "ck", "oga", "ogb", "ogc", "norm_mlp")}
    dbias_a_total = jnp.zeros_like(bias_a)
    for l in reversed(range(n_layers)):
        xin, h1, proj, bias_b, oa, ob, qh, kd, vd, oc, lse, mix, x_mid, h2, u, uu = saved[l]
        du = _matmul(dxb, wf_down[l], mode="nt", name="mlp_down_dgrad", tm=1024, tn=1024, tk=2048, epi="mul2u",
                     extra=(u,), out_dtypes=(MXU_DT,))
        g_down.append(_matmul(uu, dxb, mode="tn", name="mlp_down_wgrad", tm=1024, tn=1024, tk=1024)
                      .reshape(4, d_ff // 4, d))
        nbk = ff_shard // 2048
        dh2 = _matmul(du, wg_up[l], mode="nt", name="mlp_up_dgrad", tm=1024, tn=1024, tk=2048,
                      mkn=(s, d_ff, d),
                      b_spec=pl.BlockSpec((None, 1024, 2048), lambda i, j, kk: (kk // nbk, j, kk % nbk)))
        nbo = ff_shard // 1024
        g_up.append(_matmul(h2, du, mode="tn", name="mlp_up_wgrad", tm=1024, tn=1024, tk=1024,
                            out_spec=pl.BlockSpec((None, 1024, 1024), lambda i, j, kk: (j // nbo, i, j % nbo)),
                            out_shape=(4, d, ff_shard)))
        dx_mid, dxmb, dg = _rms_bwd(x_mid, norm_mlp[l], dh2, dx, "rms_mlp_bwd")
        small["norm_mlp"].append(dg)
        dmix = _matmul(dxmb, wf_o[l], mode="nt", name="proj_out_dgrad", tm=1024, tn=1024, tk=2048)
        g_o.append(_matmul(mix, dxmb, mode="tn", name="proj_out_wgrad", tm=1024, tn=1024, tk=1024)
                   .reshape(4, d // 4, d))
        doa, dob, doc, docb, ddc, dga, dgb, dgc = _groupnorm_bwd(
            dmix, oa, ob, oc, out_gain_a[l], out_gain_b[l], out_gain_c[l], ones, "groupnorm_bwd")
        small["oga"].append(dga)
        small["ogb"].append(dgb)
        small["ogc"].append(dgc)
        dqa, dka, dva, dbias_a, dsink = _local_attn_bwd(proj, bias_a, pad_sink(a_sink[l]), doa, cfg_a, "attn_a_bwd")
        dbias_a_total = dbias_a_total + dbias_a
        small["a_sink"].append(dsink[0, 0, :a_sink.shape[1]])
        dqb, dkb, dvb, dbias_b, _ = _local_attn_bwd(proj, bias_b, sink_b, dob, cfg_b, "attn_b_bwd")
        db5 = jnp.where(b_valid[None, None, :, None, :], dbias_b.reshape(8, 8, GRID_W, NA_ROWS, GRID_W), 0.0)
        t = jnp.einsum("vhqic,qcz->vhiz", db5, col_oh, precision=HIGHEST)
        small["b_rpb"].append(jnp.einsum("vhiz,vir->hrz", t, row_oh, precision=HIGHEST))
        dqh, dkd, dvd = _flash_bwd(qh, kd, vd, docb, lse, ddc, "attn_c_bwd")
        gq, gk = tile_gain(c_q_gain[l]), tile_gain(c_k_gain[l])
        dqc, dkc, dvc, dgq, dgk = _cprep_bwd(proj, gq, gk, cos_t, sin_t, ones, dqh, dkd, dvd, "cprep_bwd")
        small["cq"].append(dgq.reshape(CW // HEAD_DIM, HEAD_DIM).sum(0))
        small["ck"].append(dgk.reshape(CW // HEAD_DIM, HEAD_DIM).sum(0))
        dproj = jnp.concatenate([dqa, dka, dva, dqb, dkb, dvb, dqc, dkc, dvc], axis=1).astype(MXU_DT)
        dh1 = _matmul(dproj, wf_in[l], mode="nt", name="proj_in_dgrad", tm=1024, tn=1024, tk=1920)
        gw = _matmul(h1, dproj, mode="tn", name="proj_in_wgrad", tm=1024, tn=768, tk=1024)
        g_in.append(gw.reshape(d, 4, in_w // 4).transpose(1, 0, 2))
        dx, dxb, dg = _rms_bwd(xin, norm_mix[l], dh1, dx_mid, "rms_mix_bwd")
        small["norm_mix"].append(dg)

    for lst in (g_in, g_o, g_up, g_down):
        lst.reverse()
    for lst in small.values():
        lst.reverse()

    x_i, y_i, c_i = _mesh_pos()
    c_idx = c_i.astype(jnp.int32).reshape(1)
    me_idx = (2 * x_i + y_i).astype(jnp.int32).reshape(1)
    grads = g_in + g_o + g_up + g_down
    recv1 = _exchange_core_halves(grads)
    parts = [_add_half(g, r, c_idx, "add_core_halves") for g, r in zip(grads, recv1)]
    recv2 = _exchange_chips(parts)
    halves = [_sum_chips(p, r, me_idx, "sum_chips") for p, r in zip(parts, recv2)]
    gr_in, gr_o, gr_up, gr_down = _share_halves(halves, n_layers)

    big = {}
    for nm, w, g, m, v in (("w_in", w_in, gr_in, m_w_in, v_w_in), ("w_o", w_o, gr_o, m_w_o, v_w_o),
                           ("w_up", w_up, gr_up, m_w_up, v_w_up), ("w_down", w_down, gr_down, m_w_down, v_w_down)):
        shp = w.shape
        flat = (shp[0] * shp[1], shp[2])
        dl, mo, vo = _adamw(w.reshape(flat), g.reshape(flat), m.reshape(flat), v.reshape(flat), "adamw_" + nm)
        big[nm] = (g, dl.reshape(shp), mo.reshape(shp), vo.reshape(shp))

    dt5 = jnp.einsum("vhqk,vqkb->bh", dbias_a_total, a_onehot, precision=HIGHEST)
    small_names = ["norm_mix", "a_sink", "t5_table", "b_rpb", "c_q_gain", "c_k_gain", "out_gain_a", "out_gain_b",
                   "out_gain_c", "norm_mlp", "norm_final"]
    small_w = [norm_mix, a_sink, t5_table, b_rpb, c_q_gain, c_k_gain, out_gain_a, out_gain_b, out_gain_c, norm_mlp,
               norm_final]
    small_m = [m_norm_mix, m_a_sink, m_t5_table, m_b_rpb, m_c_q_gain, m_c_k_gain, m_out_gain_a, m_out_gain_b,
               m_out_gain_c, m_norm_mlp, m_norm_final]
    small_v = [v_norm_mix, v_a_sink, v_t5_table, v_b_rpb, v_c_q_gain, v_c_k_gain, v_out_gain_a, v_out_gain_b,
               v_out_gain_c, v_norm_mlp, v_norm_final]
    small_g = [jnp.stack(small["norm_mix"]), jnp.stack(small["a_sink"]), dt5, jnp.stack(small["b_rpb"]),
               jnp.stack(small["cq"]), jnp.stack(small["ck"]), jnp.stack(small["oga"]), jnp.stack(small["ogb"]),
               jnp.stack(small["ogc"]), jnp.stack(small["norm_mlp"]), dg_final]
    shapes = [w.shape for w in small_w]
    total = sum(math.prod(shp) for shp in shapes) + 1
    rows = -(-total // (8 * LANES)) * 8
    one = [jnp.ones((1,), F32)]
    gs, dl, mo, vo = _allreduce_small_adam(_pack(small_g + [loss_part[0, :1]], rows), _pack(small_w + one, rows),
                                           _pack(small_m + one, rows), _pack(small_v + one, rows))
    sg = _unpack(gs, shapes + [(1,)])
    sd, sm, sv = _unpack(dl, shapes), _unpack(mo, shapes), _unpack(vo, shapes)
    loss = sg[-1].reshape(())

    by_name = {nm: (sg[i], sd[i], sm[i], sv[i]) for i, nm in enumerate(small_names)}
    by_name.update(big)
    order = ["norm_mix", "w_in", "a_sink", "t5_table", "b_rpb", "c_q_gain", "c_k_gain", "out_gain_a", "out_gain_b",
             "out_gain_c", "w_o", "norm_mlp", "w_up", "w_down", "norm_final"]
    outs = [loss, dx.reshape(x.shape)]
    for field in range(4):
        outs.extend(by_name[nm][field] for nm in order)
    return tuple(outs)
```

```python
import functools
import math

import jax
import jax.numpy as jnp
from jax import lax
from jax.experimental import pallas as pl
from jax.experimental.pallas import tpu as pltpu

F32 = jnp.float32
MXU_DT = jnp.bfloat16
HIGHEST = lax.Precision.HIGHEST

HEAD_DIM = 64
LANES = 128
EPS = 1e-6
MASK_VALUE = -1e30
GRID_W = 64
NA_ROWS = 8
T5_BUCKETS = 32
T5_MAX_DIST = 128
ROPE_THETA = 10000.0
ADAM_LR, ADAM_B1, ADAM_B2, ADAM_EPS, ADAM_WD, ADAM_STEP = 0.001, 0.9, 0.999, 1e-08, 0.01, 10
VMEM_LIMIT = 56 * 1024 * 1024

MESH_ID = pl.DeviceIdType.MESH
ANY = pl.BlockSpec(memory_space=pl.ANY)

NT_DIMS = (((1,), (1,)), ((), ()))
TN_DIMS = (((0,), (0,)), ((), ()))
NN_DIMS = (((1,), (0,)), ((), ()))


def _dot(a, b, dims=NN_DIMS):
    return lax.dot_general(a, b, dims, preferred_element_type=F32)


def _call(body, *, name, out_shape, grid=(), in_specs=None, out_specs=None, scratch=(), sem=None,
          prefetch=0, aliases=None):
    params = {"vmem_limit_bytes": VMEM_LIMIT}
    if sem is not None:
        params["dimension_semantics"] = sem
    kwargs = {}
    if aliases:
        kwargs["input_output_aliases"] = aliases
    if prefetch:
        spec = pltpu.PrefetchScalarGridSpec(num_scalar_prefetch=prefetch, grid=grid, in_specs=in_specs,
                                            out_specs=out_specs, scratch_shapes=list(scratch))
        return pl.pallas_call(body, grid_spec=spec, out_shape=out_shape, name=name,
                              compiler_params=pltpu.CompilerParams(**params), **kwargs)
    return pl.pallas_call(body, grid=grid, in_specs=in_specs, out_specs=out_specs, out_shape=out_shape,
                          scratch_shapes=list(scratch), name=name,
                          compiler_params=pltpu.CompilerParams(**params), **kwargs)


def _sds(shape, dtype=F32):
    return jax.ShapeDtypeStruct(tuple(shape), dtype)


def _matmul(a, b, *, mode, name, tm, tn, tk, epi="plain", extra=(), out_dtypes=(F32,), mkn=None,
            b_spec=None, out_spec=None, out_shape=None):
    if mkn is None:
        if mode == "nn":
            (m, k), n = a.shape, b.shape[1]
        elif mode == "nt":
            (m, k), n = a.shape, b.shape[0]
        else:
            (k, m), n = a.shape, b.shape[1]
    else:
        m, k, n = mkn
    tm, tn, tk = min(tm, m), min(tn, n), min(tk, k)
    assert m % tm == 0 and n % tn == 0 and k % tk == 0, (name, m, n, k, tm, tn, tk)
    nk = k // tk
    dims = {"nn": NN_DIMS, "nt": NT_DIMS, "tn": TN_DIMS}[mode]
    n_extra, n_out = len(extra), len(out_dtypes)

    def body(a_ref, b_ref, *rest):
        extra_refs = rest[:n_extra]
        out_refs = rest[n_extra:n_extra + n_out]
        acc_ref = rest[n_extra + n_out]
        kk = pl.program_id(2)

        @pl.when(kk == 0)
        def _():
            acc_ref[...] = jnp.zeros_like(acc_ref)

        acc_ref[...] += _dot(a_ref[...].astype(MXU_DT), b_ref[...].astype(MXU_DT), dims)

        @pl.when(kk == nk - 1)
        def _():
            acc = acc_ref[...]
            if epi == "plain":
                out_refs[0][...] = acc.astype(out_refs[0].dtype)
            elif epi == "res":
                out_refs[0][...] = (extra_refs[0][...] + acc).astype(out_refs[0].dtype)
            elif epi == "relu2":
                u = jnp.maximum(acc, 0.0)
                out_refs[0][...] = u.astype(out_refs[0].dtype)
                out_refs[1][...] = (u * u).astype(out_refs[1].dtype)
            elif epi == "mul2u":
                out_refs[0][...] = (2.0 * extra_refs[0][...] * acc).astype(out_refs[0].dtype)
            else:
                raise ValueError(epi)

    if mode == "tn":
        a_spec = pl.BlockSpec((tk, tm), lambda i, j, kk: (kk, i))
    else:
        a_spec = pl.BlockSpec((tm, tk), lambda i, j, kk: (i, kk))
    if b_spec is None:
        if mode == "nt":
            b_spec = pl.BlockSpec((tn, tk), lambda i, j, kk: (j, kk))
        else:
            b_spec = pl.BlockSpec((tk, tn), lambda i, j, kk: (kk, j))
    mn_spec = pl.BlockSpec((tm, tn), lambda i, j, kk: (i, j))
    if out_spec is None:
        out_spec = mn_spec
    if out_shape is None:
        out_shape = (m, n)
    res = _call(body, name=name, grid=(m // tm, n // tn, nk),
                in_specs=[a_spec, b_spec] + [mn_spec] * n_extra,
                out_specs=[out_spec] * n_out,
                out_shape=[_sds(out_shape, d) for d in out_dtypes],
                scratch=[pltpu.VMEM((tm, tn), F32)],
                sem=("parallel", "parallel", "arbitrary"))(a, b, *extra)
    return res if n_out > 1 else res[0]


def _row_tile(s):
    return min(512, s)


def _rms_fwd(x, g, name):
    s, d = x.shape
    tm = _row_tile(s)

    def body(x_ref, g_ref, h_ref):
        xv = x_ref[...]
        r = lax.rsqrt(jnp.mean(xv * xv, axis=-1, keepdims=True) + EPS)
        h_ref[...] = ((xv * r) * g_ref[...]).astype(h_ref.dtype)

    return _call(body, name=name, grid=(s // tm,),
                 in_specs=[pl.BlockSpec((tm, d), lambda i: (i, 0)), pl.BlockSpec((1, d), lambda i: (0, 0))],
                 out_specs=pl.BlockSpec((tm, d), lambda i: (i, 0)),
                 out_shape=_sds((s, d), MXU_DT), sem=("parallel",))(x, g.reshape(1, d))


def _rms_bwd(x, g, dh, dres, name):
    s, d = x.shape
    tm = _row_tile(s)

    def body(x_ref, g_ref, dh_ref, dres_ref, dx_ref, dxb_ref, dg_ref):
        i = pl.program_id(0)
        xv = x_ref[...]
        r = lax.rsqrt(jnp.mean(xv * xv, axis=-1, keepdims=True) + EPS)
        xh = xv * r
        dhv = dh_ref[...]
        gd = dhv * g_ref[...]
        c = jnp.mean(gd * xh, axis=-1, keepdims=True)
        dx = dres_ref[...] + r * (gd - xh * c)
        dx_ref[...] = dx
        dxb_ref[...] = dx.astype(dxb_ref.dtype)
        part = jnp.sum(dhv * xh, axis=0, keepdims=True)

        @pl.when(i == 0)
        def _():
            dg_ref[...] = part

        @pl.when(i > 0)
        def _():
            dg_ref[...] += part

    row = pl.BlockSpec((tm, d), lambda i: (i, 0))
    vec = pl.BlockSpec((1, d), lambda i: (0, 0))
    return _call(body, name=name, grid=(s // tm,), in_specs=[row, vec, row, row],
                 out_specs=[row, row, vec],
                 out_shape=[_sds((s, d)), _sds((s, d), MXU_DT), _sds((1, d))],
                 sem=("arbitrary",))(x, g.reshape(1, d), dh, dres)


def _final_loss(x, g, target, name):
    s, d = x.shape
    tm = _row_tile(s)

    def body(x_ref, g_ref, t_ref, loss_ref, dx_ref, dxb_ref, dg_ref):
        i = pl.program_id(0)
        xv = x_ref[...]
        gv = g_ref[...]
        r = lax.rsqrt(jnp.mean(xv * xv, axis=-1, keepdims=True) + EPS)
        xh = xv * r
        err = xh * gv - t_ref[...]
        part_loss = 0.5 * jnp.sum(jnp.mean(err * err, axis=-1, keepdims=True), axis=0, keepdims=True)
        dy = err * (1.0 / d)
        gd = dy * gv
        c = jnp.mean(gd * xh, axis=-1, keepdims=True)
        dx = r * (gd - xh * c)
        dx_ref[...] = dx
        dxb_ref[...] = dx.astype(dxb_ref.dtype)
        part_g = jnp.sum(dy * xh, axis=0, keepdims=True)
        part_l = jnp.broadcast_to(part_loss, (1, LANES))

        @pl.when(i == 0)
        def _():
            dg_ref[...] = part_g
            loss_ref[...] = part_l

        @pl.when(i > 0)
        def _():
            dg_ref[...] += part_g
            loss_ref[...] += part_l

    row = pl.BlockSpec((tm, d), lambda i: (i, 0))
    vec = pl.BlockSpec((1, d), lambda i: (0, 0))
    return _call(body, name=name, grid=(s // tm,), in_specs=[row, vec, row],
                 out_specs=[pl.BlockSpec((1, LANES), lambda i: (0, 0)), row, row, vec],
                 out_shape=[_sds((1, LANES)), _sds((s, d)), _sds((s, d), MXU_DT), _sds((1, d))],
                 sem=("arbitrary",))(x, g.reshape(1, d), target)


def _lane_iota(shape):
    return lax.broadcasted_iota(jnp.int32, shape, len(shape) - 1)


def _swap_halves(x):
    return pltpu.roll(x, HEAD_DIM, 1)


def _segsum64(x, ones_ref):
    ones = ones_ref[...]
    outs = []
    for c in range(x.shape[1] // LANES):
        xc = x[:, c * LANES:(c + 1) * LANES]
        hi = xc.astype(MXU_DT)
        r1 = xc - hi.astype(F32)
        mid = r1.astype(MXU_DT)
        lo = (r1 - mid.astype(F32)).astype(MXU_DT)
        outs.append(_dot(hi, ones) + _dot(mid, ones) + _dot(lo, ones))
    return outs[0] if len(outs) == 1 else jnp.concatenate(outs, axis=1)


def _pair_ones():
    i = jnp.arange(LANES)
    return (i[:, None] // HEAD_DIM == i[None, :] // HEAD_DIM).astype(MXU_DT)


def _col(x, lane):
    return jnp.sum(jnp.where(_lane_iota(x.shape) == lane, x, 0.0), axis=-1, keepdims=True)


class _LocalCfg:
    def __init__(self, *, groups, qb, kw, qw, qcol, kcol, vcol, kvhalf, kstart, variant):
        self.groups, self.qb, self.kw, self.qw = groups, qb, kw, qw
        self.qcol, self.kcol, self.vcol = qcol, kcol, vcol
        self.kvhalf = kvhalf
        self.kstart, self.variant = kstart, variant
        self.pairs = qw // LANES


def _cfg_a(s):
    nb = s // 128
    return _LocalCfg(groups=1, qb=128, kw=384, qw=512, qcol=lambda g: 0, kcol=lambda g: 4, vcol=lambda g: 5,
                     kvhalf=lambda t, e: t // 2,
                     kstart=lambda n: 128 * jnp.clip(n - 1, 0, nb - 3),
                     variant=lambda n: jnp.where(n <= 0, 0, jnp.where(n == nb - 1, 2, 1)))


def _cfg_b(s):
    rows = s // GRID_W
    return _LocalCfg(groups=4, qb=64, kw=512, qw=128, qcol=lambda g: 6 + g, kcol=lambda g: 10 + g,
                     vcol=lambda g: 14 + g, kvhalf=lambda t, e: e,
                     kstart=lambda n: GRID_W * jnp.clip(n - NA_ROWS // 2, 0, rows - NA_ROWS),
                     variant=lambda n: jnp.where(n < 4, jnp.maximum(n, 0),
                                                 jnp.where(n > rows - 4, n - (rows - 8), 4)))


def _local_head(cfg, t, e, qp, qp_sw, kb, bias, sink_row, left_q):
    kvh = cfg.kvhalf(t, e)
    qsrc = qp if e == kvh else qp_sw
    keep = left_q if kvh == 0 else jnp.logical_not(left_q)
    qm = jnp.where(keep, qsrc, 0.0).astype(MXU_DT)
    sc = _dot(qm, kb, NT_DIMS) + bias
    snk = _col(sink_row, 2 * t + e)
    m = jnp.maximum(jnp.max(sc, axis=-1, keepdims=True), snk)
    p = jnp.exp(sc - m)
    l = jnp.sum(p, axis=-1, keepdims=True) + jnp.exp(snk - m)
    p = p / l
    return qm, keep, p, m, l, snk


def _local_attn_fwd(proj, bias, sink, cfg, name):
    s = proj.shape[0]
    qb, kw, qw, g_n = cfg.qb, cfg.kw, cfg.qw, cfg.groups
    hq = 2 * cfg.pairs

    def body(q_ref, k_ref, v_ref, b_ref, s_ref, o_ref):
        n = pl.program_id(1)
        ks = pl.multiple_of(cfg.kstart(n), 64)
        kf = k_ref[pl.ds(ks, kw), :]
        vf = v_ref[pl.ds(ks, kw), :]
        kb = kf.astype(MXU_DT)
        vf_sw = _swap_halves(vf)
        left_q = _lane_iota((qb, LANES)) < HEAD_DIM
        left_k = _lane_iota((kw, LANES)) < HEAD_DIM
        sink_row = s_ref[...]
        for t in range(cfg.pairs):
            qp = q_ref[:, t * LANES:(t + 1) * LANES] * 0.125
            qp_sw = _swap_halves(qp)
            acc = jnp.zeros((qb, LANES), F32)
            for e in range(2):
                _, _, p, _, _, _ = _local_head(cfg, t, e, qp, qp_sw, kb, b_ref[0, 2 * t + e], sink_row, left_q)
                vsrc = vf if e == cfg.kvhalf(t, e) else vf_sw
                vsel = jnp.where(left_k if e == 0 else jnp.logical_not(left_k), vsrc, 0.0).astype(MXU_DT)
                acc = acc + _dot(p.astype(MXU_DT), vsel)
            o_ref[:, t * LANES:(t + 1) * LANES] = acc

    return _call(
        body, name=name, grid=(g_n, s // qb),
        in_specs=[pl.BlockSpec((qb, qw), lambda g, n: (n, cfg.qcol(g))),
                  pl.BlockSpec((s, LANES), lambda g, n: (0, cfg.kcol(g))),
                  pl.BlockSpec((s, LANES), lambda g, n: (0, cfg.vcol(g))),
                  pl.BlockSpec((1, hq, qb, kw), lambda g, n: (cfg.variant(n), g, 0, 0)),
                  pl.BlockSpec((None, 1, LANES), lambda g, n: (g, 0, 0))],
        out_specs=pl.BlockSpec((qb, qw), lambda g, n: (n, g)),
        out_shape=_sds((s, g_n * qw)), sem=("parallel", "arbitrary"))(proj, proj, proj, bias, sink)


def _local_attn_bwd(proj, bias, sink, do, cfg, name):
    s = proj.shape[0]
    qb, kw, qw, g_n = cfg.qb, cfg.kw, cfg.qw, cfg.groups
    hq = 2 * cfg.pairs

    def body(q_ref, k_ref, v_ref, b_ref, s_ref, do_ref, dq_ref, dk_ref, dv_ref, db_ref, dsk_ref):
        n = pl.program_id(1)
        ks = pl.multiple_of(cfg.kstart(n), 64)
        first = jnp.logical_or(n == 0, cfg.variant(n) != cfg.variant(n - 1))

        @pl.when(n == 0)
        def _():
            dk_ref[...] = jnp.zeros_like(dk_ref)
            dv_ref[...] = jnp.zeros_like(dv_ref)
            dsk_ref[...] = jnp.zeros_like(dsk_ref)

        @pl.when(first)
        def _():
            db_ref[...] = jnp.zeros_like(db_ref)

        kf = k_ref[pl.ds(ks, kw), :]
        vf = v_ref[pl.ds(ks, kw), :]
        kb = kf.astype(MXU_DT)
        vb = vf.astype(MXU_DT)
        kf_sw = _swap_halves(kf)
        left_q = _lane_iota((qb, LANES)) < HEAD_DIM
        left_k = _lane_iota((kw, LANES)) < HEAD_DIM
        sink_row = s_ref[...]
        row0 = lax.broadcasted_iota(jnp.int32, (8, LANES), 0) == 0
        lane8 = _lane_iota((8, LANES))
        dk_acc = jnp.zeros((kw, LANES), F32)
        dv_acc = jnp.zeros((kw, LANES), F32)
        dsk_acc = jnp.zeros((8, LANES), F32)
        for t in range(cfg.pairs):
            qp = q_ref[:, t * LANES:(t + 1) * LANES] * 0.125
            qp_sw = _swap_halves(qp)
            dop = do_ref[:, t * LANES:(t + 1) * LANES]
            dop_sw = _swap_halves(dop)
            dq_t = jnp.zeros((qb, LANES), F32)
            for e in range(2):
                h = 2 * t + e
                qm, keep, p, m, l, snk = _local_head(cfg, t, e, qp, qp_sw, kb, b_ref[0, h], sink_row, left_q)
                kvh = cfg.kvhalf(t, e)
                dom = jnp.where(keep, dop if e == kvh else dop_sw, 0.0).astype(MXU_DT)
                dp = _dot(dom, vb, NT_DIMS)
                dd = jnp.sum(p * dp, axis=-1, keepdims=True)
                ds = p * (dp - dd)
                p_sink = jnp.exp(snk - m) / l
                dsink = jnp.sum(-p_sink * dd, axis=0, keepdims=True)
                dsk_acc = dsk_acc + jnp.where(jnp.logical_and(row0, lane8 == h), dsink, 0.0)
                dsb = ds.astype(MXU_DT)
                dv_acc = dv_acc + _dot(p.astype(MXU_DT), dom, TN_DIMS)
                dk_acc = dk_acc + _dot(dsb, qm, TN_DIMS)
                ksrc = kf if e == kvh else kf_sw
                ksel = jnp.where(left_k if e == 0 else jnp.logical_not(left_k), ksrc, 0.0).astype(MXU_DT)
                dq_t = dq_t + _dot(dsb, ksel)
                db_ref[0, h] += ds
            dq_ref[:, t * LANES:(t + 1) * LANES] = dq_t * 0.125
        dk_ref[pl.ds(ks, kw), :] += dk_acc
        dv_ref[pl.ds(ks, kw), :] += dv_acc
        dsk_ref[...] += dsk_acc

    n_var = bias.shape[0]
    return _call(
        body, name=name, grid=(g_n, s // qb),
        in_specs=[pl.BlockSpec((qb, qw), lambda g, n: (n, cfg.qcol(g))),
                  pl.BlockSpec((s, LANES), lambda g, n: (0, cfg.kcol(g))),
                  pl.BlockSpec((s, LANES), lambda g, n: (0, cfg.vcol(g))),
                  pl.BlockSpec((1, hq, qb, kw), lambda g, n: (cfg.variant(n), g, 0, 0)),
                  pl.BlockSpec((None, 1, LANES), lambda g, n: (g, 0, 0)),
                  pl.BlockSpec((qb, qw), lambda g, n: (n, g))],
        out_specs=[pl.BlockSpec((qb, qw), lambda g, n: (n, g)),
                   pl.BlockSpec((s, LANES), lambda g, n: (0, g)),
                   pl.BlockSpec((s, LANES), lambda g, n: (0, g)),
                   pl.BlockSpec((1, hq, qb, kw), lambda g, n: (cfg.variant(n), g, 0, 0)),
                   pl.BlockSpec((None, 8, LANES), lambda g, n: (g, 0, 0))],
        out_shape=[_sds((s, g_n * qw)), _sds((s, g_n * LANES)), _sds((s, g_n * LANES)),
                   _sds((n_var, g_n * hq, qb, kw)), _sds((g_n, 8, LANES))],
        sem=("parallel", "arbitrary"))(proj, proj, proj, bias, sink, do)


QC_COL, KC_COL, VC_COL = 9, 13, 14
CW = 256


def _swap16(x):
    w = x.shape[1]
    lane = _lane_iota(x.shape)
    return jnp.where(lane % 32 < 16, pltpu.roll(x, w - 16, 1), pltpu.roll(x, 16, 1))


def _dup_halves(x):
    left = _lane_iota(x.shape) < HEAD_DIM
    sw = _swap_halves(x)
    return jnp.where(left, x, sw), jnp.where(left, sw, x)


def _normrope(x, gain, cos, sin, ones_ref):
    ms = _segsum64(x * x, ones_ref) * (1.0 / HEAD_DIM)
    r = lax.rsqrt(ms + EPS)
    y = (x * r) * gain
    return y * cos + _swap16(y) * sin, r


def _cprep_fwd(proj, gq, gk, cos, sin, ones, name):
    s = proj.shape[0]
    tm = _row_tile(s)

    def body(q0, q1, q2, q3, k_ref, v_ref, gq_ref, gk_ref, cos_ref, sin_ref, ones_ref, qh_ref, kd_ref, vd_ref):
        cos_v, sin_v = cos_ref[...], sin_ref[...]
        for c, q_ref in enumerate((q0, q1, q2, q3)):
            y, _ = _normrope(q_ref[...], gq_ref[...], cos_v, sin_v, ones_ref)
            qh_ref[:, c * CW:(c + 1) * CW] = (y * 0.125).astype(qh_ref.dtype)
        yk, _ = _normrope(k_ref[...], gk_ref[...], cos_v, sin_v, ones_ref)
        vv = v_ref[...]
        for p in range(2):
            ka, kb_ = _dup_halves(yk[:, p * LANES:(p + 1) * LANES])
            va, vb_ = _dup_halves(vv[:, p * LANES:(p + 1) * LANES])
            kd_ref[:, (2 * p) * LANES:(2 * p + 1) * LANES] = ka.astype(kd_ref.dtype)
            kd_ref[:, (2 * p + 1) * LANES:(2 * p + 2) * LANES] = kb_.astype(kd_ref.dtype)
            vd_ref[:, (2 * p) * LANES:(2 * p + 1) * LANES] = va.astype(vd_ref.dtype)
            vd_ref[:, (2 * p + 1) * LANES:(2 * p + 2) * LANES] = vb_.astype(vd_ref.dtype)

    def chunk(col):
        return pl.BlockSpec((tm, CW), lambda i: (i, col))

    vec = pl.BlockSpec((1, CW), lambda i: (0, 0))
    tab = pl.BlockSpec((tm, CW), lambda i: (i, 0))
    return _call(body, name=name, grid=(s // tm,),
                 in_specs=[chunk(QC_COL), chunk(QC_COL + 1), chunk(QC_COL + 2), chunk(QC_COL + 3),
                           chunk(KC_COL), chunk(VC_COL), vec, vec, tab, tab,
                           pl.BlockSpec((LANES, LANES), lambda i: (0, 0))],
                 out_specs=[pl.BlockSpec((tm, 4 * CW), lambda i: (i, 0)),
                            pl.BlockSpec((tm, 2 * CW), lambda i: (i, 0)),
                            pl.BlockSpec((tm, 2 * CW), lambda i: (i, 0))],
                 out_shape=[_sds((s, 4 * CW), MXU_DT), _sds((s, 2 * CW), MXU_DT), _sds((s, 2 * CW), MXU_DT)],
                 sem=("parallel",))(proj, proj, proj, proj, proj, proj, gq, gk, cos, sin, ones)


def _cprep_bwd(proj, gq, gk, cos, sin, ones, dqh, dkd, dvd, name):
    s = proj.shape[0]
    tm = _row_tile(s)

    def fold(ref, p):
        a = ref[:, (2 * p) * LANES:(2 * p + 1) * LANES]
        b = ref[:, (2 * p + 1) * LANES:(2 * p + 2) * LANES]
        ta = a + _swap_halves(a)
        tb = b + _swap_halves(b)
        return jnp.where(_lane_iota(a.shape) < HEAD_DIM, ta, tb)

    def norm_bwd(x, gain, dyr, cos_v, sin_v, ones_ref):
        dy = dyr * cos_v + _swap16(dyr * sin_v)
        ms = _segsum64(x * x, ones_ref) * (1.0 / HEAD_DIM)
        r = lax.rsqrt(ms + EPS)
        xh = x * r
        gd = dy * gain
        c = _segsum64(gd * xh, ones_ref) * (1.0 / HEAD_DIM)
        return r * (gd - xh * c), jnp.sum(dy * xh, axis=0, keepdims=True)

    def body(q0, q1, q2, q3, k_ref, gq_ref, gk_ref, cos_ref, sin_ref, ones_ref, dqh_ref, dkd_ref, dvd_ref,
             dq_ref, dk_ref, dv_ref, dgq_ref, dgk_ref):
        i = pl.program_id(0)
        cos_v, sin_v = cos_ref[...], sin_ref[...]
        gq_part = jnp.zeros((1, CW), F32)
        for c, q_ref in enumerate((q0, q1, q2, q3)):
            dx, dg = norm_bwd(q_ref[...], gq_ref[...], dqh_ref[:, c * CW:(c + 1) * CW] * 0.125, cos_v, sin_v,
                              ones_ref)
            dq_ref[:, c * CW:(c + 1) * CW] = dx
            gq_part = gq_part + dg
        dkr = jnp.concatenate([fold(dkd_ref, 0), fold(dkd_ref, 1)], axis=1)
        dxk, gk_part = norm_bwd(k_ref[...], gk_ref[...], dkr, cos_v, sin_v, ones_ref)
        dk_ref[...] = dxk
        dv_ref[...] = jnp.concatenate([fold(dvd_ref, 0), fold(dvd_ref, 1)], axis=1)

        @pl.when(i == 0)
        def _():
            dgq_ref[...] = gq_part
            dgk_ref[...] = gk_part

        @pl.when(i > 0)
        def _():
            dgq_ref[...] += gq_part
            dgk_ref[...] += gk_part

    def chunk(col):
        return pl.BlockSpec((tm, CW), lambda i: (i, col))

    vec = pl.BlockSpec((1, CW), lambda i: (0, 0))
    tab = pl.BlockSpec((tm, CW), lambda i: (i, 0))
    return _call(body, name=name, grid=(s // tm,),
                 in_specs=[chunk(QC_COL), chunk(QC_COL + 1), chunk(QC_COL + 2), chunk(QC_COL + 3), chunk(KC_COL),
                           vec, vec, tab, tab, pl.BlockSpec((LANES, LANES), lambda i: (0, 0)),
                           pl.BlockSpec((tm, 4 * CW), lambda i: (i, 0)),
                           pl.BlockSpec((tm, 2 * CW), lambda i: (i, 0)),
                           pl.BlockSpec((tm, 2 * CW), lambda i: (i, 0))],
                 out_specs=[pl.BlockSpec((tm, 4 * CW), lambda i: (i, 0)), tab, tab, vec, vec],
                 out_shape=[_sds((s, 4 * CW)), _sds((s, CW)), _sds((s, CW)), _sds((1, CW)), _sds((1, CW))],
                 sem=("arbitrary",))(proj, proj, proj, proj, proj, gq, gk, cos, sin, ones, dqh, dkd, dvd)


def _flash_tiles(s):
    return min(512, s), min(512, s)


def _row_iota(shape):
    return lax.broadcasted_iota(jnp.int32, shape, 0)


def _flash_fwd(qh, kd, vdt, name):
    s = qh.shape[0]
    tq, tk = _flash_tiles(s)
    nk = s // tk

    def body(q_ref, k_ref, vt_ref, ot_ref, lse_ref, qm_ref, m_ref, l_ref, acc_ref):
        j = pl.program_id(2)
        top = _row_iota((LANES, tq)) < HEAD_DIM

        @pl.when(j == 0)
        def _():
            m_ref[...] = jnp.full(m_ref.shape, MASK_VALUE, F32)
            l_ref[...] = jnp.zeros_like(l_ref)
            acc_ref[...] = jnp.zeros_like(acc_ref)
            left_q = _lane_iota((tq, LANES)) < HEAD_DIM
            for t in range(2):
                qp = q_ref[:, t * LANES:(t + 1) * LANES]
                qm_ref[2 * t] = jnp.where(left_q, qp, jnp.zeros_like(qp))
                qm_ref[2 * t + 1] = jnp.where(left_q, jnp.zeros_like(qp), qp)

        kb = k_ref[...]
        vt = vt_ref[...]
        for t in range(2):
            pv, alpha = [], []
            for e in range(2):
                h = 2 * t + e
                st = _dot(kb, qm_ref[h], NT_DIMS)
                m_prev = m_ref[h]
                m_new = jnp.maximum(m_prev, jnp.max(st, axis=0, keepdims=True))
                a = jnp.exp(m_prev - m_new)
                pt = jnp.exp(st - m_new)
                l_ref[h] = a * l_ref[h] + jnp.sum(pt, axis=0, keepdims=True)
                m_ref[h] = m_new
                pv.append(_dot(vt, pt.astype(MXU_DT)))
                alpha.append(a)
            acc_ref[t] = acc_ref[t] * jnp.where(top, alpha[0], alpha[1]) + jnp.where(top, pv[0], pv[1])

        @pl.when(j == nk - 1)
        def _():
            for t in range(2):
                l0, l1 = l_ref[2 * t], l_ref[2 * t + 1]
                ot_ref[t * LANES:(t + 1) * LANES, :] = acc_ref[t] / jnp.where(top, l0, l1)
                lse_ref[2 * t:2 * t + 1, :] = m_ref[2 * t] + jnp.log(l0)
                lse_ref[2 * t + 1:2 * t + 2, :] = m_ref[2 * t + 1] + jnp.log(l1)

    return _call(body, name=name, grid=(4, s // tq, nk),
                 in_specs=[pl.BlockSpec((tq, CW), lambda g, i, j: (i, g)),
                           pl.BlockSpec((tk, LANES), lambda g, i, j: (j, g)),
                           pl.BlockSpec((LANES, tk), lambda g, i, j: (g, j))],
                 out_specs=[pl.BlockSpec((CW, tq), lambda g, i, j: (g, i)),
                            pl.BlockSpec((None, 4, tq), lambda g, i, j: (g, 0, i))],
                 out_shape=[_sds((4 * CW, s)), _sds((4, 4, s))],
                 scratch=[pltpu.VMEM((4, tq, LANES), MXU_DT), pltpu.VMEM((4, 1, tq), F32),
                          pltpu.VMEM((4, 1, tq), F32), pltpu.VMEM((2, LANES, tq), F32)],
                 sem=("parallel", "parallel", "arbitrary"))(qh, kd, vdt)


def _flash_bwd(qh, kd, vd, kdt, do, lse, dd, name):
    s = qh.shape[0]
    tq, tk = _flash_tiles(s)
    ni = s // tq

    def body(q_ref, k_ref, v_ref, kt_ref, do_ref, lse_ref, dd_ref, dqt_ref, dk_ref, dv_ref, dk_acc, dv_acc):
        j = pl.program_id(1)
        i = pl.program_id(2)

        @pl.when(i == 0)
        def _():
            dk_acc[...] = jnp.zeros_like(dk_acc)
            dv_acc[...] = jnp.zeros_like(dv_acc)

        kb = k_ref[...]
        vb = v_ref[...]
        kt = kt_ref[...]
        left_q = _lane_iota((tq, LANES)) < HEAD_DIM
        top = _row_iota((LANES, tq)) < HEAD_DIM
        cols = pl.ds(pl.multiple_of(i * tq, tq), tq)
        for t in range(2):
            qp = q_ref[:, t * LANES:(t + 1) * LANES]
            dop = do_ref[:, t * LANES:(t + 1) * LANES]
            dqt = []
            for e in range(2):
                h = 2 * t + e
                keep_q = left_q if e == 0 else jnp.logical_not(left_q)
                qm = jnp.where(keep_q, qp, jnp.zeros_like(qp))
                dom = jnp.where(keep_q, dop, jnp.zeros_like(dop))
                pt = jnp.exp(_dot(kb, qm, NT_DIMS) - lse_ref[h:h + 1, :])
                dpt = _dot(vb, dom, NT_DIMS)
                dsb = (pt * (dpt - dd_ref[h:h + 1, :])).astype(MXU_DT)
                dv_acc[...] += _dot(pt.astype(MXU_DT), dom)
                dk_acc[...] += _dot(dsb, qm)
                dqt.append(_dot(kt, dsb))
            dq_t = jnp.where(top, dqt[0], dqt[1])

            @pl.when(j == 0)
            def _():
                dqt_ref[t * LANES:(t + 1) * LANES, cols] = dq_t

            @pl.when(j > 0)
            def _():
                dqt_ref[t * LANES:(t + 1) * LANES, cols] += dq_t

        @pl.when(i == ni - 1)
        def _():
            dk_ref[...] = dk_acc[...]
            dv_ref[...] = dv_acc[...]

    qspec = pl.BlockSpec((tq, CW), lambda g, j, i: (i, g))
    kspec = pl.BlockSpec((tk, LANES), lambda g, j, i: (j, g))
    rowspec = pl.BlockSpec((None, 4, tq), lambda g, j, i: (g, 0, i))
    return _call(body, name=name, grid=(4, s // tk, ni),
                 in_specs=[qspec, kspec, kspec, pl.BlockSpec((LANES, tk), lambda g, j, i: (g, j)), qspec,
                           rowspec, rowspec],
                 out_specs=[pl.BlockSpec((CW, s), lambda g, j, i: (g, 0)), kspec, kspec],
                 out_shape=[_sds((4 * CW, s)), _sds((s, 2 * CW)), _sds((s, 2 * CW))],
                 scratch=[pltpu.VMEM((tk, LANES), F32), pltpu.VMEM((tk, LANES), F32)],
                 sem=("parallel", "arbitrary", "arbitrary"))(qh, kd, vd, kdt, do, lse, dd)


def _groupnorm_fwd(oa, ob, oc, ga, gb, gc, name):
    s = oa.shape[0]
    tm = _row_tile(s)
    wa, wb, wc = oa.shape[1], ob.shape[1], oc.shape[1]

    def body(oa_ref, ob_ref, oc_ref, ga_ref, gb_ref, gc_ref, mix_ref):
        off = 0
        for o_ref, g_ref, w in ((oa_ref, ga_ref, wa), (ob_ref, gb_ref, wb), (oc_ref, gc_ref, wc)):
            xv = o_ref[...]
            r = lax.rsqrt(jnp.mean(xv * xv, axis=-1, keepdims=True) + EPS)
            mix_ref[:, off:off + w] = ((xv * r) * g_ref[...]).astype(mix_ref.dtype)
            off += w

    def row(w):
        return pl.BlockSpec((tm, w), lambda i: (i, 0))

    def vec(w):
        return pl.BlockSpec((1, w), lambda i: (0, 0))

    return _call(body, name=name, grid=(s // tm,),
                 in_specs=[row(wa), row(wb), row(wc), vec(wa), vec(wb), vec(wc)],
                 out_specs=row(wa + wb + wc), out_shape=_sds((s, wa + wb + wc), MXU_DT),
                 sem=("parallel",))(oa, ob, oc, ga.reshape(1, wa), gb.reshape(1, wb), gc.reshape(1, wc))


def _groupnorm_bwd(dmix, oa, ob, oc, ga, gb, gc, ones, name):
    s = oa.shape[0]
    tm = _row_tile(s)
    wa, wb, wc = oa.shape[1], ob.shape[1], oc.shape[1]

    def body(dm_ref, oa_ref, ob_ref, oc_ref, ga_ref, gb_ref, gc_ref, ones_ref,
             doa_ref, dob_ref, doc_ref, docb_ref, dd_ref, dga_ref, dgb_ref, dgc_ref):
        i = pl.program_id(0)
        off = 0
        parts = []
        for o_ref, g_ref, do_ref, w in ((oa_ref, ga_ref, doa_ref, wa), (ob_ref, gb_ref, dob_ref, wb),
                                        (oc_ref, gc_ref, doc_ref, wc)):
            xv = o_ref[...]
            dh = dm_ref[:, off:off + w]
            r = lax.rsqrt(jnp.mean(xv * xv, axis=-1, keepdims=True) + EPS)
            xh = xv * r
            gd = dh * g_ref[...]
            c = jnp.mean(gd * xh, axis=-1, keepdims=True)
            dx = r * (gd - xh * c)
            do_ref[...] = dx
            parts.append(jnp.sum(dh * xh, axis=0, keepdims=True))
            if o_ref is oc_ref:
                docb_ref[...] = dx.astype(docb_ref.dtype)
                dd_ref[...] = _segsum64(dx * xv, ones_ref)
            off += w

        @pl.when(i == 0)
        def _():
            dga_ref[...], dgb_ref[...], dgc_ref[...] = parts

        @pl.when(i > 0)
        def _():
            dga_ref[...] += parts[0]
            dgb_ref[...] += parts[1]
            dgc_ref[...] += parts[2]

    def row(w):
        return pl.BlockSpec((tm, w), lambda i: (i, 0))

    def vec(w):
        return pl.BlockSpec((1, w), lambda i: (0, 0))

    return _call(body, name=name, grid=(s // tm,),
                 in_specs=[row(wa + wb + wc), row(wa), row(wb), row(wc), vec(wa), vec(wb), vec(wc),
                           pl.BlockSpec((LANES, LANES), lambda i: (0, 0))],
                 out_specs=[row(wa), row(wb), row(wc), row(wc), row(wc), vec(wa), vec(wb), vec(wc)],
                 out_shape=[_sds((s, wa)), _sds((s, wb)), _sds((s, wc)), _sds((s, wc), MXU_DT), _sds((s, wc)),
                            _sds((1, wa)), _sds((1, wb)), _sds((1, wc))],
                 sem=("arbitrary",))(dmix, oa, ob, oc, ga.reshape(1, wa), gb.reshape(1, wb), gc.reshape(1, wc), ones)


def _adam_math(w, g, m, v):
    m = ADAM_B1 * m + (1.0 - ADAM_B1) * g
    v = ADAM_B2 * v + (1.0 - ADAM_B2) * jnp.square(g)
    m_hat = m / (1.0 - ADAM_B1 ** ADAM_STEP)
    v_hat = v / (1.0 - ADAM_B2 ** ADAM_STEP)
    delta = -ADAM_LR * (m_hat / (jnp.sqrt(v_hat) + ADAM_EPS) + ADAM_WD * w)
    return delta, m, v


def _adamw(w, g, m, v, name):
    rows, cols = w.shape
    tr = min(256, rows)

    def body(w_ref, g_ref, m_ref, v_ref, d_ref, mo_ref, vo_ref):
        d_ref[...], mo_ref[...], vo_ref[...] = _adam_math(w_ref[...], g_ref[...], m_ref[...], v_ref[...])

    spec = pl.BlockSpec((tr, cols), lambda i: (i, 0))
    return _call(body, name=name, grid=(rows // tr,), in_specs=[spec] * 4, out_specs=[spec] * 3,
                 out_shape=[_sds((rows, cols))] * 3, sem=("parallel",))(w, g, m, v)


def _mesh_pos():
    return lax.axis_index("x"), lax.axis_index("y"), lax.axis_index("c")


def _peer_chips(x, y):
    return [(1 - x, y), (x, 1 - y), (1 - x, 1 - y)]


def _allgather_weights(shards):
    n = len(shards)

    def body(*refs):
        sh, out = refs[:n], refs[n:2 * n]
        send_sems, recv_sems, local_sems = refs[2 * n:]
        x, y, c = _mesh_pos()
        me = 2 * x + y
        sibling = (x, y, 1 - c)
        chips = _peer_chips(x, y)

        def half(t, chip, hc):
            h = sh[t].shape[1] // 2
            return out[t].at[:, chip, pl.ds(hc * h, h), :]

        def rcopy(t, k, src, dst, to):
            return pltpu.make_async_remote_copy(src_ref=src, dst_ref=dst, send_sem=send_sems.at[6 * t + k],
                                                recv_sem=recv_sems.at[6 * t + k], device_id=to,
                                                device_id_type=MESH_ID)

        local, first, passed = [], [], []
        for t in range(n):
            cp = pltpu.make_async_copy(sh[t], out[t].at[:, me], local_sems.at[t])
            cp.start()
            local.append(cp)
            h = sh[t].shape[1] // 2
            for k, (px, py) in enumerate(chips):
                cp = rcopy(t, k, sh[t].at[:, pl.ds(c * h, h), :], half(t, me, c), (px, py, c))
                cp.start()
                first.append(cp)
        for t in range(n):
            for k, (px, py) in enumerate(chips):
                blk = half(t, 2 * px + py, c)
                rcopy(t, k, blk, blk, sibling).wait_recv()
                cp = rcopy(t, 3 + k, blk, blk, sibling)
                cp.start()
                passed.append(cp)
        for t in range(n):
            for k, (px, py) in enumerate(chips):
                blk = half(t, 2 * px + py, 1 - c)
                rcopy(t, 3 + k, blk, blk, sibling).wait_recv()
        for cp in first + passed:
            cp.wait_send()
        for cp in local:
            cp.wait()

    return _call(body, name="allgather_weights", in_specs=[ANY] * n, out_specs=[ANY] * n,
                 out_shape=[_sds((w.shape[0], 4) + w.shape[1:], w.dtype) for w in shards],
                 scratch=[pltpu.SemaphoreType.DMA((6 * n,)), pltpu.SemaphoreType.DMA((6 * n,)),
                          pltpu.SemaphoreType.DMA((n,))])(*shards)


def _exchange_core_halves(grads):
    n = len(grads)

    def body(*refs):
        src, dst = refs[:n], refs[n:2 * n]
        send_sems, recv_sems = refs[2 * n:]
        x, y, c = _mesh_pos()
        copies = []
        for t in range(n):
            h = src[t].shape[1] // 2
            cp = pltpu.make_async_remote_copy(src_ref=src[t].at[:, pl.ds((1 - c) * h, h), :], dst_ref=dst[t],
                                              send_sem=send_sems.at[t], recv_sem=recv_sems.at[t],
                                              device_id=(x, y, 1 - c), device_id_type=MESH_ID)
            cp.start()
            copies.append(cp)
        for cp in copies:
            cp.wait()

    return _call(body, name="exchange_core_halves", in_specs=[ANY] * n, out_specs=[ANY] * n,
                 out_shape=[_sds((4, g.shape[1] // 2, g.shape[2])) for g in grads],
                 scratch=[pltpu.SemaphoreType.DMA((n,)), pltpu.SemaphoreType.DMA((n,))])(*grads)


def _add_half(g, r, c_idx, name):
    _, rows, cols = g.shape
    h = rows // 2
    tr = min(256, h)
    nb = h // tr

    def body(c_ref, g_ref, r_ref, o_ref):
        o_ref[...] = g_ref[...] + r_ref[...]

    return _call(body, name=name, grid=(4, nb), prefetch=1,
                 in_specs=[pl.BlockSpec((None, tr, cols), lambda s, i, c: (s, c[0] * nb + i, 0)),
                           pl.BlockSpec((None, tr, cols), lambda s, i, c: (s, i, 0))],
                 out_specs=pl.BlockSpec((None, tr, cols), lambda s, i, c: (s, i, 0)),
                 out_shape=_sds((4, h, cols)), sem=("parallel", "parallel"))(c_idx, g, r)


def _exchange_chips(parts):
    n = len(parts)

    def body(*refs):
        src, dst = refs[:n], refs[n:2 * n]
        send_sems, recv_sems = refs[2 * n:]
        x, y, c = _mesh_pos()
        copies = []
        for t in range(n):
            for k, (px, py) in enumerate(_peer_chips(x, y)):
                cp = pltpu.make_async_remote_copy(src_ref=src[t].at[2 * px + py], dst_ref=dst[t].at[k],
                                                  send_sem=send_sems.at[3 * t + k], recv_sem=recv_sems.at[3 * t + k],
                                                  device_id=(px, py, c), device_id_type=MESH_ID)
                cp.start()
                copies.append(cp)
        for cp in copies:
            cp.wait()

    return _call(body, name="exchange_chips", in_specs=[ANY] * n, out_specs=[ANY] * n,
                 out_shape=[_sds((3,) + p.shape[1:]) for p in parts],
                 scratch=[pltpu.SemaphoreType.DMA((3 * n,)), pltpu.SemaphoreType.DMA((3 * n,))])(*parts)


def _sum_chips(part, recv, me_idx, name):
    _, h, cols = part.shape
    tr = min(256, h)

    def body(me_ref, p_ref, r0_ref, r1_ref, r2_ref, o_ref):
        o_ref[...] = ((p_ref[...] + r0_ref[...]) + r1_ref[...]) + r2_ref[...]

    def slot(k):
        return pl.BlockSpec((None, tr, cols), lambda i, me: (k, i, 0))

    return _call(body, name=name, grid=(h // tr,), prefetch=1,
                 in_specs=[pl.BlockSpec((None, tr, cols), lambda i, me: (me[0], i, 0)), slot(0), slot(1), slot(2)],
                 out_specs=pl.BlockSpec((tr, cols), lambda i, me: (i, 0)),
                 out_shape=_sds((h, cols)), sem=("parallel",))(me_idx, part, recv, recv, recv)


def _share_halves(halves, n_layers):
    n = len(halves)
    n_types = n // n_layers

    def body(*refs):
        src, out = refs[:n], refs[n:n + n_types]
        send_sems, recv_sems, local_sems = refs[n + n_types:]
        x, y, c = _mesh_pos()
        copies = []
        for t in range(n):
            ty, layer = divmod(t, n_layers)
            h = src[t].shape[0]
            dst = out[ty].at[layer, pl.ds(c * h, h), :]
            lc = pltpu.make_async_copy(src[t], dst, local_sems.at[t])
            lc.start()
            rc = pltpu.make_async_remote_copy(src_ref=src[t], dst_ref=dst, send_sem=send_sems.at[t],
                                              recv_sem=recv_sems.at[t], device_id=(x, y, 1 - c),
                                              device_id_type=MESH_ID)
            rc.start()
            copies.append((lc, rc))
        for lc, rc in copies:
            lc.wait()
            rc.wait()

    out_shape = [_sds((n_layers, 2 * halves[ty * n_layers].shape[0], halves[ty * n_layers].shape[1]))
                 for ty in range(n_types)]
    return _call(body, name="share_halves", in_specs=[ANY] * n, out_specs=[ANY] * n_types, out_shape=out_shape,
                 scratch=[pltpu.SemaphoreType.DMA((n,)), pltpu.SemaphoreType.DMA((n,)),
                          pltpu.SemaphoreType.DMA((n,))])(*halves)


def _allreduce_small_adam(g, w, m, v):
    rows = g.shape[0]

    def body(g_ref, w_ref, m_ref, v_ref, gs_ref, d_ref, mo_ref, vo_ref, buf, send_sems, recv_sems):
        x, y, c = _mesh_pos()
        me = 4 * x + 2 * y + c
        buf[me] = g_ref[...]
        copies = []
        for k in range(1, 8):
            px = 1 - x if (k >> 2) & 1 else x
            py = 1 - y if (k >> 1) & 1 else y
            pc = 1 - c if k & 1 else c
            cp = pltpu.make_async_remote_copy(src_ref=g_ref, dst_ref=buf.at[me], send_sem=send_sems.at[k - 1],
                                              recv_sem=recv_sems.at[k - 1], device_id=(px, py, pc),
                                              device_id_type=MESH_ID)
            cp.start()
            copies.append(cp)
        for cp in copies:
            cp.wait()
        total = buf[0]
        for d in range(1, 8):
            total = total + buf[d]
        gs_ref[...] = total
        d_ref[...], mo_ref[...], vo_ref[...] = _adam_math(w_ref[...], total, m_ref[...], v_ref[...])

    vm = pl.BlockSpec(memory_space=pltpu.VMEM)
    return _call(body, name="allreduce_small_adam", in_specs=[vm] * 4, out_specs=[vm] * 4,
                 out_shape=[_sds((rows, LANES))] * 4,
                 scratch=[pltpu.VMEM((8, rows, LANES), F32), pltpu.SemaphoreType.DMA((7,)),
                          pltpu.SemaphoreType.DMA((7,))])(g, w, m, v)


def _t5_bucket(rel):
    nb = T5_BUCKETS // 2
    max_exact = nb // 2
    base = jnp.where(rel > 0, nb, 0)
    n = jnp.abs(rel)
    nf = jnp.maximum(n, 1).astype(F32)
    large = max_exact + (jnp.log(nf / max_exact) / math.log(T5_MAX_DIST / max_exact)
                         * (nb - max_exact)).astype(jnp.int32)
    large = jnp.minimum(large, nb - 1)
    return base + jnp.where(n < max_exact, n, large)


def _a_bias_maps():
    v = jnp.arange(3)[:, None, None]
    q = jnp.arange(128)[None, :, None]
    k = jnp.arange(384)[None, None, :]
    rel = k - 128 * v - q
    valid = jnp.abs(rel) <= 128
    onehot = (_t5_bucket(rel)[..., None] == jnp.arange(T5_BUCKETS)).astype(F32)
    return onehot * valid[..., None].astype(F32), valid


def _b_bias_maps():
    v = jnp.arange(8)[:, None]
    i = jnp.arange(NA_ROWS)[None, :]
    dr = jnp.where(v == 4, i + 3, i - v + 7)
    row_oh = (dr[..., None] == jnp.arange(2 * NA_ROWS - 1)).astype(F32)
    q = jnp.arange(GRID_W)[:, None]
    kc = jnp.arange(GRID_W)[None, :]
    cs = jnp.clip(q - 8, 0, GRID_W - 16)
    valid = (kc >= cs) & (kc < cs + 16)
    col_oh = ((kc - q + 15)[..., None] == jnp.arange(31)).astype(F32) * valid[..., None].astype(F32)
    return row_oh, col_oh, valid


def _rope_tables(s):
    t = jnp.arange(s)
    row = (t // GRID_W).astype(F32)
    col = (t % GRID_W).astype(F32)
    axis_dim = HEAD_DIM // 2
    freqs = ROPE_THETA ** (-jnp.arange(0, axis_dim, 2, dtype=F32) / axis_dim)
    ang_row = row[:, None] * freqs[None, :]
    ang_col = col[:, None] * freqs[None, :]
    cos = jnp.concatenate([jnp.cos(ang_row)] * 2 + [jnp.cos(ang_col)] * 2, axis=1)
    sin = jnp.concatenate([-jnp.sin(ang_row), jnp.sin(ang_row), -jnp.sin(ang_col), jnp.sin(ang_col)], axis=1)
    return jnp.tile(cos, (1, CW // HEAD_DIM)), jnp.tile(sin, (1, CW // HEAD_DIM))


def _pack(parts, rows):
    flat = jnp.concatenate([p.reshape(-1).astype(F32) for p in parts])
    return jnp.pad(flat, (0, rows * LANES - flat.shape[0])).reshape(rows, LANES)


def _unpack(buf, shapes):
    flat = buf.reshape(-1)
    out, off = [], 0
    for shp in shapes:
        size = math.prod(shp)
        out.append(flat[off:off + size].reshape(shp))
        off += size
    return out


def kernel(x, norm_mix, w_in, a_sink, t5_table, b_rpb, c_q_gain, c_k_gain, out_gain_a, out_gain_b, out_gain_c, w_o, norm_mlp, w_up, w_down, norm_final, loss_target, m_norm_mix, m_w_in, m_a_sink, m_t5_table, m_b_rpb, m_c_q_gain, m_c_k_gain, m_out_gain_a, m_out_gain_b, m_out_gain_c, m_w_o, m_norm_mlp, m_w_up, m_w_down, m_norm_final, v_norm_mix, v_w_in, v_a_sink, v_t5_table, v_b_rpb, v_c_q_gain, v_c_k_gain, v_out_gain_a, v_out_gain_b, v_out_gain_c, v_w_o, v_norm_mlp, v_w_up, v_w_down, v_norm_final):
    n_layers = w_in.shape[0]
    s, d = x.shape[1], x.shape[2]
    d_ff = 4 * w_up.shape[2]
    in_w = 4 * w_in.shape[2]
    xs = x.reshape(s, d)
    target = loss_target.reshape(s, d)
    cfg_a, cfg_b = _cfg_a(s), _cfg_b(s)

    wg_in, wg_o, wg_up, wg_down = _allgather_weights(
        [w_in.astype(MXU_DT), w_o.astype(MXU_DT), w_up.astype(MXU_DT), w_down.astype(MXU_DT)])
    wf_in = wg_in.transpose(0, 2, 1, 3).reshape(n_layers, d, in_w)
    wf_o = wg_o.reshape(n_layers, d, d)
    wf_down = wg_down.reshape(n_layers, d_ff, d)
    ff_shard = w_up.shape[2]

    ones = _pair_ones()
    cos_t, sin_t = _rope_tables(s)
    a_onehot, a_valid = _a_bias_maps()
    bias_a = jnp.where(a_valid[:, None], jnp.einsum("vqkb,bh->vhqk", a_onehot, t5_table, precision=HIGHEST),
                       MASK_VALUE)
    row_oh, col_oh, b_valid = _b_bias_maps()
    sink_b = jnp.full((4, 1, LANES), MASK_VALUE, F32)

    def b_bias(rpb):
        t = jnp.einsum("hrz,vir->vhiz", rpb, row_oh, precision=HIGHEST)
        t = jnp.einsum("vhiz,qcz->vhqic", t, col_oh, precision=HIGHEST)
        t = jnp.where(b_valid[None, None, :, None, :], t, MASK_VALUE)
        return t.reshape(8, 8, GRID_W, NA_ROWS * GRID_W)

    def tile_gain(gvec):
        return jnp.tile(gvec, CW // HEAD_DIM).reshape(1, CW)

    def pad_sink(svec):
        return jnp.pad(svec, (0, LANES - svec.shape[0])).reshape(1, 1, LANES)

    saved = []
    xc = xs
    for l in range(n_layers):
        h1 = _rms_fwd(xc, norm_mix[l], "rms_mix")
        proj = _matmul(h1, wf_in[l], mode="nn", name="proj_in", tm=1024, tn=768, tk=2048)
        bias_b = b_bias(b_rpb[l])
        oa = _local_attn_fwd(proj, bias_a, pad_sink(a_sink[l]), cfg_a, "attn_a_fwd")
        ob = _local_attn_fwd(proj, bias_b, sink_b, cfg_b, "attn_b_fwd")
        gq, gk = tile_gain(c_q_gain[l]), tile_gain(c_k_gain[l])
        qh, kd, vd = _cprep_fwd(proj, gq, gk, cos_t, sin_t, ones, "cprep_fwd")
        kdt, vdt = kd.T, vd.T
        oct, lse = _flash_fwd(qh, kd, vdt, "attn_c_fwd")
        oc = oct.T
        mix = _groupnorm_fwd(oa, ob, oc, out_gain_a[l], out_gain_b[l], out_gain_c[l], "groupnorm_fwd")
        x_mid = _matmul(mix, wf_o[l], mode="nn", name="proj_out", tm=1024, tn=1024, tk=2048, epi="res",
                        extra=(xc,))
        h2 = _rms_fwd(x_mid, norm_mlp[l], "rms_mlp")
        nb_up = ff_shard // 1024
        u, uu = _matmul(h2, wg_up[l], mode="nn", name="mlp_up", tm=1024, tn=1024, tk=2048, epi="relu2",
                        out_dtypes=(F32, MXU_DT), mkn=(s, d, d_ff),
                        b_spec=pl.BlockSpec((None, 2048, 1024), lambda i, j, kk: (j // nb_up, kk, j % nb_up)))
        x_out = _matmul(uu, wf_down[l], mode="nn", name="mlp_down", tm=1024, tn=1024, tk=2048, epi="res",
                        extra=(x_mid,))
        saved.append((xc, h1, proj, bias_b, oa, ob, qh, kd, vd, kdt, oc, lse, mix, x_mid, h2, u, uu))
        xc = x_out

    loss_part, dx, dxb, dg_final = _final_loss(xc, norm_final, target, "final_loss")

    g_in, g_o, g_up, g_down = [], [], [], []
    small = {k: [] for k in ("norm_mix", "a_sink", "b_rpb", "cq", "ck", "oga", "ogb", "ogc", "norm_mlp")}
    dbias_a_total = jnp.zeros_like(bias_a)
    for l in reversed(range(n_layers)):
        xin, h1, proj, bias_b, oa, ob, qh, kd, vd, kdt, oc, lse, mix, x_mid, h2, u, uu = saved[l]
        du = _matmul(dxb, wf_down[l], mode="nt", name="mlp_down_dgrad", tm=1024, tn=1024, tk=2048, epi="mul2u",
                     extra=(u,), out_dtypes=(MXU_DT,))
        g_down.append(_matmul(uu, dxb, mode="tn", name="mlp_down_wgrad", tm=1024, tn=1024, tk=1024)
                      .reshape(4, d_ff // 4, d))
        nbk = ff_shard // 2048
        dh2 = _matmul(du, wg_up[l], mode="nt", name="mlp_up_dgrad", tm=1024, tn=1024, tk=2048,
                      mkn=(s, d_ff, d),
                      b_spec=pl.BlockSpec((None, 1024, 2048), lambda i, j, kk: (kk // nbk, j, kk % nbk)))
        nbo = ff_shard // 1024
        g_up.append(_matmul(h2, du, mode="tn", name="mlp_up_wgrad", tm=1024, tn=1024, tk=1024,
                            out_spec=pl.BlockSpec((None, 1024, 1024), lambda i, j, kk: (j // nbo, i, j % nbo)),
                            out_shape=(4, d, ff_shard)))
        dx_mid, dxmb, dg = _rms_bwd(x_mid, norm_mlp[l], dh2, dx, "rms_mlp_bwd")
        small["norm_mlp"].append(dg)
        dmix = _matmul(dxmb, wf_o[l], mode="nt", name="proj_out_dgrad", tm=1024, tn=1024, tk=2048)
        g_o.append(_matmul(mix, dxmb, mode="tn", name="proj_out_wgrad", tm=1024, tn=1024, tk=1024)
                   .reshape(4, d // 4, d))
        doa, dob, doc, docb, ddc, dga, dgb, dgc = _groupnorm_bwd(
            dmix, oa, ob, oc, out_gain_a[l], out_gain_b[l], out_gain_c[l], ones, "groupnorm_bwd")
        small["oga"].append(dga)
        small["ogb"].append(dgb)
        small["ogc"].append(dgc)
        dqa, dka, dva, dbias_a, dsink = _local_attn_bwd(proj, bias_a, pad_sink(a_sink[l]), doa, cfg_a, "attn_a_bwd")
        dbias_a_total = dbias_a_total + dbias_a
        small["a_sink"].append(dsink[0, 0, :a_sink.shape[1]])
        dqb, dkb, dvb, dbias_b, _ = _local_attn_bwd(proj, bias_b, sink_b, dob, cfg_b, "attn_b_bwd")
        db5 = jnp.where(b_valid[None, None, :, None, :], dbias_b.reshape(8, 8, GRID_W, NA_ROWS, GRID_W), 0.0)
        t = jnp.einsum("vhqic,qcz->vhiz", db5, col_oh, precision=HIGHEST)
        small["b_rpb"].append(jnp.einsum("vhiz,vir->hrz", t, row_oh, precision=HIGHEST))
        dd_rows = ddc.reshape(s, 16, HEAD_DIM)[:, :, 0].T.reshape(4, 4, s)
        dqht, dkd, dvd = _flash_bwd(qh, kd, vd, kdt, docb, lse, dd_rows, "attn_c_bwd")
        dqh = dqht.T
        gq, gk = tile_gain(c_q_gain[l]), tile_gain(c_k_gain[l])
        dqc, dkc, dvc, dgq, dgk = _cprep_bwd(proj, gq, gk, cos_t, sin_t, ones, dqh, dkd, dvd, "cprep_bwd")
        small["cq"].append(dgq.reshape(CW // HEAD_DIM, HEAD_DIM).sum(0))
        small["ck"].append(dgk.reshape(CW // HEAD_DIM, HEAD_DIM).sum(0))
        dproj = jnp.concatenate([dqa, dka, dva, dqb, dkb, dvb, dqc, dkc, dvc], axis=1).astype(MXU_DT)
        dh1 = _matmul(dproj, wf_in[l], mode="nt", name="proj_in_dgrad", tm=1024, tn=1024, tk=1920)
        gw = _matmul(h1, dproj, mode="tn", name="proj_in_wgrad", tm=1024, tn=768, tk=1024)
        g_in.append(gw.reshape(d, 4, in_w // 4).transpose(1, 0, 2))
        dx, dxb, dg = _rms_bwd(xin, norm_mix[l], dh1, dx_mid, "rms_mix_bwd")
        small["norm_mix"].append(dg)

    for lst in (g_in, g_o, g_up, g_down):
        lst.reverse()
    for lst in small.values():
        lst.reverse()

    x_i, y_i, c_i = _mesh_pos()
    c_idx = c_i.astype(jnp.int32).reshape(1)
    me_idx = (2 * x_i + y_i).astype(jnp.int32).reshape(1)
    grads = g_in + g_o + g_up + g_down
    recv1 = _exchange_core_halves(grads)
    parts = [_add_half(g, r, c_idx, "add_core_halves") for g, r in zip(grads, recv1)]
    recv2 = _exchange_chips(parts)
    halves = [_sum_chips(p, r, me_idx, "sum_chips") for p, r in zip(parts, recv2)]
    gr_in, gr_o, gr_up, gr_down = _share_halves(halves, n_layers)

    big = {}
    for nm, w, g, m, v in (("w_in", w_in, gr_in, m_w_in, v_w_in), ("w_o", w_o, gr_o, m_w_o, v_w_o),
                           ("w_up", w_up, gr_up, m_w_up, v_w_up), ("w_down", w_down, gr_down, m_w_down, v_w_down)):
        shp = w.shape
        flat = (shp[0] * shp[1], shp[2])
        dl, mo, vo = _adamw(w.reshape(flat), g.reshape(flat), m.reshape(flat), v.reshape(flat), "adamw_" + nm)
        big[nm] = (g, dl.reshape(shp), mo.reshape(shp), vo.reshape(shp))

    dt5 = jnp.einsum("vhqk,vqkb->bh", dbias_a_total, a_onehot, precision=HIGHEST)
    small_names = ["norm_mix", "a_sink", "t5_table", "b_rpb", "c_q_gain", "c_k_gain", "out_gain_a", "out_gain_b",
                   "out_gain_c", "norm_mlp", "norm_final"]
    small_w = [norm_mix, a_sink, t5_table, b_rpb, c_q_gain, c_k_gain, out_gain_a, out_gain_b, out_gain_c, norm_mlp,
               norm_final]
    small_m = [m_norm_mix, m_a_sink, m_t5_table, m_b_rpb, m_c_q_gain, m_c_k_gain, m_out_gain_a, m_out_gain_b,
               m_out_gain_c, m_norm_mlp, m_norm_final]
    small_v = [v_norm_mix, v_a_sink, v_t5_table, v_b_rpb, v_c_q_gain, v_c_k_gain, v_out_gain_a, v_out_gain_b,
               v_out_gain_c, v_norm_mlp, v_norm_final]
    small_g = [jnp.stack(small["norm_mix"]), jnp.stack(small["a_sink"]), dt5, jnp.stack(small["b_rpb"]),
               jnp.stack(small["cq"]), jnp.stack(small["ck"]), jnp.stack(small["oga"]), jnp.stack(small["ogb"]),
               jnp.stack(small["ogc"]), jnp.stack(small["norm_mlp"]), dg_final]
    shapes = [w.shape for w in small_w]
    total = sum(math.prod(shp) for shp in shapes) + 1
    rows = -(-total // (8 * LANES)) * 8
    one = [jnp.ones((1,), F32)]
    gs, dl, mo, vo = _allreduce_small_adam(_pack(small_g + [loss_part[0, :1]], rows), _pack(small_w + one, rows),
                                           _pack(small_m + one, rows), _pack(small_v + one, rows))
    sg = _unpack(gs, shapes + [(1,)])
    sd, sm, sv = _unpack(dl, shapes), _unpack(mo, shapes), _unpack(vo, shapes)
    loss = sg[-1].reshape(())

    by_name = {nm: (sg[i], sd[i], sm[i], sv[i]) for i, nm in enumerate(small_names)}
    by_name.update(big)
    order = ["norm_mix", "w_in", "a_sink", "t5_table", "b_rpb", "c_q_gain", "c_k_gain", "out_gain_a", "out_gain_b",
             "out_gain_c", "w_o", "norm_mlp", "w_up", "w_down", "norm_final"]
    outs = [loss, dx.reshape(x.shape)]
    for field in range(4):
        outs.extend(by_name[nm][field] for nm in order)
    return tuple(outs)
```

```python
import functools
import math

import jax
import jax.numpy as jnp
from jax import lax
from jax.experimental import pallas as pl
from jax.experimental.pallas import tpu as pltpu

F32 = jnp.float32
MXU_DT = jnp.bfloat16
GRAD_DT = jnp.bfloat16
HIGHEST = lax.Precision.HIGHEST

HEAD_DIM = 64
LANES = 128
EPS = 1e-6
MASK_VALUE = -1e30
GRID_W = 64
NA_ROWS = 8
T5_BUCKETS = 32
T5_MAX_DIST = 128
ROPE_THETA = 10000.0
ADAM_LR, ADAM_B1, ADAM_B2, ADAM_EPS, ADAM_WD, ADAM_STEP = 0.001, 0.9, 0.999, 1e-08, 0.01, 10
VMEM_LIMIT = 56 * 1024 * 1024

MESH_ID = pl.DeviceIdType.MESH
ANY = pl.BlockSpec(memory_space=pl.ANY)

NT_DIMS = (((1,), (1,)), ((), ()))
TN_DIMS = (((0,), (0,)), ((), ()))
NN_DIMS = (((1,), (0,)), ((), ()))


def _dot(a, b, dims=NN_DIMS):
    return lax.dot_general(a, b, dims, preferred_element_type=F32)


def _call(body, *, name, out_shape, grid=(), in_specs=None, out_specs=None, scratch=(), sem=None,
          prefetch=0, aliases=None):
    params = {"vmem_limit_bytes": VMEM_LIMIT}
    if sem is not None:
        params["dimension_semantics"] = sem
    kwargs = {}
    if aliases:
        kwargs["input_output_aliases"] = aliases
    if prefetch:
        spec = pltpu.PrefetchScalarGridSpec(num_scalar_prefetch=prefetch, grid=grid, in_specs=in_specs,
                                            out_specs=out_specs, scratch_shapes=list(scratch))
        return pl.pallas_call(body, grid_spec=spec, out_shape=out_shape, name=name,
                              compiler_params=pltpu.CompilerParams(**params), **kwargs)
    return pl.pallas_call(body, grid=grid, in_specs=in_specs, out_specs=out_specs, out_shape=out_shape,
                          scratch_shapes=list(scratch), name=name,
                          compiler_params=pltpu.CompilerParams(**params), **kwargs)


def _sds(shape, dtype=F32):
    return jax.ShapeDtypeStruct(tuple(shape), dtype)


def _matmul(a, b, *, mode, name, tm, tn, tk, epi="plain", extra=(), out_dtypes=(F32,), mkn=None,
            b_spec=None, out_spec=None, out_shape=None):
    if mkn is None:
        if mode == "nn":
            (m, k), n = a.shape, b.shape[1]
        elif mode == "nt":
            (m, k), n = a.shape, b.shape[0]
        else:
            (k, m), n = a.shape, b.shape[1]
    else:
        m, k, n = mkn
    tm, tn, tk = min(tm, m), min(tn, n), min(tk, k)
    assert m % tm == 0 and n % tn == 0 and k % tk == 0, (name, m, n, k, tm, tn, tk)
    nk = k // tk
    dims = {"nn": NN_DIMS, "nt": NT_DIMS, "tn": TN_DIMS}[mode]
    n_extra, n_out = len(extra), len(out_dtypes)

    def body(a_ref, b_ref, *rest):
        extra_refs = rest[:n_extra]
        out_refs = rest[n_extra:n_extra + n_out]
        acc_ref = rest[n_extra + n_out]
        kk = pl.program_id(2)

        @pl.when(kk == 0)
        def _():
            acc_ref[...] = jnp.zeros_like(acc_ref)

        acc_ref[...] += _dot(a_ref[...].astype(MXU_DT), b_ref[...].astype(MXU_DT), dims)

        @pl.when(kk == nk - 1)
        def _():
            acc = acc_ref[...]
            if epi == "plain":
                out_refs[0][...] = acc.astype(out_refs[0].dtype)
            elif epi == "res":
                out_refs[0][...] = (extra_refs[0][...] + acc).astype(out_refs[0].dtype)
            elif epi == "relu2":
                u = jnp.maximum(acc, 0.0)
                out_refs[0][...] = u.astype(out_refs[0].dtype)
                out_refs[1][...] = (u * u).astype(out_refs[1].dtype)
            elif epi == "mul2u":
                out_refs[0][...] = (2.0 * extra_refs[0][...] * acc).astype(out_refs[0].dtype)
            else:
                raise ValueError(epi)

    if mode == "tn":
        a_spec = pl.BlockSpec((tk, tm), lambda i, j, kk: (kk, i))
    else:
        a_spec = pl.BlockSpec((tm, tk), lambda i, j, kk: (i, kk))
    if b_spec is None:
        if mode == "nt":
            b_spec = pl.BlockSpec((tn, tk), lambda i, j, kk: (j, kk))
        else:
            b_spec = pl.BlockSpec((tk, tn), lambda i, j, kk: (kk, j))
    mn_spec = pl.BlockSpec((tm, tn), lambda i, j, kk: (i, j))
    if out_spec is None:
        out_spec = mn_spec
    if out_shape is None:
        out_shape = (m, n)
    res = _call(body, name=name, grid=(m // tm, n // tn, nk),
                in_specs=[a_spec, b_spec] + [mn_spec] * n_extra,
                out_specs=[out_spec] * n_out,
                out_shape=[_sds(out_shape, d) for d in out_dtypes],
                scratch=[pltpu.VMEM((tm, tn), F32)],
                sem=("parallel", "parallel", "arbitrary"))(a, b, *extra)
    return res if n_out > 1 else res[0]


def _row_tile(s):
    return min(512, s)


def _rms_fwd(x, g, name):
    s, d = x.shape
    tm = _row_tile(s)

    def body(x_ref, g_ref, h_ref):
        xv = x_ref[...]
        r = lax.rsqrt(jnp.mean(xv * xv, axis=-1, keepdims=True) + EPS)
        h_ref[...] = ((xv * r) * g_ref[...]).astype(h_ref.dtype)

    return _call(body, name=name, grid=(s // tm,),
                 in_specs=[pl.BlockSpec((tm, d), lambda i: (i, 0)), pl.BlockSpec((1, d), lambda i: (0, 0))],
                 out_specs=pl.BlockSpec((tm, d), lambda i: (i, 0)),
                 out_shape=_sds((s, d), MXU_DT), sem=("parallel",))(x, g.reshape(1, d))


def _rms_bwd(x, g, dh, dres, name):
    s, d = x.shape
    tm = _row_tile(s)

    def body(x_ref, g_ref, dh_ref, dres_ref, dx_ref, dxb_ref, dg_ref):
        i = pl.program_id(0)
        xv = x_ref[...]
        r = lax.rsqrt(jnp.mean(xv * xv, axis=-1, keepdims=True) + EPS)
        xh = xv * r
        dhv = dh_ref[...]
        gd = dhv * g_ref[...]
        c = jnp.mean(gd * xh, axis=-1, keepdims=True)
        dx = dres_ref[...] + r * (gd - xh * c)
        dx_ref[...] = dx
        dxb_ref[...] = dx.astype(dxb_ref.dtype)
        part = jnp.sum(dhv * xh, axis=0, keepdims=True)

        @pl.when(i == 0)
        def _():
            dg_ref[...] = part

        @pl.when(i > 0)
        def _():
            dg_ref[...] += part

    row = pl.BlockSpec((tm, d), lambda i: (i, 0))
    vec = pl.BlockSpec((1, d), lambda i: (0, 0))
    return _call(body, name=name, grid=(s // tm,), in_specs=[row, vec, row, row],
                 out_specs=[row, row, vec],
                 out_shape=[_sds((s, d)), _sds((s, d), MXU_DT), _sds((1, d))],
                 sem=("arbitrary",))(x, g.reshape(1, d), dh, dres)


def _final_loss(x, g, target, name):
    s, d = x.shape
    tm = _row_tile(s)

    def body(x_ref, g_ref, t_ref, loss_ref, dx_ref, dxb_ref, dg_ref):
        i = pl.program_id(0)
        xv = x_ref[...]
        gv = g_ref[...]
        r = lax.rsqrt(jnp.mean(xv * xv, axis=-1, keepdims=True) + EPS)
        xh = xv * r
        err = xh * gv - t_ref[...]
        part_loss = 0.5 * jnp.sum(jnp.mean(err * err, axis=-1, keepdims=True), axis=0, keepdims=True)
        dy = err * (1.0 / d)
        gd = dy * gv
        c = jnp.mean(gd * xh, axis=-1, keepdims=True)
        dx = r * (gd - xh * c)
        dx_ref[...] = dx
        dxb_ref[...] = dx.astype(dxb_ref.dtype)
        part_g = jnp.sum(dy * xh, axis=0, keepdims=True)
        part_l = jnp.broadcast_to(part_loss, (1, LANES))

        @pl.when(i == 0)
        def _():
            dg_ref[...] = part_g
            loss_ref[...] = part_l

        @pl.when(i > 0)
        def _():
            dg_ref[...] += part_g
            loss_ref[...] += part_l

    row = pl.BlockSpec((tm, d), lambda i: (i, 0))
    vec = pl.BlockSpec((1, d), lambda i: (0, 0))
    return _call(body, name=name, grid=(s // tm,), in_specs=[row, vec, row],
                 out_specs=[pl.BlockSpec((1, LANES), lambda i: (0, 0)), row, row, vec],
                 out_shape=[_sds((1, LANES)), _sds((s, d)), _sds((s, d), MXU_DT), _sds((1, d))],
                 sem=("arbitrary",))(x, g.reshape(1, d), target)


def _lane_iota(shape):
    return lax.broadcasted_iota(jnp.int32, shape, len(shape) - 1)


def _swap_halves(x):
    return pltpu.roll(x, HEAD_DIM, 1)


def _segsum64(x, ones_ref):
    ones = ones_ref[...]
    outs = []
    for c in range(x.shape[1] // LANES):
        xc = x[:, c * LANES:(c + 1) * LANES]
        hi = xc.astype(MXU_DT)
        r1 = xc - hi.astype(F32)
        mid = r1.astype(MXU_DT)
        lo = (r1 - mid.astype(F32)).astype(MXU_DT)
        outs.append(_dot(hi, ones) + _dot(mid, ones) + _dot(lo, ones))
    return outs[0] if len(outs) == 1 else jnp.concatenate(outs, axis=1)


def _pair_ones():
    i = jnp.arange(LANES)
    return (i[:, None] // HEAD_DIM == i[None, :] // HEAD_DIM).astype(MXU_DT)


def _col(x, lane):
    return jnp.sum(jnp.where(_lane_iota(x.shape) == lane, x, 0.0), axis=-1, keepdims=True)


class _LocalCfg:
    def __init__(self, *, groups, qb, kw, qw, qcol, kcol, vcol, kvhalf, kstart, variant):
        self.groups, self.qb, self.kw, self.qw = groups, qb, kw, qw
        self.qcol, self.kcol, self.vcol = qcol, kcol, vcol
        self.kvhalf = kvhalf
        self.kstart, self.variant = kstart, variant
        self.pairs = qw // LANES


def _cfg_a(s):
    nb = s // 128
    return _LocalCfg(groups=1, qb=128, kw=384, qw=512, qcol=lambda g: 0, kcol=lambda g: 4, vcol=lambda g: 5,
                     kvhalf=lambda t, e: t // 2,
                     kstart=lambda n: 128 * jnp.clip(n - 1, 0, nb - 3),
                     variant=lambda n: jnp.where(n <= 0, 0, jnp.where(n == nb - 1, 2, 1)))


def _cfg_b(s):
    rows = s // GRID_W
    return _LocalCfg(groups=4, qb=64, kw=512, qw=128, qcol=lambda g: 6 + g, kcol=lambda g: 10 + g,
                     vcol=lambda g: 14 + g, kvhalf=lambda t, e: e,
                     kstart=lambda n: GRID_W * jnp.clip(n - NA_ROWS // 2, 0, rows - NA_ROWS),
                     variant=lambda n: jnp.where(n < 4, jnp.maximum(n, 0),
                                                 jnp.where(n > rows - 4, n - (rows - 8), 4)))


def _local_head(cfg, t, e, qp, qp_sw, kb, bias, sink_row, left_q):
    kvh = cfg.kvhalf(t, e)
    qsrc = qp if e == kvh else qp_sw
    keep = left_q if kvh == 0 else jnp.logical_not(left_q)
    qm = jnp.where(keep, qsrc, 0.0).astype(MXU_DT)
    sc = _dot(qm, kb, NT_DIMS) + bias
    snk = _col(sink_row, 2 * t + e)
    m = jnp.maximum(jnp.max(sc, axis=-1, keepdims=True), snk)
    p = jnp.exp(sc - m)
    l = jnp.sum(p, axis=-1, keepdims=True) + jnp.exp(snk - m)
    p = p / l
    return qm, keep, p, m, l, snk


def _local_attn_fwd(proj, bias, sink, cfg, name):
    s = proj.shape[0]
    qb, kw, qw, g_n = cfg.qb, cfg.kw, cfg.qw, cfg.groups
    hq = 2 * cfg.pairs

    def body(q_ref, k_ref, v_ref, b_ref, s_ref, o_ref):
        n = pl.program_id(1)
        ks = pl.multiple_of(cfg.kstart(n), 64)
        kf = k_ref[pl.ds(ks, kw), :]
        vf = v_ref[pl.ds(ks, kw), :]
        kb = kf.astype(MXU_DT)
        vf_sw = _swap_halves(vf)
        left_q = _lane_iota((qb, LANES)) < HEAD_DIM
        left_k = _lane_iota((kw, LANES)) < HEAD_DIM
        sink_row = s_ref[...]
        for t in range(cfg.pairs):
            qp = q_ref[:, t * LANES:(t + 1) * LANES] * 0.125
            qp_sw = _swap_halves(qp)
            acc = jnp.zeros((qb, LANES), F32)
            for e in range(2):
                _, _, p, _, _, _ = _local_head(cfg, t, e, qp, qp_sw, kb, b_ref[0, 2 * t + e], sink_row, left_q)
                vsrc = vf if e == cfg.kvhalf(t, e) else vf_sw
                vsel = jnp.where(left_k if e == 0 else jnp.logical_not(left_k), vsrc, 0.0).astype(MXU_DT)
                acc = acc + _dot(p.astype(MXU_DT), vsel)
            o_ref[:, t * LANES:(t + 1) * LANES] = acc

    return _call(
        body, name=name, grid=(g_n, s // qb),
        in_specs=[pl.BlockSpec((qb, qw), lambda g, n: (n, cfg.qcol(g))),
                  pl.BlockSpec((s, LANES), lambda g, n: (0, cfg.kcol(g))),
                  pl.BlockSpec((s, LANES), lambda g, n: (0, cfg.vcol(g))),
                  pl.BlockSpec((1, hq, qb, kw), lambda g, n: (cfg.variant(n), g, 0, 0)),
                  pl.BlockSpec((None, 1, LANES), lambda g, n: (g, 0, 0))],
        out_specs=pl.BlockSpec((qb, qw), lambda g, n: (n, g)),
        out_shape=_sds((s, g_n * qw)), sem=("parallel", "arbitrary"))(proj, proj, proj, bias, sink)


def _local_attn_bwd(proj, bias, sink, do, cfg, name):
    s = proj.shape[0]
    qb, kw, qw, g_n = cfg.qb, cfg.kw, cfg.qw, cfg.groups
    hq = 2 * cfg.pairs

    def body(q_ref, k_ref, v_ref, b_ref, s_ref, do_ref, dq_ref, dk_ref, dv_ref, db_ref, dsk_ref):
        n = pl.program_id(1)
        ks = pl.multiple_of(cfg.kstart(n), 64)
        first = jnp.logical_or(n == 0, cfg.variant(n) != cfg.variant(n - 1))

        @pl.when(n == 0)
        def _():
            dk_ref[...] = jnp.zeros_like(dk_ref)
            dv_ref[...] = jnp.zeros_like(dv_ref)
            dsk_ref[...] = jnp.zeros_like(dsk_ref)

        @pl.when(first)
        def _():
            db_ref[...] = jnp.zeros_like(db_ref)

        kf = k_ref[pl.ds(ks, kw), :]
        vf = v_ref[pl.ds(ks, kw), :]
        kb = kf.astype(MXU_DT)
        vb = vf.astype(MXU_DT)
        kf_sw = _swap_halves(kf)
        left_q = _lane_iota((qb, LANES)) < HEAD_DIM
        left_k = _lane_iota((kw, LANES)) < HEAD_DIM
        sink_row = s_ref[...]
        row0 = lax.broadcasted_iota(jnp.int32, (8, LANES), 0) == 0
        lane8 = _lane_iota((8, LANES))
        dk_acc = jnp.zeros((kw, LANES), F32)
        dv_acc = jnp.zeros((kw, LANES), F32)
        dsk_acc = jnp.zeros((8, LANES), F32)
        for t in range(cfg.pairs):
            qp = q_ref[:, t * LANES:(t + 1) * LANES] * 0.125
            qp_sw = _swap_halves(qp)
            dop = do_ref[:, t * LANES:(t + 1) * LANES]
            dop_sw = _swap_halves(dop)
            dq_t = jnp.zeros((qb, LANES), F32)
            for e in range(2):
                h = 2 * t + e
                qm, keep, p, m, l, snk = _local_head(cfg, t, e, qp, qp_sw, kb, b_ref[0, h], sink_row, left_q)
                kvh = cfg.kvhalf(t, e)
                dom = jnp.where(keep, dop if e == kvh else dop_sw, 0.0).astype(MXU_DT)
                dp = _dot(dom, vb, NT_DIMS)
                dd = jnp.sum(p * dp, axis=-1, keepdims=True)
                ds = p * (dp - dd)
                p_sink = jnp.exp(snk - m) / l
                dsink = jnp.sum(-p_sink * dd, axis=0, keepdims=True)
                dsk_acc = dsk_acc + jnp.where(jnp.logical_and(row0, lane8 == h), dsink, 0.0)
                dsb = ds.astype(MXU_DT)
                dv_acc = dv_acc + _dot(p.astype(MXU_DT), dom, TN_DIMS)
                dk_acc = dk_acc + _dot(dsb, qm, TN_DIMS)
                ksrc = kf if e == kvh else kf_sw
                ksel = jnp.where(left_k if e == 0 else jnp.logical_not(left_k), ksrc, 0.0).astype(MXU_DT)
                dq_t = dq_t + _dot(dsb, ksel)
                db_ref[0, h] += ds
            dq_ref[:, t * LANES:(t + 1) * LANES] = dq_t * 0.125
        dk_ref[pl.ds(ks, kw), :] += dk_acc
        dv_ref[pl.ds(ks, kw), :] += dv_acc
        dsk_ref[...] += dsk_acc

    n_var = bias.shape[0]
    return _call(
        body, name=name, grid=(g_n, s // qb),
        in_specs=[pl.BlockSpec((qb, qw), lambda g, n: (n, cfg.qcol(g))),
                  pl.BlockSpec((s, LANES), lambda g, n: (0, cfg.kcol(g))),
                  pl.BlockSpec((s, LANES), lambda g, n: (0, cfg.vcol(g))),
                  pl.BlockSpec((1, hq, qb, kw), lambda g, n: (cfg.variant(n), g, 0, 0)),
                  pl.BlockSpec((None, 1, LANES), lambda g, n: (g, 0, 0)),
                  pl.BlockSpec((qb, qw), lambda g, n: (n, g))],
        out_specs=[pl.BlockSpec((qb, qw), lambda g, n: (n, g)),
                   pl.BlockSpec((s, LANES), lambda g, n: (0, g)),
                   pl.BlockSpec((s, LANES), lambda g, n: (0, g)),
                   pl.BlockSpec((1, hq, qb, kw), lambda g, n: (cfg.variant(n), g, 0, 0)),
                   pl.BlockSpec((None, 8, LANES), lambda g, n: (g, 0, 0))],
        out_shape=[_sds((s, g_n * qw)), _sds((s, g_n * LANES)), _sds((s, g_n * LANES)),
                   _sds((n_var, g_n * hq, qb, kw)), _sds((g_n, 8, LANES))],
        sem=("parallel", "arbitrary"))(proj, proj, proj, bias, sink, do)


QC_COL, KC_COL, VC_COL = 9, 13, 14
CW = 256


def _swap16(x):
    w = x.shape[1]
    lane = _lane_iota(x.shape)
    return jnp.where(lane % 32 < 16, pltpu.roll(x, w - 16, 1), pltpu.roll(x, 16, 1))


def _dup_halves(x):
    left = _lane_iota(x.shape) < HEAD_DIM
    sw = _swap_halves(x)
    return jnp.where(left, x, sw), jnp.where(left, sw, x)


def _normrope(x, gain, cos, sin, ones_ref):
    ms = _segsum64(x * x, ones_ref) * (1.0 / HEAD_DIM)
    r = lax.rsqrt(ms + EPS)
    y = (x * r) * gain
    return y * cos + _swap16(y) * sin, r


def _cprep_fwd(proj, gq, gk, cos, sin, ones, name):
    s = proj.shape[0]
    tm = _row_tile(s)

    def body(q0, q1, q2, q3, k_ref, v_ref, gq_ref, gk_ref, cos_ref, sin_ref, ones_ref, qh_ref, kd_ref, vd_ref):
        cos_v, sin_v = cos_ref[...], sin_ref[...]
        for c, q_ref in enumerate((q0, q1, q2, q3)):
            y, _ = _normrope(q_ref[...], gq_ref[...], cos_v, sin_v, ones_ref)
            qh_ref[:, c * CW:(c + 1) * CW] = (y * 0.125).astype(qh_ref.dtype)
        yk, _ = _normrope(k_ref[...], gk_ref[...], cos_v, sin_v, ones_ref)
        vv = v_ref[...]
        for p in range(2):
            ka, kb_ = _dup_halves(yk[:, p * LANES:(p + 1) * LANES])
            va, vb_ = _dup_halves(vv[:, p * LANES:(p + 1) * LANES])
            kd_ref[:, (2 * p) * LANES:(2 * p + 1) * LANES] = ka.astype(kd_ref.dtype)
            kd_ref[:, (2 * p + 1) * LANES:(2 * p + 2) * LANES] = kb_.astype(kd_ref.dtype)
            vd_ref[:, (2 * p) * LANES:(2 * p + 1) * LANES] = va.astype(vd_ref.dtype)
            vd_ref[:, (2 * p + 1) * LANES:(2 * p + 2) * LANES] = vb_.astype(vd_ref.dtype)

    def chunk(col):
        return pl.BlockSpec((tm, CW), lambda i: (i, col))

    vec = pl.BlockSpec((1, CW), lambda i: (0, 0))
    tab = pl.BlockSpec((tm, CW), lambda i: (i, 0))
    return _call(body, name=name, grid=(s // tm,),
                 in_specs=[chunk(QC_COL), chunk(QC_COL + 1), chunk(QC_COL + 2), chunk(QC_COL + 3),
                           chunk(KC_COL), chunk(VC_COL), vec, vec, tab, tab,
                           pl.BlockSpec((LANES, LANES), lambda i: (0, 0))],
                 out_specs=[pl.BlockSpec((tm, 4 * CW), lambda i: (i, 0)),
                            pl.BlockSpec((tm, 2 * CW), lambda i: (i, 0)),
                            pl.BlockSpec((tm, 2 * CW), lambda i: (i, 0))],
                 out_shape=[_sds((s, 4 * CW), MXU_DT), _sds((s, 2 * CW), MXU_DT), _sds((s, 2 * CW), MXU_DT)],
                 sem=("parallel",))(proj, proj, proj, proj, proj, proj, gq, gk, cos, sin, ones)


def _cprep_bwd(proj, gq, gk, cos, sin, ones, dqh, dkd, dvd, name):
    s = proj.shape[0]
    tm = _row_tile(s)

    def fold(ref, p):
        a = ref[:, (2 * p) * LANES:(2 * p + 1) * LANES]
        b = ref[:, (2 * p + 1) * LANES:(2 * p + 2) * LANES]
        ta = a + _swap_halves(a)
        tb = b + _swap_halves(b)
        return jnp.where(_lane_iota(a.shape) < HEAD_DIM, ta, tb)

    def norm_bwd(x, gain, dyr, cos_v, sin_v, ones_ref):
        dy = dyr * cos_v + _swap16(dyr * sin_v)
        ms = _segsum64(x * x, ones_ref) * (1.0 / HEAD_DIM)
        r = lax.rsqrt(ms + EPS)
        xh = x * r
        gd = dy * gain
        c = _segsum64(gd * xh, ones_ref) * (1.0 / HEAD_DIM)
        return r * (gd - xh * c), jnp.sum(dy * xh, axis=0, keepdims=True)

    def body(q0, q1, q2, q3, k_ref, gq_ref, gk_ref, cos_ref, sin_ref, ones_ref, dqh_ref, dkd_ref, dvd_ref,
             dq_ref, dk_ref, dv_ref, dgq_ref, dgk_ref):
        i = pl.program_id(0)
        cos_v, sin_v = cos_ref[...], sin_ref[...]
        gq_part = jnp.zeros((1, CW), F32)
        for c, q_ref in enumerate((q0, q1, q2, q3)):
            dx, dg = norm_bwd(q_ref[...], gq_ref[...], dqh_ref[:, c * CW:(c + 1) * CW] * 0.125, cos_v, sin_v,
                              ones_ref)
            dq_ref[:, c * CW:(c + 1) * CW] = dx
            gq_part = gq_part + dg
        dkr = jnp.concatenate([fold(dkd_ref, 0), fold(dkd_ref, 1)], axis=1)
        dxk, gk_part = norm_bwd(k_ref[...], gk_ref[...], dkr, cos_v, sin_v, ones_ref)
        dk_ref[...] = dxk
        dv_ref[...] = jnp.concatenate([fold(dvd_ref, 0), fold(dvd_ref, 1)], axis=1)

        @pl.when(i == 0)
        def _():
            dgq_ref[...] = gq_part
            dgk_ref[...] = gk_part

        @pl.when(i > 0)
        def _():
            dgq_ref[...] += gq_part
            dgk_ref[...] += gk_part

    def chunk(col):
        return pl.BlockSpec((tm, CW), lambda i: (i, col))

    vec = pl.BlockSpec((1, CW), lambda i: (0, 0))
    tab = pl.BlockSpec((tm, CW), lambda i: (i, 0))
    return _call(body, name=name, grid=(s // tm,),
                 in_specs=[chunk(QC_COL), chunk(QC_COL + 1), chunk(QC_COL + 2), chunk(QC_COL + 3), chunk(KC_COL),
                           vec, vec, tab, tab, pl.BlockSpec((LANES, LANES), lambda i: (0, 0)),
                           pl.BlockSpec((tm, 4 * CW), lambda i: (i, 0)),
                           pl.BlockSpec((tm, 2 * CW), lambda i: (i, 0)),
                           pl.BlockSpec((tm, 2 * CW), lambda i: (i, 0))],
                 out_specs=[pl.BlockSpec((tm, 4 * CW), lambda i: (i, 0)), tab, tab, vec, vec],
                 out_shape=[_sds((s, 4 * CW)), _sds((s, CW)), _sds((s, CW)), _sds((1, CW)), _sds((1, CW))],
                 sem=("arbitrary",))(proj, proj, proj, proj, proj, gq, gk, cos, sin, ones, dqh, dkd, dvd)


def _flash_tiles(s):
    return min(512, s), min(512, s)


def _row_iota(shape):
    return lax.broadcasted_iota(jnp.int32, shape, 0)


def _flash_fwd(qh, kd, vdt, name):
    s = qh.shape[0]
    tq, tk = _flash_tiles(s)
    nk = s // tk

    def body(q_ref, k_ref, vt_ref, ot_ref, lse_ref, qm_ref, m_ref, l_ref, acc_ref):
        j = pl.program_id(2)
        top = _row_iota((LANES, tq)) < HEAD_DIM

        @pl.when(j == 0)
        def _():
            m_ref[...] = jnp.full(m_ref.shape, MASK_VALUE, F32)
            l_ref[...] = jnp.zeros_like(l_ref)
            acc_ref[...] = jnp.zeros_like(acc_ref)
            left_q = _lane_iota((tq, LANES)) < HEAD_DIM
            for t in range(2):
                qp = q_ref[:, t * LANES:(t + 1) * LANES]
                qm_ref[2 * t] = jnp.where(left_q, qp, jnp.zeros_like(qp))
                qm_ref[2 * t + 1] = jnp.where(left_q, jnp.zeros_like(qp), qp)

        kb = k_ref[...]
        vt = vt_ref[...]
        for t in range(2):
            pv, alpha = [], []
            for e in range(2):
                h = 2 * t + e
                st = _dot(kb, qm_ref[h], NT_DIMS)
                m_prev = m_ref[h]
                m_new = jnp.maximum(m_prev, jnp.max(st, axis=0, keepdims=True))
                a = jnp.exp(m_prev - m_new)
                pt = jnp.exp(st - m_new)
                l_ref[h] = a * l_ref[h] + jnp.sum(pt, axis=0, keepdims=True)
                m_ref[h] = m_new
                pv.append(_dot(vt, pt.astype(MXU_DT)))
                alpha.append(a)
            acc_ref[t] = acc_ref[t] * jnp.where(top, alpha[0], alpha[1]) + jnp.where(top, pv[0], pv[1])

        @pl.when(j == nk - 1)
        def _():
            for t in range(2):
                l0, l1 = l_ref[2 * t], l_ref[2 * t + 1]
                ot_ref[t * LANES:(t + 1) * LANES, :] = acc_ref[t] / jnp.where(top, l0, l1)
                lse_ref[2 * t:2 * t + 1, :] = m_ref[2 * t] + jnp.log(l0)
                lse_ref[2 * t + 1:2 * t + 2, :] = m_ref[2 * t + 1] + jnp.log(l1)

    return _call(body, name=name, grid=(4, s // tq, nk),
                 in_specs=[pl.BlockSpec((tq, CW), lambda g, i, j: (i, g)),
                           pl.BlockSpec((tk, LANES), lambda g, i, j: (j, g)),
                           pl.BlockSpec((LANES, tk), lambda g, i, j: (g, j))],
                 out_specs=[pl.BlockSpec((CW, tq), lambda g, i, j: (g, i)),
                            pl.BlockSpec((None, 4, tq), lambda g, i, j: (g, 0, i))],
                 out_shape=[_sds((4 * CW, s)), _sds((4, 4, s))],
                 scratch=[pltpu.VMEM((4, tq, LANES), MXU_DT), pltpu.VMEM((4, 1, tq), F32),
                          pltpu.VMEM((4, 1, tq), F32), pltpu.VMEM((2, LANES, tq), F32)],
                 sem=("parallel", "parallel", "arbitrary"))(qh, kd, vdt)


def _flash_bwd(qh, kd, vd, kdt, do, lse, dd, name):
    s = qh.shape[0]
    tq, tk = _flash_tiles(s)
    ni = s // tq

    def body(q_ref, k_ref, v_ref, kt_ref, do_ref, lse_ref, dd_ref, dqt_ref, dk_ref, dv_ref, dk_acc, dv_acc):
        j = pl.program_id(1)
        i = pl.program_id(2)

        @pl.when(i == 0)
        def _():
            dk_acc[...] = jnp.zeros_like(dk_acc)
            dv_acc[...] = jnp.zeros_like(dv_acc)

        kb = k_ref[...]
        vb = v_ref[...]
        kt = kt_ref[...]
        left_q = _lane_iota((tq, LANES)) < HEAD_DIM
        top = _row_iota((LANES, tq)) < HEAD_DIM
        cols = pl.ds(pl.multiple_of(i * tq, tq), tq)
        for t in range(2):
            qp = q_ref[:, t * LANES:(t + 1) * LANES]
            dop = do_ref[:, t * LANES:(t + 1) * LANES]
            dqt = []
            for e in range(2):
                h = 2 * t + e
                keep_q = left_q if e == 0 else jnp.logical_not(left_q)
                qm = jnp.where(keep_q, qp, jnp.zeros_like(qp))
                dom = jnp.where(keep_q, dop, jnp.zeros_like(dop))
                pt = jnp.exp(_dot(kb, qm, NT_DIMS) - lse_ref[h:h + 1, :])
                dpt = _dot(vb, dom, NT_DIMS)
                dsb = (pt * (dpt - dd_ref[h:h + 1, :])).astype(MXU_DT)
                dv_acc[...] += _dot(pt.astype(MXU_DT), dom)
                dk_acc[...] += _dot(dsb, qm)
                dqt.append(_dot(kt, dsb))
            dq_t = jnp.where(top, dqt[0], dqt[1])

            @pl.when(j == 0)
            def _():
                dqt_ref[t * LANES:(t + 1) * LANES, cols] = dq_t

            @pl.when(j > 0)
            def _():
                dqt_ref[t * LANES:(t + 1) * LANES, cols] += dq_t

        @pl.when(i == ni - 1)
        def _():
            dk_ref[...] = dk_acc[...]
            dv_ref[...] = dv_acc[...]

    qspec = pl.BlockSpec((tq, CW), lambda g, j, i: (i, g))
    kspec = pl.BlockSpec((tk, LANES), lambda g, j, i: (j, g))
    rowspec = pl.BlockSpec((None, 4, tq), lambda g, j, i: (g, 0, i))
    return _call(body, name=name, grid=(4, s // tk, ni),
                 in_specs=[qspec, kspec, kspec, pl.BlockSpec((LANES, tk), lambda g, j, i: (g, j)), qspec,
                           rowspec, rowspec],
                 out_specs=[pl.BlockSpec((CW, s), lambda g, j, i: (g, 0)), kspec, kspec],
                 out_shape=[_sds((4 * CW, s)), _sds((s, 2 * CW)), _sds((s, 2 * CW))],
                 scratch=[pltpu.VMEM((tk, LANES), F32), pltpu.VMEM((tk, LANES), F32)],
                 sem=("parallel", "arbitrary", "arbitrary"))(qh, kd, vd, kdt, do, lse, dd)


def _groupnorm_fwd(oa, ob, oc, ga, gb, gc, name):
    s = oa.shape[0]
    tm = _row_tile(s)
    wa, wb, wc = oa.shape[1], ob.shape[1], oc.shape[1]

    def body(oa_ref, ob_ref, oc_ref, ga_ref, gb_ref, gc_ref, mix_ref):
        off = 0
        for o_ref, g_ref, w in ((oa_ref, ga_ref, wa), (ob_ref, gb_ref, wb), (oc_ref, gc_ref, wc)):
            xv = o_ref[...]
            r = lax.rsqrt(jnp.mean(xv * xv, axis=-1, keepdims=True) + EPS)
            mix_ref[:, off:off + w] = ((xv * r) * g_ref[...]).astype(mix_ref.dtype)
            off += w

    def row(w):
        return pl.BlockSpec((tm, w), lambda i: (i, 0))

    def vec(w):
        return pl.BlockSpec((1, w), lambda i: (0, 0))

    return _call(body, name=name, grid=(s // tm,),
                 in_specs=[row(wa), row(wb), row(wc), vec(wa), vec(wb), vec(wc)],
                 out_specs=row(wa + wb + wc), out_shape=_sds((s, wa + wb + wc), MXU_DT),
                 sem=("parallel",))(oa, ob, oc, ga.reshape(1, wa), gb.reshape(1, wb), gc.reshape(1, wc))


def _groupnorm_bwd(dmix, oa, ob, oc, ga, gb, gc, ones, name):
    s = oa.shape[0]
    tm = _row_tile(s)
    wa, wb, wc = oa.shape[1], ob.shape[1], oc.shape[1]

    def body(dm_ref, oa_ref, ob_ref, oc_ref, ga_ref, gb_ref, gc_ref, ones_ref,
             doa_ref, dob_ref, doc_ref, docb_ref, dd_ref, dga_ref, dgb_ref, dgc_ref):
        i = pl.program_id(0)
        off = 0
        parts = []
        for o_ref, g_ref, do_ref, w in ((oa_ref, ga_ref, doa_ref, wa), (ob_ref, gb_ref, dob_ref, wb),
                                        (oc_ref, gc_ref, doc_ref, wc)):
            xv = o_ref[...]
            dh = dm_ref[:, off:off + w]
            r = lax.rsqrt(jnp.mean(xv * xv, axis=-1, keepdims=True) + EPS)
            xh = xv * r
            gd = dh * g_ref[...]
            c = jnp.mean(gd * xh, axis=-1, keepdims=True)
            dx = r * (gd - xh * c)
            do_ref[...] = dx
            parts.append(jnp.sum(dh * xh, axis=0, keepdims=True))
            if o_ref is oc_ref:
                docb_ref[...] = dx.astype(docb_ref.dtype)
                dd_ref[...] = _segsum64(dx * xv, ones_ref)
            off += w

        @pl.when(i == 0)
        def _():
            dga_ref[...], dgb_ref[...], dgc_ref[...] = parts

        @pl.when(i > 0)
        def _():
            dga_ref[...] += parts[0]
            dgb_ref[...] += parts[1]
            dgc_ref[...] += parts[2]

    def row(w):
        return pl.BlockSpec((tm, w), lambda i: (i, 0))

    def vec(w):
        return pl.BlockSpec((1, w), lambda i: (0, 0))

    return _call(body, name=name, grid=(s // tm,),
                 in_specs=[row(wa + wb + wc), row(wa), row(wb), row(wc), vec(wa), vec(wb), vec(wc),
                           pl.BlockSpec((LANES, LANES), lambda i: (0, 0))],
                 out_specs=[row(wa), row(wb), row(wc), row(wc), row(wc), vec(wa), vec(wb), vec(wc)],
                 out_shape=[_sds((s, wa)), _sds((s, wb)), _sds((s, wc)), _sds((s, wc), MXU_DT), _sds((s, wc)),
                            _sds((1, wa)), _sds((1, wb)), _sds((1, wc))],
                 sem=("arbitrary",))(dmix, oa, ob, oc, ga.reshape(1, wa), gb.reshape(1, wb), gc.reshape(1, wc), ones)


def _adam_math(w, g, m, v):
    m = ADAM_B1 * m + (1.0 - ADAM_B1) * g
    v = ADAM_B2 * v + (1.0 - ADAM_B2) * jnp.square(g)
    m_hat = m / (1.0 - ADAM_B1 ** ADAM_STEP)
    v_hat = v / (1.0 - ADAM_B2 ** ADAM_STEP)
    delta = -ADAM_LR * (m_hat / (jnp.sqrt(v_hat) + ADAM_EPS) + ADAM_WD * w)
    return delta, m, v


def _adamw(w, g, m, v, name):
    rows, cols = w.shape
    tr = min(256, rows)

    def body(w_ref, g_ref, m_ref, v_ref, d_ref, mo_ref, vo_ref):
        d_ref[...], mo_ref[...], vo_ref[...] = _adam_math(w_ref[...], g_ref[...], m_ref[...], v_ref[...])

    spec = pl.BlockSpec((tr, cols), lambda i: (i, 0))
    return _call(body, name=name, grid=(rows // tr,), in_specs=[spec] * 4, out_specs=[spec] * 3,
                 out_shape=[_sds((rows, cols))] * 3, sem=("parallel",))(w, g, m, v)


def _mesh_pos():
    return lax.axis_index("x"), lax.axis_index("y"), lax.axis_index("c")


def _peer_chips(x, y):
    return [(1 - x, y), (x, 1 - y), (1 - x, 1 - y)]


def _allgather_weights(shards):
    n = len(shards)

    def body(*refs):
        sh, out = refs[:n], refs[n:2 * n]
        send_sems, recv_sems, local_sems = refs[2 * n:]
        x, y, c = _mesh_pos()
        me = 2 * x + y
        sibling = (x, y, 1 - c)
        chips = _peer_chips(x, y)

        def half(t, chip, hc):
            h = sh[t].shape[1] // 2
            return out[t].at[:, chip, pl.ds(hc * h, h), :]

        def rcopy(t, k, src, dst, to):
            return pltpu.make_async_remote_copy(src_ref=src, dst_ref=dst, send_sem=send_sems.at[6 * t + k],
                                                recv_sem=recv_sems.at[6 * t + k], device_id=to,
                                                device_id_type=MESH_ID)

        local, first, passed = [], [], []
        for t in range(n):
            cp = pltpu.make_async_copy(sh[t], out[t].at[:, me], local_sems.at[t])
            cp.start()
            local.append(cp)
            h = sh[t].shape[1] // 2
            for k, (px, py) in enumerate(chips):
                cp = rcopy(t, k, sh[t].at[:, pl.ds(c * h, h), :], half(t, me, c), (px, py, c))
                cp.start()
                first.append(cp)
        for t in range(n):
            for k, (px, py) in enumerate(chips):
                blk = half(t, 2 * px + py, c)
                rcopy(t, k, blk, blk, sibling).wait_recv()
                cp = rcopy(t, 3 + k, blk, blk, sibling)
                cp.start()
                passed.append(cp)
        for t in range(n):
            for k, (px, py) in enumerate(chips):
                blk = half(t, 2 * px + py, 1 - c)
                rcopy(t, 3 + k, blk, blk, sibling).wait_recv()
        for cp in first + passed:
            cp.wait_send()
        for cp in local:
            cp.wait()

    return _call(body, name="allgather_weights", in_specs=[ANY] * n, out_specs=[ANY] * n,
                 out_shape=[_sds((w.shape[0], 4) + w.shape[1:], w.dtype) for w in shards],
                 scratch=[pltpu.SemaphoreType.DMA((6 * n,)), pltpu.SemaphoreType.DMA((6 * n,)),
                          pltpu.SemaphoreType.DMA((n,))])(*shards)


def _exchange_core_halves(grads):
    n = len(grads)

    def body(*refs):
        src, dst = refs[:n], refs[n:2 * n]
        send_sems, recv_sems = refs[2 * n:]
        x, y, c = _mesh_pos()
        copies = []
        for t in range(n):
            h = src[t].shape[1] // 2
            cp = pltpu.make_async_remote_copy(src_ref=src[t].at[:, pl.ds((1 - c) * h, h), :], dst_ref=dst[t],
                                              send_sem=send_sems.at[t], recv_sem=recv_sems.at[t],
                                              device_id=(x, y, 1 - c), device_id_type=MESH_ID)
            cp.start()
            copies.append(cp)
        for cp in copies:
            cp.wait()

    return _call(body, name="exchange_core_halves", in_specs=[ANY] * n, out_specs=[ANY] * n,
                 out_shape=[_sds((4, g.shape[1] // 2, g.shape[2])) for g in grads],
                 scratch=[pltpu.SemaphoreType.DMA((n,)), pltpu.SemaphoreType.DMA((n,))])(*grads)


def _add_half(g, r, c_idx, name):
    _, rows, cols = g.shape
    h = rows // 2
    tr = min(256, h)
    nb = h // tr

    def body(c_ref, g_ref, r_ref, o_ref):
        o_ref[...] = g_ref[...] + r_ref[...]

    return _call(body, name=name, grid=(4, nb), prefetch=1,
                 in_specs=[pl.BlockSpec((None, tr, cols), lambda s, i, c: (s, c[0] * nb + i, 0)),
                           pl.BlockSpec((None, tr, cols), lambda s, i, c: (s, i, 0))],
                 out_specs=pl.BlockSpec((None, tr, cols), lambda s, i, c: (s, i, 0)),
                 out_shape=_sds((4, h, cols)), sem=("parallel", "parallel"))(c_idx, g, r)


def _exchange_chips(parts):
    n = len(parts)

    def body(*refs):
        src, dst = refs[:n], refs[n:2 * n]
        send_sems, recv_sems = refs[2 * n:]
        x, y, c = _mesh_pos()
        copies = []
        for t in range(n):
            for k, (px, py) in enumerate(_peer_chips(x, y)):
                cp = pltpu.make_async_remote_copy(src_ref=src[t].at[2 * px + py], dst_ref=dst[t].at[k],
                                                  send_sem=send_sems.at[3 * t + k], recv_sem=recv_sems.at[3 * t + k],
                                                  device_id=(px, py, c), device_id_type=MESH_ID)
                cp.start()
                copies.append(cp)
        for cp in copies:
            cp.wait()

    return _call(body, name="exchange_chips", in_specs=[ANY] * n, out_specs=[ANY] * n,
                 out_shape=[_sds((3,) + p.shape[1:]) for p in parts],
                 scratch=[pltpu.SemaphoreType.DMA((3 * n,)), pltpu.SemaphoreType.DMA((3 * n,))])(*parts)


def _sum_chips(part, recv, me_idx, name):
    _, h, cols = part.shape
    tr = min(256, h)

    def body(me_ref, p_ref, r0_ref, r1_ref, r2_ref, o_ref):
        o_ref[...] = ((p_ref[...] + r0_ref[...]) + r1_ref[...]) + r2_ref[...]

    def slot(k):
        return pl.BlockSpec((None, tr, cols), lambda i, me: (k, i, 0))

    return _call(body, name=name, grid=(h // tr,), prefetch=1,
                 in_specs=[pl.BlockSpec((None, tr, cols), lambda i, me: (me[0], i, 0)), slot(0), slot(1), slot(2)],
                 out_specs=pl.BlockSpec((tr, cols), lambda i, me: (i, 0)),
                 out_shape=_sds((h, cols)), sem=("parallel",))(me_idx, part, recv, recv, recv)


def _share_halves(halves, n_layers):
    n = len(halves)
    n_types = n // n_layers

    def body(*refs):
        src, out = refs[:n], refs[n:n + n_types]
        send_sems, recv_sems, local_sems = refs[n + n_types:]
        x, y, c = _mesh_pos()
        copies = []
        for t in range(n):
            ty, layer = divmod(t, n_layers)
            h = src[t].shape[0]
            dst = out[ty].at[layer, pl.ds(c * h, h), :]
            lc = pltpu.make_async_copy(src[t], dst, local_sems.at[t])
            lc.start()
            rc = pltpu.make_async_remote_copy(src_ref=src[t], dst_ref=dst, send_sem=send_sems.at[t],
                                              recv_sem=recv_sems.at[t], device_id=(x, y, 1 - c),
                                              device_id_type=MESH_ID)
            rc.start()
            copies.append((lc, rc))
        for lc, rc in copies:
            lc.wait()
            rc.wait()

    out_shape = [_sds((n_layers, 2 * halves[ty * n_layers].shape[0], halves[ty * n_layers].shape[1]))
                 for ty in range(n_types)]
    return _call(body, name="share_halves", in_specs=[ANY] * n, out_specs=[ANY] * n_types, out_shape=out_shape,
                 scratch=[pltpu.SemaphoreType.DMA((n,)), pltpu.SemaphoreType.DMA((n,)),
                          pltpu.SemaphoreType.DMA((n,))])(*halves)


def _allreduce_small_adam(g, w, m, v):
    rows = g.shape[0]

    def body(g_ref, w_ref, m_ref, v_ref, gs_ref, d_ref, mo_ref, vo_ref, buf, send_sems, recv_sems):
        x, y, c = _mesh_pos()
        me = 4 * x + 2 * y + c
        buf[me] = g_ref[...]
        copies = []
        for k in range(1, 8):
            px = 1 - x if (k >> 2) & 1 else x
            py = 1 - y if (k >> 1) & 1 else y
            pc = 1 - c if k & 1 else c
            cp = pltpu.make_async_remote_copy(src_ref=g_ref, dst_ref=buf.at[me], send_sem=send_sems.at[k - 1],
                                              recv_sem=recv_sems.at[k - 1], device_id=(px, py, pc),
                                              device_id_type=MESH_ID)
            cp.start()
            copies.append(cp)
        for cp in copies:
            cp.wait()
        total = buf[0]
        for d in range(1, 8):
            total = total + buf[d]
        gs_ref[...] = total
        d_ref[...], mo_ref[...], vo_ref[...] = _adam_math(w_ref[...], total, m_ref[...], v_ref[...])

    vm = pl.BlockSpec(memory_space=pltpu.VMEM)
    return _call(body, name="allreduce_small_adam", in_specs=[vm] * 4, out_specs=[vm] * 4,
                 out_shape=[_sds((rows, LANES))] * 4,
                 scratch=[pltpu.VMEM((8, rows, LANES), F32), pltpu.SemaphoreType.DMA((7,)),
                          pltpu.SemaphoreType.DMA((7,))])(g, w, m, v)


HBM_SPEC = pl.BlockSpec(memory_space=pltpu.HBM)
SEM_SPEC = pl.BlockSpec(memory_space=pltpu.SEMAPHORE)
VMEM_SPEC = pl.BlockSpec(memory_space=pltpu.VMEM)
SIDE_EFFECT = pltpu.SideEffectType.DATAFLOW_SIDE_EFFECTING
N_PEERS = 7


def _in_hbm(a):
    return pltpu.with_memory_space_constraint(a, pltpu.HBM)


def _landing(shape, dtype):
    return _in_hbm(lax.empty(shape, dtype))


def _token_shape():
    return _sds((8, LANES))


def _gather_start(shards):
    n = len(shards)
    n_layers = shards[0].shape[0]
    jobs = [(l, t) for l in range(n_layers) for t in range(n)]
    nj = len(jobs)

    def body(*refs):
        sh = refs[:n]
        outs = refs[n + nj:]
        send, recv, land, token = outs[:nj], outs[nj:2 * nj], outs[2 * nj:3 * nj], outs[3 * nj]
        x, y, c = _mesh_pos()
        me = 2 * x + y
        for j, (l, t) in enumerate(jobs):
            for k, (px, py) in enumerate(_peer_chips(x, y)):
                pltpu.make_async_remote_copy(src_ref=sh[t].at[l], dst_ref=land[j].at[me], send_sem=send[j].at[k],
                                             recv_sem=recv[j].at[k], device_id=(px, py, c),
                                             device_id_type=MESH_ID).start()
        token[...] = jnp.zeros_like(token)

    lands = [_landing((4,) + shards[t].shape[1:], shards[t].dtype) for _, t in jobs]
    res = pl.pallas_call(
        body, name="gather_start",
        out_shape=tuple([pltpu.SemaphoreType.DMA((3,))] * (2 * nj)
                        + [pltpu.HBM(a.shape, a.dtype) for a in lands] + [_token_shape()]),
        in_specs=[HBM_SPEC] * (n + nj), out_specs=tuple([SEM_SPEC] * (2 * nj) + [HBM_SPEC] * nj + [VMEM_SPEC]),
        input_output_aliases={n + j: 2 * nj + j for j in range(nj)},
        compiler_params=pltpu.CompilerParams(has_side_effects=SIDE_EFFECT),
    )(*[_in_hbm(a) for a in shards], *lands)
    return jobs, res[:nj], res[nj:2 * nj], res[2 * nj:3 * nj], res[3 * nj]


def _gather_wait(shard, layer, land, send_sem, recv_sem, after, name):
    def body(sh_ref, land_ref, send_ref, recv_ref, after_ref, land_out):
        x, y, c = _mesh_pos()
        for k in range(3):
            cp = pltpu.make_async_remote_copy(src_ref=sh_ref.at[layer], dst_ref=land_ref.at[k],
                                              send_sem=send_ref.at[k], recv_sem=recv_ref.at[k],
                                              device_id=(x, y, 1 - c), device_id_type=MESH_ID)
            cp.wait_send()
            cp.wait_recv()

    return pl.pallas_call(
        body, name=name, out_shape=pltpu.HBM(land.shape, land.dtype),
        in_specs=[HBM_SPEC, HBM_SPEC, SEM_SPEC, SEM_SPEC, ANY], out_specs=HBM_SPEC,
        input_output_aliases={1: 0},
        compiler_params=pltpu.CompilerParams(has_side_effects=SIDE_EFFECT),
    )(shard, land, send_sem, recv_sem, after)


def _grad_start(g, name):
    def body(g_ref, land_in, send, recv, land, token):
        x, y, c = _mesh_pos()
        me = 2 * x + y
        pltpu.make_async_remote_copy(src_ref=g_ref.at[me], dst_ref=land.at[0], send_sem=send.at[0],
                                     recv_sem=recv.at[0], device_id=(x, y, 1 - c), device_id_type=MESH_ID).start()
        for k, (px, py) in enumerate(_peer_chips(x, y)):
            for c2 in range(2):
                pltpu.make_async_remote_copy(src_ref=g_ref.at[2 * px + py], dst_ref=land.at[1 + 2 * k + c],
                                             send_sem=send.at[1 + 2 * k + c2], recv_sem=recv.at[1 + 2 * k + c],
                                             device_id=(px, py, c2), device_id_type=MESH_ID).start()
        token[...] = jnp.zeros_like(token)

    land = _landing((N_PEERS,) + g.shape[1:], g.dtype)
    return pl.pallas_call(
        body, name=name,
        out_shape=(pltpu.SemaphoreType.DMA((N_PEERS,)), pltpu.SemaphoreType.DMA((N_PEERS,)),
                   pltpu.HBM(land.shape, land.dtype), _token_shape()),
        in_specs=[HBM_SPEC, HBM_SPEC], out_specs=(SEM_SPEC, SEM_SPEC, HBM_SPEC, VMEM_SPEC),
        input_output_aliases={1: 2},
        compiler_params=pltpu.CompilerParams(has_side_effects=SIDE_EFFECT),
    )(_in_hbm(g), land)


def _grad_wait(g, land, send_sem, recv_sem, after, name):
    def body(g_ref, land_ref, send_ref, recv_ref, after_ref, land_out):
        x, y, c = _mesh_pos()
        for k in range(N_PEERS):
            cp = pltpu.make_async_remote_copy(src_ref=g_ref.at[0], dst_ref=land_ref.at[k], send_sem=send_ref.at[k],
                                              recv_sem=recv_ref.at[k], device_id=(x, y, 1 - c),
                                              device_id_type=MESH_ID)
            cp.wait_send()
            cp.wait_recv()

    return pl.pallas_call(
        body, name=name, out_shape=pltpu.HBM(land.shape, land.dtype),
        in_specs=[HBM_SPEC, HBM_SPEC, SEM_SPEC, SEM_SPEC, ANY], out_specs=HBM_SPEC,
        input_output_aliases={1: 0},
        compiler_params=pltpu.CompilerParams(has_side_effects=SIDE_EFFECT),
    )(g, land, send_sem, recv_sem, after)


def _sum_adam(g, land, w, m, v, prev, layer, me_idx, name):
    _, r, cols = g.shape
    tr = min(128, r)

    def body(me_ref, g_ref, l0, l1, l2, l3, l4, l5, l6, w_ref, m_ref, v_ref, p0, p1, p2, p3,
             go_ref, d_ref, mo_ref, vo_ref):
        total = g_ref[...].astype(F32) + l0[...].astype(F32)
        for ref in (l1, l2, l3, l4, l5, l6):
            total = total + ref[...].astype(F32)
        go_ref[...] = total
        d_ref[...], mo_ref[...], vo_ref[...] = _adam_math(w_ref[...], total, m_ref[...], v_ref[...])

    def slot(k):
        return pl.BlockSpec((None, tr, cols), lambda i, me: (k, i, 0))

    lay = pl.BlockSpec((None, tr, cols), lambda i, me: (layer, i, 0))
    return _call(body, name=name, grid=(r // tr,), prefetch=1,
                 in_specs=[pl.BlockSpec((None, tr, cols), lambda i, me: (me[0], i, 0))]
                 + [slot(k) for k in range(N_PEERS)] + [lay, lay, lay] + [ANY] * 4,
                 out_specs=[lay] * 4, out_shape=[_sds(w.shape)] * 4,
                 aliases={12 + k: k for k in range(4)}, sem=("parallel",))(
                     me_idx, g, *([land] * N_PEERS), w, m, v, *prev)


def _t5_bucket(rel):
    nb = T5_BUCKETS // 2
    max_exact = nb // 2
    base = jnp.where(rel > 0, nb, 0)
    n = jnp.abs(rel)
    nf = jnp.maximum(n, 1).astype(F32)
    large = max_exact + (jnp.log(nf / max_exact) / math.log(T5_MAX_DIST / max_exact)
                         * (nb - max_exact)).astype(jnp.int32)
    large = jnp.minimum(large, nb - 1)
    return base + jnp.where(n < max_exact, n, large)


def _a_bias_maps():
    v = jnp.arange(3)[:, None, None]
    q = jnp.arange(128)[None, :, None]
    k = jnp.arange(384)[None, None, :]
    rel = k - 128 * v - q
    valid = jnp.abs(rel) <= 128
    onehot = (_t5_bucket(rel)[..., None] == jnp.arange(T5_BUCKETS)).astype(F32)
    return onehot * valid[..., None].astype(F32), valid


def _b_bias_maps():
    v = jnp.arange(8)[:, None]
    i = jnp.arange(NA_ROWS)[None, :]
    dr = jnp.where(v == 4, i + 3, i - v + 7)
    row_oh = (dr[..., None] == jnp.arange(2 * NA_ROWS - 1)).astype(F32)
    q = jnp.arange(GRID_W)[:, None]
    kc = jnp.arange(GRID_W)[None, :]
    cs = jnp.clip(q - 8, 0, GRID_W - 16)
    valid = (kc >= cs) & (kc < cs + 16)
    col_oh = ((kc - q + 15)[..., None] == jnp.arange(31)).astype(F32) * valid[..., None].astype(F32)
    return row_oh, col_oh, valid


def _rope_tables(s):
    t = jnp.arange(s)
    row = (t // GRID_W).astype(F32)
    col = (t % GRID_W).astype(F32)
    axis_dim = HEAD_DIM // 2
    freqs = ROPE_THETA ** (-jnp.arange(0, axis_dim, 2, dtype=F32) / axis_dim)
    ang_row = row[:, None] * freqs[None, :]
    ang_col = col[:, None] * freqs[None, :]
    cos = jnp.concatenate([jnp.cos(ang_row)] * 2 + [jnp.cos(ang_col)] * 2, axis=1)
    sin = jnp.concatenate([-jnp.sin(ang_row), jnp.sin(ang_row), -jnp.sin(ang_col), jnp.sin(ang_col)], axis=1)
    return jnp.tile(cos, (1, CW // HEAD_DIM)), jnp.tile(sin, (1, CW // HEAD_DIM))


def _pack(parts, rows):
    flat = jnp.concatenate([p.reshape(-1).astype(F32) for p in parts])
    return jnp.pad(flat, (0, rows * LANES - flat.shape[0])).reshape(rows, LANES)


def _unpack(buf, shapes):
    flat = buf.reshape(-1)
    out, off = [], 0
    for shp in shapes:
        size = math.prod(shp)
        out.append(flat[off:off + size].reshape(shp))
        off += size
    return out


def kernel(x, norm_mix, w_in, a_sink, t5_table, b_rpb, c_q_gain, c_k_gain, out_gain_a, out_gain_b, out_gain_c, w_o, norm_mlp, w_up, w_down, norm_final, loss_target, m_norm_mix, m_w_in, m_a_sink, m_t5_table, m_b_rpb, m_c_q_gain, m_c_k_gain, m_out_gain_a, m_out_gain_b, m_out_gain_c, m_w_o, m_norm_mlp, m_w_up, m_w_down, m_norm_final, v_norm_mix, v_w_in, v_a_sink, v_t5_table, v_b_rpb, v_c_q_gain, v_c_k_gain, v_out_gain_a, v_out_gain_b, v_out_gain_c, v_w_o, v_norm_mlp, v_w_up, v_w_down, v_norm_final):
    n_layers = w_in.shape[0]
    s, d = x.shape[1], x.shape[2]
    d_ff = 4 * w_up.shape[2]
    in_w = 4 * w_in.shape[2]
    xs = x.reshape(s, d)
    target = loss_target.reshape(s, d)
    cfg_a, cfg_b = _cfg_a(s), _cfg_b(s)

    x_i, y_i, _ = _mesh_pos()
    me_chip = 2 * x_i + y_i
    me_idx = me_chip.astype(jnp.int32).reshape(1)
    w_bf = [w_in.astype(MXU_DT), w_o.astype(MXU_DT), w_up.astype(MXU_DT), w_down.astype(MXU_DT)]
    jobs, gather_send, gather_recv, gather_land, gather_token = _gather_start(w_bf)
    job_of = {job: j for j, job in enumerate(jobs)}
    ff_shard = w_up.shape[2]

    def gathered(l, t, after):
        j = job_of[(l, t)]
        land = _gather_wait(w_bf[t], l, gather_land[j], gather_send[j], gather_recv[j], after,
                            "gather_wait_%d_%d" % (l, t))
        return lax.dynamic_update_slice(land, w_bf[t][l][None], (me_chip, 0, 0))

    ones = _pair_ones()
    cos_t, sin_t = _rope_tables(s)
    a_onehot, a_valid = _a_bias_maps()
    bias_a = jnp.where(a_valid[:, None], jnp.einsum("vqkb,bh->vhqk", a_onehot, t5_table, precision=HIGHEST),
                       MASK_VALUE)
    row_oh, col_oh, b_valid = _b_bias_maps()
    sink_b = jnp.full((4, 1, LANES), MASK_VALUE, F32)

    def b_bias(rpb):
        t = jnp.einsum("hrz,vir->vhiz", rpb, row_oh, precision=HIGHEST)
        t = jnp.einsum("vhiz,qcz->vhqic", t, col_oh, precision=HIGHEST)
        t = jnp.where(b_valid[None, None, :, None, :], t, MASK_VALUE)
        return t.reshape(8, 8, GRID_W, NA_ROWS * GRID_W)

    def tile_gain(gvec):
        return jnp.tile(gvec, CW // HEAD_DIM).reshape(1, CW)

    def pad_sink(svec):
        return jnp.pad(svec, (0, LANES - svec.shape[0])).reshape(1, 1, LANES)

    saved = []
    xc = xs
    for l in range(n_layers):
        h1 = _rms_fwd(xc, norm_mix[l] + gather_token[0, 0] if l == 0 else norm_mix[l], "rms_mix")
        wf_in = gathered(l, 0, h1).transpose(1, 0, 2).reshape(d, in_w)
        proj = _matmul(h1, wf_in, mode="nn", name="proj_in", tm=1024, tn=768, tk=2048)
        bias_b = b_bias(b_rpb[l])
        oa = _local_attn_fwd(proj, bias_a, pad_sink(a_sink[l]), cfg_a, "attn_a_fwd")
        ob = _local_attn_fwd(proj, bias_b, sink_b, cfg_b, "attn_b_fwd")
        gq, gk = tile_gain(c_q_gain[l]), tile_gain(c_k_gain[l])
        qh, kd, vd = _cprep_fwd(proj, gq, gk, cos_t, sin_t, ones, "cprep_fwd")
        kdt, vdt = kd.T, vd.T
        oct, lse = _flash_fwd(qh, kd, vdt, "attn_c_fwd")
        oc = oct.T
        mix = _groupnorm_fwd(oa, ob, oc, out_gain_a[l], out_gain_b[l], out_gain_c[l], "groupnorm_fwd")
        wf_o = gathered(l, 1, mix).reshape(d, d)
        x_mid = _matmul(mix, wf_o, mode="nn", name="proj_out", tm=1024, tn=1024, tk=2048, epi="res",
                        extra=(xc,))
        h2 = _rms_fwd(x_mid, norm_mlp[l], "rms_mlp")
        wg_up = gathered(l, 2, h2)
        nb_up = ff_shard // 1024
        u, uu = _matmul(h2, wg_up, mode="nn", name="mlp_up", tm=1024, tn=1024, tk=2048, epi="relu2",
                        out_dtypes=(F32, MXU_DT), mkn=(s, d, d_ff),
                        b_spec=pl.BlockSpec((None, 2048, 1024), lambda i, j, kk: (j // nb_up, kk, j % nb_up)))
        wf_down = gathered(l, 3, uu).reshape(d_ff, d)
        x_out = _matmul(uu, wf_down, mode="nn", name="mlp_down", tm=1024, tn=1024, tk=2048, epi="res",
                        extra=(x_mid,))
        saved.append((xc, h1, proj, bias_b, oa, ob, qh, kd, vd, kdt, oc, lse, mix, x_mid, h2, u, uu,
                      wf_in, wf_o, wg_up, wf_down))
        xc = x_out

    loss_part, dx, dxb, dg_final = _final_loss(xc, norm_final, target, "final_loss")

    small = {k: [] for k in ("norm_mix", "a_sink", "b_rpb", "cq", "ck", "oga", "ogb", "ogc", "norm_mlp")}
    dbias_a_total = jnp.zeros_like(bias_a)
    big_w = {"w_in": (w_in, m_w_in, v_w_in), "w_o": (w_o, m_w_o, v_w_o), "w_up": (w_up, m_w_up, v_w_up),
             "w_down": (w_down, m_w_down, v_w_down)}
    big = {nm: [lax.empty(wmv[0].shape, F32) for _ in range(4)] for nm, wmv in big_w.items()}

    def send_grad(nm, l, g):
        send, recv, land, token = _grad_start(g, "grad_start_%s_%d" % (nm, l))
        return (nm, l, g, send, recv, land), token[0, 0]

    def finish_grads(pending, after):
        for nm, l, g, send, recv, land in pending:
            land = _grad_wait(g, land, send, recv, after, "grad_wait_%s_%d" % (nm, l))
            wmv = big_w[nm]
            big[nm] = _sum_adam(g, land, wmv[0], wmv[1], wmv[2], big[nm], l, me_idx, "sum_adam_%s_%d" % (nm, l))

    pending = []
    for l in reversed(range(n_layers)):
        (xin, h1, proj, bias_b, oa, ob, qh, kd, vd, kdt, oc, lse, mix, x_mid, h2, u, uu,
         wf_in, wf_o, wg_up, wf_down) = saved[l]
        started = []
        du = _matmul(dxb, wf_down, mode="nt", name="mlp_down_dgrad", tm=1024, tn=1024, tk=2048, epi="mul2u",
                     extra=(u,), out_dtypes=(MXU_DT,))
        gw = _matmul(uu, dxb, mode="tn", name="mlp_down_wgrad", tm=1024, tn=1024, tk=1024, out_dtypes=(GRAD_DT,))
        rec, tok_down = send_grad("w_down", l, gw.reshape(4, d_ff // 4, d))
        started.append(rec)
        nbk = ff_shard // 2048
        dh2 = _matmul(du, wg_up, mode="nt", name="mlp_up_dgrad", tm=1024, tn=1024, tk=2048,
                      mkn=(s, d_ff, d),
                      b_spec=pl.BlockSpec((None, 1024, 2048), lambda i, j, kk: (kk // nbk, j, kk % nbk)))
        nbo = ff_shard // 1024
        gw = _matmul(h2, du, mode="tn", name="mlp_up_wgrad", tm=1024, tn=1024, tk=1024, out_dtypes=(GRAD_DT,),
                     out_spec=pl.BlockSpec((None, 1024, 1024), lambda i, j, kk: (j // nbo, i, j % nbo)),
                     out_shape=(4, d, ff_shard))
        rec, tok_up = send_grad("w_up", l, gw)
        started.append(rec)
        dx_mid, dxmb, dg = _rms_bwd(x_mid, norm_mlp[l] + (tok_down + tok_up), dh2, dx, "rms_mlp_bwd")
        small["norm_mlp"].append(dg)
        dmix = _matmul(dxmb, wf_o, mode="nt", name="proj_out_dgrad", tm=1024, tn=1024, tk=2048)
        gw = _matmul(mix, dxmb, mode="tn", name="proj_out_wgrad", tm=1024, tn=1024, tk=1024, out_dtypes=(GRAD_DT,))
        rec, tok_o = send_grad("w_o", l, gw.reshape(4, d // 4, d))
        started.append(rec)
        doa, dob, doc, docb, ddc, dga, dgb, dgc = _groupnorm_bwd(
            dmix, oa, ob, oc, out_gain_a[l] + tok_o, out_gain_b[l], out_gain_c[l], ones, "groupnorm_bwd")
        small["oga"].append(dga)
        small["ogb"].append(dgb)
        small["ogc"].append(dgc)
        dqa, dka, dva, dbias_a, dsink = _local_attn_bwd(proj, bias_a, pad_sink(a_sink[l]), doa, cfg_a, "attn_a_bwd")
        dbias_a_total = dbias_a_total + dbias_a
        small["a_sink"].append(dsink[0, 0, :a_sink.shape[1]])
        dqb, dkb, dvb, dbias_b, _ = _local_attn_bwd(proj, bias_b, sink_b, dob, cfg_b, "attn_b_bwd")
        db5 = jnp.where(b_valid[None, None, :, None, :], dbias_b.reshape(8, 8, GRID_W, NA_ROWS, GRID_W), 0.0)
        t = jnp.einsum("vhqic,qcz->vhiz", db5, col_oh, precision=HIGHEST)
        small["b_rpb"].append(jnp.einsum("vhiz,vir->hrz", t, row_oh, precision=HIGHEST))
        dd_rows = ddc.reshape(s, 16, HEAD_DIM)[:, :, 0].T.reshape(4, 4, s)
        dqht, dkd, dvd = _flash_bwd(qh, kd, vd, kdt, docb, lse, dd_rows, "attn_c_bwd")
        dqh = dqht.T
        gq, gk = tile_gain(c_q_gain[l]), tile_gain(c_k_gain[l])
        dqc, dkc, dvc, dgq, dgk = _cprep_bwd(proj, gq, gk, cos_t, sin_t, ones, dqh, dkd, dvd, "cprep_bwd")
        small["cq"].append(dgq.reshape(CW // HEAD_DIM, HEAD_DIM).sum(0))
        small["ck"].append(dgk.reshape(CW // HEAD_DIM, HEAD_DIM).sum(0))
        dproj = jnp.concatenate([dqa, dka, dva, dqb, dkb, dvb, dqc, dkc, dvc], axis=1).astype(MXU_DT)
        dh1 = _matmul(dproj, wf_in, mode="nt", name="proj_in_dgrad", tm=1024, tn=1024, tk=1920)
        gw = _matmul(h1, dproj, mode="tn", name="proj_in_wgrad", tm=1024, tn=768, tk=1024, out_dtypes=(GRAD_DT,))
        rec, tok_in = send_grad("w_in", l, gw.reshape(d, 4, in_w // 4).transpose(1, 0, 2))
        started.append(rec)
        dx, dxb, dg = _rms_bwd(xin, norm_mix[l] + tok_in, dh1, dx_mid, "rms_mix_bwd")
        small["norm_mix"].append(dg)
        finish_grads(pending, dx)
        pending = started
    finish_grads(pending, dx)

    for lst in small.values():
        lst.reverse()

    dt5 = jnp.einsum("vhqk,vqkb->bh", dbias_a_total, a_onehot, precision=HIGHEST)
    small_names = ["norm_mix", "a_sink", "t5_table", "b_rpb", "c_q_gain", "c_k_gain", "out_gain_a", "out_gain_b",
                   "out_gain_c", "norm_mlp", "norm_final"]
    small_w = [norm_mix, a_sink, t5_table, b_rpb, c_q_gain, c_k_gain, out_gain_a, out_gain_b, out_gain_c, norm_mlp,
               norm_final]
    small_m = [m_norm_mix, m_a_sink, m_t5_table, m_b_rpb, m_c_q_gain, m_c_k_gain, m_out_gain_a, m_out_gain_b,
               m_out_gain_c, m_norm_mlp, m_norm_final]
    small_v = [v_norm_mix, v_a_sink, v_t5_table, v_b_rpb, v_c_q_gain, v_c_k_gain, v_out_gain_a, v_out_gain_b,
               v_out_gain_c, v_norm_mlp, v_norm_final]
    small_g = [jnp.stack(small["norm_mix"]), jnp.stack(small["a_sink"]), dt5, jnp.stack(small["b_rpb"]),
               jnp.stack(small["cq"]), jnp.stack(small["ck"]), jnp.stack(small["oga"]), jnp.stack(small["ogb"]),
               jnp.stack(small["ogc"]), jnp.stack(small["norm_mlp"]), dg_final]
    shapes = [w.shape for w in small_w]
    total = sum(math.prod(shp) for shp in shapes) + 1
    rows = -(-total // (8 * LANES)) * 8
    one = [jnp.ones((1,), F32)]
    gs, dl, mo, vo = _allreduce_small_adam(_pack(small_g + [loss_part[0, :1]], rows), _pack(small_w + one, rows),
                                           _pack(small_m + one, rows), _pack(small_v + one, rows))
    sg = _unpack(gs, shapes + [(1,)])
    sd, sm, sv = _unpack(dl, shapes), _unpack(mo, shapes), _unpack(vo, shapes)
    loss = sg[-1].reshape(())

    by_name = {nm: (sg[i], sd[i], sm[i], sv[i]) for i, nm in enumerate(small_names)}
    by_name.update(big)
    order = ["norm_mix", "w_in", "a_sink", "t5_table", "b_rpb", "c_q_gain", "c_k_gain", "out_gain_a", "out_gain_b",
             "out_gain_c", "w_o", "norm_mlp", "w_up", "w_down", "norm_final"]
    outs = [loss, dx.reshape(x.shape)]
    for field in range(4):
        outs.extend(by_name[nm][field] for nm in order)
    return tuple(outs)
```

```python
import functools
import math

import jax
import jax.numpy as jnp
from jax import lax
from jax.experimental import pallas as pl
from jax.experimental.pallas import tpu as pltpu

F32 = jnp.float32
MXU_DT = jnp.bfloat16
GRAD_DT = jnp.bfloat16
HIGHEST = lax.Precision.HIGHEST

HEAD_DIM = 64
LANES = 128
EPS = 1e-6
MASK_VALUE = -1e30
GRID_W = 64
NA_ROWS = 8
T5_BUCKETS = 32
T5_MAX_DIST = 128
ROPE_THETA = 10000.0
ADAM_LR, ADAM_B1, ADAM_B2, ADAM_EPS, ADAM_WD, ADAM_STEP = 0.001, 0.9, 0.999, 1e-08, 0.01, 10
VMEM_LIMIT = 56 * 1024 * 1024

MESH_ID = pl.DeviceIdType.MESH
ANY = pl.BlockSpec(memory_space=pl.ANY)

NT_DIMS = (((1,), (1,)), ((), ()))
TN_DIMS = (((0,), (0,)), ((), ()))
NN_DIMS = (((1,), (0,)), ((), ()))


def _dot(a, b, dims=NN_DIMS):
    return lax.dot_general(a, b, dims, preferred_element_type=F32)


def _call(body, *, name, out_shape, grid=(), in_specs=None, out_specs=None, scratch=(), sem=None,
          prefetch=0, aliases=None):
    params = {"vmem_limit_bytes": VMEM_LIMIT}
    if sem is not None:
        params["dimension_semantics"] = sem
    kwargs = {}
    if aliases:
        kwargs["input_output_aliases"] = aliases
    if prefetch:
        spec = pltpu.PrefetchScalarGridSpec(num_scalar_prefetch=prefetch, grid=grid, in_specs=in_specs,
                                            out_specs=out_specs, scratch_shapes=list(scratch))
        return pl.pallas_call(body, grid_spec=spec, out_shape=out_shape, name=name,
                              compiler_params=pltpu.CompilerParams(**params), **kwargs)
    return pl.pallas_call(body, grid=grid, in_specs=in_specs, out_specs=out_specs, out_shape=out_shape,
                          scratch_shapes=list(scratch), name=name,
                          compiler_params=pltpu.CompilerParams(**params), **kwargs)


def _sds(shape, dtype=F32):
    return jax.ShapeDtypeStruct(tuple(shape), dtype)


def _matmul(a, b, *, mode, name, tm, tn, tk, epi="plain", extra=(), out_dtypes=(F32,), mkn=None,
            b_spec=None, out_spec=None, out_shape=None):
    if mkn is None:
        if mode == "nn":
            (m, k), n = a.shape, b.shape[1]
        elif mode == "nt":
            (m, k), n = a.shape, b.shape[0]
        else:
            (k, m), n = a.shape, b.shape[1]
    else:
        m, k, n = mkn
    tm, tn, tk = min(tm, m), min(tn, n), min(tk, k)
    assert m % tm == 0 and n % tn == 0 and k % tk == 0, (name, m, n, k, tm, tn, tk)
    nk = k // tk
    dims = {"nn": NN_DIMS, "nt": NT_DIMS, "tn": TN_DIMS}[mode]
    n_extra, n_out = len(extra), len(out_dtypes)

    def body(a_ref, b_ref, *rest):
        extra_refs = rest[:n_extra]
        out_refs = rest[n_extra:n_extra + n_out]
        acc_ref = rest[n_extra + n_out]
        kk = pl.program_id(2)

        @pl.when(kk == 0)
        def _():
            acc_ref[...] = jnp.zeros_like(acc_ref)

        acc_ref[...] += _dot(a_ref[...].astype(MXU_DT), b_ref[...].astype(MXU_DT), dims)

        @pl.when(kk == nk - 1)
        def _():
            acc = acc_ref[...]
            if epi == "plain":
                out_refs[0][...] = acc.astype(out_refs[0].dtype)
            elif epi == "res":
                out_refs[0][...] = (extra_refs[0][...] + acc).astype(out_refs[0].dtype)
            elif epi == "relu2":
                u = jnp.maximum(acc, 0.0)
                out_refs[0][...] = u.astype(out_refs[0].dtype)
                out_refs[1][...] = (u * u).astype(out_refs[1].dtype)
            elif epi == "mul2u":
                out_refs[0][...] = (2.0 * extra_refs[0][...] * acc).astype(out_refs[0].dtype)
            else:
                raise ValueError(epi)

    if mode == "tn":
        a_spec = pl.BlockSpec((tk, tm), lambda i, j, kk: (kk, i))
    else:
        a_spec = pl.BlockSpec((tm, tk), lambda i, j, kk: (i, kk))
    if b_spec is None:
        if mode == "nt":
            b_spec = pl.BlockSpec((tn, tk), lambda i, j, kk: (j, kk))
        else:
            b_spec = pl.BlockSpec((tk, tn), lambda i, j, kk: (kk, j))
    mn_spec = pl.BlockSpec((tm, tn), lambda i, j, kk: (i, j))
    if out_spec is None:
        out_spec = mn_spec
    if out_shape is None:
        out_shape = (m, n)
    res = _call(body, name=name, grid=(m // tm, n // tn, nk),
                in_specs=[a_spec, b_spec] + [mn_spec] * n_extra,
                out_specs=[out_spec] * n_out,
                out_shape=[_sds(out_shape, d) for d in out_dtypes],
                scratch=[pltpu.VMEM((tm, tn), F32)],
                sem=("parallel", "parallel", "arbitrary"))(a, b, *extra)
    return res if n_out > 1 else res[0]


def _row_tile(s):
    return min(512, s)


def _rms_fwd(x, g, name):
    s, d = x.shape
    tm = _row_tile(s)

    def body(x_ref, g_ref, h_ref):
        xv = x_ref[...]
        r = lax.rsqrt(jnp.mean(xv * xv, axis=-1, keepdims=True) + EPS)
        h_ref[...] = ((xv * r) * g_ref[...]).astype(h_ref.dtype)

    return _call(body, name=name, grid=(s // tm,),
                 in_specs=[pl.BlockSpec((tm, d), lambda i: (i, 0)), pl.BlockSpec((1, d), lambda i: (0, 0))],
                 out_specs=pl.BlockSpec((tm, d), lambda i: (i, 0)),
                 out_shape=_sds((s, d), MXU_DT), sem=("parallel",))(x, g.reshape(1, d))


def _rms_bwd(x, g, dh, dres, name):
    s, d = x.shape
    tm = _row_tile(s)

    def body(x_ref, g_ref, dh_ref, dres_ref, dx_ref, dxb_ref, dg_ref):
        i = pl.program_id(0)
        xv = x_ref[...]
        r = lax.rsqrt(jnp.mean(xv * xv, axis=-1, keepdims=True) + EPS)
        xh = xv * r
        dhv = dh_ref[...]
        gd = dhv * g_ref[...]
        c = jnp.mean(gd * xh, axis=-1, keepdims=True)
        dx = dres_ref[...] + r * (gd - xh * c)
        dx_ref[...] = dx
        dxb_ref[...] = dx.astype(dxb_ref.dtype)
        part = jnp.sum(dhv * xh, axis=0, keepdims=True)

        @pl.when(i == 0)
        def _():
            dg_ref[...] = part

        @pl.when(i > 0)
        def _():
            dg_ref[...] += part

    row = pl.BlockSpec((tm, d), lambda i: (i, 0))
    vec = pl.BlockSpec((1, d), lambda i: (0, 0))
    return _call(body, name=name, grid=(s // tm,), in_specs=[row, vec, row, row],
                 out_specs=[row, row, vec],
                 out_shape=[_sds((s, d)), _sds((s, d), MXU_DT), _sds((1, d))],
                 sem=("arbitrary",))(x, g.reshape(1, d), dh, dres)


def _final_loss(x, g, target, name):
    s, d = x.shape
    tm = _row_tile(s)

    def body(x_ref, g_ref, t_ref, loss_ref, dx_ref, dxb_ref, dg_ref):
        i = pl.program_id(0)
        xv = x_ref[...]
        gv = g_ref[...]
        r = lax.rsqrt(jnp.mean(xv * xv, axis=-1, keepdims=True) + EPS)
        xh = xv * r
        err = xh * gv - t_ref[...]
        part_loss = 0.5 * jnp.sum(jnp.mean(err * err, axis=-1, keepdims=True), axis=0, keepdims=True)
        dy = err * (1.0 / d)
        gd = dy * gv
        c = jnp.mean(gd * xh, axis=-1, keepdims=True)
        dx = r * (gd - xh * c)
        dx_ref[...] = dx
        dxb_ref[...] = dx.astype(dxb_ref.dtype)
        part_g = jnp.sum(dy * xh, axis=0, keepdims=True)
        part_l = jnp.broadcast_to(part_loss, (1, LANES))

        @pl.when(i == 0)
        def _():
            dg_ref[...] = part_g
            loss_ref[...] = part_l

        @pl.when(i > 0)
        def _():
            dg_ref[...] += part_g
            loss_ref[...] += part_l

    row = pl.BlockSpec((tm, d), lambda i: (i, 0))
    vec = pl.BlockSpec((1, d), lambda i: (0, 0))
    return _call(body, name=name, grid=(s // tm,), in_specs=[row, vec, row],
                 out_specs=[pl.BlockSpec((1, LANES), lambda i: (0, 0)), row, row, vec],
                 out_shape=[_sds((1, LANES)), _sds((s, d)), _sds((s, d), MXU_DT), _sds((1, d))],
                 sem=("arbitrary",))(x, g.reshape(1, d), target)


def _lane_iota(shape):
    return lax.broadcasted_iota(jnp.int32, shape, len(shape) - 1)


def _swap_halves(x):
    return pltpu.roll(x, HEAD_DIM, 1)


def _segsum64(x, ones_ref):
    ones = ones_ref[...]
    outs = []
    for c in range(x.shape[1] // LANES):
        xc = x[:, c * LANES:(c + 1) * LANES]
        hi = xc.astype(MXU_DT)
        r1 = xc - hi.astype(F32)
        mid = r1.astype(MXU_DT)
        lo = (r1 - mid.astype(F32)).astype(MXU_DT)
        outs.append(_dot(hi, ones) + _dot(mid, ones) + _dot(lo, ones))
    return outs[0] if len(outs) == 1 else jnp.concatenate(outs, axis=1)


def _pair_ones():
    i = jnp.arange(LANES)
    return (i[:, None] // HEAD_DIM == i[None, :] // HEAD_DIM).astype(MXU_DT)


def _col(x, lane):
    return jnp.sum(jnp.where(_lane_iota(x.shape) == lane, x, 0.0), axis=-1, keepdims=True)


class _LocalCfg:
    def __init__(self, *, groups, qb, kw, qw, qcol, kcol, vcol, kvhalf, kstart, variant):
        self.groups, self.qb, self.kw, self.qw = groups, qb, kw, qw
        self.qcol, self.kcol, self.vcol = qcol, kcol, vcol
        self.kvhalf = kvhalf
        self.kstart, self.variant = kstart, variant
        self.pairs = qw // LANES


def _cfg_a(s):
    nb = s // 128
    return _LocalCfg(groups=1, qb=128, kw=384, qw=512, qcol=lambda g: 0, kcol=lambda g: 4, vcol=lambda g: 5,
                     kvhalf=lambda t, e: t // 2,
                     kstart=lambda n: 128 * jnp.clip(n - 1, 0, nb - 3),
                     variant=lambda n: jnp.where(n <= 0, 0, jnp.where(n == nb - 1, 2, 1)))


def _cfg_b(s):
    rows = s // GRID_W
    return _LocalCfg(groups=4, qb=64, kw=512, qw=128, qcol=lambda g: 6 + g, kcol=lambda g: 10 + g,
                     vcol=lambda g: 14 + g, kvhalf=lambda t, e: e,
                     kstart=lambda n: GRID_W * jnp.clip(n - NA_ROWS // 2, 0, rows - NA_ROWS),
                     variant=lambda n: jnp.where(n < 4, jnp.maximum(n, 0),
                                                 jnp.where(n > rows - 4, n - (rows - 8), 4)))


def _local_head(cfg, t, e, qp, qp_sw, kb, bias, sink_row, left_q):
    kvh = cfg.kvhalf(t, e)
    qsrc = qp if e == kvh else qp_sw
    keep = left_q if kvh == 0 else jnp.logical_not(left_q)
    qm = jnp.where(keep, qsrc, 0.0).astype(MXU_DT)
    sc = _dot(qm, kb, NT_DIMS) + bias
    snk = _col(sink_row, 2 * t + e)
    m = jnp.maximum(jnp.max(sc, axis=-1, keepdims=True), snk)
    p = jnp.exp(sc - m)
    l = jnp.sum(p, axis=-1, keepdims=True) + jnp.exp(snk - m)
    p = p / l
    return qm, keep, p, m, l, snk


def _local_attn_fwd(proj, bias, sink, cfg, name):
    s = proj.shape[0]
    qb, kw, qw, g_n = cfg.qb, cfg.kw, cfg.qw, cfg.groups
    hq = 2 * cfg.pairs

    def body(q_ref, k_ref, v_ref, b_ref, s_ref, o_ref):
        n = pl.program_id(1)
        ks = pl.multiple_of(cfg.kstart(n), 64)
        kf = k_ref[pl.ds(ks, kw), :]
        vf = v_ref[pl.ds(ks, kw), :]
        kb = kf.astype(MXU_DT)
        vf_sw = _swap_halves(vf)
        left_q = _lane_iota((qb, LANES)) < HEAD_DIM
        left_k = _lane_iota((kw, LANES)) < HEAD_DIM
        sink_row = s_ref[...]
        for t in range(cfg.pairs):
            qp = q_ref[:, t * LANES:(t + 1) * LANES] * 0.125
            qp_sw = _swap_halves(qp)
            acc = jnp.zeros((qb, LANES), F32)
            for e in range(2):
                _, _, p, _, _, _ = _local_head(cfg, t, e, qp, qp_sw, kb, b_ref[0, 2 * t + e], sink_row, left_q)
                vsrc = vf if e == cfg.kvhalf(t, e) else vf_sw
                vsel = jnp.where(left_k if e == 0 else jnp.logical_not(left_k), vsrc, 0.0).astype(MXU_DT)
                acc = acc + _dot(p.astype(MXU_DT), vsel)
            o_ref[:, t * LANES:(t + 1) * LANES] = acc

    return _call(
        body, name=name, grid=(g_n, s // qb),
        in_specs=[pl.BlockSpec((qb, qw), lambda g, n: (n, cfg.qcol(g))),
                  pl.BlockSpec((s, LANES), lambda g, n: (0, cfg.kcol(g))),
                  pl.BlockSpec((s, LANES), lambda g, n: (0, cfg.vcol(g))),
                  pl.BlockSpec((1, hq, qb, kw), lambda g, n: (cfg.variant(n), g, 0, 0)),
                  pl.BlockSpec((None, 1, LANES), lambda g, n: (g, 0, 0))],
        out_specs=pl.BlockSpec((qb, qw), lambda g, n: (n, g)),
        out_shape=_sds((s, g_n * qw)), sem=("parallel", "arbitrary"))(proj, proj, proj, bias, sink)


def _local_attn_bwd(proj, bias, sink, do, cfg, name):
    s = proj.shape[0]
    qb, kw, qw, g_n = cfg.qb, cfg.kw, cfg.qw, cfg.groups
    hq = 2 * cfg.pairs

    def body(q_ref, k_ref, v_ref, b_ref, s_ref, do_ref, dq_ref, dk_ref, dv_ref, db_ref, dsk_ref):
        n = pl.program_id(1)
        ks = pl.multiple_of(cfg.kstart(n), 64)
        first = jnp.logical_or(n == 0, cfg.variant(n) != cfg.variant(n - 1))

        @pl.when(n == 0)
        def _():
            dk_ref[...] = jnp.zeros_like(dk_ref)
            dv_ref[...] = jnp.zeros_like(dv_ref)
            dsk_ref[...] = jnp.zeros_like(dsk_ref)

        @pl.when(first)
        def _():
            db_ref[...] = jnp.zeros_like(db_ref)

        kf = k_ref[pl.ds(ks, kw), :]
        vf = v_ref[pl.ds(ks, kw), :]
        kb = kf.astype(MXU_DT)
        vb = vf.astype(MXU_DT)
        kf_sw = _swap_halves(kf)
        left_q = _lane_iota((qb, LANES)) < HEAD_DIM
        left_k = _lane_iota((kw, LANES)) < HEAD_DIM
        sink_row = s_ref[...]
        row0 = lax.broadcasted_iota(jnp.int32, (8, LANES), 0) == 0
        lane8 = _lane_iota((8, LANES))
        dk_acc = jnp.zeros((kw, LANES), F32)
        dv_acc = jnp.zeros((kw, LANES), F32)
        dsk_acc = jnp.zeros((8, LANES), F32)
        for t in range(cfg.pairs):
            qp = q_ref[:, t * LANES:(t + 1) * LANES] * 0.125
            qp_sw = _swap_halves(qp)
            dop = do_ref[:, t * LANES:(t + 1) * LANES]
            dop_sw = _swap_halves(dop)
            dq_t = jnp.zeros((qb, LANES), F32)
            for e in range(2):
                h = 2 * t + e
                qm, keep, p, m, l, snk = _local_head(cfg, t, e, qp, qp_sw, kb, b_ref[0, h], sink_row, left_q)
                kvh = cfg.kvhalf(t, e)
                dom = jnp.where(keep, dop if e == kvh else dop_sw, 0.0).astype(MXU_DT)
                dp = _dot(dom, vb, NT_DIMS)
                dd = jnp.sum(p * dp, axis=-1, keepdims=True)
                ds = p * (dp - dd)
                p_sink = jnp.exp(snk - m) / l
                dsink = jnp.sum(-p_sink * dd, axis=0, keepdims=True)
                dsk_acc = dsk_acc + jnp.where(jnp.logical_and(row0, lane8 == h), dsink, 0.0)
                dsb = ds.astype(MXU_DT)
                dv_acc = dv_acc + _dot(p.astype(MXU_DT), dom, TN_DIMS)
                dk_acc = dk_acc + _dot(dsb, qm, TN_DIMS)
                ksrc = kf if e == kvh else kf_sw
                ksel = jnp.where(left_k if e == 0 else jnp.logical_not(left_k), ksrc, 0.0).astype(MXU_DT)
                dq_t = dq_t + _dot(dsb, ksel)
                db_ref[0, h] += ds
            dq_ref[:, t * LANES:(t + 1) * LANES] = dq_t * 0.125
        dk_ref[pl.ds(ks, kw), :] += dk_acc
        dv_ref[pl.ds(ks, kw), :] += dv_acc
        dsk_ref[...] += dsk_acc

    n_var = bias.shape[0]
    return _call(
        body, name=name, grid=(g_n, s // qb),
        in_specs=[pl.BlockSpec((qb, qw), lambda g, n: (n, cfg.qcol(g))),
                  pl.BlockSpec((s, LANES), lambda g, n: (0, cfg.kcol(g))),
                  pl.BlockSpec((s, LANES), lambda g, n: (0, cfg.vcol(g))),
                  pl.BlockSpec((1, hq, qb, kw), lambda g, n: (cfg.variant(n), g, 0, 0)),
                  pl.BlockSpec((None, 1, LANES), lambda g, n: (g, 0, 0)),
                  pl.BlockSpec((qb, qw), lambda g, n: (n, g))],
        out_specs=[pl.BlockSpec((qb, qw), lambda g, n: (n, g)),
                   pl.BlockSpec((s, LANES), lambda g, n: (0, g)),
                   pl.BlockSpec((s, LANES), lambda g, n: (0, g)),
                   pl.BlockSpec((1, hq, qb, kw), lambda g, n: (cfg.variant(n), g, 0, 0)),
                   pl.BlockSpec((None, 8, LANES), lambda g, n: (g, 0, 0))],
        out_shape=[_sds((s, g_n * qw)), _sds((s, g_n * LANES)), _sds((s, g_n * LANES)),
                   _sds((n_var, g_n * hq, qb, kw)), _sds((g_n, 8, LANES))],
        sem=("parallel", "arbitrary"))(proj, proj, proj, bias, sink, do)


QC_COL, KC_COL, VC_COL = 9, 13, 14
CW = 256


def _swap16(x):
    w = x.shape[1]
    lane = _lane_iota(x.shape)
    return jnp.where(lane % 32 < 16, pltpu.roll(x, w - 16, 1), pltpu.roll(x, 16, 1))


def _dup_halves(x):
    left = _lane_iota(x.shape) < HEAD_DIM
    sw = _swap_halves(x)
    return jnp.where(left, x, sw), jnp.where(left, sw, x)


def _normrope(x, gain, cos, sin, ones_ref):
    ms = _segsum64(x * x, ones_ref) * (1.0 / HEAD_DIM)
    r = lax.rsqrt(ms + EPS)
    y = (x * r) * gain
    return y * cos + _swap16(y) * sin, r


def _cprep_fwd(proj, gq, gk, cos, sin, ones, name):
    s = proj.shape[0]
    tm = _row_tile(s)

    def body(q0, q1, q2, q3, k_ref, v_ref, gq_ref, gk_ref, cos_ref, sin_ref, ones_ref, qh_ref, kd_ref, vd_ref):
        cos_v, sin_v = cos_ref[...], sin_ref[...]
        for c, q_ref in enumerate((q0, q1, q2, q3)):
            y, _ = _normrope(q_ref[...], gq_ref[...], cos_v, sin_v, ones_ref)
            qh_ref[:, c * CW:(c + 1) * CW] = (y * 0.125).astype(qh_ref.dtype)
        yk, _ = _normrope(k_ref[...], gk_ref[...], cos_v, sin_v, ones_ref)
        vv = v_ref[...]
        for p in range(2):
            ka, kb_ = _dup_halves(yk[:, p * LANES:(p + 1) * LANES])
            va, vb_ = _dup_halves(vv[:, p * LANES:(p + 1) * LANES])
            kd_ref[:, (2 * p) * LANES:(2 * p + 1) * LANES] = ka.astype(kd_ref.dtype)
            kd_ref[:, (2 * p + 1) * LANES:(2 * p + 2) * LANES] = kb_.astype(kd_ref.dtype)
            vd_ref[:, (2 * p) * LANES:(2 * p + 1) * LANES] = va.astype(vd_ref.dtype)
            vd_ref[:, (2 * p + 1) * LANES:(2 * p + 2) * LANES] = vb_.astype(vd_ref.dtype)

    def chunk(col):
        return pl.BlockSpec((tm, CW), lambda i: (i, col))

    vec = pl.BlockSpec((1, CW), lambda i: (0, 0))
    tab = pl.BlockSpec((tm, CW), lambda i: (i, 0))
    return _call(body, name=name, grid=(s // tm,),
                 in_specs=[chunk(QC_COL), chunk(QC_COL + 1), chunk(QC_COL + 2), chunk(QC_COL + 3),
                           chunk(KC_COL), chunk(VC_COL), vec, vec, tab, tab,
                           pl.BlockSpec((LANES, LANES), lambda i: (0, 0))],
                 out_specs=[pl.BlockSpec((tm, 4 * CW), lambda i: (i, 0)),
                            pl.BlockSpec((tm, 2 * CW), lambda i: (i, 0)),
                            pl.BlockSpec((tm, 2 * CW), lambda i: (i, 0))],
                 out_shape=[_sds((s, 4 * CW), MXU_DT), _sds((s, 2 * CW), MXU_DT), _sds((s, 2 * CW), MXU_DT)],
                 sem=("parallel",))(proj, proj, proj, proj, proj, proj, gq, gk, cos, sin, ones)


def _cprep_bwd(proj, gq, gk, cos, sin, ones, dqh, dkd, dvd, name):
    s = proj.shape[0]
    tm = _row_tile(s)

    def fold(ref, p):
        a = ref[:, (2 * p) * LANES:(2 * p + 1) * LANES]
        b = ref[:, (2 * p + 1) * LANES:(2 * p + 2) * LANES]
        ta = a + _swap_halves(a)
        tb = b + _swap_halves(b)
        return jnp.where(_lane_iota(a.shape) < HEAD_DIM, ta, tb)

    def norm_bwd(x, gain, dyr, cos_v, sin_v, ones_ref):
        dy = dyr * cos_v + _swap16(dyr * sin_v)
        ms = _segsum64(x * x, ones_ref) * (1.0 / HEAD_DIM)
        r = lax.rsqrt(ms + EPS)
        xh = x * r
        gd = dy * gain
        c = _segsum64(gd * xh, ones_ref) * (1.0 / HEAD_DIM)
        return r * (gd - xh * c), jnp.sum(dy * xh, axis=0, keepdims=True)

    def body(q0, q1, q2, q3, k_ref, gq_ref, gk_ref, cos_ref, sin_ref, ones_ref, dqh_ref, dkd_ref, dvd_ref,
             dq_ref, dk_ref, dv_ref, dgq_ref, dgk_ref):
        i = pl.program_id(0)
        cos_v, sin_v = cos_ref[...], sin_ref[...]
        gq_part = jnp.zeros((1, CW), F32)
        for c, q_ref in enumerate((q0, q1, q2, q3)):
            dx, dg = norm_bwd(q_ref[...], gq_ref[...], dqh_ref[:, c * CW:(c + 1) * CW] * 0.125, cos_v, sin_v,
                              ones_ref)
            dq_ref[:, c * CW:(c + 1) * CW] = dx
            gq_part = gq_part + dg
        dkr = jnp.concatenate([fold(dkd_ref, 0), fold(dkd_ref, 1)], axis=1)
        dxk, gk_part = norm_bwd(k_ref[...], gk_ref[...], dkr, cos_v, sin_v, ones_ref)
        dk_ref[...] = dxk
        dv_ref[...] = jnp.concatenate([fold(dvd_ref, 0), fold(dvd_ref, 1)], axis=1)

        @pl.when(i == 0)
        def _():
            dgq_ref[...] = gq_part
            dgk_ref[...] = gk_part

        @pl.when(i > 0)
        def _():
            dgq_ref[...] += gq_part
            dgk_ref[...] += gk_part

    def chunk(col):
        return pl.BlockSpec((tm, CW), lambda i: (i, col))

    vec = pl.BlockSpec((1, CW), lambda i: (0, 0))
    tab = pl.BlockSpec((tm, CW), lambda i: (i, 0))
    return _call(body, name=name, grid=(s // tm,),
                 in_specs=[chunk(QC_COL), chunk(QC_COL + 1), chunk(QC_COL + 2), chunk(QC_COL + 3), chunk(KC_COL),
                           vec, vec, tab, tab, pl.BlockSpec((LANES, LANES), lambda i: (0, 0)),
                           pl.BlockSpec((tm, 4 * CW), lambda i: (i, 0)),
                           pl.BlockSpec((tm, 2 * CW), lambda i: (i, 0)),
                           pl.BlockSpec((tm, 2 * CW), lambda i: (i, 0))],
                 out_specs=[pl.BlockSpec((tm, 4 * CW), lambda i: (i, 0)), tab, tab, vec, vec],
                 out_shape=[_sds((s, 4 * CW)), _sds((s, CW)), _sds((s, CW)), _sds((1, CW)), _sds((1, CW))],
                 sem=("arbitrary",))(proj, proj, proj, proj, proj, gq, gk, cos, sin, ones, dqh, dkd, dvd)


def _flash_tiles(s):
    return min(512, s), min(512, s)


def _row_iota(shape):
    return lax.broadcasted_iota(jnp.int32, shape, 0)


def _flash_fwd(qh, kd, vdt, name):
    s = qh.shape[0]
    tq, tk = _flash_tiles(s)
    nk = s // tk

    n_chunks = 1
    cw = tq // n_chunks
    units = [(t, c, e) for t in range(2) for c in range(n_chunks) for e in range(2)]

    def body(q_ref, k_ref, vt_ref, ot_ref, lse_ref, qm_ref, m_ref, lacc_ref, acc_ref):
        j = pl.program_id(2)

        @pl.when(j == 0)
        def _():
            m_ref[...] = jnp.full(m_ref.shape, MASK_VALUE, F32)
            lacc_ref[...] = jnp.zeros_like(lacc_ref)
            acc_ref[...] = jnp.zeros_like(acc_ref)
            left_q = _lane_iota((tq, LANES)) < HEAD_DIM
            for t in range(2):
                qp = q_ref[:, t * LANES:(t + 1) * LANES]
                qm_ref[2 * t] = jnp.where(left_q, qp, jnp.zeros_like(qp))
                qm_ref[2 * t + 1] = jnp.where(left_q, jnp.zeros_like(qp), qp)

        kb = k_ref[...]
        vt = vt_ref[...]
        top_k = _row_iota((LANES, tk)) < HEAD_DIM
        top_c = _row_iota((LANES, cw)) < HEAD_DIM
        vt_e = (jnp.where(top_k, vt, jnp.ones_like(vt)), jnp.where(top_k, jnp.ones_like(vt), vt))

        def scores(unit):
            t, c, e = unit
            return _dot(kb, qm_ref[2 * t + e, c * cw:(c + 1) * cw, :], NT_DIMS)

        nxt = scores(units[0])
        pv, alpha = [], []
        for n, (t, c, e) in enumerate(units):
            st = nxt
            if n + 1 < len(units):
                nxt = scores(units[n + 1])
            h = 2 * t + e
            cols = slice(c * cw, (c + 1) * cw)
            m_prev = m_ref[h, :, cols]
            m_new = jnp.maximum(m_prev, jnp.max(st, axis=0, keepdims=True))
            alpha.append(jnp.exp(m_prev - m_new))
            pt = jnp.exp(st - m_new)
            m_ref[h, :, cols] = m_new
            pv.append(_dot(vt_e[e], pt.astype(MXU_DT)))
            if e == 1:
                acc_ref[t, :, cols] = (acc_ref[t, :, cols] * jnp.where(top_c, alpha[0], alpha[1])
                                       + jnp.where(top_c, pv[0], pv[1]))
                lacc_ref[t, :, cols] = (lacc_ref[t, :, cols] * jnp.where(top_c, alpha[1], alpha[0])
                                        + jnp.where(top_c, pv[1], pv[0]))
                pv, alpha = [], []

        @pl.when(j == nk - 1)
        def _():
            for t in range(2):
                lacc = lacc_ref[t]
                l_sw = jnp.concatenate([lacc[HEAD_DIM:], lacc[:HEAD_DIM]], axis=0)
                ot_ref[t * LANES:(t + 1) * LANES, :] = acc_ref[t] / l_sw
                lse_ref[2 * t:2 * t + 1, :] = m_ref[2 * t] + jnp.log(lacc[HEAD_DIM:HEAD_DIM + 1])
                lse_ref[2 * t + 1:2 * t + 2, :] = m_ref[2 * t + 1] + jnp.log(lacc[0:1])

    return _call(body, name=name, grid=(4, s // tq, nk),
                 in_specs=[pl.BlockSpec((tq, CW), lambda g, i, j: (i, g)),
                           pl.BlockSpec((tk, LANES), lambda g, i, j: (j, g)),
                           pl.BlockSpec((LANES, tk), lambda g, i, j: (g, j))],
                 out_specs=[pl.BlockSpec((CW, tq), lambda g, i, j: (g, i)),
                            pl.BlockSpec((None, 4, tq), lambda g, i, j: (g, 0, i))],
                 out_shape=[_sds((4 * CW, s)), _sds((4, 4, s))],
                 scratch=[pltpu.VMEM((4, tq, LANES), MXU_DT), pltpu.VMEM((4, 1, tq), F32),
                          pltpu.VMEM((2, LANES, tq), F32), pltpu.VMEM((2, LANES, tq), F32)],
                 sem=("parallel", "parallel", "arbitrary"))(qh, kd, vdt)


def _flash_bwd(qh, kd, vd, kdt, do, lse, dd, name):
    s = qh.shape[0]
    tq, tk = _flash_tiles(s)
    ni = s // tq

    def body(q_ref, k_ref, v_ref, kt_ref, do_ref, lse_ref, dd_ref, dqt_ref, dk_ref, dv_ref, dk_acc, dv_acc):
        j = pl.program_id(1)
        i = pl.program_id(2)

        @pl.when(i == 0)
        def _():
            dk_acc[...] = jnp.zeros_like(dk_acc)
            dv_acc[...] = jnp.zeros_like(dv_acc)

        kb = k_ref[...]
        vb = v_ref[...]
        kt = kt_ref[...]
        left_q = _lane_iota((tq, LANES)) < HEAD_DIM
        top = _row_iota((LANES, tq)) < HEAD_DIM
        cols = pl.ds(pl.multiple_of(i * tq, tq), tq)
        def first_stage(h):
            t, e = divmod(h, 2)
            keep_q = left_q if e == 0 else jnp.logical_not(left_q)
            qp = q_ref[:, t * LANES:(t + 1) * LANES]
            dop = do_ref[:, t * LANES:(t + 1) * LANES]
            qm = jnp.where(keep_q, qp, jnp.zeros_like(qp))
            dom = jnp.where(keep_q, dop, jnp.zeros_like(dop))
            return qm, dom, _dot(kb, qm, NT_DIMS), _dot(vb, dom, NT_DIMS)

        nxt = first_stage(0)
        dqt = []
        for h in range(4):
            qm, dom, st, dpt = nxt
            if h < 3:
                nxt = first_stage(h + 1)
            pt = jnp.exp(st - lse_ref[h:h + 1, :])
            dsb = (pt * (dpt - dd_ref[h:h + 1, :])).astype(MXU_DT)
            dv_acc[...] += _dot(pt.astype(MXU_DT), dom)
            dk_acc[...] += _dot(dsb, qm)
            dqt.append(_dot(kt, dsb))
            if h % 2 == 1:
                t = h // 2
                dq_t = jnp.where(top, dqt[0], dqt[1])
                dqt = []

                @pl.when(j == 0)
                def _():
                    dqt_ref[t * LANES:(t + 1) * LANES, cols] = dq_t

                @pl.when(j > 0)
                def _():
                    dqt_ref[t * LANES:(t + 1) * LANES, cols] += dq_t

        @pl.when(i == ni - 1)
        def _():
            dk_ref[...] = dk_acc[...]
            dv_ref[...] = dv_acc[...]

    qspec = pl.BlockSpec((tq, CW), lambda g, j, i: (i, g))
    kspec = pl.BlockSpec((tk, LANES), lambda g, j, i: (j, g))
    rowspec = pl.BlockSpec((None, 4, tq), lambda g, j, i: (g, 0, i))
    return _call(body, name=name, grid=(4, s // tk, ni),
                 in_specs=[qspec, kspec, kspec, pl.BlockSpec((LANES, tk), lambda g, j, i: (g, j)), qspec,
                           rowspec, rowspec],
                 out_specs=[pl.BlockSpec((CW, s), lambda g, j, i: (g, 0)), kspec, kspec],
                 out_shape=[_sds((4 * CW, s)), _sds((s, 2 * CW)), _sds((s, 2 * CW))],
                 scratch=[pltpu.VMEM((tk, LANES), F32), pltpu.VMEM((tk, LANES), F32)],
                 sem=("parallel", "arbitrary", "arbitrary"))(qh, kd, vd, kdt, do, lse, dd)


def _groupnorm_fwd(oa, ob, oc, ga, gb, gc, name):
    s = oa.shape[0]
    tm = _row_tile(s)
    wa, wb, wc = oa.shape[1], ob.shape[1], oc.shape[1]

    def body(oa_ref, ob_ref, oc_ref, ga_ref, gb_ref, gc_ref, mix_ref):
        off = 0
        for o_ref, g_ref, w in ((oa_ref, ga_ref, wa), (ob_ref, gb_ref, wb), (oc_ref, gc_ref, wc)):
            xv = o_ref[...]
            r = lax.rsqrt(jnp.mean(xv * xv, axis=-1, keepdims=True) + EPS)
            mix_ref[:, off:off + w] = ((xv * r) * g_ref[...]).astype(mix_ref.dtype)
            off += w

    def row(w):
        return pl.BlockSpec((tm, w), lambda i: (i, 0))

    def vec(w):
        return pl.BlockSpec((1, w), lambda i: (0, 0))

    return _call(body, name=name, grid=(s // tm,),
                 in_specs=[row(wa), row(wb), row(wc), vec(wa), vec(wb), vec(wc)],
                 out_specs=row(wa + wb + wc), out_shape=_sds((s, wa + wb + wc), MXU_DT),
                 sem=("parallel",))(oa, ob, oc, ga.reshape(1, wa), gb.reshape(1, wb), gc.reshape(1, wc))


def _groupnorm_bwd(dmix, oa, ob, oc, ga, gb, gc, ones, name):
    s = oa.shape[0]
    tm = _row_tile(s)
    wa, wb, wc = oa.shape[1], ob.shape[1], oc.shape[1]

    def body(dm_ref, oa_ref, ob_ref, oc_ref, ga_ref, gb_ref, gc_ref, ones_ref,
             doa_ref, dob_ref, doc_ref, docb_ref, dd_ref, dga_ref, dgb_ref, dgc_ref):
        i = pl.program_id(0)
        off = 0
        parts = []
        for o_ref, g_ref, do_ref, w in ((oa_ref, ga_ref, doa_ref, wa), (ob_ref, gb_ref, dob_ref, wb),
                                        (oc_ref, gc_ref, doc_ref, wc)):
            xv = o_ref[...]
            dh = dm_ref[:, off:off + w]
            r = lax.rsqrt(jnp.mean(xv * xv, axis=-1, keepdims=True) + EPS)
            xh = xv * r
            gd = dh * g_ref[...]
            c = jnp.mean(gd * xh, axis=-1, keepdims=True)
            dx = r * (gd - xh * c)
            do_ref[...] = dx
            parts.append(jnp.sum(dh * xh, axis=0, keepdims=True))
            if o_ref is oc_ref:
                docb_ref[...] = dx.astype(docb_ref.dtype)
                dd_ref[...] = _segsum64(dx * xv, ones_ref)
            off += w

        @pl.when(i == 0)
        def _():
            dga_ref[...], dgb_ref[...], dgc_ref[...] = parts

        @pl.when(i > 0)
        def _():
            dga_ref[...] += parts[0]
            dgb_ref[...] += parts[1]
            dgc_ref[...] += parts[2]

    def row(w):
        return pl.BlockSpec((tm, w), lambda i: (i, 0))

    def vec(w):
        return pl.BlockSpec((1, w), lambda i: (0, 0))

    return _call(body, name=name, grid=(s // tm,),
                 in_specs=[row(wa + wb + wc), row(wa), row(wb), row(wc), vec(wa), vec(wb), vec(wc),
                           pl.BlockSpec((LANES, LANES), lambda i: (0, 0))],
                 out_specs=[row(wa), row(wb), row(wc), row(wc), row(wc), vec(wa), vec(wb), vec(wc)],
                 out_shape=[_sds((s, wa)), _sds((s, wb)), _sds((s, wc)), _sds((s, wc), MXU_DT), _sds((s, wc)),
                            _sds((1, wa)), _sds((1, wb)), _sds((1, wc))],
                 sem=("arbitrary",))(dmix, oa, ob, oc, ga.reshape(1, wa), gb.reshape(1, wb), gc.reshape(1, wc), ones)


def _adam_math(w, g, m, v):
    m = ADAM_B1 * m + (1.0 - ADAM_B1) * g
    v = ADAM_B2 * v + (1.0 - ADAM_B2) * jnp.square(g)
    m_hat = m / (1.0 - ADAM_B1 ** ADAM_STEP)
    v_hat = v / (1.0 - ADAM_B2 ** ADAM_STEP)
    delta = -ADAM_LR * (m_hat / (jnp.sqrt(v_hat) + ADAM_EPS) + ADAM_WD * w)
    return delta, m, v


def _mesh_pos():
    return lax.axis_index("x"), lax.axis_index("y"), lax.axis_index("c")


def _peer_chips(x, y):
    return [(1 - x, y), (x, 1 - y), (1 - x, 1 - y)]


def _allreduce_small_adam(g, w, m, v):
    rows = g.shape[0]

    def body(g_ref, w_ref, m_ref, v_ref, gs_ref, d_ref, mo_ref, vo_ref, buf, send_sems, recv_sems):
        x, y, c = _mesh_pos()
        me = 4 * x + 2 * y + c
        buf[me] = g_ref[...]
        copies = []
        for k in range(1, 8):
            px = 1 - x if (k >> 2) & 1 else x
            py = 1 - y if (k >> 1) & 1 else y
            pc = 1 - c if k & 1 else c
            cp = pltpu.make_async_remote_copy(src_ref=g_ref, dst_ref=buf.at[me], send_sem=send_sems.at[k - 1],
                                              recv_sem=recv_sems.at[k - 1], device_id=(px, py, pc),
                                              device_id_type=MESH_ID)
            cp.start()
            copies.append(cp)
        for cp in copies:
            cp.wait()
        total = buf[0]
        for d in range(1, 8):
            total = total + buf[d]
        gs_ref[...] = total
        d_ref[...], mo_ref[...], vo_ref[...] = _adam_math(w_ref[...], total, m_ref[...], v_ref[...])

    vm = pl.BlockSpec(memory_space=pltpu.VMEM)
    return _call(body, name="allreduce_small_adam", in_specs=[vm] * 4, out_specs=[vm] * 4,
                 out_shape=[_sds((rows, LANES))] * 4,
                 scratch=[pltpu.VMEM((8, rows, LANES), F32), pltpu.SemaphoreType.DMA((7,)),
                          pltpu.SemaphoreType.DMA((7,))])(g, w, m, v)


HBM_SPEC = pl.BlockSpec(memory_space=pltpu.HBM)
SEM_SPEC = pl.BlockSpec(memory_space=pltpu.SEMAPHORE)
VMEM_SPEC = pl.BlockSpec(memory_space=pltpu.VMEM)
SIDE_EFFECT = pltpu.SideEffectType.DATAFLOW_SIDE_EFFECTING
N_PEERS = 7


def _in_hbm(a):
    return pltpu.with_memory_space_constraint(a, pltpu.HBM)


def _landing(shape, dtype):
    return _in_hbm(lax.empty(shape, dtype))


def _token_shape():
    return _sds((8, LANES))


def _gather_start(shards):
    n = len(shards)
    n_layers = shards[0].shape[0]
    jobs = [(l, t) for l in range(n_layers) for t in range(n)]
    nj = len(jobs)

    def body(*refs):
        sh = refs[:n]
        outs = refs[n + nj:]
        send, recv, land, token = outs[:nj], outs[nj:2 * nj], outs[2 * nj:3 * nj], outs[3 * nj]
        x, y, c = _mesh_pos()
        me = 2 * x + y
        for j, (l, t) in enumerate(jobs):
            for k, (px, py) in enumerate(_peer_chips(x, y)):
                pltpu.make_async_remote_copy(src_ref=sh[t].at[l], dst_ref=land[j].at[me], send_sem=send[j].at[k],
                                             recv_sem=recv[j].at[k], device_id=(px, py, c),
                                             device_id_type=MESH_ID).start()
        token[...] = jnp.zeros_like(token)

    lands = [_landing((4,) + shards[t].shape[1:], shards[t].dtype) for _, t in jobs]
    res = pl.pallas_call(
        body, name="gather_start",
        out_shape=tuple([pltpu.SemaphoreType.DMA((3,))] * (2 * nj)
                        + [pltpu.HBM(a.shape, a.dtype) for a in lands] + [_token_shape()]),
        in_specs=[HBM_SPEC] * (n + nj), out_specs=tuple([SEM_SPEC] * (2 * nj) + [HBM_SPEC] * nj + [VMEM_SPEC]),
        input_output_aliases={n + j: 2 * nj + j for j in range(nj)},
        compiler_params=pltpu.CompilerParams(has_side_effects=SIDE_EFFECT),
    )(*[_in_hbm(a) for a in shards], *lands)
    return jobs, res[:nj], res[nj:2 * nj], res[2 * nj:3 * nj], res[3 * nj]


def _gather_wait(shard, layer, land, send_sem, recv_sem, after, name):
    def body(sh_ref, land_ref, send_ref, recv_ref, after_ref, land_out):
        x, y, c = _mesh_pos()
        for k in range(3):
            cp = pltpu.make_async_remote_copy(src_ref=sh_ref.at[layer], dst_ref=land_ref.at[k],
                                              send_sem=send_ref.at[k], recv_sem=recv_ref.at[k],
                                              device_id=(x, y, 1 - c), device_id_type=MESH_ID)
            cp.wait_send()
            cp.wait_recv()

    return pl.pallas_call(
        body, name=name, out_shape=pltpu.HBM(land.shape, land.dtype),
        in_specs=[HBM_SPEC, HBM_SPEC, SEM_SPEC, SEM_SPEC, ANY], out_specs=HBM_SPEC,
        input_output_aliases={1: 0},
        compiler_params=pltpu.CompilerParams(has_side_effects=SIDE_EFFECT),
    )(shard, land, send_sem, recv_sem, after)


def _grad_start(g, name):
    def body(g_ref, land_in, send, recv, land, token):
        x, y, c = _mesh_pos()
        me = 2 * x + y
        pltpu.make_async_remote_copy(src_ref=g_ref.at[me], dst_ref=land.at[0], send_sem=send.at[0],
                                     recv_sem=recv.at[0], device_id=(x, y, 1 - c), device_id_type=MESH_ID).start()
        for k, (px, py) in enumerate(_peer_chips(x, y)):
            for c2 in range(2):
                pltpu.make_async_remote_copy(src_ref=g_ref.at[2 * px + py], dst_ref=land.at[1 + 2 * k + c],
                                             send_sem=send.at[1 + 2 * k + c2], recv_sem=recv.at[1 + 2 * k + c],
                                             device_id=(px, py, c2), device_id_type=MESH_ID).start()
        token[...] = jnp.zeros_like(token)

    land = _landing((N_PEERS,) + g.shape[1:], g.dtype)
    return pl.pallas_call(
        body, name=name,
        out_shape=(pltpu.SemaphoreType.DMA((N_PEERS,)), pltpu.SemaphoreType.DMA((N_PEERS,)),
                   pltpu.HBM(land.shape, land.dtype), _token_shape()),
        in_specs=[HBM_SPEC, HBM_SPEC], out_specs=(SEM_SPEC, SEM_SPEC, HBM_SPEC, VMEM_SPEC),
        input_output_aliases={1: 2},
        compiler_params=pltpu.CompilerParams(has_side_effects=SIDE_EFFECT),
    )(_in_hbm(g), land)


def _grad_wait(g, land, send_sem, recv_sem, after, name):
    def body(g_ref, land_ref, send_ref, recv_ref, after_ref, land_out):
        x, y, c = _mesh_pos()
        for k in range(N_PEERS):
            cp = pltpu.make_async_remote_copy(src_ref=g_ref.at[0], dst_ref=land_ref.at[k], send_sem=send_ref.at[k],
                                              recv_sem=recv_ref.at[k], device_id=(x, y, 1 - c),
                                              device_id_type=MESH_ID)
            cp.wait_send()
            cp.wait_recv()

    return pl.pallas_call(
        body, name=name, out_shape=pltpu.HBM(land.shape, land.dtype),
        in_specs=[HBM_SPEC, HBM_SPEC, SEM_SPEC, SEM_SPEC, ANY], out_specs=HBM_SPEC,
        input_output_aliases={1: 0},
        compiler_params=pltpu.CompilerParams(has_side_effects=SIDE_EFFECT),
    )(g, land, send_sem, recv_sem, after)


def _sum_adam(g, land, w, m, v, prev, layer, me_idx, name):
    _, r, cols = g.shape
    tr = min(128, r)

    def body(me_ref, g_ref, l0, l1, l2, l3, l4, l5, l6, w_ref, m_ref, v_ref, p0, p1, p2, p3,
             go_ref, d_ref, mo_ref, vo_ref):
        total = g_ref[...].astype(F32) + l0[...].astype(F32)
        for ref in (l1, l2, l3, l4, l5, l6):
            total = total + ref[...].astype(F32)
        go_ref[...] = total
        d_ref[...], mo_ref[...], vo_ref[...] = _adam_math(w_ref[...], total, m_ref[...], v_ref[...])

    def slot(k):
        return pl.BlockSpec((None, tr, cols), lambda i, me: (k, i, 0))

    lay = pl.BlockSpec((None, tr, cols), lambda i, me: (layer, i, 0))
    return _call(body, name=name, grid=(r // tr,), prefetch=1,
                 in_specs=[pl.BlockSpec((None, tr, cols), lambda i, me: (me[0], i, 0))]
                 + [slot(k) for k in range(N_PEERS)] + [lay, lay, lay] + [ANY] * 4,
                 out_specs=[lay] * 4, out_shape=[_sds(w.shape)] * 4,
                 aliases={12 + k: k for k in range(4)}, sem=("parallel",))(
                     me_idx, g, *([land] * N_PEERS), w, m, v, *prev)


def _t5_bucket(rel):
    nb = T5_BUCKETS // 2
    max_exact = nb // 2
    base = jnp.where(rel > 0, nb, 0)
    n = jnp.abs(rel)
    nf = jnp.maximum(n, 1).astype(F32)
    large = max_exact + (jnp.log(nf / max_exact) / math.log(T5_MAX_DIST / max_exact)
                         * (nb - max_exact)).astype(jnp.int32)
    large = jnp.minimum(large, nb - 1)
    return base + jnp.where(n < max_exact, n, large)


def _a_bias_maps():
    v = jnp.arange(3)[:, None, None]
    q = jnp.arange(128)[None, :, None]
    k = jnp.arange(384)[None, None, :]
    rel = k - 128 * v - q
    valid = jnp.abs(rel) <= 128
    onehot = (_t5_bucket(rel)[..., None] == jnp.arange(T5_BUCKETS)).astype(F32)
    return onehot * valid[..., None].astype(F32), valid


def _b_bias_maps():
    v = jnp.arange(8)[:, None]
    i = jnp.arange(NA_ROWS)[None, :]
    dr = jnp.where(v == 4, i + 3, i - v + 7)
    row_oh = (dr[..., None] == jnp.arange(2 * NA_ROWS - 1)).astype(F32)
    q = jnp.arange(GRID_W)[:, None]
    kc = jnp.arange(GRID_W)[None, :]
    cs = jnp.clip(q - 8, 0, GRID_W - 16)
    valid = (kc >= cs) & (kc < cs + 16)
    col_oh = ((kc - q + 15)[..., None] == jnp.arange(31)).astype(F32) * valid[..., None].astype(F32)
    return row_oh, col_oh, valid


def _rope_tables(s):
    t = jnp.arange(s)
    row = (t // GRID_W).astype(F32)
    col = (t % GRID_W).astype(F32)
    axis_dim = HEAD_DIM // 2
    freqs = ROPE_THETA ** (-jnp.arange(0, axis_dim, 2, dtype=F32) / axis_dim)
    ang_row = row[:, None] * freqs[None, :]
    ang_col = col[:, None] * freqs[None, :]
    cos = jnp.concatenate([jnp.cos(ang_row)] * 2 + [jnp.cos(ang_col)] * 2, axis=1)
    sin = jnp.concatenate([-jnp.sin(ang_row), jnp.sin(ang_row), -jnp.sin(ang_col), jnp.sin(ang_col)], axis=1)
    return jnp.tile(cos, (1, CW // HEAD_DIM)), jnp.tile(sin, (1, CW // HEAD_DIM))


def _pack(parts, rows):
    flat = jnp.concatenate([p.reshape(-1).astype(F32) for p in parts])
    return jnp.pad(flat, (0, rows * LANES - flat.shape[0])).reshape(rows, LANES)


def _unpack(buf, shapes):
    flat = buf.reshape(-1)
    out, off = [], 0
    for shp in shapes:
        size = math.prod(shp)
        out.append(flat[off:off + size].reshape(shp))
        off += size
    return out


def kernel(x, norm_mix, w_in, a_sink, t5_table, b_rpb, c_q_gain, c_k_gain, out_gain_a, out_gain_b, out_gain_c, w_o, norm_mlp, w_up, w_down, norm_final, loss_target, m_norm_mix, m_w_in, m_a_sink, m_t5_table, m_b_rpb, m_c_q_gain, m_c_k_gain, m_out_gain_a, m_out_gain_b, m_out_gain_c, m_w_o, m_norm_mlp, m_w_up, m_w_down, m_norm_final, v_norm_mix, v_w_in, v_a_sink, v_t5_table, v_b_rpb, v_c_q_gain, v_c_k_gain, v_out_gain_a, v_out_gain_b, v_out_gain_c, v_w_o, v_norm_mlp, v_w_up, v_w_down, v_norm_final):
    n_layers = w_in.shape[0]
    s, d = x.shape[1], x.shape[2]
    d_ff = 4 * w_up.shape[2]
    in_w = 4 * w_in.shape[2]
    xs = x.reshape(s, d)
    target = loss_target.reshape(s, d)
    cfg_a, cfg_b = _cfg_a(s), _cfg_b(s)

    x_i, y_i, _ = _mesh_pos()
    me_chip = 2 * x_i + y_i
    me_idx = me_chip.astype(jnp.int32).reshape(1)
    w_bf = [w_in.astype(MXU_DT), w_o.astype(MXU_DT), w_up.astype(MXU_DT), w_down.astype(MXU_DT)]
    jobs, gather_send, gather_recv, gather_land, gather_token = _gather_start(w_bf)
    job_of = {job: j for j, job in enumerate(jobs)}
    ff_shard = w_up.shape[2]

    def gathered(l, t, after):
        j = job_of[(l, t)]
        land = _gather_wait(w_bf[t], l, gather_land[j], gather_send[j], gather_recv[j], after,
                            "gather_wait_%d_%d" % (l, t))
        return lax.dynamic_update_slice(land, w_bf[t][l][None], (me_chip, 0, 0))

    ones = _pair_ones()
    cos_t, sin_t = _rope_tables(s)
    a_onehot, a_valid = _a_bias_maps()
    bias_a = jnp.where(a_valid[:, None], jnp.einsum("vqkb,bh->vhqk", a_onehot, t5_table, precision=HIGHEST),
                       MASK_VALUE)
    row_oh, col_oh, b_valid = _b_bias_maps()
    sink_b = jnp.full((4, 1, LANES), MASK_VALUE, F32)

    def b_bias(rpb):
        t = jnp.einsum("hrz,vir->vhiz", rpb, row_oh, precision=HIGHEST)
        t = jnp.einsum("vhiz,qcz->vhqic", t, col_oh, precision=HIGHEST)
        t = jnp.where(b_valid[None, None, :, None, :], t, MASK_VALUE)
        return t.reshape(8, 8, GRID_W, NA_ROWS * GRID_W)

    def tile_gain(gvec):
        return jnp.tile(gvec, CW // HEAD_DIM).reshape(1, CW)

    def pad_sink(svec):
        return jnp.pad(svec, (0, LANES - svec.shape[0])).reshape(1, 1, LANES)

    saved = []
    xc = xs
    for l in range(n_layers):
        h1 = _rms_fwd(xc, norm_mix[l] + gather_token[0, 0] if l == 0 else norm_mix[l], "rms_mix")
        wf_in = gathered(l, 0, h1).transpose(1, 0, 2).reshape(d, in_w)
        proj = _matmul(h1, wf_in, mode="nn", name="proj_in", tm=1024, tn=768, tk=2048)
        bias_b = b_bias(b_rpb[l])
        oa = _local_attn_fwd(proj, bias_a, pad_sink(a_sink[l]), cfg_a, "attn_a_fwd")
        ob = _local_attn_fwd(proj, bias_b, sink_b, cfg_b, "attn_b_fwd")
        gq, gk = tile_gain(c_q_gain[l]), tile_gain(c_k_gain[l])
        qh, kd, vd = _cprep_fwd(proj, gq, gk, cos_t, sin_t, ones, "cprep_fwd")
        kdt, vdt = kd.T, vd.T
        oct, lse = _flash_fwd(qh, kd, vdt, "attn_c_fwd")
        oc = oct.T
        mix = _groupnorm_fwd(oa, ob, oc, out_gain_a[l], out_gain_b[l], out_gain_c[l], "groupnorm_fwd")
        wf_o = gathered(l, 1, mix).reshape(d, d)
        x_mid = _matmul(mix, wf_o, mode="nn", name="proj_out", tm=1024, tn=1024, tk=2048, epi="res",
                        extra=(xc,))
        h2 = _rms_fwd(x_mid, norm_mlp[l], "rms_mlp")
        wg_up = gathered(l, 2, h2)
        nb_up = ff_shard // 1024
        u, uu = _matmul(h2, wg_up, mode="nn", name="mlp_up", tm=1024, tn=1024, tk=2048, epi="relu2",
                        out_dtypes=(F32, MXU_DT), mkn=(s, d, d_ff),
                        b_spec=pl.BlockSpec((None, 2048, 1024), lambda i, j, kk: (j // nb_up, kk, j % nb_up)))
        wf_down = gathered(l, 3, uu).reshape(d_ff, d)
        x_out = _matmul(uu, wf_down, mode="nn", name="mlp_down", tm=1024, tn=1024, tk=2048, epi="res",
                        extra=(x_mid,))
        saved.append((xc, h1, proj, bias_b, oa, ob, qh, kd, vd, kdt, oc, lse, mix, x_mid, h2, u, uu,
                      wf_in, wf_o, wg_up, wf_down))
        xc = x_out

    loss_part, dx, dxb, dg_final = _final_loss(xc, norm_final, target, "final_loss")

    small = {k: [] for k in ("norm_mix", "a_sink", "b_rpb", "cq", "ck", "oga", "ogb", "ogc", "norm_mlp")}
    dbias_a_total = jnp.zeros_like(bias_a)
    big_w = {"w_in": (w_in, m_w_in, v_w_in), "w_o": (w_o, m_w_o, v_w_o), "w_up": (w_up, m_w_up, v_w_up),
             "w_down": (w_down, m_w_down, v_w_down)}
    big = {nm: [lax.empty(wmv[0].shape, F32) for _ in range(4)] for nm, wmv in big_w.items()}

    def send_grad(nm, l, g):
        send, recv, land, token = _grad_start(g, "grad_start_%s_%d" % (nm, l))
        return (nm, l, g, send, recv, land), token[0, 0]

    def finish_grads(pending, after):
        for nm, l, g, send, recv, land in pending:
            land = _grad_wait(g, land, send, recv, after, "grad_wait_%s_%d" % (nm, l))
            wmv = big_w[nm]
            big[nm] = _sum_adam(g, land, wmv[0], wmv[1], wmv[2], big[nm], l, me_idx, "sum_adam_%s_%d" % (nm, l))

    pending = []
    for l in reversed(range(n_layers)):
        (xin, h1, proj, bias_b, oa, ob, qh, kd, vd, kdt, oc, lse, mix, x_mid, h2, u, uu,
         wf_in, wf_o, wg_up, wf_down) = saved[l]
        started = []
        du = _matmul(dxb, wf_down, mode="nt", name="mlp_down_dgrad", tm=1024, tn=1024, tk=2048, epi="mul2u",
                     extra=(u,), out_dtypes=(MXU_DT,))
        gw = _matmul(uu, dxb, mode="tn", name="mlp_down_wgrad", tm=1024, tn=1024, tk=1024, out_dtypes=(GRAD_DT,))
        rec, tok_down = send_grad("w_down", l, gw.reshape(4, d_ff // 4, d))
        started.append(rec)
        nbk = ff_shard // 2048
        dh2 = _matmul(du, wg_up, mode="nt", name="mlp_up_dgrad", tm=1024, tn=1024, tk=2048,
                      mkn=(s, d_ff, d),
                      b_spec=pl.BlockSpec((None, 1024, 2048), lambda i, j, kk: (kk // nbk, j, kk % nbk)))
        nbo = ff_shard // 1024
        gw = _matmul(h2, du, mode="tn", name="mlp_up_wgrad", tm=1024, tn=1024, tk=1024, out_dtypes=(GRAD_DT,),
                     out_spec=pl.BlockSpec((None, 1024, 1024), lambda i, j, kk: (j // nbo, i, j % nbo)),
                     out_shape=(4, d, ff_shard))
        rec, tok_up = send_grad("w_up", l, gw)
        started.append(rec)
        dx_mid, dxmb, dg = _rms_bwd(x_mid, norm_mlp[l] + (tok_down + tok_up), dh2, dx, "rms_mlp_bwd")
        small["norm_mlp"].append(dg)
        dmix = _matmul(dxmb, wf_o, mode="nt", name="proj_out_dgrad", tm=1024, tn=1024, tk=2048)
        gw = _matmul(mix, dxmb, mode="tn", name="proj_out_wgrad", tm=1024, tn=1024, tk=1024, out_dtypes=(GRAD_DT,))
        rec, tok_o = send_grad("w_o", l, gw.reshape(4, d // 4, d))
        started.append(rec)
        doa, dob, doc, docb, ddc, dga, dgb, dgc = _groupnorm_bwd(
            dmix, oa, ob, oc, out_gain_a[l] + tok_o, out_gain_b[l], out_gain_c[l], ones, "groupnorm_bwd")
        small["oga"].append(dga)
        small["ogb"].append(dgb)
        small["ogc"].append(dgc)
        dqa, dka, dva, dbias_a, dsink = _local_attn_bwd(proj, bias_a, pad_sink(a_sink[l]), doa, cfg_a, "attn_a_bwd")
        dbias_a_total = dbias_a_total + dbias_a
        small["a_sink"].append(dsink[0, 0, :a_sink.shape[1]])
        dqb, dkb, dvb, dbias_b, _ = _local_attn_bwd(proj, bias_b, sink_b, dob, cfg_b, "attn_b_bwd")
        db5 = jnp.where(b_valid[None, None, :, None, :], dbias_b.reshape(8, 8, GRID_W, NA_ROWS, GRID_W), 0.0)
        t = jnp.einsum("vhqic,qcz->vhiz", db5, col_oh, precision=HIGHEST)
        small["b_rpb"].append(jnp.einsum("vhiz,vir->hrz", t, row_oh, precision=HIGHEST))
        dd_rows = ddc.reshape(s, 16, HEAD_DIM)[:, :, 0].T.reshape(4, 4, s)
        dqht, dkd, dvd = _flash_bwd(qh, kd, vd, kdt, docb, lse, dd_rows, "attn_c_bwd")
        dqh = dqht.T
        gq, gk = tile_gain(c_q_gain[l]), tile_gain(c_k_gain[l])
        dqc, dkc, dvc, dgq, dgk = _cprep_bwd(proj, gq, gk, cos_t, sin_t, ones, dqh, dkd, dvd, "cprep_bwd")
        small["cq"].append(dgq.reshape(CW // HEAD_DIM, HEAD_DIM).sum(0))
        small["ck"].append(dgk.reshape(CW // HEAD_DIM, HEAD_DIM).sum(0))
        dproj = jnp.concatenate([dqa, dka, dva, dqb, dkb, dvb, dqc, dkc, dvc], axis=1).astype(MXU_DT)
        dh1 = _matmul(dproj, wf_in, mode="nt", name="proj_in_dgrad", tm=1024, tn=1024, tk=1920)
        gw = _matmul(h1, dproj, mode="tn", name="proj_in_wgrad", tm=1024, tn=768, tk=1024, out_dtypes=(GRAD_DT,))
        rec, tok_in = send_grad("w_in", l, gw.reshape(d, 4, in_w // 4).transpose(1, 0, 2))
        started.append(rec)
        dx, dxb, dg = _rms_bwd(xin, norm_mix[l] + tok_in, dh1, dx_mid, "rms_mix_bwd")
        small["norm_mix"].append(dg)
        finish_grads(pending, dx)
        pending = started
    finish_grads(pending, dx)

    for lst in small.values():
        lst.reverse()

    dt5 = jnp.einsum("vhqk,vqkb->bh", dbias_a_total, a_onehot, precision=HIGHEST)
    small_names = ["norm_mix", "a_sink", "t5_table", "b_rpb", "c_q_gain", "c_k_gain", "out_gain_a", "out_gain_b",
                   "out_gain_c", "norm_mlp", "norm_final"]
    small_w = [norm_mix, a_sink, t5_table, b_rpb, c_q_gain, c_k_gain, out_gain_a, out_gain_b, out_gain_c, norm_mlp,
               norm_final]
    small_m = [m_norm_mix, m_a_sink, m_t5_table, m_b_rpb, m_c_q_gain, m_c_k_gain, m_out_gain_a, m_out_gain_b,
               m_out_gain_c, m_norm_mlp, m_norm_final]
    small_v = [v_norm_mix, v_a_sink, v_t5_table, v_b_rpb, v_c_q_gain, v_c_k_gain, v_out_gain_a, v_out_gain_b,
               v_out_gain_c, v_norm_mlp, v_norm_final]
    small_g = [jnp.stack(small["norm_mix"]), jnp.stack(small["a_sink"]), dt5, jnp.stack(small["b_rpb"]),
               jnp.stack(small["cq"]), jnp.stack(small["ck"]), jnp.stack(small["oga"]), jnp.stack(small["ogb"]),
               jnp.stack(small["ogc"]), jnp.stack(small["norm_mlp"]), dg_final]
    shapes = [w.shape for w in small_w]
    total = sum(math.prod(shp) for shp in shapes) + 1
    rows = -(-total // (8 * LANES)) * 8
    one = [jnp.ones((1,), F32)]
    gs, dl, mo, vo = _allreduce_small_adam(_pack(small_g + [loss_part[0, :1]], rows), _pack(small_w + one, rows),
                                           _pack(small_m + one, rows), _pack(small_v + one, rows))
    sg = _unpack(gs, shapes + [(1,)])
    sd, sm, sv = _unpack(dl, shapes), _unpack(mo, shapes), _unpack(vo, shapes)
    loss = sg[-1].reshape(())

    by_name = {nm: (sg[i], sd[i], sm[i], sv[i]) for i, nm in enumerate(small_names)}
    by_name.update(big)
    order = ["norm_mix", "w_in", "a_sink", "t5_table", "b_rpb", "c_q_gain", "c_k_gain", "out_gain_a", "out_gain_b",
             "out_gain_c", "w_o", "norm_mlp", "w_up", "w_down", "norm_final"]
    outs = [loss, dx.reshape(x.shape)]
    for field in range(4):
        outs.extend(by_name[nm][field] for nm in order)
    return tuple(outs)
```

```python
import functools
import math

import jax
import jax.numpy as jnp
from jax import lax
from jax.experimental import pallas as pl
from jax.experimental.pallas import tpu as pltpu

F32 = jnp.float32
MXU_DT = jnp.bfloat16
GRAD_DT = jnp.bfloat16
HIGHEST = lax.Precision.HIGHEST

HEAD_DIM = 64
LANES = 128
EPS = 1e-6
MASK_VALUE = -1e30
GRID_W = 64
NA_ROWS = 8
T5_BUCKETS = 32
T5_MAX_DIST = 128
ROPE_THETA = 10000.0
ADAM_LR, ADAM_B1, ADAM_B2, ADAM_EPS, ADAM_WD, ADAM_STEP = 0.001, 0.9, 0.999, 1e-08, 0.01, 10
VMEM_LIMIT = 56 * 1024 * 1024

MESH_ID = pl.DeviceIdType.MESH
ANY = pl.BlockSpec(memory_space=pl.ANY)

NT_DIMS = (((1,), (1,)), ((), ()))
TN_DIMS = (((0,), (0,)), ((), ()))
NN_DIMS = (((1,), (0,)), ((), ()))


def _dot(a, b, dims=NN_DIMS):
    return lax.dot_general(a, b, dims, preferred_element_type=F32)


def _call(body, *, name, out_shape, grid=(), in_specs=None, out_specs=None, scratch=(), sem=None,
          prefetch=0, aliases=None):
    params = {"vmem_limit_bytes": VMEM_LIMIT}
    if sem is not None:
        params["dimension_semantics"] = sem
    kwargs = {}
    if aliases:
        kwargs["input_output_aliases"] = aliases
    if prefetch:
        spec = pltpu.PrefetchScalarGridSpec(num_scalar_prefetch=prefetch, grid=grid, in_specs=in_specs,
                                            out_specs=out_specs, scratch_shapes=list(scratch))
        return pl.pallas_call(body, grid_spec=spec, out_shape=out_shape, name=name,
                              compiler_params=pltpu.CompilerParams(**params), **kwargs)
    return pl.pallas_call(body, grid=grid, in_specs=in_specs, out_specs=out_specs, out_shape=out_shape,
                          scratch_shapes=list(scratch), name=name,
                          compiler_params=pltpu.CompilerParams(**params), **kwargs)


def _sds(shape, dtype=F32):
    return jax.ShapeDtypeStruct(tuple(shape), dtype)


def _matmul(a, b, *, mode, name, tm, tn, tk, epi="plain", extra=(), out_dtypes=(F32,), mkn=None,
            b_spec=None, out_spec=None, out_shape=None):
    if mkn is None:
        if mode == "nn":
            (m, k), n = a.shape, b.shape[1]
        elif mode == "nt":
            (m, k), n = a.shape, b.shape[0]
        else:
            (k, m), n = a.shape, b.shape[1]
    else:
        m, k, n = mkn
    tm, tn, tk = min(tm, m), min(tn, n), min(tk, k)
    assert m % tm == 0 and n % tn == 0 and k % tk == 0, (name, m, n, k, tm, tn, tk)
    nk = k // tk
    dims = {"nn": NN_DIMS, "nt": NT_DIMS, "tn": TN_DIMS}[mode]
    n_extra, n_out = len(extra), len(out_dtypes)

    def body(a_ref, b_ref, *rest):
        extra_refs = rest[:n_extra]
        out_refs = rest[n_extra:n_extra + n_out]
        acc_ref = rest[n_extra + n_out]
        kk = pl.program_id(2)

        @pl.when(kk == 0)
        def _():
            acc_ref[...] = jnp.zeros_like(acc_ref)

        acc_ref[...] += _dot(a_ref[...].astype(MXU_DT), b_ref[...].astype(MXU_DT), dims)

        @pl.when(kk == nk - 1)
        def _():
            acc = acc_ref[...]
            if epi == "plain":
                out_refs[0][...] = acc.astype(out_refs[0].dtype)
            elif epi == "res":
                out_refs[0][...] = (extra_refs[0][...] + acc).astype(out_refs[0].dtype)
            elif epi == "relu2":
                u = jnp.maximum(acc, 0.0)
                out_refs[0][...] = u.astype(out_refs[0].dtype)
                out_refs[1][...] = (u * u).astype(out_refs[1].dtype)
            elif epi == "mul2u":
                out_refs[0][...] = (2.0 * extra_refs[0][...] * acc).astype(out_refs[0].dtype)
            else:
                raise ValueError(epi)

    if mode == "tn":
        a_spec = pl.BlockSpec((tk, tm), lambda i, j, kk: (kk, i))
    else:
        a_spec = pl.BlockSpec((tm, tk), lambda i, j, kk: (i, kk))
    if b_spec is None:
        if mode == "nt":
            b_spec = pl.BlockSpec((tn, tk), lambda i, j, kk: (j, kk))
        else:
            b_spec = pl.BlockSpec((tk, tn), lambda i, j, kk: (kk, j))
    mn_spec = pl.BlockSpec((tm, tn), lambda i, j, kk: (i, j))
    if out_spec is None:
        out_spec = mn_spec
    if out_shape is None:
        out_shape = (m, n)
    res = _call(body, name=name, grid=(m // tm, n // tn, nk),
                in_specs=[a_spec, b_spec] + [mn_spec] * n_extra,
                out_specs=[out_spec] * n_out,
                out_shape=[_sds(out_shape, d) for d in out_dtypes],
                scratch=[pltpu.VMEM((tm, tn), F32)],
                sem=("parallel", "parallel", "arbitrary"))(a, b, *extra)
    return res if n_out > 1 else res[0]


def _row_tile(s):
    return min(512, s)


def _rms_fwd(x, g, name):
    s, d = x.shape
    tm = _row_tile(s)

    def body(x_ref, g_ref, h_ref):
        xv = x_ref[...]
        r = lax.rsqrt(jnp.mean(xv * xv, axis=-1, keepdims=True) + EPS)
        h_ref[...] = ((xv * r) * g_ref[...]).astype(h_ref.dtype)

    return _call(body, name=name, grid=(s // tm,),
                 in_specs=[pl.BlockSpec((tm, d), lambda i: (i, 0)), pl.BlockSpec((1, d), lambda i: (0, 0))],
                 out_specs=pl.BlockSpec((tm, d), lambda i: (i, 0)),
                 out_shape=_sds((s, d), MXU_DT), sem=("parallel",))(x, g.reshape(1, d))


def _rms_bwd(x, g, dh, dres, name):
    s, d = x.shape
    tm = _row_tile(s)

    def body(x_ref, g_ref, dh_ref, dres_ref, dx_ref, dxb_ref, dg_ref):
        i = pl.program_id(0)
        xv = x_ref[...]
        r = lax.rsqrt(jnp.mean(xv * xv, axis=-1, keepdims=True) + EPS)
        xh = xv * r
        dhv = dh_ref[...]
        gd = dhv * g_ref[...]
        c = jnp.mean(gd * xh, axis=-1, keepdims=True)
        dx = dres_ref[...] + r * (gd - xh * c)
        dx_ref[...] = dx
        dxb_ref[...] = dx.astype(dxb_ref.dtype)
        part = jnp.sum(dhv * xh, axis=0, keepdims=True)

        @pl.when(i == 0)
        def _():
            dg_ref[...] = part

        @pl.when(i > 0)
        def _():
            dg_ref[...] += part

    row = pl.BlockSpec((tm, d), lambda i: (i, 0))
    vec = pl.BlockSpec((1, d), lambda i: (0, 0))
    return _call(body, name=name, grid=(s // tm,), in_specs=[row, vec, row, row],
                 out_specs=[row, row, vec],
                 out_shape=[_sds((s, d)), _sds((s, d), MXU_DT), _sds((1, d))],
                 sem=("arbitrary",))(x, g.reshape(1, d), dh, dres)


def _final_loss(x, g, target, name):
    s, d = x.shape
    tm = _row_tile(s)

    def body(x_ref, g_ref, t_ref, loss_ref, dx_ref, dxb_ref, dg_ref):
        i = pl.program_id(0)
        xv = x_ref[...]
        gv = g_ref[...]
        r = lax.rsqrt(jnp.mean(xv * xv, axis=-1, keepdims=True) + EPS)
        xh = xv * r
        err = xh * gv - t_ref[...]
        part_loss = 0.5 * jnp.sum(jnp.mean(err * err, axis=-1, keepdims=True), axis=0, keepdims=True)
        dy = err * (1.0 / d)
        gd = dy * gv
        c = jnp.mean(gd * xh, axis=-1, keepdims=True)
        dx = r * (gd - xh * c)
        dx_ref[...] = dx
        dxb_ref[...] = dx.astype(dxb_ref.dtype)
        part_g = jnp.sum(dy * xh, axis=0, keepdims=True)
        part_l = jnp.broadcast_to(part_loss, (1, LANES))

        @pl.when(i == 0)
        def _():
            dg_ref[...] = part_g
            loss_ref[...] = part_l

        @pl.when(i > 0)
        def _():
            dg_ref[...] += part_g
            loss_ref[...] += part_l

    row = pl.BlockSpec((tm, d), lambda i: (i, 0))
    vec = pl.BlockSpec((1, d), lambda i: (0, 0))
    return _call(body, name=name, grid=(s // tm,), in_specs=[row, vec, row],
                 out_specs=[pl.BlockSpec((1, LANES), lambda i: (0, 0)), row, row, vec],
                 out_shape=[_sds((1, LANES)), _sds((s, d)), _sds((s, d), MXU_DT), _sds((1, d))],
                 sem=("arbitrary",))(x, g.reshape(1, d), target)


def _lane_iota(shape):
    return lax.broadcasted_iota(jnp.int32, shape, len(shape) - 1)


def _swap_halves(x):
    return pltpu.roll(x, HEAD_DIM, 1)


def _segsum64(x, ones_ref):
    ones = ones_ref[...]
    outs = []
    for c in range(x.shape[1] // LANES):
        xc = x[:, c * LANES:(c + 1) * LANES]
        hi = xc.astype(MXU_DT)
        r1 = xc - hi.astype(F32)
        mid = r1.astype(MXU_DT)
        lo = (r1 - mid.astype(F32)).astype(MXU_DT)
        outs.append(_dot(hi, ones) + _dot(mid, ones) + _dot(lo, ones))
    return outs[0] if len(outs) == 1 else jnp.concatenate(outs, axis=1)


def _pair_ones():
    i = jnp.arange(LANES)
    return (i[:, None] // HEAD_DIM == i[None, :] // HEAD_DIM).astype(MXU_DT)


def _col(x, lane):
    return jnp.sum(jnp.where(_lane_iota(x.shape) == lane, x, 0.0), axis=-1, keepdims=True)


class _LocalCfg:
    def __init__(self, *, groups, qb, kw, qw, sub, qcol, kcol, vcol, kvhalf, kstart, variant, variant_py):
        self.groups, self.qb, self.kw, self.qw = groups, qb, kw, qw
        self.sub = sub
        self.qcol, self.kcol, self.vcol = qcol, kcol, vcol
        self.kvhalf = kvhalf
        self.kstart, self.variant = kstart, variant
        self.variant_py = variant_py
        self.pairs = qw // LANES


def _cfg_a(s):
    nb = s // 128
    return _LocalCfg(groups=1, qb=128, kw=384, qw=512, sub=1, qcol=lambda g: 0, kcol=lambda g: 4,
                     vcol=lambda g: 5, kvhalf=lambda t, e: t // 2,
                     kstart=lambda n: 128 * jnp.clip(n - 1, 0, nb - 3),
                     variant=lambda n: jnp.where(n <= 0, 0, jnp.where(n == nb - 1, 2, 1)),
                     variant_py=lambda n: 0 if n <= 0 else (2 if n == nb - 1 else 1))


def _cfg_b(s):
    rows = s // GRID_W
    return _LocalCfg(groups=4, qb=64, kw=512, qw=128, sub=4, qcol=lambda g: 6 + g, kcol=lambda g: 10 + g,
                     vcol=lambda g: 14 + g, kvhalf=lambda t, e: e,
                     kstart=lambda n: GRID_W * jnp.clip(n - NA_ROWS // 2, 0, rows - NA_ROWS),
                     variant=lambda n: jnp.where(n < 4, jnp.maximum(n, 0),
                                                 jnp.where(n > rows - 4, n - (rows - 8), 4)),
                     variant_py=lambda n: max(n, 0) if n < 4 else (n - (rows - 8) if n > rows - 4 else 4))


def _sum_visited(parts, cfg, s):
    n_var = parts[0].shape[0]
    variants = [cfg.variant_py(n) for n in range(s // cfg.qb)]
    total = None
    for i, part in enumerate(parts):
        seen = jnp.array([v in variants[i::cfg.sub] for v in range(n_var)]).reshape(n_var, 1, 1, 1)
        term = jnp.where(seen, part, 0.0)
        total = term if total is None else total + term
    return total


def _local_head(cfg, t, e, qp, qp_sw, kb, bias, sink_row, left_q):
    kvh = cfg.kvhalf(t, e)
    qsrc = qp if e == kvh else qp_sw
    keep = left_q if kvh == 0 else jnp.logical_not(left_q)
    qm = jnp.where(keep, qsrc, 0.0).astype(MXU_DT)
    sc = _dot(qm, kb, NT_DIMS) + bias
    snk = _col(sink_row, 2 * t + e)
    m = jnp.maximum(jnp.max(sc, axis=-1, keepdims=True), snk)
    p = jnp.exp(sc - m)
    l = jnp.sum(p, axis=-1, keepdims=True) + jnp.exp(snk - m)
    p = p / l
    return qm, keep, p, m, l, snk


def _local_attn_fwd(proj, bias, sink, cfg, name):
    s = proj.shape[0]
    qb, kw, qw, g_n = cfg.qb, cfg.kw, cfg.qw, cfg.groups
    hq = 2 * cfg.pairs

    sub = cfg.sub

    def body(q_ref, k_ref, v_ref, *rest):
        b_refs, s_ref, o_ref = rest[:sub], rest[sub], rest[sub + 1]
        n = pl.program_id(1)
        left_q = _lane_iota((qb, LANES)) < HEAD_DIM
        left_k = _lane_iota((kw, LANES)) < HEAD_DIM
        sink_row = s_ref[...]
        for i in range(sub):
            ks = pl.multiple_of(cfg.kstart(sub * n + i), 64)
            kf = k_ref[pl.ds(ks, kw), :]
            vf = v_ref[pl.ds(ks, kw), :]
            kb = kf.astype(MXU_DT)
            vf_sw = _swap_halves(vf)
            rows = slice(i * qb, (i + 1) * qb)
            for t in range(cfg.pairs):
                qp = q_ref[rows, t * LANES:(t + 1) * LANES] * 0.125
                qp_sw = _swap_halves(qp)
                acc = jnp.zeros((qb, LANES), F32)
                for e in range(2):
                    _, _, p, _, _, _ = _local_head(cfg, t, e, qp, qp_sw, kb, b_refs[i][0, 2 * t + e], sink_row,
                                                   left_q)
                    vsrc = vf if e == cfg.kvhalf(t, e) else vf_sw
                    vsel = jnp.where(left_k if e == 0 else jnp.logical_not(left_k), vsrc, 0.0).astype(MXU_DT)
                    acc = acc + _dot(p.astype(MXU_DT), vsel)
                o_ref[rows, t * LANES:(t + 1) * LANES] = acc

    def bias_spec(i):
        return pl.BlockSpec((1, hq, qb, kw), lambda g, n: (cfg.variant(sub * n + i), g, 0, 0))

    return _call(
        body, name=name, grid=(g_n, s // (sub * qb)),
        in_specs=[pl.BlockSpec((sub * qb, qw), lambda g, n: (n, cfg.qcol(g))),
                  pl.BlockSpec((s, LANES), lambda g, n: (0, cfg.kcol(g))),
                  pl.BlockSpec((s, LANES), lambda g, n: (0, cfg.vcol(g)))]
        + [bias_spec(i) for i in range(sub)]
        + [pl.BlockSpec((None, 1, LANES), lambda g, n: (g, 0, 0))],
        out_specs=pl.BlockSpec((sub * qb, qw), lambda g, n: (n, g)),
        out_shape=_sds((s, g_n * qw)), sem=("parallel", "arbitrary"))(proj, proj, proj, *([bias] * sub), sink)


def _local_attn_bwd(proj, bias, sink, do, cfg, name):
    s = proj.shape[0]
    qb, kw, qw, g_n = cfg.qb, cfg.kw, cfg.qw, cfg.groups
    hq = 2 * cfg.pairs

    sub = cfg.sub

    def body(q_ref, k_ref, v_ref, *rest):
        b_refs, s_ref, do_ref = rest[:sub], rest[sub], rest[sub + 1]
        dq_ref, dk_ref, dv_ref = rest[sub + 2:sub + 5]
        db_refs, dsk_ref = rest[sub + 5:2 * sub + 5], rest[2 * sub + 5]
        n = pl.program_id(1)

        @pl.when(n == 0)
        def _():
            dk_ref[...] = jnp.zeros_like(dk_ref)
            dv_ref[...] = jnp.zeros_like(dv_ref)
            dsk_ref[...] = jnp.zeros_like(dsk_ref)

        left_q = _lane_iota((qb, LANES)) < HEAD_DIM
        left_k = _lane_iota((kw, LANES)) < HEAD_DIM
        sink_row = s_ref[...]
        row0 = lax.broadcasted_iota(jnp.int32, (8, LANES), 0) == 0
        lane8 = _lane_iota((8, LANES))
        dsk_acc = jnp.zeros((8, LANES), F32)
        for i in range(sub):
            blk = sub * n + i
            ks = pl.multiple_of(cfg.kstart(blk), 64)
            db_ref = db_refs[i]

            @pl.when(jnp.logical_or(n == 0, cfg.variant(blk) != cfg.variant(blk - sub)))
            def _():
                db_ref[...] = jnp.zeros_like(db_ref)

            kf = k_ref[pl.ds(ks, kw), :]
            vf = v_ref[pl.ds(ks, kw), :]
            kb = kf.astype(MXU_DT)
            vb = vf.astype(MXU_DT)
            kf_sw = _swap_halves(kf)
            rows = slice(i * qb, (i + 1) * qb)
            dk_acc = jnp.zeros((kw, LANES), F32)
            dv_acc = jnp.zeros((kw, LANES), F32)
            for t in range(cfg.pairs):
                qp = q_ref[rows, t * LANES:(t + 1) * LANES] * 0.125
                qp_sw = _swap_halves(qp)
                dop = do_ref[rows, t * LANES:(t + 1) * LANES]
                dop_sw = _swap_halves(dop)
                dq_t = jnp.zeros((qb, LANES), F32)
                for e in range(2):
                    h = 2 * t + e
                    qm, keep, p, m, l, snk = _local_head(cfg, t, e, qp, qp_sw, kb, b_refs[i][0, h], sink_row,
                                                         left_q)
                    kvh = cfg.kvhalf(t, e)
                    dom = jnp.where(keep, dop if e == kvh else dop_sw, 0.0).astype(MXU_DT)
                    dp = _dot(dom, vb, NT_DIMS)
                    dd = jnp.sum(p * dp, axis=-1, keepdims=True)
                    ds = p * (dp - dd)
                    p_sink = jnp.exp(snk - m) / l
                    dsink = jnp.sum(-p_sink * dd, axis=0, keepdims=True)
                    dsk_acc = dsk_acc + jnp.where(jnp.logical_and(row0, lane8 == h), dsink, 0.0)
                    dsb = ds.astype(MXU_DT)
                    dv_acc = dv_acc + _dot(p.astype(MXU_DT), dom, TN_DIMS)
                    dk_acc = dk_acc + _dot(dsb, qm, TN_DIMS)
                    ksrc = kf if e == kvh else kf_sw
                    ksel = jnp.where(left_k if e == 0 else jnp.logical_not(left_k), ksrc, 0.0).astype(MXU_DT)
                    dq_t = dq_t + _dot(dsb, ksel)
                    db_ref[0, h] += ds
                dq_ref[rows, t * LANES:(t + 1) * LANES] = dq_t * 0.125
            dk_ref[pl.ds(ks, kw), :] += dk_acc
            dv_ref[pl.ds(ks, kw), :] += dv_acc
        dsk_ref[...] += dsk_acc

    def bias_spec(i):
        return pl.BlockSpec((1, hq, qb, kw), lambda g, n: (cfg.variant(sub * n + i), g, 0, 0))

    n_var = bias.shape[0]
    res = _call(
        body, name=name, grid=(g_n, s // (sub * qb)),
        in_specs=[pl.BlockSpec((sub * qb, qw), lambda g, n: (n, cfg.qcol(g))),
                  pl.BlockSpec((s, LANES), lambda g, n: (0, cfg.kcol(g))),
                  pl.BlockSpec((s, LANES), lambda g, n: (0, cfg.vcol(g)))]
        + [bias_spec(i) for i in range(sub)]
        + [pl.BlockSpec((None, 1, LANES), lambda g, n: (g, 0, 0)),
           pl.BlockSpec((sub * qb, qw), lambda g, n: (n, g))],
        out_specs=[pl.BlockSpec((sub * qb, qw), lambda g, n: (n, g)),
                   pl.BlockSpec((s, LANES), lambda g, n: (0, g)),
                   pl.BlockSpec((s, LANES), lambda g, n: (0, g))]
        + [bias_spec(i) for i in range(sub)]
        + [pl.BlockSpec((None, 8, LANES), lambda g, n: (g, 0, 0))],
        out_shape=[_sds((s, g_n * qw)), _sds((s, g_n * LANES)), _sds((s, g_n * LANES))]
        + [_sds((n_var, g_n * hq, qb, kw))] * sub + [_sds((g_n, 8, LANES))],
        sem=("parallel", "arbitrary"))(proj, proj, proj, *([bias] * sub), sink, do)
    dq, dk, dv = res[:3]
    dbias = _sum_visited(res[3:3 + sub], cfg, s)
    return dq, dk, dv, dbias, res[3 + sub]


QC_COL, KC_COL, VC_COL = 9, 13, 14
CW = 256


def _swap16(x):
    w = x.shape[1]
    lane = _lane_iota(x.shape)
    return jnp.where(lane % 32 < 16, pltpu.roll(x, w - 16, 1), pltpu.roll(x, 16, 1))


def _dup_halves(x):
    left = _lane_iota(x.shape) < HEAD_DIM
    sw = _swap_halves(x)
    return jnp.where(left, x, sw), jnp.where(left, sw, x)


def _normrope(x, gain, cos, sin, ones_ref):
    ms = _segsum64(x * x, ones_ref) * (1.0 / HEAD_DIM)
    r = lax.rsqrt(ms + EPS)
    y = (x * r) * gain
    return y * cos + _swap16(y) * sin, r


def _cprep_fwd(proj, gq, gk, cos, sin, ones, name):
    s = proj.shape[0]
    tm = _row_tile(s)

    def body(q0, q1, q2, q3, k_ref, v_ref, gq_ref, gk_ref, cos_ref, sin_ref, ones_ref, qh_ref, kd_ref, vd_ref):
        cos_v, sin_v = cos_ref[...], sin_ref[...]
        for c, q_ref in enumerate((q0, q1, q2, q3)):
            y, _ = _normrope(q_ref[...], gq_ref[...], cos_v, sin_v, ones_ref)
            qh_ref[:, c * CW:(c + 1) * CW] = (y * 0.125).astype(qh_ref.dtype)
        yk, _ = _normrope(k_ref[...], gk_ref[...], cos_v, sin_v, ones_ref)
        vv = v_ref[...]
        for p in range(2):
            ka, kb_ = _dup_halves(yk[:, p * LANES:(p + 1) * LANES])
            va, vb_ = _dup_halves(vv[:, p * LANES:(p + 1) * LANES])
            kd_ref[:, (2 * p) * LANES:(2 * p + 1) * LANES] = ka.astype(kd_ref.dtype)
            kd_ref[:, (2 * p + 1) * LANES:(2 * p + 2) * LANES] = kb_.astype(kd_ref.dtype)
            vd_ref[:, (2 * p) * LANES:(2 * p + 1) * LANES] = va.astype(vd_ref.dtype)
            vd_ref[:, (2 * p + 1) * LANES:(2 * p + 2) * LANES] = vb_.astype(vd_ref.dtype)

    def chunk(col):
        return pl.BlockSpec((tm, CW), lambda i: (i, col))

    vec = pl.BlockSpec((1, CW), lambda i: (0, 0))
    tab = pl.BlockSpec((tm, CW), lambda i: (i, 0))
    return _call(body, name=name, grid=(s // tm,),
                 in_specs=[chunk(QC_COL), chunk(QC_COL + 1), chunk(QC_COL + 2), chunk(QC_COL + 3),
                           chunk(KC_COL), chunk(VC_COL), vec, vec, tab, tab,
                           pl.BlockSpec((LANES, LANES), lambda i: (0, 0))],
                 out_specs=[pl.BlockSpec((tm, 4 * CW), lambda i: (i, 0)),
                            pl.BlockSpec((tm, 2 * CW), lambda i: (i, 0)),
                            pl.BlockSpec((tm, 2 * CW), lambda i: (i, 0))],
                 out_shape=[_sds((s, 4 * CW), MXU_DT), _sds((s, 2 * CW), MXU_DT), _sds((s, 2 * CW), MXU_DT)],
                 sem=("parallel",))(proj, proj, proj, proj, proj, proj, gq, gk, cos, sin, ones)


def _cprep_bwd(proj, gq, gk, cos, sin, ones, dqh, dkd, dvd, name):
    s = proj.shape[0]
    tm = _row_tile(s)

    def fold(ref, p):
        a = ref[:, (2 * p) * LANES:(2 * p + 1) * LANES]
        b = ref[:, (2 * p + 1) * LANES:(2 * p + 2) * LANES]
        ta = a + _swap_halves(a)
        tb = b + _swap_halves(b)
        return jnp.where(_lane_iota(a.shape) < HEAD_DIM, ta, tb)

    def norm_bwd(x, gain, dyr, cos_v, sin_v, ones_ref):
        dy = dyr * cos_v + _swap16(dyr * sin_v)
        ms = _segsum64(x * x, ones_ref) * (1.0 / HEAD_DIM)
        r = lax.rsqrt(ms + EPS)
        xh = x * r
        gd = dy * gain
        c = _segsum64(gd * xh, ones_ref) * (1.0 / HEAD_DIM)
        return r * (gd - xh * c), jnp.sum(dy * xh, axis=0, keepdims=True)

    def body(q0, q1, q2, q3, k_ref, gq_ref, gk_ref, cos_ref, sin_ref, ones_ref, dqh_ref, dkd_ref, dvd_ref,
             dq_ref, dk_ref, dv_ref, dgq_ref, dgk_ref):
        i = pl.program_id(0)
        cos_v, sin_v = cos_ref[...], sin_ref[...]
        gq_part = jnp.zeros((1, CW), F32)
        for c, q_ref in enumerate((q0, q1, q2, q3)):
            dx, dg = norm_bwd(q_ref[...], gq_ref[...], dqh_ref[:, c * CW:(c + 1) * CW] * 0.125, cos_v, sin_v,
                              ones_ref)
            dq_ref[:, c * CW:(c + 1) * CW] = dx
            gq_part = gq_part + dg
        dkr = jnp.concatenate([fold(dkd_ref, 0), fold(dkd_ref, 1)], axis=1)
        dxk, gk_part = norm_bwd(k_ref[...], gk_ref[...], dkr, cos_v, sin_v, ones_ref)
        dk_ref[...] = dxk
        dv_ref[...] = jnp.concatenate([fold(dvd_ref, 0), fold(dvd_ref, 1)], axis=1)

        @pl.when(i == 0)
        def _():
            dgq_ref[...] = gq_part
            dgk_ref[...] = gk_part

        @pl.when(i > 0)
        def _():
            dgq_ref[...] += gq_part
            dgk_ref[...] += gk_part

    def chunk(col):
        return pl.BlockSpec((tm, CW), lambda i: (i, col))

    vec = pl.BlockSpec((1, CW), lambda i: (0, 0))
    tab = pl.BlockSpec((tm, CW), lambda i: (i, 0))
    return _call(body, name=name, grid=(s // tm,),
                 in_specs=[chunk(QC_COL), chunk(QC_COL + 1), chunk(QC_COL + 2), chunk(QC_COL + 3), chunk(KC_COL),
                           vec, vec, tab, tab, pl.BlockSpec((LANES, LANES), lambda i: (0, 0)),
                           pl.BlockSpec((tm, 4 * CW), lambda i: (i, 0)),
                           pl.BlockSpec((tm, 2 * CW), lambda i: (i, 0)),
                           pl.BlockSpec((tm, 2 * CW), lambda i: (i, 0))],
                 out_specs=[pl.BlockSpec((tm, 4 * CW), lambda i: (i, 0)), tab, tab, vec, vec],
                 out_shape=[_sds((s, 4 * CW)), _sds((s, CW)), _sds((s, CW)), _sds((1, CW)), _sds((1, CW))],
                 sem=("arbitrary",))(proj, proj, proj, proj, proj, gq, gk, cos, sin, ones, dqh, dkd, dvd)


def _flash_tiles(s):
    return min(512, s), min(512, s)


def _row_iota(shape):
    return lax.broadcasted_iota(jnp.int32, shape, 0)


def _flash_fwd(qh, kd, vdt, name):
    s = qh.shape[0]
    tq, tk = _flash_tiles(s)
    nk = s // tk

    n_chunks = 1
    cw = tq // n_chunks
    units = [(t, c, e) for t in range(2) for c in range(n_chunks) for e in range(2)]

    def body(q_ref, k_ref, vt_ref, ot_ref, lse_ref, qm_ref, m_ref, lacc_ref, acc_ref):
        j = pl.program_id(2)

        @pl.when(j == 0)
        def _():
            m_ref[...] = jnp.full(m_ref.shape, MASK_VALUE, F32)
            lacc_ref[...] = jnp.zeros_like(lacc_ref)
            acc_ref[...] = jnp.zeros_like(acc_ref)
            left_q = _lane_iota((tq, LANES)) < HEAD_DIM
            for t in range(2):
                qp = q_ref[:, t * LANES:(t + 1) * LANES]
                qm_ref[2 * t] = jnp.where(left_q, qp, jnp.zeros_like(qp))
                qm_ref[2 * t + 1] = jnp.where(left_q, jnp.zeros_like(qp), qp)

        kb = k_ref[...]
        vt = vt_ref[...]
        top_k = _row_iota((LANES, tk)) < HEAD_DIM
        top_c = _row_iota((LANES, cw)) < HEAD_DIM
        vt_e = (jnp.where(top_k, vt, jnp.ones_like(vt)), jnp.where(top_k, jnp.ones_like(vt), vt))

        def scores(unit):
            t, c, e = unit
            return _dot(kb, qm_ref[2 * t + e, c * cw:(c + 1) * cw, :], NT_DIMS)

        nxt = scores(units[0])
        pv, alpha = [], []
        for n, (t, c, e) in enumerate(units):
            st = nxt
            if n + 1 < len(units):
                nxt = scores(units[n + 1])
            h = 2 * t + e
            cols = slice(c * cw, (c + 1) * cw)
            m_prev = m_ref[h, :, cols]
            m_new = jnp.maximum(m_prev, jnp.max(st, axis=0, keepdims=True))
            alpha.append(jnp.exp(m_prev - m_new))
            pt = jnp.exp(st - m_new)
            m_ref[h, :, cols] = m_new
            pv.append(_dot(vt_e[e], pt.astype(MXU_DT)))
            if e == 1:
                acc_ref[t, :, cols] = (acc_ref[t, :, cols] * jnp.where(top_c, alpha[0], alpha[1])
                                       + jnp.where(top_c, pv[0], pv[1]))
                lacc_ref[t, :, cols] = (lacc_ref[t, :, cols] * jnp.where(top_c, alpha[1], alpha[0])
                                        + jnp.where(top_c, pv[1], pv[0]))
                pv, alpha = [], []

        @pl.when(j == nk - 1)
        def _():
            for t in range(2):
                lacc = lacc_ref[t]
                l_sw = jnp.concatenate([lacc[HEAD_DIM:], lacc[:HEAD_DIM]], axis=0)
                ot_ref[t * LANES:(t + 1) * LANES, :] = acc_ref[t] / l_sw
                lse_ref[2 * t:2 * t + 1, :] = m_ref[2 * t] + jnp.log(lacc[HEAD_DIM:HEAD_DIM + 1])
                lse_ref[2 * t + 1:2 * t + 2, :] = m_ref[2 * t + 1] + jnp.log(lacc[0:1])

    return _call(body, name=name, grid=(4, s // tq, nk),
                 in_specs=[pl.BlockSpec((tq, CW), lambda g, i, j: (i, g)),
                           pl.BlockSpec((tk, LANES), lambda g, i, j: (j, g)),
                           pl.BlockSpec((LANES, tk), lambda g, i, j: (g, j))],
                 out_specs=[pl.BlockSpec((CW, tq), lambda g, i, j: (g, i)),
                            pl.BlockSpec((None, 4, tq), lambda g, i, j: (g, 0, i))],
                 out_shape=[_sds((4 * CW, s)), _sds((4, 4, s))],
                 scratch=[pltpu.VMEM((4, tq, LANES), MXU_DT), pltpu.VMEM((4, 1, tq), F32),
                          pltpu.VMEM((2, LANES, tq), F32), pltpu.VMEM((2, LANES, tq), F32)],
                 sem=("parallel", "parallel", "arbitrary"))(qh, kd, vdt)


def _flash_bwd(qh, kd, vd, kdt, do, lse, dd, name):
    s = qh.shape[0]
    tq, tk = _flash_tiles(s)
    ni = s // tq

    def body(q_ref, k_ref, v_ref, kt_ref, do_ref, lse_ref, dd_ref, dqt_ref, dk_ref, dv_ref, dk_acc, dv_acc):
        j = pl.program_id(1)
        i = pl.program_id(2)

        @pl.when(i == 0)
        def _():
            dk_acc[...] = jnp.zeros_like(dk_acc)
            dv_acc[...] = jnp.zeros_like(dv_acc)

        kb = k_ref[...]
        vb = v_ref[...]
        kt = kt_ref[...]
        left_q = _lane_iota((tq, LANES)) < HEAD_DIM
        top = _row_iota((LANES, tq)) < HEAD_DIM
        cols = pl.ds(pl.multiple_of(i * tq, tq), tq)
        def first_stage(h):
            t, e = divmod(h, 2)
            keep_q = left_q if e == 0 else jnp.logical_not(left_q)
            qp = q_ref[:, t * LANES:(t + 1) * LANES]
            dop = do_ref[:, t * LANES:(t + 1) * LANES]
            qm = jnp.where(keep_q, qp, jnp.zeros_like(qp))
            dom = jnp.where(keep_q, dop, jnp.zeros_like(dop))
            return qm, dom, _dot(kb, qm, NT_DIMS), _dot(vb, dom, NT_DIMS)

        nxt = first_stage(0)
        dqt = []
        for h in range(4):
            qm, dom, st, dpt = nxt
            if h < 3:
                nxt = first_stage(h + 1)
            pt = jnp.exp(st - lse_ref[h:h + 1, :])
            dsb = (pt * (dpt - dd_ref[h:h + 1, :])).astype(MXU_DT)
            dv_acc[...] += _dot(pt.astype(MXU_DT), dom)
            dk_acc[...] += _dot(dsb, qm)
            dqt.append(_dot(kt, dsb))
            if h % 2 == 1:
                t = h // 2
                dq_t = jnp.where(top, dqt[0], dqt[1])
                dqt = []

                @pl.when(j == 0)
                def _():
                    dqt_ref[t * LANES:(t + 1) * LANES, cols] = dq_t

                @pl.when(j > 0)
                def _():
                    dqt_ref[t * LANES:(t + 1) * LANES, cols] += dq_t

        @pl.when(i == ni - 1)
        def _():
            dk_ref[...] = dk_acc[...]
            dv_ref[...] = dv_acc[...]

    qspec = pl.BlockSpec((tq, CW), lambda g, j, i: (i, g))
    kspec = pl.BlockSpec((tk, LANES), lambda g, j, i: (j, g))
    rowspec = pl.BlockSpec((None, 4, tq), lambda g, j, i: (g, 0, i))
    return _call(body, name=name, grid=(4, s // tk, ni),
                 in_specs=[qspec, kspec, kspec, pl.BlockSpec((LANES, tk), lambda g, j, i: (g, j)), qspec,
                           rowspec, rowspec],
                 out_specs=[pl.BlockSpec((CW, s), lambda g, j, i: (g, 0)), kspec, kspec],
                 out_shape=[_sds((4 * CW, s)), _sds((s, 2 * CW)), _sds((s, 2 * CW))],
                 scratch=[pltpu.VMEM((tk, LANES), F32), pltpu.VMEM((tk, LANES), F32)],
                 sem=("parallel", "arbitrary", "arbitrary"))(qh, kd, vd, kdt, do, lse, dd)


def _groupnorm_fwd(oa, ob, oc, ga, gb, gc, name):
    s = oa.shape[0]
    tm = _row_tile(s)
    wa, wb, wc = oa.shape[1], ob.shape[1], oc.shape[1]

    def body(oa_ref, ob_ref, oc_ref, ga_ref, gb_ref, gc_ref, mix_ref):
        off = 0
        for o_ref, g_ref, w in ((oa_ref, ga_ref, wa), (ob_ref, gb_ref, wb), (oc_ref, gc_ref, wc)):
            xv = o_ref[...]
            r = lax.rsqrt(jnp.mean(xv * xv, axis=-1, keepdims=True) + EPS)
            mix_ref[:, off:off + w] = ((xv * r) * g_ref[...]).astype(mix_ref.dtype)
            off += w

    def row(w):
        return pl.BlockSpec((tm, w), lambda i: (i, 0))

    def vec(w):
        return pl.BlockSpec((1, w), lambda i: (0, 0))

    return _call(body, name=name, grid=(s // tm,),
                 in_specs=[row(wa), row(wb), row(wc), vec(wa), vec(wb), vec(wc)],
                 out_specs=row(wa + wb + wc), out_shape=_sds((s, wa + wb + wc), MXU_DT),
                 sem=("parallel",))(oa, ob, oc, ga.reshape(1, wa), gb.reshape(1, wb), gc.reshape(1, wc))


def _groupnorm_bwd(dmix, oa, ob, oc, ga, gb, gc, ones, name):
    s = oa.shape[0]
    tm = _row_tile(s)
    wa, wb, wc = oa.shape[1], ob.shape[1], oc.shape[1]

    def body(dm_ref, oa_ref, ob_ref, oc_ref, ga_ref, gb_ref, gc_ref, ones_ref,
             doa_ref, dob_ref, doc_ref, docb_ref, dd_ref, dga_ref, dgb_ref, dgc_ref):
        i = pl.program_id(0)
        off = 0
        parts = []
        for o_ref, g_ref, do_ref, w in ((oa_ref, ga_ref, doa_ref, wa), (ob_ref, gb_ref, dob_ref, wb),
                                        (oc_ref, gc_ref, doc_ref, wc)):
            xv = o_ref[...]
            dh = dm_ref[:, off:off + w]
            r = lax.rsqrt(jnp.mean(xv * xv, axis=-1, keepdims=True) + EPS)
            xh = xv * r
            gd = dh * g_ref[...]
            c = jnp.mean(gd * xh, axis=-1, keepdims=True)
            dx = r * (gd - xh * c)
            do_ref[...] = dx
            parts.append(jnp.sum(dh * xh, axis=0, keepdims=True))
            if o_ref is oc_ref:
                docb_ref[...] = dx.astype(docb_ref.dtype)
                dd_ref[...] = _segsum64(dx * xv, ones_ref)
            off += w

        @pl.when(i == 0)
        def _():
            dga_ref[...], dgb_ref[...], dgc_ref[...] = parts

        @pl.when(i > 0)
        def _():
            dga_ref[...] += parts[0]
            dgb_ref[...] += parts[1]
            dgc_ref[...] += parts[2]

    def row(w):
        return pl.BlockSpec((tm, w), lambda i: (i, 0))

    def vec(w):
        return pl.BlockSpec((1, w), lambda i: (0, 0))

    return _call(body, name=name, grid=(s // tm,),
                 in_specs=[row(wa + wb + wc), row(wa), row(wb), row(wc), vec(wa), vec(wb), vec(wc),
                           pl.BlockSpec((LANES, LANES), lambda i: (0, 0))],
                 out_specs=[row(wa), row(wb), row(wc), row(wc), row(wc), vec(wa), vec(wb), vec(wc)],
                 out_shape=[_sds((s, wa)), _sds((s, wb)), _sds((s, wc)), _sds((s, wc), MXU_DT), _sds((s, wc)),
                            _sds((1, wa)), _sds((1, wb)), _sds((1, wc))],
                 sem=("arbitrary",))(dmix, oa, ob, oc, ga.reshape(1, wa), gb.reshape(1, wb), gc.reshape(1, wc), ones)


def _adam_math(w, g, m, v):
    m = ADAM_B1 * m + (1.0 - ADAM_B1) * g
    v = ADAM_B2 * v + (1.0 - ADAM_B2) * jnp.square(g)
    m_hat = m / (1.0 - ADAM_B1 ** ADAM_STEP)
    v_hat = v / (1.0 - ADAM_B2 ** ADAM_STEP)
    delta = -ADAM_LR * (m_hat / (jnp.sqrt(v_hat) + ADAM_EPS) + ADAM_WD * w)
    return delta, m, v


def _mesh_pos():
    return lax.axis_index("x"), lax.axis_index("y"), lax.axis_index("c")


def _peer_chips(x, y):
    return [(1 - x, y), (x, 1 - y), (1 - x, 1 - y)]


def _allreduce_small_adam(g, w, m, v):
    rows = g.shape[0]

    def body(g_ref, w_ref, m_ref, v_ref, gs_ref, d_ref, mo_ref, vo_ref, buf, send_sems, recv_sems):
        x, y, c = _mesh_pos()
        me = 4 * x + 2 * y + c
        buf[me] = g_ref[...]
        copies = []
        for k in range(1, 8):
            px = 1 - x if (k >> 2) & 1 else x
            py = 1 - y if (k >> 1) & 1 else y
            pc = 1 - c if k & 1 else c
            cp = pltpu.make_async_remote_copy(src_ref=g_ref, dst_ref=buf.at[me], send_sem=send_sems.at[k - 1],
                                              recv_sem=recv_sems.at[k - 1], device_id=(px, py, pc),
                                              device_id_type=MESH_ID)
            cp.start()
            copies.append(cp)
        for cp in copies:
            cp.wait()
        total = buf[0]
        for d in range(1, 8):
            total = total + buf[d]
        gs_ref[...] = total
        d_ref[...], mo_ref[...], vo_ref[...] = _adam_math(w_ref[...], total, m_ref[...], v_ref[...])

    vm = pl.BlockSpec(memory_space=pltpu.VMEM)
    return _call(body, name="allreduce_small_adam", in_specs=[vm] * 4, out_specs=[vm] * 4,
                 out_shape=[_sds((rows, LANES))] * 4,
                 scratch=[pltpu.VMEM((8, rows, LANES), F32), pltpu.SemaphoreType.DMA((7,)),
                          pltpu.SemaphoreType.DMA((7,))])(g, w, m, v)


HBM_SPEC = pl.BlockSpec(memory_space=pltpu.HBM)
SEM_SPEC = pl.BlockSpec(memory_space=pltpu.SEMAPHORE)
VMEM_SPEC = pl.BlockSpec(memory_space=pltpu.VMEM)
SIDE_EFFECT = pltpu.SideEffectType.DATAFLOW_SIDE_EFFECTING
N_PEERS = 7


def _in_hbm(a):
    return pltpu.with_memory_space_constraint(a, pltpu.HBM)


def _landing(shape, dtype):
    return _in_hbm(lax.empty(shape, dtype))


def _token_shape():
    return _sds((8, LANES))


def _gather_start(shards):
    n = len(shards)
    n_layers = shards[0].shape[0]
    jobs = [(l, t) for l in range(n_layers) for t in range(n)]
    nj = len(jobs)

    def body(*refs):
        sh = refs[:n]
        outs = refs[n + nj:]
        send, recv, land, token = outs[:nj], outs[nj:2 * nj], outs[2 * nj:3 * nj], outs[3 * nj]
        x, y, c = _mesh_pos()
        me = 2 * x + y
        for j, (l, t) in enumerate(jobs):
            for k, (px, py) in enumerate(_peer_chips(x, y)):
                pltpu.make_async_remote_copy(src_ref=sh[t].at[l], dst_ref=land[j].at[me], send_sem=send[j].at[k],
                                             recv_sem=recv[j].at[k], device_id=(px, py, c),
                                             device_id_type=MESH_ID).start()
        token[...] = jnp.zeros_like(token)

    lands = [_landing((4,) + shards[t].shape[1:], shards[t].dtype) for _, t in jobs]
    res = pl.pallas_call(
        body, name="gather_start",
        out_shape=tuple([pltpu.SemaphoreType.DMA((3,))] * (2 * nj)
                        + [pltpu.HBM(a.shape, a.dtype) for a in lands] + [_token_shape()]),
        in_specs=[HBM_SPEC] * (n + nj), out_specs=tuple([SEM_SPEC] * (2 * nj) + [HBM_SPEC] * nj + [VMEM_SPEC]),
        input_output_aliases={n + j: 2 * nj + j for j in range(nj)},
        compiler_params=pltpu.CompilerParams(has_side_effects=SIDE_EFFECT),
    )(*[_in_hbm(a) for a in shards], *lands)
    return jobs, res[:nj], res[nj:2 * nj], res[2 * nj:3 * nj], res[3 * nj]


def _gather_wait(shard, layer, land, send_sem, recv_sem, after, name):
    def body(sh_ref, land_ref, send_ref, recv_ref, after_ref, land_out):
        x, y, c = _mesh_pos()
        for k in range(3):
            cp = pltpu.make_async_remote_copy(src_ref=sh_ref.at[layer], dst_ref=land_ref.at[k],
                                              send_sem=send_ref.at[k], recv_sem=recv_ref.at[k],
                                              device_id=(x, y, 1 - c), device_id_type=MESH_ID)
            cp.wait_send()
            cp.wait_recv()

    return pl.pallas_call(
        body, name=name, out_shape=pltpu.HBM(land.shape, land.dtype),
        in_specs=[HBM_SPEC, HBM_SPEC, SEM_SPEC, SEM_SPEC, ANY], out_specs=HBM_SPEC,
        input_output_aliases={1: 0},
        compiler_params=pltpu.CompilerParams(has_side_effects=SIDE_EFFECT),
    )(shard, land, send_sem, recv_sem, after)


def _grad_start(g, name):
    def body(g_ref, land_in, send, recv, land, token):
        x, y, c = _mesh_pos()
        me = 2 * x + y
        pltpu.make_async_remote_copy(src_ref=g_ref.at[me], dst_ref=land.at[0], send_sem=send.at[0],
                                     recv_sem=recv.at[0], device_id=(x, y, 1 - c), device_id_type=MESH_ID).start()
        for k, (px, py) in enumerate(_peer_chips(x, y)):
            for c2 in range(2):
                pltpu.make_async_remote_copy(src_ref=g_ref.at[2 * px + py], dst_ref=land.at[1 + 2 * k + c],
                                             send_sem=send.at[1 + 2 * k + c2], recv_sem=recv.at[1 + 2 * k + c],
                                             device_id=(px, py, c2), device_id_type=MESH_ID).start()
        token[...] = jnp.zeros_like(token)

    land = _landing((N_PEERS,) + g.shape[1:], g.dtype)
    return pl.pallas_call(
        body, name=name,
        out_shape=(pltpu.SemaphoreType.DMA((N_PEERS,)), pltpu.SemaphoreType.DMA((N_PEERS,)),
                   pltpu.HBM(land.shape, land.dtype), _token_shape()),
        in_specs=[HBM_SPEC, HBM_SPEC], out_specs=(SEM_SPEC, SEM_SPEC, HBM_SPEC, VMEM_SPEC),
        input_output_aliases={1: 2},
        compiler_params=pltpu.CompilerParams(has_side_effects=SIDE_EFFECT),
    )(_in_hbm(g), land)


def _grad_wait(g, land, send_sem, recv_sem, after, name):
    def body(g_ref, land_ref, send_ref, recv_ref, after_ref, land_out):
        x, y, c = _mesh_pos()
        for k in range(N_PEERS):
            cp = pltpu.make_async_remote_copy(src_ref=g_ref.at[0], dst_ref=land_ref.at[k], send_sem=send_ref.at[k],
                                              recv_sem=recv_ref.at[k], device_id=(x, y, 1 - c),
                                              device_id_type=MESH_ID)
            cp.wait_send()
            cp.wait_recv()

    return pl.pallas_call(
        body, name=name, out_shape=pltpu.HBM(land.shape, land.dtype),
        in_specs=[HBM_SPEC, HBM_SPEC, SEM_SPEC, SEM_SPEC, ANY], out_specs=HBM_SPEC,
        input_output_aliases={1: 0},
        compiler_params=pltpu.CompilerParams(has_side_effects=SIDE_EFFECT),
    )(g, land, send_sem, recv_sem, after)


def _sum_adam(g, land, w, m, v, prev, layer, me_idx, name):
    _, r, cols = g.shape
    tr = min(128, r)

    def body(me_ref, g_ref, l0, l1, l2, l3, l4, l5, l6, w_ref, m_ref, v_ref, p0, p1, p2, p3,
             go_ref, d_ref, mo_ref, vo_ref):
        total = g_ref[...].astype(F32) + l0[...].astype(F32)
        for ref in (l1, l2, l3, l4, l5, l6):
            total = total + ref[...].astype(F32)
        go_ref[...] = total
        d_ref[...], mo_ref[...], vo_ref[...] = _adam_math(w_ref[...], total, m_ref[...], v_ref[...])

    def slot(k):
        return pl.BlockSpec((None, tr, cols), lambda i, me: (k, i, 0))

    lay = pl.BlockSpec((None, tr, cols), lambda i, me: (layer, i, 0))
    return _call(body, name=name, grid=(r // tr,), prefetch=1,
                 in_specs=[pl.BlockSpec((None, tr, cols), lambda i, me: (me[0], i, 0))]
                 + [slot(k) for k in range(N_PEERS)] + [lay, lay, lay] + [ANY] * 4,
                 out_specs=[lay] * 4, out_shape=[_sds(w.shape)] * 4,
                 aliases={12 + k: k for k in range(4)}, sem=("parallel",))(
                     me_idx, g, *([land] * N_PEERS), w, m, v, *prev)


def _t5_bucket(rel):
    nb = T5_BUCKETS // 2
    max_exact = nb // 2
    base = jnp.where(rel > 0, nb, 0)
    n = jnp.abs(rel)
    nf = jnp.maximum(n, 1).astype(F32)
    large = max_exact + (jnp.log(nf / max_exact) / math.log(T5_MAX_DIST / max_exact)
                         * (nb - max_exact)).astype(jnp.int32)
    large = jnp.minimum(large, nb - 1)
    return base + jnp.where(n < max_exact, n, large)


def _a_bias_maps():
    v = jnp.arange(3)[:, None, None]
    q = jnp.arange(128)[None, :, None]
    k = jnp.arange(384)[None, None, :]
    rel = k - 128 * v - q
    valid = jnp.abs(rel) <= 128
    onehot = (_t5_bucket(rel)[..., None] == jnp.arange(T5_BUCKETS)).astype(F32)
    return onehot * valid[..., None].astype(F32), valid


def _b_bias_maps():
    v = jnp.arange(8)[:, None]
    i = jnp.arange(NA_ROWS)[None, :]
    dr = jnp.where(v == 4, i + 3, i - v + 7)
    row_oh = (dr[..., None] == jnp.arange(2 * NA_ROWS - 1)).astype(F32)
    q = jnp.arange(GRID_W)[:, None]
    kc = jnp.arange(GRID_W)[None, :]
    cs = jnp.clip(q - 8, 0, GRID_W - 16)
    valid = (kc >= cs) & (kc < cs + 16)
    col_oh = ((kc - q + 15)[..., None] == jnp.arange(31)).astype(F32) * valid[..., None].astype(F32)
    return row_oh, col_oh, valid


def _rope_tables(s):
    t = jnp.arange(s)
    row = (t // GRID_W).astype(F32)
    col = (t % GRID_W).astype(F32)
    axis_dim = HEAD_DIM // 2
    freqs = ROPE_THETA ** (-jnp.arange(0, axis_dim, 2, dtype=F32) / axis_dim)
    ang_row = row[:, None] * freqs[None, :]
    ang_col = col[:, None] * freqs[None, :]
    cos = jnp.concatenate([jnp.cos(ang_row)] * 2 + [jnp.cos(ang_col)] * 2, axis=1)
    sin = jnp.concatenate([-jnp.sin(ang_row), jnp.sin(ang_row), -jnp.sin(ang_col), jnp.sin(ang_col)], axis=1)
    return jnp.tile(cos, (1, CW // HEAD_DIM)), jnp.tile(sin, (1, CW // HEAD_DIM))


def _pack(parts, rows):
    flat = jnp.concatenate([p.reshape(-1).astype(F32) for p in parts])
    return jnp.pad(flat, (0, rows * LANES - flat.shape[0])).reshape(rows, LANES)


def _unpack(buf, shapes):
    flat = buf.reshape(-1)
    out, off = [], 0
    for shp in shapes:
        size = math.prod(shp)
        out.append(flat[off:off + size].reshape(shp))
        off += size
    return out


def kernel(x, norm_mix, w_in, a_sink, t5_table, b_rpb, c_q_gain, c_k_gain, out_gain_a, out_gain_b, out_gain_c, w_o, norm_mlp, w_up, w_down, norm_final, loss_target, m_norm_mix, m_w_in, m_a_sink, m_t5_table, m_b_rpb, m_c_q_gain, m_c_k_gain, m_out_gain_a, m_out_gain_b, m_out_gain_c, m_w_o, m_norm_mlp, m_w_up, m_w_down, m_norm_final, v_norm_mix, v_w_in, v_a_sink, v_t5_table, v_b_rpb, v_c_q_gain, v_c_k_gain, v_out_gain_a, v_out_gain_b, v_out_gain_c, v_w_o, v_norm_mlp, v_w_up, v_w_down, v_norm_final):
    n_layers = w_in.shape[0]
    s, d = x.shape[1], x.shape[2]
    d_ff = 4 * w_up.shape[2]
    in_w = 4 * w_in.shape[2]
    xs = x.reshape(s, d)
    target = loss_target.reshape(s, d)
    cfg_a, cfg_b = _cfg_a(s), _cfg_b(s)

    x_i, y_i, _ = _mesh_pos()
    me_chip = 2 * x_i + y_i
    me_idx = me_chip.astype(jnp.int32).reshape(1)
    w_bf = [w_in.astype(MXU_DT), w_o.astype(MXU_DT), w_up.astype(MXU_DT), w_down.astype(MXU_DT)]
    jobs, gather_send, gather_recv, gather_land, gather_token = _gather_start(w_bf)
    job_of = {job: j for j, job in enumerate(jobs)}
    ff_shard = w_up.shape[2]

    def gathered(l, t, after):
        j = job_of[(l, t)]
        land = _gather_wait(w_bf[t], l, gather_land[j], gather_send[j], gather_recv[j], after,
                            "gather_wait_%d_%d" % (l, t))
        return lax.dynamic_update_slice(land, w_bf[t][l][None], (me_chip, 0, 0))

    ones = _pair_ones()
    cos_t, sin_t = _rope_tables(s)
    a_onehot, a_valid = _a_bias_maps()
    bias_a = jnp.where(a_valid[:, None], jnp.einsum("vqkb,bh->vhqk", a_onehot, t5_table, precision=HIGHEST),
                       MASK_VALUE)
    row_oh, col_oh, b_valid = _b_bias_maps()
    sink_b = jnp.full((4, 1, LANES), MASK_VALUE, F32)

    def b_bias(rpb):
        t = jnp.einsum("hrz,vir->vhiz", rpb, row_oh, precision=HIGHEST)
        t = jnp.einsum("vhiz,qcz->vhqic", t, col_oh, precision=HIGHEST)
        t = jnp.where(b_valid[None, None, :, None, :], t, MASK_VALUE)
        return t.reshape(8, 8, GRID_W, NA_ROWS * GRID_W)

    def tile_gain(gvec):
        return jnp.tile(gvec, CW // HEAD_DIM).reshape(1, CW)

    def pad_sink(svec):
        return jnp.pad(svec, (0, LANES - svec.shape[0])).reshape(1, 1, LANES)

    saved = []
    xc = xs
    for l in range(n_layers):
        h1 = _rms_fwd(xc, norm_mix[l] + gather_token[0, 0] if l == 0 else norm_mix[l], "rms_mix")
        wf_in = gathered(l, 0, h1).transpose(1, 0, 2).reshape(d, in_w)
        proj = _matmul(h1, wf_in, mode="nn", name="proj_in", tm=1024, tn=768, tk=2048)
        bias_b = b_bias(b_rpb[l])
        oa = _local_attn_fwd(proj, bias_a, pad_sink(a_sink[l]), cfg_a, "attn_a_fwd")
        ob = _local_attn_fwd(proj, bias_b, sink_b, cfg_b, "attn_b_fwd")
        gq, gk = tile_gain(c_q_gain[l]), tile_gain(c_k_gain[l])
        qh, kd, vd = _cprep_fwd(proj, gq, gk, cos_t, sin_t, ones, "cprep_fwd")
        kdt, vdt = kd.T, vd.T
        oct, lse = _flash_fwd(qh, kd, vdt, "attn_c_fwd")
        oc = oct.T
        mix = _groupnorm_fwd(oa, ob, oc, out_gain_a[l], out_gain_b[l], out_gain_c[l], "groupnorm_fwd")
        wf_o = gathered(l, 1, mix).reshape(d, d)
        x_mid = _matmul(mix, wf_o, mode="nn", name="proj_out", tm=1024, tn=1024, tk=2048, epi="res",
                        extra=(xc,))
        h2 = _rms_fwd(x_mid, norm_mlp[l], "rms_mlp")
        wg_up = gathered(l, 2, h2)
        nb_up = ff_shard // 1024
        u, uu = _matmul(h2, wg_up, mode="nn", name="mlp_up", tm=1024, tn=1024, tk=2048, epi="relu2",
                        out_dtypes=(F32, MXU_DT), mkn=(s, d, d_ff),
                        b_spec=pl.BlockSpec((None, 2048, 1024), lambda i, j, kk: (j // nb_up, kk, j % nb_up)))
        wf_down = gathered(l, 3, uu).reshape(d_ff, d)
        x_out = _matmul(uu, wf_down, mode="nn", name="mlp_down", tm=1024, tn=1024, tk=2048, epi="res",
                        extra=(x_mid,))
        saved.append((xc, h1, proj, bias_b, oa, ob, qh, kd, vd, kdt, oc, lse, mix, x_mid, h2, u, uu,
                      wf_in, wf_o, wg_up, wf_down))
        xc = x_out

    loss_part, dx, dxb, dg_final = _final_loss(xc, norm_final, target, "final_loss")

    small = {k: [] for k in ("norm_mix", "a_sink", "b_rpb", "cq", "ck", "oga", "ogb", "ogc", "norm_mlp")}
    dbias_a_total = jnp.zeros_like(bias_a)
    big_w = {"w_in": (w_in, m_w_in, v_w_in), "w_o": (w_o, m_w_o, v_w_o), "w_up": (w_up, m_w_up, v_w_up),
             "w_down": (w_down, m_w_down, v_w_down)}
    big = {nm: [lax.empty(wmv[0].shape, F32) for _ in range(4)] for nm, wmv in big_w.items()}

    def send_grad(nm, l, g):
        send, recv, land, token = _grad_start(g, "grad_start_%s_%d" % (nm, l))
        return (nm, l, g, send, recv, land), token[0, 0]

    def finish_grads(pending, after):
        for nm, l, g, send, recv, land in pending:
            land = _grad_wait(g, land, send, recv, after, "grad_wait_%s_%d" % (nm, l))
            wmv = big_w[nm]
            big[nm] = _sum_adam(g, land, wmv[0], wmv[1], wmv[2], big[nm], l, me_idx, "sum_adam_%s_%d" % (nm, l))

    pending = []
    for l in reversed(range(n_layers)):
        (xin, h1, proj, bias_b, oa, ob, qh, kd, vd, kdt, oc, lse, mix, x_mid, h2, u, uu,
         wf_in, wf_o, wg_up, wf_down) = saved[l]
        started = []
        du = _matmul(dxb, wf_down, mode="nt", name="mlp_down_dgrad", tm=1024, tn=1024, tk=2048, epi="mul2u",
                     extra=(u,), out_dtypes=(MXU_DT,))
        gw = _matmul(uu, dxb, mode="tn", name="mlp_down_wgrad", tm=1024, tn=1024, tk=1024, out_dtypes=(GRAD_DT,))
        rec, tok_down = send_grad("w_down", l, gw.reshape(4, d_ff // 4, d))
        started.append(rec)
        nbk = ff_shard // 2048
        dh2 = _matmul(du, wg_up, mode="nt", name="mlp_up_dgrad", tm=1024, tn=1024, tk=2048,
                      mkn=(s, d_ff, d),
                      b_spec=pl.BlockSpec((None, 1024, 2048), lambda i, j, kk: (kk // nbk, j, kk % nbk)))
        nbo = ff_shard // 1024
        gw = _matmul(h2, du, mode="tn", name="mlp_up_wgrad", tm=1024, tn=1024, tk=1024, out_dtypes=(GRAD_DT,),
                     out_spec=pl.BlockSpec((None, 1024, 1024), lambda i, j, kk: (j // nbo, i, j % nbo)),
                     out_shape=(4, d, ff_shard))
        rec, tok_up = send_grad("w_up", l, gw)
        started.append(rec)
        dx_mid, dxmb, dg = _rms_bwd(x_mid, norm_mlp[l] + (tok_down + tok_up), dh2, dx, "rms_mlp_bwd")
        small["norm_mlp"].append(dg)
        dmix = _matmul(dxmb, wf_o, mode="nt", name="proj_out_dgrad", tm=1024, tn=1024, tk=2048)
        gw = _matmul(mix, dxmb, mode="tn", name="proj_out_wgrad", tm=1024, tn=1024, tk=1024, out_dtypes=(GRAD_DT,))
        rec, tok_o = send_grad("w_o", l, gw.reshape(4, d // 4, d))
        started.append(rec)
        doa, dob, doc, docb, ddc, dga, dgb, dgc = _groupnorm_bwd(
            dmix, oa, ob, oc, out_gain_a[l] + tok_o, out_gain_b[l], out_gain_c[l], ones, "groupnorm_bwd")
        small["oga"].append(dga)
        small["ogb"].append(dgb)
        small["ogc"].append(dgc)
        dqa, dka, dva, dbias_a, dsink = _local_attn_bwd(proj, bias_a, pad_sink(a_sink[l]), doa, cfg_a, "attn_a_bwd")
        dbias_a_total = dbias_a_total + dbias_a
        small["a_sink"].append(dsink[0, 0, :a_sink.shape[1]])
        dqb, dkb, dvb, dbias_b, _ = _local_attn_bwd(proj, bias_b, sink_b, dob, cfg_b, "attn_b_bwd")
        db5 = jnp.where(b_valid[None, None, :, None, :], dbias_b.reshape(8, 8, GRID_W, NA_ROWS, GRID_W), 0.0)
        t = jnp.einsum("vhqic,qcz->vhiz", db5, col_oh, precision=HIGHEST)
        small["b_rpb"].append(jnp.einsum("vhiz,vir->hrz", t, row_oh, precision=HIGHEST))
        dd_rows = ddc.reshape(s, 16, HEAD_DIM)[:, :, 0].T.reshape(4, 4, s)
        dqht, dkd, dvd = _flash_bwd(qh, kd, vd, kdt, docb, lse, dd_rows, "attn_c_bwd")
        dqh = dqht.T
        gq, gk = tile_gain(c_q_gain[l]), tile_gain(c_k_gain[l])
        dqc, dkc, dvc, dgq, dgk = _cprep_bwd(proj, gq, gk, cos_t, sin_t, ones, dqh, dkd, dvd, "cprep_bwd")
        small["cq"].append(dgq.reshape(CW // HEAD_DIM, HEAD_DIM).sum(0))
        small["ck"].append(dgk.reshape(CW // HEAD_DIM, HEAD_DIM).sum(0))
        dproj = jnp.concatenate([dqa, dka, dva, dqb, dkb, dvb, dqc, dkc, dvc], axis=1).astype(MXU_DT)
        dh1 = _matmul(dproj, wf_in, mode="nt", name="proj_in_dgrad", tm=1024, tn=1024, tk=1920)
        gw = _matmul(h1, dproj, mode="tn", name="proj_in_wgrad", tm=1024, tn=768, tk=1024, out_dtypes=(GRAD_DT,))
        rec, tok_in = send_grad("w_in", l, gw.reshape(d, 4, in_w // 4).transpose(1, 0, 2))
        started.append(rec)
        dx, dxb, dg = _rms_bwd(xin, norm_mix[l] + tok_in, dh1, dx_mid, "rms_mix_bwd")
        small["norm_mix"].append(dg)
        finish_grads(pending, dx)
        pending = started
    finish_grads(pending, dx)

    for lst in small.values():
        lst.reverse()

    dt5 = jnp.einsum("vhqk,vqkb->bh", dbias_a_total, a_onehot, precision=HIGHEST)
    small_names = ["norm_mix", "a_sink", "t5_table", "b_rpb", "c_q_gain", "c_k_gain", "out_gain_a", "out_gain_b",
                   "out_gain_c", "norm_mlp", "norm_final"]
    small_w = [norm_mix, a_sink, t5_table, b_rpb, c_q_gain, c_k_gain, out_gain_a, out_gain_b, out_gain_c, norm_mlp,
               norm_final]
    small_m = [m_norm_mix, m_a_sink, m_t5_table, m_b_rpb, m_c_q_gain, m_c_k_gain, m_out_gain_a, m_out_gain_b,
               m_out_gain_c, m_norm_mlp, m_norm_final]
    small_v = [v_norm_mix, v_a_sink, v_t5_table, v_b_rpb, v_c_q_gain, v_c_k_gain, v_out_gain_a, v_out_gain_b,
               v_out_gain_c, v_norm_mlp, v_norm_final]
    small_g = [jnp.stack(small["norm_mix"]), jnp.stack(small["a_sink"]), dt5, jnp.stack(small["b_rpb"]),
               jnp.stack(small["cq"]), jnp.stack(small["ck"]), jnp.stack(small["oga"]), jnp.stack(small["ogb"]),
               jnp.stack(small["ogc"]), jnp.stack(small["norm_mlp"]), dg_final]
    shapes = [w.shape for w in small_w]
    total = sum(math.prod(shp) for shp in shapes) + 1
    rows = -(-total // (8 * LANES)) * 8
    one = [jnp.ones((1,), F32)]
    gs, dl, mo, vo = _allreduce_small_adam(_pack(small_g + [loss_part[0, :1]], rows), _pack(small_w + one, rows),
                                           _pack(small_m + one, rows), _pack(small_v + one, rows))
    sg = _unpack(gs, shapes + [(1,)])
    sd, sm, sv = _unpack(dl, shapes), _unpack(mo, shapes), _unpack(vo, shapes)
    loss = sg[-1].reshape(())

    by_name = {nm: (sg[i], sd[i], sm[i], sv[i]) for i, nm in enumerate(small_names)}
    by_name.update(big)
    order = ["norm_mix", "w_in", "a_sink", "t5_table", "b_rpb", "c_q_gain", "c_k_gain", "out_gain_a", "out_gain_b",
             "out_gain_c", "w_o", "norm_mlp", "w_up", "w_down", "norm_final"]
    outs = [loss, dx.reshape(x.shape)]
    for field in range(4):
        outs.extend(by_name[nm][field] for nm in order)
    return tuple(outs)
```

```python
import functools
import math

import jax
import jax.numpy as jnp
from jax import lax
from jax.experimental import pallas as pl
from jax.experimental.pallas import tpu as pltpu

F32 = jnp.float32
MXU_DT = jnp.bfloat16
GRAD_DT = jnp.bfloat16
HIGHEST = lax.Precision.HIGHEST

HEAD_DIM = 64
LANES = 128
EPS = 1e-6
MASK_VALUE = -1e30
GRID_W = 64
NA_ROWS = 8
T5_BUCKETS = 32
T5_MAX_DIST = 128
ROPE_THETA = 10000.0
ADAM_LR, ADAM_B1, ADAM_B2, ADAM_EPS, ADAM_WD, ADAM_STEP = 0.001, 0.9, 0.999, 1e-08, 0.01, 10
VMEM_LIMIT = 56 * 1024 * 1024

MESH_ID = pl.DeviceIdType.MESH
ANY = pl.BlockSpec(memory_space=pl.ANY)

NT_DIMS = (((1,), (1,)), ((), ()))
TN_DIMS = (((0,), (0,)), ((), ()))
NN_DIMS = (((1,), (0,)), ((), ()))


def _dot(a, b, dims=NN_DIMS):
    return lax.dot_general(a, b, dims, preferred_element_type=F32)


def _call(body, *, name, out_shape, grid=(), in_specs=None, out_specs=None, scratch=(), sem=None,
          prefetch=0, aliases=None):
    params = {"vmem_limit_bytes": VMEM_LIMIT}
    if sem is not None:
        params["dimension_semantics"] = sem
    kwargs = {}
    if aliases:
        kwargs["input_output_aliases"] = aliases
    if prefetch:
        spec = pltpu.PrefetchScalarGridSpec(num_scalar_prefetch=prefetch, grid=grid, in_specs=in_specs,
                                            out_specs=out_specs, scratch_shapes=list(scratch))
        return pl.pallas_call(body, grid_spec=spec, out_shape=out_shape, name=name,
                              compiler_params=pltpu.CompilerParams(**params), **kwargs)
    return pl.pallas_call(body, grid=grid, in_specs=in_specs, out_specs=out_specs, out_shape=out_shape,
                          scratch_shapes=list(scratch), name=name,
                          compiler_params=pltpu.CompilerParams(**params), **kwargs)


def _sds(shape, dtype=F32):
    return jax.ShapeDtypeStruct(tuple(shape), dtype)


def _matmul(a, b, *, mode, name, tm, tn, tk, epi="plain", extra=(), out_dtypes=(F32,), mkn=None,
            b_spec=None, out_spec=None, out_shape=None):
    if mkn is None:
        if mode == "nn":
            (m, k), n = a.shape, b.shape[1]
        elif mode == "nt":
            (m, k), n = a.shape, b.shape[0]
        else:
            (k, m), n = a.shape, b.shape[1]
    else:
        m, k, n = mkn
    tm, tn, tk = min(tm, m), min(tn, n), min(tk, k)
    assert m % tm == 0 and n % tn == 0 and k % tk == 0, (name, m, n, k, tm, tn, tk)
    nk = k // tk
    dims = {"nn": NN_DIMS, "nt": NT_DIMS, "tn": TN_DIMS}[mode]
    n_extra, n_out = len(extra), len(out_dtypes)

    def body(a_ref, b_ref, *rest):
        extra_refs = rest[:n_extra]
        out_refs = rest[n_extra:n_extra + n_out]
        acc_ref = rest[n_extra + n_out]
        kk = pl.program_id(2)

        @pl.when(kk == 0)
        def _():
            acc_ref[...] = jnp.zeros_like(acc_ref)

        acc_ref[...] += _dot(a_ref[...].astype(MXU_DT), b_ref[...].astype(MXU_DT), dims)

        @pl.when(kk == nk - 1)
        def _():
            acc = acc_ref[...]
            if epi == "plain":
                out_refs[0][...] = acc.astype(out_refs[0].dtype)
            elif epi == "res":
                out_refs[0][...] = (extra_refs[0][...] + acc).astype(out_refs[0].dtype)
            elif epi == "relu2":
                u = jnp.maximum(acc, 0.0)
                out_refs[0][...] = u.astype(out_refs[0].dtype)
                out_refs[1][...] = (u * u).astype(out_refs[1].dtype)
            elif epi == "mul2u":
                out_refs[0][...] = (2.0 * extra_refs[0][...] * acc).astype(out_refs[0].dtype)
            else:
                raise ValueError(epi)

    if mode == "tn":
        a_spec = pl.BlockSpec((tk, tm), lambda i, j, kk: (kk, i))
    else:
        a_spec = pl.BlockSpec((tm, tk), lambda i, j, kk: (i, kk))
    if b_spec is None:
        if mode == "nt":
            b_spec = pl.BlockSpec((tn, tk), lambda i, j, kk: (j, kk))
        else:
            b_spec = pl.BlockSpec((tk, tn), lambda i, j, kk: (kk, j))
    mn_spec = pl.BlockSpec((tm, tn), lambda i, j, kk: (i, j))
    if out_spec is None:
        out_spec = mn_spec
    if out_shape is None:
        out_shape = (m, n)
    res = _call(body, name=name, grid=(m // tm, n // tn, nk),
                in_specs=[a_spec, b_spec] + [mn_spec] * n_extra,
                out_specs=[out_spec] * n_out,
                out_shape=[_sds(out_shape, d) for d in out_dtypes],
                scratch=[pltpu.VMEM((tm, tn), F32)],
                sem=("parallel", "parallel", "arbitrary"))(a, b, *extra)
    return res if n_out > 1 else res[0]


def _row_tile(s):
    return min(512, s)


def _rms_fwd(x, g, name):
    s, d = x.shape
    tm = _row_tile(s)

    def body(x_ref, g_ref, h_ref):
        xv = x_ref[...]
        r = lax.rsqrt(jnp.mean(xv * xv, axis=-1, keepdims=True) + EPS)
        h_ref[...] = ((xv * r) * g_ref[...]).astype(h_ref.dtype)

    return _call(body, name=name, grid=(s // tm,),
                 in_specs=[pl.BlockSpec((tm, d), lambda i: (i, 0)), pl.BlockSpec((1, d), lambda i: (0, 0))],
                 out_specs=pl.BlockSpec((tm, d), lambda i: (i, 0)),
                 out_shape=_sds((s, d), MXU_DT), sem=("parallel",))(x, g.reshape(1, d))


def _rms_bwd(x, g, dh, dres, name):
    s, d = x.shape
    tm = _row_tile(s)

    def body(x_ref, g_ref, dh_ref, dres_ref, dx_ref, dxb_ref, dg_ref):
        i = pl.program_id(0)
        xv = x_ref[...]
        r = lax.rsqrt(jnp.mean(xv * xv, axis=-1, keepdims=True) + EPS)
        xh = xv * r
        dhv = dh_ref[...]
        gd = dhv * g_ref[...]
        c = jnp.mean(gd * xh, axis=-1, keepdims=True)
        dx = dres_ref[...] + r * (gd - xh * c)
        dx_ref[...] = dx
        dxb_ref[...] = dx.astype(dxb_ref.dtype)
        part = jnp.sum(dhv * xh, axis=0, keepdims=True)

        @pl.when(i == 0)
        def _():
            dg_ref[...] = part

        @pl.when(i > 0)
        def _():
            dg_ref[...] += part

    row = pl.BlockSpec((tm, d), lambda i: (i, 0))
    vec = pl.BlockSpec((1, d), lambda i: (0, 0))
    return _call(body, name=name, grid=(s // tm,), in_specs=[row, vec, row, row],
                 out_specs=[row, row, vec],
                 out_shape=[_sds((s, d)), _sds((s, d), MXU_DT), _sds((1, d))],
                 sem=("arbitrary",))(x, g.reshape(1, d), dh, dres)


def _final_loss(x, g, target, name):
    s, d = x.shape
    tm = _row_tile(s)

    def body(x_ref, g_ref, t_ref, loss_ref, dx_ref, dxb_ref, dg_ref):
        i = pl.program_id(0)
        xv = x_ref[...]
        gv = g_ref[...]
        r = lax.rsqrt(jnp.mean(xv * xv, axis=-1, keepdims=True) + EPS)
        xh = xv * r
        err = xh * gv - t_ref[...]
        part_loss = 0.5 * jnp.sum(jnp.mean(err * err, axis=-1, keepdims=True), axis=0, keepdims=True)
        dy = err * (1.0 / d)
        gd = dy * gv
        c = jnp.mean(gd * xh, axis=-1, keepdims=True)
        dx = r * (gd - xh * c)
        dx_ref[...] = dx
        dxb_ref[...] = dx.astype(dxb_ref.dtype)
        part_g = jnp.sum(dy * xh, axis=0, keepdims=True)
        part_l = jnp.broadcast_to(part_loss, (1, LANES))

        @pl.when(i == 0)
        def _():
            dg_ref[...] = part_g
            loss_ref[...] = part_l

        @pl.when(i > 0)
        def _():
            dg_ref[...] += part_g
            loss_ref[...] += part_l

    row = pl.BlockSpec((tm, d), lambda i: (i, 0))
    vec = pl.BlockSpec((1, d), lambda i: (0, 0))
    return _call(body, name=name, grid=(s // tm,), in_specs=[row, vec, row],
                 out_specs=[pl.BlockSpec((1, LANES), lambda i: (0, 0)), row, row, vec],
                 out_shape=[_sds((1, LANES)), _sds((s, d)), _sds((s, d), MXU_DT), _sds((1, d))],
                 sem=("arbitrary",))(x, g.reshape(1, d), target)


def _lane_iota(shape):
    return lax.broadcasted_iota(jnp.int32, shape, len(shape) - 1)


def _swap_halves(x):
    return pltpu.roll(x, HEAD_DIM, 1)


def _segsum64(x, ones_ref):
    ones = ones_ref[...]
    outs = []
    for c in range(x.shape[1] // LANES):
        xc = x[:, c * LANES:(c + 1) * LANES]
        hi = xc.astype(MXU_DT)
        r1 = xc - hi.astype(F32)
        mid = r1.astype(MXU_DT)
        lo = (r1 - mid.astype(F32)).astype(MXU_DT)
        outs.append(_dot(hi, ones) + _dot(mid, ones) + _dot(lo, ones))
    return outs[0] if len(outs) == 1 else jnp.concatenate(outs, axis=1)


def _pair_ones():
    i = jnp.arange(LANES)
    return (i[:, None] // HEAD_DIM == i[None, :] // HEAD_DIM).astype(MXU_DT)


def _col(x, lane):
    return jnp.sum(jnp.where(_lane_iota(x.shape) == lane, x, 0.0), axis=-1, keepdims=True)


class _LocalCfg:
    def __init__(self, *, groups, qb, kw, qw, sub, qcol, kcol, vcol, kvhalf, kstart, variant, variant_py):
        self.groups, self.qb, self.kw, self.qw = groups, qb, kw, qw
        self.sub = sub
        self.qcol, self.kcol, self.vcol = qcol, kcol, vcol
        self.kvhalf = kvhalf
        self.kstart, self.variant = kstart, variant
        self.variant_py = variant_py
        self.pairs = qw // LANES


def _cfg_a(s):
    nb = s // 128
    return _LocalCfg(groups=1, qb=128, kw=384, qw=512, sub=1, qcol=lambda g: 0, kcol=lambda g: 4,
                     vcol=lambda g: 5, kvhalf=lambda t, e: t // 2,
                     kstart=lambda n: 128 * jnp.clip(n - 1, 0, nb - 3),
                     variant=lambda n: jnp.where(n <= 0, 0, jnp.where(n == nb - 1, 2, 1)),
                     variant_py=lambda n: 0 if n <= 0 else (2 if n == nb - 1 else 1))


def _cfg_b(s):
    rows = s // GRID_W
    return _LocalCfg(groups=4, qb=64, kw=512, qw=128, sub=4, qcol=lambda g: 6 + g, kcol=lambda g: 10 + g,
                     vcol=lambda g: 14 + g, kvhalf=lambda t, e: e,
                     kstart=lambda n: GRID_W * jnp.clip(n - NA_ROWS // 2, 0, rows - NA_ROWS),
                     variant=lambda n: jnp.where(n < 4, jnp.maximum(n, 0),
                                                 jnp.where(n > rows - 4, n - (rows - 8), 4)),
                     variant_py=lambda n: max(n, 0) if n < 4 else (n - (rows - 8) if n > rows - 4 else 4))


def _sum_visited(parts, cfg, s):
    n_var = parts[0].shape[0]
    variants = [cfg.variant_py(n) for n in range(s // cfg.qb)]
    total = None
    for i, part in enumerate(parts):
        seen = jnp.array([v in variants[i::cfg.sub] for v in range(n_var)]).reshape(n_var, 1, 1, 1)
        term = jnp.where(seen, part, 0.0)
        total = term if total is None else total + term
    return total


def _local_head(cfg, t, e, qp, qp_sw, kb, bias, sink_row, left_q):
    kvh = cfg.kvhalf(t, e)
    qsrc = qp if e == kvh else qp_sw
    keep = left_q if kvh == 0 else jnp.logical_not(left_q)
    qm = jnp.where(keep, qsrc, 0.0).astype(MXU_DT)
    sc = _dot(qm, kb, NT_DIMS) + bias
    snk = _col(sink_row, 2 * t + e)
    m = jnp.maximum(jnp.max(sc, axis=-1, keepdims=True), snk)
    p = jnp.exp(sc - m)
    l = jnp.sum(p, axis=-1, keepdims=True) + jnp.exp(snk - m)
    p = p / l
    return qm, keep, p, m, l, snk


def _local_attn_fwd(proj, bias, sink, cfg, name):
    s = proj.shape[0]
    qb, kw, qw, g_n = cfg.qb, cfg.kw, cfg.qw, cfg.groups
    hq = 2 * cfg.pairs

    sub = cfg.sub

    def body(q_ref, k_ref, v_ref, *rest):
        b_refs, s_ref, o_ref = rest[:sub], rest[sub], rest[sub + 1]
        n = pl.program_id(1)
        left_q = _lane_iota((qb, LANES)) < HEAD_DIM
        left_k = _lane_iota((kw, LANES)) < HEAD_DIM
        sink_row = s_ref[...]
        for i in range(sub):
            ks = pl.multiple_of(cfg.kstart(sub * n + i), 64)
            kf = k_ref[pl.ds(ks, kw), :]
            vf = v_ref[pl.ds(ks, kw), :]
            kb = kf.astype(MXU_DT)
            vf_sw = _swap_halves(vf)
            rows = slice(i * qb, (i + 1) * qb)
            for t in range(cfg.pairs):
                qp = q_ref[rows, t * LANES:(t + 1) * LANES] * 0.125
                qp_sw = _swap_halves(qp)
                acc = jnp.zeros((qb, LANES), F32)
                for e in range(2):
                    _, _, p, _, _, _ = _local_head(cfg, t, e, qp, qp_sw, kb, b_refs[i][0, 2 * t + e], sink_row,
                                                   left_q)
                    vsrc = vf if e == cfg.kvhalf(t, e) else vf_sw
                    vsel = jnp.where(left_k if e == 0 else jnp.logical_not(left_k), vsrc, 0.0).astype(MXU_DT)
                    acc = acc + _dot(p.astype(MXU_DT), vsel)
                o_ref[rows, t * LANES:(t + 1) * LANES] = acc

    def bias_spec(i):
        return pl.BlockSpec((1, hq, qb, kw), lambda g, n: (cfg.variant(sub * n + i), g, 0, 0))

    return _call(
        body, name=name, grid=(g_n, s // (sub * qb)),
        in_specs=[pl.BlockSpec((sub * qb, qw), lambda g, n: (n, cfg.qcol(g))),
                  pl.BlockSpec((s, LANES), lambda g, n: (0, cfg.kcol(g))),
                  pl.BlockSpec((s, LANES), lambda g, n: (0, cfg.vcol(g)))]
        + [bias_spec(i) for i in range(sub)]
        + [pl.BlockSpec((None, 1, LANES), lambda g, n: (g, 0, 0))],
        out_specs=pl.BlockSpec((sub * qb, qw), lambda g, n: (n, g)),
        out_shape=_sds((s, g_n * qw)), sem=("parallel", "arbitrary"))(proj, proj, proj, *([bias] * sub), sink)


def _local_attn_bwd(proj, bias, sink, do, cfg, name):
    s = proj.shape[0]
    qb, kw, qw, g_n = cfg.qb, cfg.kw, cfg.qw, cfg.groups
    hq = 2 * cfg.pairs

    sub = cfg.sub

    def body(q_ref, k_ref, v_ref, *rest):
        b_refs, s_ref, do_ref = rest[:sub], rest[sub], rest[sub + 1]
        dq_ref, dk_ref, dv_ref = rest[sub + 2:sub + 5]
        db_refs, dsk_ref = rest[sub + 5:2 * sub + 5], rest[2 * sub + 5]
        n = pl.program_id(1)

        @pl.when(n == 0)
        def _():
            dk_ref[...] = jnp.zeros_like(dk_ref)
            dv_ref[...] = jnp.zeros_like(dv_ref)
            dsk_ref[...] = jnp.zeros_like(dsk_ref)

        left_q = _lane_iota((qb, LANES)) < HEAD_DIM
        left_k = _lane_iota((kw, LANES)) < HEAD_DIM
        sink_row = s_ref[...]
        row0 = lax.broadcasted_iota(jnp.int32, (8, LANES), 0) == 0
        lane8 = _lane_iota((8, LANES))
        dsk_acc = jnp.zeros((8, LANES), F32)
        for i in range(sub):
            blk = sub * n + i
            ks = pl.multiple_of(cfg.kstart(blk), 64)
            db_ref = db_refs[i]

            @pl.when(jnp.logical_or(n == 0, cfg.variant(blk) != cfg.variant(blk - sub)))
            def _():
                db_ref[...] = jnp.zeros_like(db_ref)

            kf = k_ref[pl.ds(ks, kw), :]
            vf = v_ref[pl.ds(ks, kw), :]
            kb = kf.astype(MXU_DT)
            vb = vf.astype(MXU_DT)
            kf_sw = _swap_halves(kf)
            rows = slice(i * qb, (i + 1) * qb)
            dk_acc = jnp.zeros((kw, LANES), F32)
            dv_acc = jnp.zeros((kw, LANES), F32)
            for t in range(cfg.pairs):
                qp = q_ref[rows, t * LANES:(t + 1) * LANES] * 0.125
                qp_sw = _swap_halves(qp)
                dop = do_ref[rows, t * LANES:(t + 1) * LANES]
                dop_sw = _swap_halves(dop)
                dq_t = jnp.zeros((qb, LANES), F32)
                for e in range(2):
                    h = 2 * t + e
                    qm, keep, p, m, l, snk = _local_head(cfg, t, e, qp, qp_sw, kb, b_refs[i][0, h], sink_row,
                                                         left_q)
                    kvh = cfg.kvhalf(t, e)
                    dom = jnp.where(keep, dop if e == kvh else dop_sw, 0.0).astype(MXU_DT)
                    dp = _dot(dom, vb, NT_DIMS)
                    dd = jnp.sum(p * dp, axis=-1, keepdims=True)
                    ds = p * (dp - dd)
                    p_sink = jnp.exp(snk - m) / l
                    dsink = jnp.sum(-p_sink * dd, axis=0, keepdims=True)
                    dsk_acc = dsk_acc + jnp.where(jnp.logical_and(row0, lane8 == h), dsink, 0.0)
                    dsb = ds.astype(MXU_DT)
                    dv_acc = dv_acc + _dot(p.astype(MXU_DT), dom, TN_DIMS)
                    dk_acc = dk_acc + _dot(dsb, qm, TN_DIMS)
                    ksrc = kf if e == kvh else kf_sw
                    ksel = jnp.where(left_k if e == 0 else jnp.logical_not(left_k), ksrc, 0.0).astype(MXU_DT)
                    dq_t = dq_t + _dot(dsb, ksel)
                    db_ref[0, h] += ds
                dq_ref[rows, t * LANES:(t + 1) * LANES] = dq_t * 0.125
            dk_ref[pl.ds(ks, kw), :] += dk_acc
            dv_ref[pl.ds(ks, kw), :] += dv_acc
        dsk_ref[...] += dsk_acc

    def bias_spec(i):
        return pl.BlockSpec((1, hq, qb, kw), lambda g, n: (cfg.variant(sub * n + i), g, 0, 0))

    n_var = bias.shape[0]
    res = _call(
        body, name=name, grid=(g_n, s // (sub * qb)),
        in_specs=[pl.BlockSpec((sub * qb, qw), lambda g, n: (n, cfg.qcol(g))),
                  pl.BlockSpec((s, LANES), lambda g, n: (0, cfg.kcol(g))),
                  pl.BlockSpec((s, LANES), lambda g, n: (0, cfg.vcol(g)))]
        + [bias_spec(i) for i in range(sub)]
        + [pl.BlockSpec((None, 1, LANES), lambda g, n: (g, 0, 0)),
           pl.BlockSpec((sub * qb, qw), lambda g, n: (n, g))],
        out_specs=[pl.BlockSpec((sub * qb, qw), lambda g, n: (n, g)),
                   pl.BlockSpec((s, LANES), lambda g, n: (0, g)),
                   pl.BlockSpec((s, LANES), lambda g, n: (0, g))]
        + [bias_spec(i) for i in range(sub)]
        + [pl.BlockSpec((None, 8, LANES), lambda g, n: (g, 0, 0))],
        out_shape=[_sds((s, g_n * qw)), _sds((s, g_n * LANES)), _sds((s, g_n * LANES))]
        + [_sds((n_var, g_n * hq, qb, kw))] * sub + [_sds((g_n, 8, LANES))],
        sem=("parallel", "arbitrary"))(proj, proj, proj, *([bias] * sub), sink, do)
    dq, dk, dv = res[:3]
    dbias = _sum_visited(res[3:3 + sub], cfg, s)
    return dq, dk, dv, dbias, res[3 + sub]


QC_COL, KC_COL, VC_COL = 9, 13, 14
CW = 256


def _swap16(x):
    w = x.shape[1]
    lane = _lane_iota(x.shape)
    return jnp.where(lane % 32 < 16, pltpu.roll(x, w - 16, 1), pltpu.roll(x, 16, 1))


def _dup_halves(x):
    left = _lane_iota(x.shape) < HEAD_DIM
    sw = _swap_halves(x)
    return jnp.where(left, x, sw), jnp.where(left, sw, x)


def _normrope(x, gain, cos, sin, ones_ref):
    ms = _segsum64(x * x, ones_ref) * (1.0 / HEAD_DIM)
    r = lax.rsqrt(ms + EPS)
    y = (x * r) * gain
    return y * cos + _swap16(y) * sin, r


def _cprep_fwd(proj, gq, gk, cos, sin, ones, name):
    s = proj.shape[0]
    tm = _row_tile(s)

    def body(q0, q1, q2, q3, k_ref, v_ref, gq_ref, gk_ref, cos_ref, sin_ref, ones_ref, qh_ref, kd_ref, vd_ref):
        cos_v, sin_v = cos_ref[...], sin_ref[...]
        for c, q_ref in enumerate((q0, q1, q2, q3)):
            y, _ = _normrope(q_ref[...], gq_ref[...], cos_v, sin_v, ones_ref)
            qh_ref[:, c * CW:(c + 1) * CW] = (y * 0.125).astype(qh_ref.dtype)
        yk, _ = _normrope(k_ref[...], gk_ref[...], cos_v, sin_v, ones_ref)
        vv = v_ref[...]
        for p in range(2):
            ka, kb_ = _dup_halves(yk[:, p * LANES:(p + 1) * LANES])
            va, vb_ = _dup_halves(vv[:, p * LANES:(p + 1) * LANES])
            kd_ref[:, (2 * p) * LANES:(2 * p + 1) * LANES] = ka.astype(kd_ref.dtype)
            kd_ref[:, (2 * p + 1) * LANES:(2 * p + 2) * LANES] = kb_.astype(kd_ref.dtype)
            vd_ref[:, (2 * p) * LANES:(2 * p + 1) * LANES] = va.astype(vd_ref.dtype)
            vd_ref[:, (2 * p + 1) * LANES:(2 * p + 2) * LANES] = vb_.astype(vd_ref.dtype)

    def chunk(col):
        return pl.BlockSpec((tm, CW), lambda i: (i, col))

    vec = pl.BlockSpec((1, CW), lambda i: (0, 0))
    tab = pl.BlockSpec((tm, CW), lambda i: (i, 0))
    return _call(body, name=name, grid=(s // tm,),
                 in_specs=[chunk(QC_COL), chunk(QC_COL + 1), chunk(QC_COL + 2), chunk(QC_COL + 3),
                           chunk(KC_COL), chunk(VC_COL), vec, vec, tab, tab,
                           pl.BlockSpec((LANES, LANES), lambda i: (0, 0))],
                 out_specs=[pl.BlockSpec((tm, 4 * CW), lambda i: (i, 0)),
                            pl.BlockSpec((tm, 2 * CW), lambda i: (i, 0)),
                            pl.BlockSpec((tm, 2 * CW), lambda i: (i, 0))],
                 out_shape=[_sds((s, 4 * CW), MXU_DT), _sds((s, 2 * CW), MXU_DT), _sds((s, 2 * CW), MXU_DT)],
                 sem=("parallel",))(proj, proj, proj, proj, proj, proj, gq, gk, cos, sin, ones)


def _cprep_bwd(proj, gq, gk, cos, sin, ones, dqh, dkd, dvd, name):
    s = proj.shape[0]
    tm = _row_tile(s)

    def fold(ref, p):
        a = ref[:, (2 * p) * LANES:(2 * p + 1) * LANES]
        b = ref[:, (2 * p + 1) * LANES:(2 * p + 2) * LANES]
        ta = a + _swap_halves(a)
        tb = b + _swap_halves(b)
        return jnp.where(_lane_iota(a.shape) < HEAD_DIM, ta, tb)

    def norm_bwd(x, gain, dyr, cos_v, sin_v, ones_ref):
        dy = dyr * cos_v + _swap16(dyr * sin_v)
        ms = _segsum64(x * x, ones_ref) * (1.0 / HEAD_DIM)
        r = lax.rsqrt(ms + EPS)
        xh = x * r
        gd = dy * gain
        c = _segsum64(gd * xh, ones_ref) * (1.0 / HEAD_DIM)
        return r * (gd - xh * c), jnp.sum(dy * xh, axis=0, keepdims=True)

    def body(q0, q1, q2, q3, k_ref, gq_ref, gk_ref, cos_ref, sin_ref, ones_ref, dqh_ref, dkd_ref, dvd_ref,
             dq_ref, dk_ref, dv_ref, dgq_ref, dgk_ref):
        i = pl.program_id(0)
        cos_v, sin_v = cos_ref[...], sin_ref[...]
        gq_part = jnp.zeros((1, CW), F32)
        for c, q_ref in enumerate((q0, q1, q2, q3)):
            dx, dg = norm_bwd(q_ref[...], gq_ref[...], dqh_ref[:, c * CW:(c + 1) * CW] * 0.125, cos_v, sin_v,
                              ones_ref)
            dq_ref[:, c * CW:(c + 1) * CW] = dx
            gq_part = gq_part + dg
        dkr = jnp.concatenate([fold(dkd_ref, 0), fold(dkd_ref, 1)], axis=1)
        dxk, gk_part = norm_bwd(k_ref[...], gk_ref[...], dkr, cos_v, sin_v, ones_ref)
        dk_ref[...] = dxk
        dv_ref[...] = jnp.concatenate([fold(dvd_ref, 0), fold(dvd_ref, 1)], axis=1)

        @pl.when(i == 0)
        def _():
            dgq_ref[...] = gq_part
            dgk_ref[...] = gk_part

        @pl.when(i > 0)
        def _():
            dgq_ref[...] += gq_part
            dgk_ref[...] += gk_part

    def chunk(col):
        return pl.BlockSpec((tm, CW), lambda i: (i, col))

    vec = pl.BlockSpec((1, CW), lambda i: (0, 0))
    tab = pl.BlockSpec((tm, CW), lambda i: (i, 0))
    return _call(body, name=name, grid=(s // tm,),
                 in_specs=[chunk(QC_COL), chunk(QC_COL + 1), chunk(QC_COL + 2), chunk(QC_COL + 3), chunk(KC_COL),
                           vec, vec, tab, tab, pl.BlockSpec((LANES, LANES), lambda i: (0, 0)),
                           pl.BlockSpec((tm, 4 * CW), lambda i: (i, 0)),
                           pl.BlockSpec((tm, 2 * CW), lambda i: (i, 0)),
                           pl.BlockSpec((tm, 2 * CW), lambda i: (i, 0))],
                 out_specs=[pl.BlockSpec((tm, 4 * CW), lambda i: (i, 0)), tab, tab, vec, vec],
                 out_shape=[_sds((s, 4 * CW)), _sds((s, CW)), _sds((s, CW)), _sds((1, CW)), _sds((1, CW))],
                 sem=("arbitrary",))(proj, proj, proj, proj, proj, gq, gk, cos, sin, ones, dqh, dkd, dvd)


def _flash_tiles(s):
    return min(512, s), min(1024, s)


def _row_iota(shape):
    return lax.broadcasted_iota(jnp.int32, shape, 0)


def _flash_fwd(qh, kd, vdt, name):
    s = qh.shape[0]
    tq, tk = _flash_tiles(s)
    nk = s // tk

    n_chunks = 1
    cw = tq // n_chunks
    units = [(t, c, e) for t in range(2) for c in range(n_chunks) for e in range(2)]

    def body(q_ref, k_ref, vt_ref, ot_ref, lse_ref, qm_ref, m_ref, lacc_ref, acc_ref):
        j = pl.program_id(2)

        @pl.when(j == 0)
        def _():
            m_ref[...] = jnp.full(m_ref.shape, MASK_VALUE, F32)
            lacc_ref[...] = jnp.zeros_like(lacc_ref)
            acc_ref[...] = jnp.zeros_like(acc_ref)
            left_q = _lane_iota((tq, LANES)) < HEAD_DIM
            for t in range(2):
                qp = q_ref[:, t * LANES:(t + 1) * LANES]
                qm_ref[2 * t] = jnp.where(left_q, qp, jnp.zeros_like(qp))
                qm_ref[2 * t + 1] = jnp.where(left_q, jnp.zeros_like(qp), qp)

        kb = k_ref[...]
        vt = vt_ref[...]
        top_k = _row_iota((LANES, tk)) < HEAD_DIM
        top_c = _row_iota((LANES, cw)) < HEAD_DIM
        vt_e = (jnp.where(top_k, vt, jnp.ones_like(vt)), jnp.where(top_k, jnp.ones_like(vt), vt))

        def scores(unit):
            t, c, e = unit
            return _dot(kb, qm_ref[2 * t + e, c * cw:(c + 1) * cw, :], NT_DIMS)

        nxt = scores(units[0])
        pv, alpha = [], []
        for n, (t, c, e) in enumerate(units):
            st = nxt
            if n + 1 < len(units):
                nxt = scores(units[n + 1])
            h = 2 * t + e
            cols = slice(c * cw, (c + 1) * cw)
            m_prev = m_ref[h, :, cols]
            m_new = jnp.maximum(m_prev, jnp.max(st, axis=0, keepdims=True))
            alpha.append(jnp.exp(m_prev - m_new))
            pt = jnp.exp(st - m_new)
            m_ref[h, :, cols] = m_new
            pv.append(_dot(vt_e[e], pt.astype(MXU_DT)))
            if e == 1:
                acc_ref[t, :, cols] = (acc_ref[t, :, cols] * jnp.where(top_c, alpha[0], alpha[1])
                                       + jnp.where(top_c, pv[0], pv[1]))
                lacc_ref[t, :, cols] = (lacc_ref[t, :, cols] * jnp.where(top_c, alpha[1], alpha[0])
                                        + jnp.where(top_c, pv[1], pv[0]))
                pv, alpha = [], []

        @pl.when(j == nk - 1)
        def _():
            for t in range(2):
                lacc = lacc_ref[t]
                l_sw = jnp.concatenate([lacc[HEAD_DIM:], lacc[:HEAD_DIM]], axis=0)
                ot_ref[t * LANES:(t + 1) * LANES, :] = acc_ref[t] / l_sw
                lse_ref[2 * t:2 * t + 1, :] = m_ref[2 * t] + jnp.log(lacc[HEAD_DIM:HEAD_DIM + 1])
                lse_ref[2 * t + 1:2 * t + 2, :] = m_ref[2 * t + 1] + jnp.log(lacc[0:1])

    return _call(body, name=name, grid=(4, s // tq, nk),
                 in_specs=[pl.BlockSpec((tq, CW), lambda g, i, j: (i, g)),
                           pl.BlockSpec((tk, LANES), lambda g, i, j: (j, g)),
                           pl.BlockSpec((LANES, tk), lambda g, i, j: (g, j))],
                 out_specs=[pl.BlockSpec((CW, tq), lambda g, i, j: (g, i)),
                            pl.BlockSpec((None, 4, tq), lambda g, i, j: (g, 0, i))],
                 out_shape=[_sds((4 * CW, s)), _sds((4, 4, s))],
                 scratch=[pltpu.VMEM((4, tq, LANES), MXU_DT), pltpu.VMEM((4, 1, tq), F32),
                          pltpu.VMEM((2, LANES, tq), F32), pltpu.VMEM((2, LANES, tq), F32)],
                 sem=("parallel", "parallel", "arbitrary"))(qh, kd, vdt)


def _flash_bwd(qht, kd, vd, kdt, dot, lse, dd, name):
    s = qht.shape[1]
    tq, tk = _flash_tiles(s)
    ni = s // tq

    def body(qt_ref, k_ref, v_ref, kt_ref, dot_ref, lse_ref, dd_ref, dqt_ref, dk_ref, dv_ref, dk_acc, dv_acc):
        j = pl.program_id(1)
        i = pl.program_id(2)

        @pl.when(i == 0)
        def _():
            dk_acc[...] = jnp.zeros_like(dk_acc)
            dv_acc[...] = jnp.zeros_like(dv_acc)

        kb = k_ref[...]
        vb = v_ref[...]
        kt = kt_ref[...]
        top = _row_iota((LANES, tq)) < HEAD_DIM
        cols = pl.ds(pl.multiple_of(i * tq, tq), tq)

        def first_stage(h):
            t, e = divmod(h, 2)
            keep = top if e == 0 else jnp.logical_not(top)
            qtp = qt_ref[t * LANES:(t + 1) * LANES, :]
            dtp = dot_ref[t * LANES:(t + 1) * LANES, :]
            qmt = jnp.where(keep, qtp, jnp.zeros_like(qtp))
            domt = jnp.where(keep, dtp, jnp.zeros_like(dtp))
            return qmt, domt, _dot(kb, qmt), _dot(vb, domt)

        nxt = first_stage(0)
        dqt = []
        for h in range(4):
            qmt, domt, st, dpt = nxt
            if h < 3:
                nxt = first_stage(h + 1)
            pt = jnp.exp(st - lse_ref[h:h + 1, :])
            dsb = (pt * (dpt - dd_ref[h:h + 1, :])).astype(MXU_DT)
            dv_acc[...] += _dot(domt, pt.astype(MXU_DT), NT_DIMS)
            dk_acc[...] += _dot(qmt, dsb, NT_DIMS)
            dqt.append(_dot(kt, dsb))
            if h % 2 == 1:
                t = h // 2
                dq_t = jnp.where(top, dqt[0], dqt[1])
                dqt = []

                @pl.when(j == 0)
                def _():
                    dqt_ref[t * LANES:(t + 1) * LANES, cols] = dq_t

                @pl.when(j > 0)
                def _():
                    dqt_ref[t * LANES:(t + 1) * LANES, cols] += dq_t

        @pl.when(i == ni - 1)
        def _():
            dk_ref[...] = dk_acc[...]
            dv_ref[...] = dv_acc[...]

    qtspec = pl.BlockSpec((CW, tq), lambda g, j, i: (g, i))
    kspec = pl.BlockSpec((tk, LANES), lambda g, j, i: (j, g))
    ktspec = pl.BlockSpec((LANES, tk), lambda g, j, i: (g, j))
    rowspec = pl.BlockSpec((None, 4, tq), lambda g, j, i: (g, 0, i))
    return _call(body, name=name, grid=(4, s // tk, ni),
                 in_specs=[qtspec, kspec, kspec, ktspec, qtspec, rowspec, rowspec],
                 out_specs=[pl.BlockSpec((CW, s), lambda g, j, i: (g, 0)), ktspec, ktspec],
                 out_shape=[_sds((4 * CW, s)), _sds((2 * CW, s)), _sds((2 * CW, s))],
                 scratch=[pltpu.VMEM((LANES, tk), F32), pltpu.VMEM((LANES, tk), F32)],
                 sem=("parallel", "arbitrary", "arbitrary"))(qht, kd, vd, kdt, dot, lse, dd)


def _groupnorm_fwd(oa, ob, oc, ga, gb, gc, name):
    s = oa.shape[0]
    tm = _row_tile(s)
    wa, wb, wc = oa.shape[1], ob.shape[1], oc.shape[1]

    def body(oa_ref, ob_ref, oc_ref, ga_ref, gb_ref, gc_ref, mix_ref):
        off = 0
        for o_ref, g_ref, w in ((oa_ref, ga_ref, wa), (ob_ref, gb_ref, wb), (oc_ref, gc_ref, wc)):
            xv = o_ref[...]
            r = lax.rsqrt(jnp.mean(xv * xv, axis=-1, keepdims=True) + EPS)
            mix_ref[:, off:off + w] = ((xv * r) * g_ref[...]).astype(mix_ref.dtype)
            off += w

    def row(w):
        return pl.BlockSpec((tm, w), lambda i: (i, 0))

    def vec(w):
        return pl.BlockSpec((1, w), lambda i: (0, 0))

    return _call(body, name=name, grid=(s // tm,),
                 in_specs=[row(wa), row(wb), row(wc), vec(wa), vec(wb), vec(wc)],
                 out_specs=row(wa + wb + wc), out_shape=_sds((s, wa + wb + wc), MXU_DT),
                 sem=("parallel",))(oa, ob, oc, ga.reshape(1, wa), gb.reshape(1, wb), gc.reshape(1, wc))


def _groupnorm_bwd(dmix, oa, ob, oc, ga, gb, gc, ones, name):
    s = oa.shape[0]
    tm = _row_tile(s)
    wa, wb, wc = oa.shape[1], ob.shape[1], oc.shape[1]

    def body(dm_ref, oa_ref, ob_ref, oc_ref, ga_ref, gb_ref, gc_ref, ones_ref,
             doa_ref, dob_ref, doc_ref, docb_ref, dd_ref, dga_ref, dgb_ref, dgc_ref):
        i = pl.program_id(0)
        off = 0
        parts = []
        for o_ref, g_ref, do_ref, w in ((oa_ref, ga_ref, doa_ref, wa), (ob_ref, gb_ref, dob_ref, wb),
                                        (oc_ref, gc_ref, doc_ref, wc)):
            xv = o_ref[...]
            dh = dm_ref[:, off:off + w]
            r = lax.rsqrt(jnp.mean(xv * xv, axis=-1, keepdims=True) + EPS)
            xh = xv * r
            gd = dh * g_ref[...]
            c = jnp.mean(gd * xh, axis=-1, keepdims=True)
            dx = r * (gd - xh * c)
            do_ref[...] = dx
            parts.append(jnp.sum(dh * xh, axis=0, keepdims=True))
            if o_ref is oc_ref:
                docb_ref[...] = dx.astype(docb_ref.dtype)
                dd_ref[...] = _segsum64(dx * xv, ones_ref)
            off += w

        @pl.when(i == 0)
        def _():
            dga_ref[...], dgb_ref[...], dgc_ref[...] = parts

        @pl.when(i > 0)
        def _():
            dga_ref[...] += parts[0]
            dgb_ref[...] += parts[1]
            dgc_ref[...] += parts[2]

    def row(w):
        return pl.BlockSpec((tm, w), lambda i: (i, 0))

    def vec(w):
        return pl.BlockSpec((1, w), lambda i: (0, 0))

    return _call(body, name=name, grid=(s // tm,),
                 in_specs=[row(wa + wb + wc), row(wa), row(wb), row(wc), vec(wa), vec(wb), vec(wc),
                           pl.BlockSpec((LANES, LANES), lambda i: (0, 0))],
                 out_specs=[row(wa), row(wb), row(wc), row(wc), row(wc), vec(wa), vec(wb), vec(wc)],
                 out_shape=[_sds((s, wa)), _sds((s, wb)), _sds((s, wc)), _sds((s, wc), MXU_DT), _sds((s, wc)),
                            _sds((1, wa)), _sds((1, wb)), _sds((1, wc))],
                 sem=("arbitrary",))(dmix, oa, ob, oc, ga.reshape(1, wa), gb.reshape(1, wb), gc.reshape(1, wc), ones)


def _adam_math(w, g, m, v):
    m = ADAM_B1 * m + (1.0 - ADAM_B1) * g
    v = ADAM_B2 * v + (1.0 - ADAM_B2) * jnp.square(g)
    m_hat = m / (1.0 - ADAM_B1 ** ADAM_STEP)
    v_hat = v / (1.0 - ADAM_B2 ** ADAM_STEP)
    delta = -ADAM_LR * (m_hat / (jnp.sqrt(v_hat) + ADAM_EPS) + ADAM_WD * w)
    return delta, m, v


def _mesh_pos():
    return lax.axis_index("x"), lax.axis_index("y"), lax.axis_index("c")


def _peer_chips(x, y):
    return [(1 - x, y), (x, 1 - y), (1 - x, 1 - y)]


def _allreduce_small_adam(g, w, m, v):
    rows = g.shape[0]

    def body(g_ref, w_ref, m_ref, v_ref, gs_ref, d_ref, mo_ref, vo_ref, buf, send_sems, recv_sems):
        x, y, c = _mesh_pos()
        me = 4 * x + 2 * y + c
        buf[me] = g_ref[...]
        copies = []
        for k in range(1, 8):
            px = 1 - x if (k >> 2) & 1 else x
            py = 1 - y if (k >> 1) & 1 else y
            pc = 1 - c if k & 1 else c
            cp = pltpu.make_async_remote_copy(src_ref=g_ref, dst_ref=buf.at[me], send_sem=send_sems.at[k - 1],
                                              recv_sem=recv_sems.at[k - 1], device_id=(px, py, pc),
                                              device_id_type=MESH_ID)
            cp.start()
            copies.append(cp)
        for cp in copies:
            cp.wait()
        total = buf[0]
        for d in range(1, 8):
            total = total + buf[d]
        gs_ref[...] = total
        d_ref[...], mo_ref[...], vo_ref[...] = _adam_math(w_ref[...], total, m_ref[...], v_ref[...])

    vm = pl.BlockSpec(memory_space=pltpu.VMEM)
    return _call(body, name="allreduce_small_adam", in_specs=[vm] * 4, out_specs=[vm] * 4,
                 out_shape=[_sds((rows, LANES))] * 4,
                 scratch=[pltpu.VMEM((8, rows, LANES), F32), pltpu.SemaphoreType.DMA((7,)),
                          pltpu.SemaphoreType.DMA((7,))])(g, w, m, v)


HBM_SPEC = pl.BlockSpec(memory_space=pltpu.HBM)
SEM_SPEC = pl.BlockSpec(memory_space=pltpu.SEMAPHORE)
VMEM_SPEC = pl.BlockSpec(memory_space=pltpu.VMEM)
SIDE_EFFECT = pltpu.SideEffectType.DATAFLOW_SIDE_EFFECTING
N_PEERS = 7


def _in_hbm(a):
    return pltpu.with_memory_space_constraint(a, pltpu.HBM)


def _landing(shape, dtype):
    return _in_hbm(lax.empty(shape, dtype))


def _token_shape():
    return _sds((8, LANES))


def _gather_start(shards):
    n = len(shards)
    n_layers = shards[0].shape[0]
    jobs = [(l, t) for l in range(n_layers) for t in range(n)]
    nj = len(jobs)

    def body(*refs):
        sh = refs[:n]
        outs = refs[n + nj:]
        send, recv, land, token = outs[:nj], outs[nj:2 * nj], outs[2 * nj:3 * nj], outs[3 * nj]
        x, y, c = _mesh_pos()
        me = 2 * x + y
        for j, (l, t) in enumerate(jobs):
            for k, (px, py) in enumerate(_peer_chips(x, y)):
                pltpu.make_async_remote_copy(src_ref=sh[t].at[l], dst_ref=land[j].at[me], send_sem=send[j].at[k],
                                             recv_sem=recv[j].at[k], device_id=(px, py, c),
                                             device_id_type=MESH_ID).start()
        token[...] = jnp.zeros_like(token)

    lands = [_landing((4,) + shards[t].shape[1:], shards[t].dtype) for _, t in jobs]
    res = pl.pallas_call(
        body, name="gather_start",
        out_shape=tuple([pltpu.SemaphoreType.DMA((3,))] * (2 * nj)
                        + [pltpu.HBM(a.shape, a.dtype) for a in lands] + [_token_shape()]),
        in_specs=[HBM_SPEC] * (n + nj), out_specs=tuple([SEM_SPEC] * (2 * nj) + [HBM_SPEC] * nj + [VMEM_SPEC]),
        input_output_aliases={n + j: 2 * nj + j for j in range(nj)},
        compiler_params=pltpu.CompilerParams(has_side_effects=SIDE_EFFECT),
    )(*[_in_hbm(a) for a in shards], *lands)
    return jobs, res[:nj], res[nj:2 * nj], res[2 * nj:3 * nj], res[3 * nj]


def _gather_wait(shard, layer, land, send_sem, recv_sem, after, name):
    def body(sh_ref, land_ref, send_ref, recv_ref, after_ref, land_out):
        x, y, c = _mesh_pos()
        for k in range(3):
            cp = pltpu.make_async_remote_copy(src_ref=sh_ref.at[layer], dst_ref=land_ref.at[k],
                                              send_sem=send_ref.at[k], recv_sem=recv_ref.at[k],
                                              device_id=(x, y, 1 - c), device_id_type=MESH_ID)
            cp.wait_send()
            cp.wait_recv()

    return pl.pallas_call(
        body, name=name, out_shape=pltpu.HBM(land.shape, land.dtype),
        in_specs=[HBM_SPEC, HBM_SPEC, SEM_SPEC, SEM_SPEC, ANY], out_specs=HBM_SPEC,
        input_output_aliases={1: 0},
        compiler_params=pltpu.CompilerParams(has_side_effects=SIDE_EFFECT),
    )(shard, land, send_sem, recv_sem, after)


def _grad_start(g, name):
    def body(g_ref, land_in, send, recv, land, token):
        x, y, c = _mesh_pos()
        me = 2 * x + y
        pltpu.make_async_remote_copy(src_ref=g_ref.at[me], dst_ref=land.at[0], send_sem=send.at[0],
                                     recv_sem=recv.at[0], device_id=(x, y, 1 - c), device_id_type=MESH_ID).start()
        for k, (px, py) in enumerate(_peer_chips(x, y)):
            for c2 in range(2):
                pltpu.make_async_remote_copy(src_ref=g_ref.at[2 * px + py], dst_ref=land.at[1 + 2 * k + c],
                                             send_sem=send.at[1 + 2 * k + c2], recv_sem=recv.at[1 + 2 * k + c],
                                             device_id=(px, py, c2), device_id_type=MESH_ID).start()
        token[...] = jnp.zeros_like(token)

    land = _landing((N_PEERS,) + g.shape[1:], g.dtype)
    return pl.pallas_call(
        body, name=name,
        out_shape=(pltpu.SemaphoreType.DMA((N_PEERS,)), pltpu.SemaphoreType.DMA((N_PEERS,)),
                   pltpu.HBM(land.shape, land.dtype), _token_shape()),
        in_specs=[HBM_SPEC, HBM_SPEC], out_specs=(SEM_SPEC, SEM_SPEC, HBM_SPEC, VMEM_SPEC),
        input_output_aliases={1: 2},
        compiler_params=pltpu.CompilerParams(has_side_effects=SIDE_EFFECT),
    )(_in_hbm(g), land)


def _grad_wait(g, land, send_sem, recv_sem, after, name):
    def body(g_ref, land_ref, send_ref, recv_ref, after_ref, land_out):
        x, y, c = _mesh_pos()
        for k in range(N_PEERS):
            cp = pltpu.make_async_remote_copy(src_ref=g_ref.at[0], dst_ref=land_ref.at[k], send_sem=send_ref.at[k],
                                              recv_sem=recv_ref.at[k], device_id=(x, y, 1 - c),
                                              device_id_type=MESH_ID)
            cp.wait_send()
            cp.wait_recv()

    return pl.pallas_call(
        body, name=name, out_shape=pltpu.HBM(land.shape, land.dtype),
        in_specs=[HBM_SPEC, HBM_SPEC, SEM_SPEC, SEM_SPEC, ANY], out_specs=HBM_SPEC,
        input_output_aliases={1: 0},
        compiler_params=pltpu.CompilerParams(has_side_effects=SIDE_EFFECT),
    )(g, land, send_sem, recv_sem, after)


def _sum_adam(g, land, w, m, v, prev, layer, me_idx, name):
    _, r, cols = g.shape
    tr = min(128, r)

    def body(me_ref, g_ref, l0, l1, l2, l3, l4, l5, l6, w_ref, m_ref, v_ref, p0, p1, p2, p3,
             go_ref, d_ref, mo_ref, vo_ref):
        total = g_ref[...].astype(F32) + l0[...].astype(F32)
        for ref in (l1, l2, l3, l4, l5, l6):
            total = total + ref[...].astype(F32)
        go_ref[...] = total
        d_ref[...], mo_ref[...], vo_ref[...] = _adam_math(w_ref[...], total, m_ref[...], v_ref[...])

    def slot(k):
        return pl.BlockSpec((None, tr, cols), lambda i, me: (k, i, 0))

    lay = pl.BlockSpec((None, tr, cols), lambda i, me: (layer, i, 0))
    return _call(body, name=name, grid=(r // tr,), prefetch=1,
                 in_specs=[pl.BlockSpec((None, tr, cols), lambda i, me: (me[0], i, 0))]
                 + [slot(k) for k in range(N_PEERS)] + [lay, lay, lay] + [ANY] * 4,
                 out_specs=[lay] * 4, out_shape=[_sds(w.shape)] * 4,
                 aliases={12 + k: k for k in range(4)}, sem=("parallel",))(
                     me_idx, g, *([land] * N_PEERS), w, m, v, *prev)


def _t5_bucket(rel):
    nb = T5_BUCKETS // 2
    max_exact = nb // 2
    base = jnp.where(rel > 0, nb, 0)
    n = jnp.abs(rel)
    nf = jnp.maximum(n, 1).astype(F32)
    large = max_exact + (jnp.log(nf / max_exact) / math.log(T5_MAX_DIST / max_exact)
                         * (nb - max_exact)).astype(jnp.int32)
    large = jnp.minimum(large, nb - 1)
    return base + jnp.where(n < max_exact, n, large)


def _a_bias_maps():
    v = jnp.arange(3)[:, None, None]
    q = jnp.arange(128)[None, :, None]
    k = jnp.arange(384)[None, None, :]
    rel = k - 128 * v - q
    valid = jnp.abs(rel) <= 128
    onehot = (_t5_bucket(rel)[..., None] == jnp.arange(T5_BUCKETS)).astype(F32)
    return onehot * valid[..., None].astype(F32), valid


def _b_bias_maps():
    v = jnp.arange(8)[:, None]
    i = jnp.arange(NA_ROWS)[None, :]
    dr = jnp.where(v == 4, i + 3, i - v + 7)
    row_oh = (dr[..., None] == jnp.arange(2 * NA_ROWS - 1)).astype(F32)
    q = jnp.arange(GRID_W)[:, None]
    kc = jnp.arange(GRID_W)[None, :]
    cs = jnp.clip(q - 8, 0, GRID_W - 16)
    valid = (kc >= cs) & (kc < cs + 16)
    col_oh = ((kc - q + 15)[..., None] == jnp.arange(31)).astype(F32) * valid[..., None].astype(F32)
    return row_oh, col_oh, valid


def _rope_tables(s):
    t = jnp.arange(s)
    row = (t // GRID_W).astype(F32)
    col = (t % GRID_W).astype(F32)
    axis_dim = HEAD_DIM // 2
    freqs = ROPE_THETA ** (-jnp.arange(0, axis_dim, 2, dtype=F32) / axis_dim)
    ang_row = row[:, None] * freqs[None, :]
    ang_col = col[:, None] * freqs[None, :]
    cos = jnp.concatenate([jnp.cos(ang_row)] * 2 + [jnp.cos(ang_col)] * 2, axis=1)
    sin = jnp.concatenate([-jnp.sin(ang_row), jnp.sin(ang_row), -jnp.sin(ang_col), jnp.sin(ang_col)], axis=1)
    return jnp.tile(cos, (1, CW // HEAD_DIM)), jnp.tile(sin, (1, CW // HEAD_DIM))


def _pack(parts, rows):
    flat = jnp.concatenate([p.reshape(-1).astype(F32) for p in parts])
    return jnp.pad(flat, (0, rows * LANES - flat.shape[0])).reshape(rows, LANES)


def _unpack(buf, shapes):
    flat = buf.reshape(-1)
    out, off = [], 0
    for shp in shapes:
        size = math.prod(shp)
        out.append(flat[off:off + size].reshape(shp))
        off += size
    return out


def kernel(x, norm_mix, w_in, a_sink, t5_table, b_rpb, c_q_gain, c_k_gain, out_gain_a, out_gain_b, out_gain_c, w_o, norm_mlp, w_up, w_down, norm_final, loss_target, m_norm_mix, m_w_in, m_a_sink, m_t5_table, m_b_rpb, m_c_q_gain, m_c_k_gain, m_out_gain_a, m_out_gain_b, m_out_gain_c, m_w_o, m_norm_mlp, m_w_up, m_w_down, m_norm_final, v_norm_mix, v_w_in, v_a_sink, v_t5_table, v_b_rpb, v_c_q_gain, v_c_k_gain, v_out_gain_a, v_out_gain_b, v_out_gain_c, v_w_o, v_norm_mlp, v_w_up, v_w_down, v_norm_final):
    n_layers = w_in.shape[0]
    s, d = x.shape[1], x.shape[2]
    d_ff = 4 * w_up.shape[2]
    in_w = 4 * w_in.shape[2]
    xs = x.reshape(s, d)
    target = loss_target.reshape(s, d)
    cfg_a, cfg_b = _cfg_a(s), _cfg_b(s)

    x_i, y_i, _ = _mesh_pos()
    me_chip = 2 * x_i + y_i
    me_idx = me_chip.astype(jnp.int32).reshape(1)
    w_bf = [w_in.astype(MXU_DT), w_o.astype(MXU_DT), w_up.astype(MXU_DT), w_down.astype(MXU_DT)]
    jobs, gather_send, gather_recv, gather_land, gather_token = _gather_start(w_bf)
    job_of = {job: j for j, job in enumerate(jobs)}
    ff_shard = w_up.shape[2]

    def gathered(l, t, after):
        j = job_of[(l, t)]
        land = _gather_wait(w_bf[t], l, gather_land[j], gather_send[j], gather_recv[j], after,
                            "gather_wait_%d_%d" % (l, t))
        return lax.dynamic_update_slice(land, w_bf[t][l][None], (me_chip, 0, 0))

    ones = _pair_ones()
    cos_t, sin_t = _rope_tables(s)
    a_onehot, a_valid = _a_bias_maps()
    bias_a = jnp.where(a_valid[:, None], jnp.einsum("vqkb,bh->vhqk", a_onehot, t5_table, precision=HIGHEST),
                       MASK_VALUE)
    row_oh, col_oh, b_valid = _b_bias_maps()
    sink_b = jnp.full((4, 1, LANES), MASK_VALUE, F32)

    def b_bias(rpb):
        t = jnp.einsum("hrz,vir->vhiz", rpb, row_oh, precision=HIGHEST)
        t = jnp.einsum("vhiz,qcz->vhqic", t, col_oh, precision=HIGHEST)
        t = jnp.where(b_valid[None, None, :, None, :], t, MASK_VALUE)
        return t.reshape(8, 8, GRID_W, NA_ROWS * GRID_W)

    def tile_gain(gvec):
        return jnp.tile(gvec, CW // HEAD_DIM).reshape(1, CW)

    def pad_sink(svec):
        return jnp.pad(svec, (0, LANES - svec.shape[0])).reshape(1, 1, LANES)

    saved = []
    xc = xs
    for l in range(n_layers):
        h1 = _rms_fwd(xc, norm_mix[l] + gather_token[0, 0] if l == 0 else norm_mix[l], "rms_mix")
        wf_in = gathered(l, 0, h1).transpose(1, 0, 2).reshape(d, in_w)
        proj = _matmul(h1, wf_in, mode="nn", name="proj_in", tm=1024, tn=768, tk=2048)
        bias_b = b_bias(b_rpb[l])
        oa = _local_attn_fwd(proj, bias_a, pad_sink(a_sink[l]), cfg_a, "attn_a_fwd")
        ob = _local_attn_fwd(proj, bias_b, sink_b, cfg_b, "attn_b_fwd")
        gq, gk = tile_gain(c_q_gain[l]), tile_gain(c_k_gain[l])
        qh, kd, vd = _cprep_fwd(proj, gq, gk, cos_t, sin_t, ones, "cprep_fwd")
        kdt, vdt = kd.T, vd.T
        oct, lse = _flash_fwd(qh, kd, vdt, "attn_c_fwd")
        oc = oct.T
        mix = _groupnorm_fwd(oa, ob, oc, out_gain_a[l], out_gain_b[l], out_gain_c[l], "groupnorm_fwd")
        wf_o = gathered(l, 1, mix).reshape(d, d)
        x_mid = _matmul(mix, wf_o, mode="nn", name="proj_out", tm=1024, tn=1024, tk=2048, epi="res",
                        extra=(xc,))
        h2 = _rms_fwd(x_mid, norm_mlp[l], "rms_mlp")
        wg_up = gathered(l, 2, h2)
        nb_up = ff_shard // 1024
        u, uu = _matmul(h2, wg_up, mode="nn", name="mlp_up", tm=1024, tn=1024, tk=2048, epi="relu2",
                        out_dtypes=(F32, MXU_DT), mkn=(s, d, d_ff),
                        b_spec=pl.BlockSpec((None, 2048, 1024), lambda i, j, kk: (j // nb_up, kk, j % nb_up)))
        wf_down = gathered(l, 3, uu).reshape(d_ff, d)
        x_out = _matmul(uu, wf_down, mode="nn", name="mlp_down", tm=1024, tn=1024, tk=2048, epi="res",
                        extra=(x_mid,))
        saved.append((xc, h1, proj, bias_b, oa, ob, qh, kd, vd, kdt, oc, lse, mix, x_mid, h2, u, uu,
                      wf_in, wf_o, wg_up, wf_down))
        xc = x_out

    loss_part, dx, dxb, dg_final = _final_loss(xc, norm_final, target, "final_loss")

    small = {k: [] for k in ("norm_mix", "a_sink", "b_rpb", "cq", "ck", "oga", "ogb", "ogc", "norm_mlp")}
    dbias_a_total = jnp.zeros_like(bias_a)
    big_w = {"w_in": (w_in, m_w_in, v_w_in), "w_o": (w_o, m_w_o, v_w_o), "w_up": (w_up, m_w_up, v_w_up),
             "w_down": (w_down, m_w_down, v_w_down)}
    big = {nm: [lax.empty(wmv[0].shape, F32) for _ in range(4)] for nm, wmv in big_w.items()}

    def send_grad(nm, l, g):
        send, recv, land, token = _grad_start(g, "grad_start_%s_%d" % (nm, l))
        return (nm, l, g, send, recv, land), token[0, 0]

    def finish_grads(pending, after):
        for nm, l, g, send, recv, land in pending:
            land = _grad_wait(g, land, send, recv, after, "grad_wait_%s_%d" % (nm, l))
            wmv = big_w[nm]
            big[nm] = _sum_adam(g, land, wmv[0], wmv[1], wmv[2], big[nm], l, me_idx, "sum_adam_%s_%d" % (nm, l))

    pending = []
    for l in reversed(range(n_layers)):
        (xin, h1, proj, bias_b, oa, ob, qh, kd, vd, kdt, oc, lse, mix, x_mid, h2, u, uu,
         wf_in, wf_o, wg_up, wf_down) = saved[l]
        started = []
        du = _matmul(dxb, wf_down, mode="nt", name="mlp_down_dgrad", tm=1024, tn=1024, tk=2048, epi="mul2u",
                     extra=(u,), out_dtypes=(MXU_DT,))
        gw = _matmul(uu, dxb, mode="tn", name="mlp_down_wgrad", tm=1024, tn=1024, tk=1024, out_dtypes=(GRAD_DT,))
        rec, tok_down = send_grad("w_down", l, gw.reshape(4, d_ff // 4, d))
        started.append(rec)
        nbk = ff_shard // 2048
        dh2 = _matmul(du, wg_up, mode="nt", name="mlp_up_dgrad", tm=1024, tn=1024, tk=2048,
                      mkn=(s, d_ff, d),
                      b_spec=pl.BlockSpec((None, 1024, 2048), lambda i, j, kk: (kk // nbk, j, kk % nbk)))
        nbo = ff_shard // 1024
        gw = _matmul(h2, du, mode="tn", name="mlp_up_wgrad", tm=1024, tn=1024, tk=1024, out_dtypes=(GRAD_DT,),
                     out_spec=pl.BlockSpec((None, 1024, 1024), lambda i, j, kk: (j // nbo, i, j % nbo)),
                     out_shape=(4, d, ff_shard))
        rec, tok_up = send_grad("w_up", l, gw)
        started.append(rec)
        dx_mid, dxmb, dg = _rms_bwd(x_mid, norm_mlp[l] + (tok_down + tok_up), dh2, dx, "rms_mlp_bwd")
        small["norm_mlp"].append(dg)
        dmix = _matmul(dxmb, wf_o, mode="nt", name="proj_out_dgrad", tm=1024, tn=1024, tk=2048)
        gw = _matmul(mix, dxmb, mode="tn", name="proj_out_wgrad", tm=1024, tn=1024, tk=1024, out_dtypes=(GRAD_DT,))
        rec, tok_o = send_grad("w_o", l, gw.reshape(4, d // 4, d))
        started.append(rec)
        doa, dob, doc, docb, ddc, dga, dgb, dgc = _groupnorm_bwd(
            dmix, oa, ob, oc, out_gain_a[l] + tok_o, out_gain_b[l], out_gain_c[l], ones, "groupnorm_bwd")
        small["oga"].append(dga)
        small["ogb"].append(dgb)
        small["ogc"].append(dgc)
        dqa, dka, dva, dbias_a, dsink = _local_attn_bwd(proj, bias_a, pad_sink(a_sink[l]), doa, cfg_a, "attn_a_bwd")
        dbias_a_total = dbias_a_total + dbias_a
        small["a_sink"].append(dsink[0, 0, :a_sink.shape[1]])
        dqb, dkb, dvb, dbias_b, _ = _local_attn_bwd(proj, bias_b, sink_b, dob, cfg_b, "attn_b_bwd")
        db5 = jnp.where(b_valid[None, None, :, None, :], dbias_b.reshape(8, 8, GRID_W, NA_ROWS, GRID_W), 0.0)
        t = jnp.einsum("vhqic,qcz->vhiz", db5, col_oh, precision=HIGHEST)
        small["b_rpb"].append(jnp.einsum("vhiz,vir->hrz", t, row_oh, precision=HIGHEST))
        dd_rows = ddc.reshape(s, 16, HEAD_DIM)[:, :, 0].T.reshape(4, 4, s)
        dqht, dkdt, dvdt = _flash_bwd(qh.T, kd, vd, kdt, docb.T, lse, dd_rows, "attn_c_bwd")
        dqh, dkd, dvd = dqht.T, dkdt.T, dvdt.T
        gq, gk = tile_gain(c_q_gain[l]), tile_gain(c_k_gain[l])
        dqc, dkc, dvc, dgq, dgk = _cprep_bwd(proj, gq, gk, cos_t, sin_t, ones, dqh, dkd, dvd, "cprep_bwd")
        small["cq"].append(dgq.reshape(CW // HEAD_DIM, HEAD_DIM).sum(0))
        small["ck"].append(dgk.reshape(CW // HEAD_DIM, HEAD_DIM).sum(0))
        dproj = jnp.concatenate([dqa, dka, dva, dqb, dkb, dvb, dqc, dkc, dvc], axis=1).astype(MXU_DT)
        dh1 = _matmul(dproj, wf_in, mode="nt", name="proj_in_dgrad", tm=1024, tn=1024, tk=1920)
        gw = _matmul(h1, dproj, mode="tn", name="proj_in_wgrad", tm=1024, tn=768, tk=1024, out_dtypes=(GRAD_DT,))
        rec, tok_in = send_grad("w_in", l, gw.reshape(d, 4, in_w // 4).transpose(1, 0, 2))
        started.append(rec)
        dx, dxb, dg = _rms_bwd(xin, norm_mix[l] + tok_in, dh1, dx_mid, "rms_mix_bwd")
        small["norm_mix"].append(dg)
        finish_grads(pending, dx)
        pending = started
    finish_grads(pending, dx)

    for lst in small.values():
        lst.reverse()

    dt5 = jnp.einsum("vhqk,vqkb->bh", dbias_a_total, a_onehot, precision=HIGHEST)
    small_names = ["norm_mix", "a_sink", "t5_table", "b_rpb", "c_q_gain", "c_k_gain", "out_gain_a", "out_gain_b",
                   "out_gain_c", "norm_mlp", "norm_final"]
    small_w = [norm_mix, a_sink, t5_table, b_rpb, c_q_gain, c_k_gain, out_gain_a, out_gain_b, out_gain_c, norm_mlp,
               norm_final]
    small_m = [m_norm_mix, m_a_sink, m_t5_table, m_b_rpb, m_c_q_gain, m_c_k_gain, m_out_gain_a, m_out_gain_b,
               m_out_gain_c, m_norm_mlp, m_norm_final]
    small_v = [v_norm_mix, v_a_sink, v_t5_table, v_b_rpb, v_c_q_gain, v_c_k_gain, v_out_gain_a, v_out_gain_b,
               v_out_gain_c, v_norm_mlp, v_norm_final]
    small_g = [jnp.stack(small["norm_mix"]), jnp.stack(small["a_sink"]), dt5, jnp.stack(small["b_rpb"]),
               jnp.stack(small["cq"]), jnp.stack(small["ck"]), jnp.stack(small["oga"]), jnp.stack(small["ogb"]),
               jnp.stack(small["ogc"]), jnp.stack(small["norm_mlp"]), dg_final]
    shapes = [w.shape for w in small_w]
    total = sum(math.prod(shp) for shp in shapes) + 1
    rows = -(-total // (8 * LANES)) * 8
    one = [jnp.ones((1,), F32)]
    gs, dl, mo, vo = _allreduce_small_adam(_pack(small_g + [loss_part[0, :1]], rows), _pack(small_w + one, rows),
                                           _pack(small_m + one, rows), _pack(small_v + one, rows))
    sg = _unpack(gs, shapes + [(1,)])
    sd, sm, sv = _unpack(dl, shapes), _unpack(mo, shapes), _unpack(vo, shapes)
    loss = sg[-1].reshape(())

    by_name = {nm: (sg[i], sd[i], sm[i], sv[i]) for i, nm in enumerate(small_names)}
    by_name.update(big)
    order = ["norm_mix", "w_in", "a_sink", "t5_table", "b_rpb", "c_q_gain", "c_k_gain", "out_gain_a", "out_gain_b",
             "out_gain_c", "w_o", "norm_mlp", "w_up", "w_down", "norm_final"]
    outs = [loss, dx.reshape(x.shape)]
    for field in range(4):
        outs.extend(by_name[nm][field] for nm in order)
    return tuple(outs)
```

```python
import functools
import math

import jax
import jax.numpy as jnp
from jax import lax
from jax.experimental import pallas as pl
from jax.experimental.pallas import tpu as pltpu

F32 = jnp.float32
MXU_DT = jnp.bfloat16
GRAD_DT = jnp.bfloat16
HIGHEST = lax.Precision.HIGHEST

HEAD_DIM = 64
LANES = 128
EPS = 1e-6
MASK_VALUE = -1e30
GRID_W = 64
NA_ROWS = 8
T5_BUCKETS = 32
T5_MAX_DIST = 128
ROPE_THETA = 10000.0
ADAM_LR, ADAM_B1, ADAM_B2, ADAM_EPS, ADAM_WD, ADAM_STEP = 0.001, 0.9, 0.999, 1e-08, 0.01, 10
VMEM_LIMIT = 56 * 1024 * 1024

MESH_ID = pl.DeviceIdType.MESH
ANY = pl.BlockSpec(memory_space=pl.ANY)

NT_DIMS = (((1,), (1,)), ((), ()))
TN_DIMS = (((0,), (0,)), ((), ()))
NN_DIMS = (((1,), (0,)), ((), ()))


def _dot(a, b, dims=NN_DIMS):
    return lax.dot_general(a, b, dims, preferred_element_type=F32)


def _call(body, *, name, out_shape, grid=(), in_specs=None, out_specs=None, scratch=(), sem=None,
          prefetch=0, aliases=None):
    params = {"vmem_limit_bytes": VMEM_LIMIT}
    if sem is not None:
        params["dimension_semantics"] = sem
    kwargs = {}
    if aliases:
        kwargs["input_output_aliases"] = aliases
    if prefetch:
        spec = pltpu.PrefetchScalarGridSpec(num_scalar_prefetch=prefetch, grid=grid, in_specs=in_specs,
                                            out_specs=out_specs, scratch_shapes=list(scratch))
        return pl.pallas_call(body, grid_spec=spec, out_shape=out_shape, name=name,
                              compiler_params=pltpu.CompilerParams(**params), **kwargs)
    return pl.pallas_call(body, grid=grid, in_specs=in_specs, out_specs=out_specs, out_shape=out_shape,
                          scratch_shapes=list(scratch), name=name,
                          compiler_params=pltpu.CompilerParams(**params), **kwargs)


def _sds(shape, dtype=F32):
    return jax.ShapeDtypeStruct(tuple(shape), dtype)


def _matmul(a, b, *, mode, name, tm, tn, tk, epi="plain", extra=(), out_dtypes=(F32,), mkn=None,
            b_spec=None, out_spec=None, out_shape=None):
    if mkn is None:
        if mode == "nn":
            (m, k), n = a.shape, b.shape[1]
        elif mode == "nt":
            (m, k), n = a.shape, b.shape[0]
        else:
            (k, m), n = a.shape, b.shape[1]
    else:
        m, k, n = mkn
    tm, tn, tk = min(tm, m), min(tn, n), min(tk, k)
    assert m % tm == 0 and n % tn == 0 and k % tk == 0, (name, m, n, k, tm, tn, tk)
    nk = k // tk
    dims = {"nn": NN_DIMS, "nt": NT_DIMS, "tn": TN_DIMS}[mode]
    n_extra, n_out = len(extra), len(out_dtypes)

    def body(a_ref, b_ref, *rest):
        extra_refs = rest[:n_extra]
        out_refs = rest[n_extra:n_extra + n_out]
        acc_ref = rest[n_extra + n_out]
        kk = pl.program_id(2)

        @pl.when(kk == 0)
        def _():
            acc_ref[...] = jnp.zeros_like(acc_ref)

        acc_ref[...] += _dot(a_ref[...].astype(MXU_DT), b_ref[...].astype(MXU_DT), dims)

        @pl.when(kk == nk - 1)
        def _():
            acc = acc_ref[...]
            if epi == "plain":
                out_refs[0][...] = acc.astype(out_refs[0].dtype)
            elif epi == "res":
                out_refs[0][...] = (extra_refs[0][...] + acc).astype(out_refs[0].dtype)
            elif epi == "relu2":
                u = jnp.maximum(acc, 0.0)
                out_refs[0][...] = u.astype(out_refs[0].dtype)
                out_refs[1][...] = (u * u).astype(out_refs[1].dtype)
            elif epi == "mul2u":
                out_refs[0][...] = (2.0 * extra_refs[0][...] * acc).astype(out_refs[0].dtype)
            else:
                raise ValueError(epi)

    if mode == "tn":
        a_spec = pl.BlockSpec((tk, tm), lambda i, j, kk: (kk, i))
    else:
        a_spec = pl.BlockSpec((tm, tk), lambda i, j, kk: (i, kk))
    if b_spec is None:
        if mode == "nt":
            b_spec = pl.BlockSpec((tn, tk), lambda i, j, kk: (j, kk))
        else:
            b_spec = pl.BlockSpec((tk, tn), lambda i, j, kk: (kk, j))
    mn_spec = pl.BlockSpec((tm, tn), lambda i, j, kk: (i, j))
    if out_spec is None:
        out_spec = mn_spec
    if out_shape is None:
        out_shape = (m, n)
    res = _call(body, name=name, grid=(m // tm, n // tn, nk),
                in_specs=[a_spec, b_spec] + [mn_spec] * n_extra,
                out_specs=[out_spec] * n_out,
                out_shape=[_sds(out_shape, d) for d in out_dtypes],
                scratch=[pltpu.VMEM((tm, tn), F32)],
                sem=("parallel", "parallel", "arbitrary"))(a, b, *extra)
    return res if n_out > 1 else res[0]


def _row_tile(s):
    return min(512, s)


def _rms_fwd(x, g, name):
    s, d = x.shape
    tm = _row_tile(s)

    def body(x_ref, g_ref, h_ref):
        xv = x_ref[...]
        r = lax.rsqrt(jnp.mean(xv * xv, axis=-1, keepdims=True) + EPS)
        h_ref[...] = ((xv * r) * g_ref[...]).astype(h_ref.dtype)

    return _call(body, name=name, grid=(s // tm,),
                 in_specs=[pl.BlockSpec((tm, d), lambda i: (i, 0)), pl.BlockSpec((1, d), lambda i: (0, 0))],
                 out_specs=pl.BlockSpec((tm, d), lambda i: (i, 0)),
                 out_shape=_sds((s, d), MXU_DT), sem=("parallel",))(x, g.reshape(1, d))


def _rms_bwd(x, g, dh, dres, name):
    s, d = x.shape
    tm = _row_tile(s)

    def body(x_ref, g_ref, dh_ref, dres_ref, dx_ref, dxb_ref, dg_ref):
        i = pl.program_id(0)
        xv = x_ref[...]
        r = lax.rsqrt(jnp.mean(xv * xv, axis=-1, keepdims=True) + EPS)
        xh = xv * r
        dhv = dh_ref[...]
        gd = dhv * g_ref[...]
        c = jnp.mean(gd * xh, axis=-1, keepdims=True)
        dx = dres_ref[...] + r * (gd - xh * c)
        dx_ref[...] = dx
        dxb_ref[...] = dx.astype(dxb_ref.dtype)
        part = jnp.sum(dhv * xh, axis=0, keepdims=True)

        @pl.when(i == 0)
        def _():
            dg_ref[...] = part

        @pl.when(i > 0)
        def _():
            dg_ref[...] += part

    row = pl.BlockSpec((tm, d), lambda i: (i, 0))
    vec = pl.BlockSpec((1, d), lambda i: (0, 0))
    return _call(body, name=name, grid=(s // tm,), in_specs=[row, vec, row, row],
                 out_specs=[row, row, vec],
                 out_shape=[_sds((s, d)), _sds((s, d), MXU_DT), _sds((1, d))],
                 sem=("arbitrary",))(x, g.reshape(1, d), dh, dres)


def _final_loss(x, g, target, name):
    s, d = x.shape
    tm = _row_tile(s)

    def body(x_ref, g_ref, t_ref, loss_ref, dx_ref, dxb_ref, dg_ref):
        i = pl.program_id(0)
        xv = x_ref[...]
        gv = g_ref[...]
        r = lax.rsqrt(jnp.mean(xv * xv, axis=-1, keepdims=True) + EPS)
        xh = xv * r
        err = xh * gv - t_ref[...]
        part_loss = 0.5 * jnp.sum(jnp.mean(err * err, axis=-1, keepdims=True), axis=0, keepdims=True)
        dy = err * (1.0 / d)
        gd = dy * gv
        c = jnp.mean(gd * xh, axis=-1, keepdims=True)
        dx = r * (gd - xh * c)
        dx_ref[...] = dx
        dxb_ref[...] = dx.astype(dxb_ref.dtype)
        part_g = jnp.sum(dy * xh, axis=0, keepdims=True)
        part_l = jnp.broadcast_to(part_loss, (1, LANES))

        @pl.when(i == 0)
        def _():
            dg_ref[...] = part_g
            loss_ref[...] = part_l

        @pl.when(i > 0)
        def _():
            dg_ref[...] += part_g
            loss_ref[...] += part_l

    row = pl.BlockSpec((tm, d), lambda i: (i, 0))
    vec = pl.BlockSpec((1, d), lambda i: (0, 0))
    return _call(body, name=name, grid=(s // tm,), in_specs=[row, vec, row],
                 out_specs=[pl.BlockSpec((1, LANES), lambda i: (0, 0)), row, row, vec],
                 out_shape=[_sds((1, LANES)), _sds((s, d)), _sds((s, d), MXU_DT), _sds((1, d))],
                 sem=("arbitrary",))(x, g.reshape(1, d), target)


def _lane_iota(shape):
    return lax.broadcasted_iota(jnp.int32, shape, len(shape) - 1)


def _swap_halves(x):
    return pltpu.roll(x, HEAD_DIM, 1)


def _segsum64(x, ones_ref):
    ones = ones_ref[...]
    outs = []
    for c in range(x.shape[1] // LANES):
        xc = x[:, c * LANES:(c + 1) * LANES]
        hi = xc.astype(MXU_DT)
        r1 = xc - hi.astype(F32)
        mid = r1.astype(MXU_DT)
        lo = (r1 - mid.astype(F32)).astype(MXU_DT)
        outs.append(_dot(hi, ones) + _dot(mid, ones) + _dot(lo, ones))
    return outs[0] if len(outs) == 1 else jnp.concatenate(outs, axis=1)


def _pair_ones():
    i = jnp.arange(LANES)
    return (i[:, None] // HEAD_DIM == i[None, :] // HEAD_DIM).astype(MXU_DT)


def _col(x, lane):
    return jnp.sum(jnp.where(_lane_iota(x.shape) == lane, x, 0.0), axis=-1, keepdims=True)


class _LocalCfg:
    def __init__(self, *, groups, qb, kw, qw, sub, qcol, kcol, vcol, kvhalf, kstart, variant, variant_py):
        self.groups, self.qb, self.kw, self.qw = groups, qb, kw, qw
        self.sub = sub
        self.qcol, self.kcol, self.vcol = qcol, kcol, vcol
        self.kvhalf = kvhalf
        self.kstart, self.variant = kstart, variant
        self.variant_py = variant_py
        self.pairs = qw // LANES


def _cfg_a(s):
    nb = s // 128
    return _LocalCfg(groups=1, qb=128, kw=384, qw=512, sub=1, qcol=lambda g: 0, kcol=lambda g: 4,
                     vcol=lambda g: 5, kvhalf=lambda t, e: t // 2,
                     kstart=lambda n: 128 * jnp.clip(n - 1, 0, nb - 3),
                     variant=lambda n: jnp.where(n <= 0, 0, jnp.where(n == nb - 1, 2, 1)),
                     variant_py=lambda n: 0 if n <= 0 else (2 if n == nb - 1 else 1))


def _cfg_b(s):
    rows = s // GRID_W
    return _LocalCfg(groups=4, qb=64, kw=512, qw=128, sub=4, qcol=lambda g: 6 + g, kcol=lambda g: 10 + g,
                     vcol=lambda g: 14 + g, kvhalf=lambda t, e: e,
                     kstart=lambda n: GRID_W * jnp.clip(n - NA_ROWS // 2, 0, rows - NA_ROWS),
                     variant=lambda n: jnp.where(n < 4, jnp.maximum(n, 0),
                                                 jnp.where(n > rows - 4, n - (rows - 8), 4)),
                     variant_py=lambda n: max(n, 0) if n < 4 else (n - (rows - 8) if n > rows - 4 else 4))


def _sum_visited(parts, cfg, s):
    n_var = parts[0].shape[0]
    variants = [cfg.variant_py(n) for n in range(s // cfg.qb)]
    total = None
    for i, part in enumerate(parts):
        seen = jnp.array([v in variants[i::cfg.sub] for v in range(n_var)]).reshape(n_var, 1, 1, 1)
        term = jnp.where(seen, part, 0.0)
        total = term if total is None else total + term
    return total


def _local_head(cfg, t, e, qp, qp_sw, kb, bias, sink_row, left_q):
    kvh = cfg.kvhalf(t, e)
    qsrc = qp if e == kvh else qp_sw
    keep = left_q if kvh == 0 else jnp.logical_not(left_q)
    qm = jnp.where(keep, qsrc, 0.0).astype(MXU_DT)
    sc = _dot(qm, kb, NT_DIMS) + bias
    snk = _col(sink_row, 2 * t + e)
    m = jnp.maximum(jnp.max(sc, axis=-1, keepdims=True), snk)
    p = jnp.exp(sc - m)
    l = jnp.sum(p, axis=-1, keepdims=True) + jnp.exp(snk - m)
    p = p / l
    return qm, keep, p, m, l, snk


def _local_attn_fwd(proj, bias, sink, cfg, name):
    s = proj.shape[0]
    qb, kw, qw, g_n = cfg.qb, cfg.kw, cfg.qw, cfg.groups
    hq = 2 * cfg.pairs

    sub = cfg.sub

    def body(q_ref, k_ref, v_ref, *rest):
        b_refs, s_ref, o_ref = rest[:sub], rest[sub], rest[sub + 1]
        n = pl.program_id(1)
        left_q = _lane_iota((qb, LANES)) < HEAD_DIM
        left_k = _lane_iota((kw, LANES)) < HEAD_DIM
        sink_row = s_ref[...]
        for i in range(sub):
            ks = pl.multiple_of(cfg.kstart(sub * n + i), 64)
            kf = k_ref[pl.ds(ks, kw), :]
            vf = v_ref[pl.ds(ks, kw), :]
            kb = kf.astype(MXU_DT)
            vf_sw = _swap_halves(vf)
            rows = slice(i * qb, (i + 1) * qb)
            for t in range(cfg.pairs):
                qp = q_ref[rows, t * LANES:(t + 1) * LANES] * 0.125
                qp_sw = _swap_halves(qp)
                acc = jnp.zeros((qb, LANES), F32)
                for e in range(2):
                    _, _, p, _, _, _ = _local_head(cfg, t, e, qp, qp_sw, kb, b_refs[i][0, 2 * t + e], sink_row,
                                                   left_q)
                    vsrc = vf if e == cfg.kvhalf(t, e) else vf_sw
                    vsel = jnp.where(left_k if e == 0 else jnp.logical_not(left_k), vsrc, 0.0).astype(MXU_DT)
                    acc = acc + _dot(p.astype(MXU_DT), vsel)
                o_ref[rows, t * LANES:(t + 1) * LANES] = acc

    def bias_spec(i):
        return pl.BlockSpec((1, hq, qb, kw), lambda g, n: (cfg.variant(sub * n + i), g, 0, 0))

    return _call(
        body, name=name, grid=(g_n, s // (sub * qb)),
        in_specs=[pl.BlockSpec((sub * qb, qw), lambda g, n: (n, cfg.qcol(g))),
                  pl.BlockSpec((s, LANES), lambda g, n: (0, cfg.kcol(g))),
                  pl.BlockSpec((s, LANES), lambda g, n: (0, cfg.vcol(g)))]
        + [bias_spec(i) for i in range(sub)]
        + [pl.BlockSpec((None, 1, LANES), lambda g, n: (g, 0, 0))],
        out_specs=pl.BlockSpec((sub * qb, qw), lambda g, n: (n, g)),
        out_shape=_sds((s, g_n * qw)), sem=("parallel", "arbitrary"))(proj, proj, proj, *([bias] * sub), sink)


def _local_attn_bwd(proj, bias, sink, do, cfg, name):
    s = proj.shape[0]
    qb, kw, qw, g_n = cfg.qb, cfg.kw, cfg.qw, cfg.groups
    hq = 2 * cfg.pairs

    sub = cfg.sub

    def body(q_ref, k_ref, v_ref, *rest):
        b_refs, s_ref, do_ref = rest[:sub], rest[sub], rest[sub + 1]
        dq_ref, dk_ref, dv_ref = rest[sub + 2:sub + 5]
        db_refs, dsk_ref = rest[sub + 5:2 * sub + 5], rest[2 * sub + 5]
        n = pl.program_id(1)

        @pl.when(n == 0)
        def _():
            dk_ref[...] = jnp.zeros_like(dk_ref)
            dv_ref[...] = jnp.zeros_like(dv_ref)
            dsk_ref[...] = jnp.zeros_like(dsk_ref)

        left_q = _lane_iota((qb, LANES)) < HEAD_DIM
        left_k = _lane_iota((kw, LANES)) < HEAD_DIM
        sink_row = s_ref[...]
        row0 = lax.broadcasted_iota(jnp.int32, (8, LANES), 0) == 0
        lane8 = _lane_iota((8, LANES))
        dsk_acc = jnp.zeros((8, LANES), F32)
        for i in range(sub):
            blk = sub * n + i
            ks = pl.multiple_of(cfg.kstart(blk), 64)
            db_ref = db_refs[i]

            @pl.when(jnp.logical_or(n == 0, cfg.variant(blk) != cfg.variant(blk - sub)))
            def _():
                db_ref[...] = jnp.zeros_like(db_ref)

            kf = k_ref[pl.ds(ks, kw), :]
            vf = v_ref[pl.ds(ks, kw), :]
            kb = kf.astype(MXU_DT)
            vb = vf.astype(MXU_DT)
            kf_sw = _swap_halves(kf)
            rows = slice(i * qb, (i + 1) * qb)
            dk_acc = jnp.zeros((kw, LANES), F32)
            dv_acc = jnp.zeros((kw, LANES), F32)
            for t in range(cfg.pairs):
                qp = q_ref[rows, t * LANES:(t + 1) * LANES] * 0.125
                qp_sw = _swap_halves(qp)
                dop = do_ref[rows, t * LANES:(t + 1) * LANES]
                dop_sw = _swap_halves(dop)
                dq_t = jnp.zeros((qb, LANES), F32)
                for e in range(2):
                    h = 2 * t + e
                    qm, keep, p, m, l, snk = _local_head(cfg, t, e, qp, qp_sw, kb, b_refs[i][0, h], sink_row,
                                                         left_q)
                    kvh = cfg.kvhalf(t, e)
                    dom = jnp.where(keep, dop if e == kvh else dop_sw, 0.0).astype(MXU_DT)
                    dp = _dot(dom, vb, NT_DIMS)
                    dd = jnp.sum(p * dp, axis=-1, keepdims=True)
                    ds = p * (dp - dd)
                    p_sink = jnp.exp(snk - m) / l
                    dsink = jnp.sum(-p_sink * dd, axis=0, keepdims=True)
                    dsk_acc = dsk_acc + jnp.where(jnp.logical_and(row0, lane8 == h), dsink, 0.0)
                    dsb = ds.astype(MXU_DT)
                    dv_acc = dv_acc + _dot(p.astype(MXU_DT), dom, TN_DIMS)
                    dk_acc = dk_acc + _dot(dsb, qm, TN_DIMS)
                    ksrc = kf if e == kvh else kf_sw
                    ksel = jnp.where(left_k if e == 0 else jnp.logical_not(left_k), ksrc, 0.0).astype(MXU_DT)
                    dq_t = dq_t + _dot(dsb, ksel)
                    db_ref[0, h] += ds
                dq_ref[rows, t * LANES:(t + 1) * LANES] = dq_t * 0.125
            dk_ref[pl.ds(ks, kw), :] += dk_acc
            dv_ref[pl.ds(ks, kw), :] += dv_acc
        dsk_ref[...] += dsk_acc

    def bias_spec(i):
        return pl.BlockSpec((1, hq, qb, kw), lambda g, n: (cfg.variant(sub * n + i), g, 0, 0))

    n_var = bias.shape[0]
    res = _call(
        body, name=name, grid=(g_n, s // (sub * qb)),
        in_specs=[pl.BlockSpec((sub * qb, qw), lambda g, n: (n, cfg.qcol(g))),
                  pl.BlockSpec((s, LANES), lambda g, n: (0, cfg.kcol(g))),
                  pl.BlockSpec((s, LANES), lambda g, n: (0, cfg.vcol(g)))]
        + [bias_spec(i) for i in range(sub)]
        + [pl.BlockSpec((None, 1, LANES), lambda g, n: (g, 0, 0)),
           pl.BlockSpec((sub * qb, qw), lambda g, n: (n, g))],
        out_specs=[pl.BlockSpec((sub * qb, qw), lambda g, n: (n, g)),
                   pl.BlockSpec((s, LANES), lambda g, n: (0, g)),
                   pl.BlockSpec((s, LANES), lambda g, n: (0, g))]
        + [bias_spec(i) for i in range(sub)]
        + [pl.BlockSpec((None, 8, LANES), lambda g, n: (g, 0, 0))],
        out_shape=[_sds((s, g_n * qw)), _sds((s, g_n * LANES)), _sds((s, g_n * LANES))]
        + [_sds((n_var, g_n * hq, qb, kw))] * sub + [_sds((g_n, 8, LANES))],
        sem=("parallel", "arbitrary"))(proj, proj, proj, *([bias] * sub), sink, do)
    dq, dk, dv = res[:3]
    dbias = _sum_visited(res[3:3 + sub], cfg, s)
    return dq, dk, dv, dbias, res[3 + sub]


QC_COL, KC_COL, VC_COL = 9, 13, 14
CW = 256


def _swap16(x):
    w = x.shape[1]
    lane = _lane_iota(x.shape)
    return jnp.where(lane % 32 < 16, pltpu.roll(x, w - 16, 1), pltpu.roll(x, 16, 1))


def _dup_halves(x):
    left = _lane_iota(x.shape) < HEAD_DIM
    sw = _swap_halves(x)
    return jnp.where(left, x, sw), jnp.where(left, sw, x)


def _normrope(x, gain, cos, sin, ones_ref):
    ms = _segsum64(x * x, ones_ref) * (1.0 / HEAD_DIM)
    r = lax.rsqrt(ms + EPS)
    y = (x * r) * gain
    return y * cos + _swap16(y) * sin, r


def _cprep_fwd(proj, gq, gk, cos, sin, ones, name):
    s = proj.shape[0]
    tm = _row_tile(s)

    def body(q0, q1, q2, q3, k_ref, v_ref, gq_ref, gk_ref, cos_ref, sin_ref, ones_ref, qh_ref, kd_ref, vd_ref):
        cos_v, sin_v = cos_ref[...], sin_ref[...]
        for c, q_ref in enumerate((q0, q1, q2, q3)):
            y, _ = _normrope(q_ref[...], gq_ref[...], cos_v, sin_v, ones_ref)
            qh_ref[:, c * CW:(c + 1) * CW] = (y * 0.125).astype(qh_ref.dtype)
        yk, _ = _normrope(k_ref[...], gk_ref[...], cos_v, sin_v, ones_ref)
        vv = v_ref[...]
        for p in range(2):
            ka, kb_ = _dup_halves(yk[:, p * LANES:(p + 1) * LANES])
            va, vb_ = _dup_halves(vv[:, p * LANES:(p + 1) * LANES])
            kd_ref[:, (2 * p) * LANES:(2 * p + 1) * LANES] = ka.astype(kd_ref.dtype)
            kd_ref[:, (2 * p + 1) * LANES:(2 * p + 2) * LANES] = kb_.astype(kd_ref.dtype)
            vd_ref[:, (2 * p) * LANES:(2 * p + 1) * LANES] = va.astype(vd_ref.dtype)
            vd_ref[:, (2 * p + 1) * LANES:(2 * p + 2) * LANES] = vb_.astype(vd_ref.dtype)

    def chunk(col):
        return pl.BlockSpec((tm, CW), lambda i: (i, col))

    vec = pl.BlockSpec((1, CW), lambda i: (0, 0))
    tab = pl.BlockSpec((tm, CW), lambda i: (i, 0))
    return _call(body, name=name, grid=(s // tm,),
                 in_specs=[chunk(QC_COL), chunk(QC_COL + 1), chunk(QC_COL + 2), chunk(QC_COL + 3),
                           chunk(KC_COL), chunk(VC_COL), vec, vec, tab, tab,
                           pl.BlockSpec((LANES, LANES), lambda i: (0, 0))],
                 out_specs=[pl.BlockSpec((tm, 4 * CW), lambda i: (i, 0)),
                            pl.BlockSpec((tm, 2 * CW), lambda i: (i, 0)),
                            pl.BlockSpec((tm, 2 * CW), lambda i: (i, 0))],
                 out_shape=[_sds((s, 4 * CW), MXU_DT), _sds((s, 2 * CW), MXU_DT), _sds((s, 2 * CW), MXU_DT)],
                 sem=("parallel",))(proj, proj, proj, proj, proj, proj, gq, gk, cos, sin, ones)


def _cprep_bwd(proj, gq, gk, cos, sin, ones, dqh, dkd, dvd, name):
    s = proj.shape[0]
    tm = _row_tile(s)

    def fold(ref, p):
        a = ref[:, (2 * p) * LANES:(2 * p + 1) * LANES]
        b = ref[:, (2 * p + 1) * LANES:(2 * p + 2) * LANES]
        ta = a + _swap_halves(a)
        tb = b + _swap_halves(b)
        return jnp.where(_lane_iota(a.shape) < HEAD_DIM, ta, tb)

    def norm_bwd(x, gain, dyr, cos_v, sin_v, ones_ref):
        dy = dyr * cos_v + _swap16(dyr * sin_v)
        ms = _segsum64(x * x, ones_ref) * (1.0 / HEAD_DIM)
        r = lax.rsqrt(ms + EPS)
        xh = x * r
        gd = dy * gain
        c = _segsum64(gd * xh, ones_ref) * (1.0 / HEAD_DIM)
        return r * (gd - xh * c), jnp.sum(dy * xh, axis=0, keepdims=True)

    def body(q0, q1, q2, q3, k_ref, gq_ref, gk_ref, cos_ref, sin_ref, ones_ref, dqh_ref, dkd_ref, dvd_ref,
             dq_ref, dk_ref, dv_ref, dgq_ref, dgk_ref):
        i = pl.program_id(0)
        cos_v, sin_v = cos_ref[...], sin_ref[...]
        gq_part = jnp.zeros((1, CW), F32)
        for c, q_ref in enumerate((q0, q1, q2, q3)):
            dx, dg = norm_bwd(q_ref[...], gq_ref[...], dqh_ref[:, c * CW:(c + 1) * CW] * 0.125, cos_v, sin_v,
                              ones_ref)
            dq_ref[:, c * CW:(c + 1) * CW] = dx
            gq_part = gq_part + dg
        dkr = jnp.concatenate([fold(dkd_ref, 0), fold(dkd_ref, 1)], axis=1)
        dxk, gk_part = norm_bwd(k_ref[...], gk_ref[...], dkr, cos_v, sin_v, ones_ref)
        dk_ref[...] = dxk
        dv_ref[...] = jnp.concatenate([fold(dvd_ref, 0), fold(dvd_ref, 1)], axis=1)

        @pl.when(i == 0)
        def _():
            dgq_ref[...] = gq_part
            dgk_ref[...] = gk_part

        @pl.when(i > 0)
        def _():
            dgq_ref[...] += gq_part
            dgk_ref[...] += gk_part

    def chunk(col):
        return pl.BlockSpec((tm, CW), lambda i: (i, col))

    vec = pl.BlockSpec((1, CW), lambda i: (0, 0))
    tab = pl.BlockSpec((tm, CW), lambda i: (i, 0))
    return _call(body, name=name, grid=(s // tm,),
                 in_specs=[chunk(QC_COL), chunk(QC_COL + 1), chunk(QC_COL + 2), chunk(QC_COL + 3), chunk(KC_COL),
                           vec, vec, tab, tab, pl.BlockSpec((LANES, LANES), lambda i: (0, 0)),
                           pl.BlockSpec((tm, 4 * CW), lambda i: (i, 0)),
                           pl.BlockSpec((tm, 2 * CW), lambda i: (i, 0)),
                           pl.BlockSpec((tm, 2 * CW), lambda i: (i, 0))],
                 out_specs=[pl.BlockSpec((tm, 4 * CW), lambda i: (i, 0)), tab, tab, vec, vec],
                 out_shape=[_sds((s, 4 * CW)), _sds((s, CW)), _sds((s, CW)), _sds((1, CW)), _sds((1, CW))],
                 sem=("arbitrary",))(proj, proj, proj, proj, proj, gq, gk, cos, sin, ones, dqh, dkd, dvd)


def _flash_tiles(s):
    return min(512, s), min(1024, s)


def _row_iota(shape):
    return lax.broadcasted_iota(jnp.int32, shape, 0)


def _flash_fwd(qh, kd, vdt, name):
    s = qh.shape[0]
    tq, tk = _flash_tiles(s)
    nk = s // tk

    n_chunks = 1
    cw = tq // n_chunks
    units = [(t, c, e) for t in range(2) for c in range(n_chunks) for e in range(2)]

    def body(q_ref, k_ref, vt_ref, ot_ref, lse_ref, qm_ref, m_ref, lacc_ref, acc_ref):
        j = pl.program_id(2)

        @pl.when(j == 0)
        def _():
            m_ref[...] = jnp.full(m_ref.shape, MASK_VALUE, F32)
            lacc_ref[...] = jnp.zeros_like(lacc_ref)
            acc_ref[...] = jnp.zeros_like(acc_ref)
            left_q = _lane_iota((tq, LANES)) < HEAD_DIM
            for t in range(2):
                qp = q_ref[:, t * LANES:(t + 1) * LANES]
                qm_ref[2 * t] = jnp.where(left_q, qp, jnp.zeros_like(qp))
                qm_ref[2 * t + 1] = jnp.where(left_q, jnp.zeros_like(qp), qp)

        kb = k_ref[...]
        vt = vt_ref[...]
        top_k = _row_iota((LANES, tk)) < HEAD_DIM
        top_c = _row_iota((LANES, cw)) < HEAD_DIM
        vt_e = (jnp.where(top_k, vt, jnp.ones_like(vt)), jnp.where(top_k, jnp.ones_like(vt), vt))

        def scores(unit):
            t, c, e = unit
            return _dot(kb, qm_ref[2 * t + e, c * cw:(c + 1) * cw, :], NT_DIMS)

        nxt = scores(units[0])
        pv, alpha = [], []
        for n, (t, c, e) in enumerate(units):
            st = nxt
            if n + 1 < len(units):
                nxt = scores(units[n + 1])
            h = 2 * t + e
            cols = slice(c * cw, (c + 1) * cw)
            m_prev = m_ref[h, :, cols]
            m_new = jnp.maximum(m_prev, jnp.max(st, axis=0, keepdims=True))
            alpha.append(jnp.exp(m_prev - m_new))
            pt = jnp.exp(st - m_new)
            m_ref[h, :, cols] = m_new
            pv.append(_dot(vt_e[e], pt.astype(MXU_DT)))
            if e == 1:
                acc_ref[t, :, cols] = (acc_ref[t, :, cols] * jnp.where(top_c, alpha[0], alpha[1])
                                       + jnp.where(top_c, pv[0], pv[1]))
                lacc_ref[t, :, cols] = (lacc_ref[t, :, cols] * jnp.where(top_c, alpha[1], alpha[0])
                                        + jnp.where(top_c, pv[1], pv[0]))
                pv, alpha = [], []

        @pl.when(j == nk - 1)
        def _():
            for t in range(2):
                lacc = lacc_ref[t]
                l_sw = jnp.concatenate([lacc[HEAD_DIM:], lacc[:HEAD_DIM]], axis=0)
                ot_ref[t * LANES:(t + 1) * LANES, :] = acc_ref[t] / l_sw
                lse_ref[2 * t:2 * t + 1, :] = m_ref[2 * t] + jnp.log(lacc[HEAD_DIM:HEAD_DIM + 1])
                lse_ref[2 * t + 1:2 * t + 2, :] = m_ref[2 * t + 1] + jnp.log(lacc[0:1])

    return _call(body, name=name, grid=(4, s // tq, nk),
                 in_specs=[pl.BlockSpec((tq, CW), lambda g, i, j: (i, g)),
                           pl.BlockSpec((tk, LANES), lambda g, i, j: (j, g)),
                           pl.BlockSpec((LANES, tk), lambda g, i, j: (g, j))],
                 out_specs=[pl.BlockSpec((CW, tq), lambda g, i, j: (g, i)),
                            pl.BlockSpec((None, 4, tq), lambda g, i, j: (g, 0, i))],
                 out_shape=[_sds((4 * CW, s)), _sds((4, 4, s))],
                 scratch=[pltpu.VMEM((4, tq, LANES), MXU_DT), pltpu.VMEM((4, 1, tq), F32),
                          pltpu.VMEM((2, LANES, tq), F32), pltpu.VMEM((2, LANES, tq), F32)],
                 sem=("parallel", "parallel", "arbitrary"))(qh, kd, vdt)


def _flash_bwd(qh, kd, vd, kdt, do, lse, dd, name):
    s = qh.shape[0]
    tq, tk = _flash_tiles(s)[0], min(512, s)
    ni = s // tq

    def body(q_ref, k_ref, v_ref, kt_ref, do_ref, lse_ref, dd_ref, dqt_ref, dk_ref, dv_ref, dk_acc, dv_acc):
        j = pl.program_id(1)
        i = pl.program_id(2)

        @pl.when(i == 0)
        def _():
            dk_acc[...] = jnp.zeros_like(dk_acc)
            dv_acc[...] = jnp.zeros_like(dv_acc)

        kb = k_ref[...]
        vb = v_ref[...]
        kt = kt_ref[...]
        left_q = _lane_iota((tq, LANES)) < HEAD_DIM
        top = _row_iota((LANES, tq)) < HEAD_DIM
        cols = pl.ds(pl.multiple_of(i * tq, tq), tq)

        def first_stage(h):
            t, e = divmod(h, 2)
            keep_q = left_q if e == 0 else jnp.logical_not(left_q)
            qp = q_ref[:, t * LANES:(t + 1) * LANES]
            dop = do_ref[:, t * LANES:(t + 1) * LANES]
            qm = jnp.where(keep_q, qp, jnp.zeros_like(qp))
            dom = jnp.where(keep_q, dop, jnp.zeros_like(dop))
            return qm, dom, _dot(kb, qm, NT_DIMS), _dot(vb, dom, NT_DIMS)

        nxt = first_stage(0)
        dqt = []
        for h in range(4):
            qm, dom, st, dpt = nxt
            if h < 3:
                nxt = first_stage(h + 1)
            pt = jnp.exp(st - lse_ref[h:h + 1, :])
            dsb = (pt * (dpt - dd_ref[h:h + 1, :])).astype(MXU_DT)
            dv_acc[...] += _dot(pt.astype(MXU_DT), dom)
            dk_acc[...] += _dot(dsb, qm)
            dqt.append(_dot(kt, dsb))
            if h % 2 == 1:
                t = h // 2
                dq_t = jnp.where(top, dqt[0], dqt[1])
                dqt = []

                @pl.when(j == 0)
                def _():
                    dqt_ref[t * LANES:(t + 1) * LANES, cols] = dq_t

                @pl.when(j > 0)
                def _():
                    dqt_ref[t * LANES:(t + 1) * LANES, cols] += dq_t

        @pl.when(i == ni - 1)
        def _():
            dk_ref[...] = dk_acc[...]
            dv_ref[...] = dv_acc[...]

    qspec = pl.BlockSpec((tq, CW), lambda g, j, i: (i, g))
    kspec = pl.BlockSpec((tk, LANES), lambda g, j, i: (j, g))
    rowspec = pl.BlockSpec((None, 4, tq), lambda g, j, i: (g, 0, i))
    return _call(body, name=name, grid=(4, s // tk, ni),
                 in_specs=[qspec, kspec, kspec, pl.BlockSpec((LANES, tk), lambda g, j, i: (g, j)), qspec,
                           rowspec, rowspec],
                 out_specs=[pl.BlockSpec((CW, s), lambda g, j, i: (g, 0)), kspec, kspec],
                 out_shape=[_sds((4 * CW, s)), _sds((s, 2 * CW)), _sds((s, 2 * CW))],
                 scratch=[pltpu.VMEM((tk, LANES), F32), pltpu.VMEM((tk, LANES), F32)],
                 sem=("parallel", "arbitrary", "arbitrary"))(qh, kd, vd, kdt, do, lse, dd)


def _groupnorm_fwd(oa, ob, oc, ga, gb, gc, name):
    s = oa.shape[0]
    tm = _row_tile(s)
    wa, wb, wc = oa.shape[1], ob.shape[1], oc.shape[1]

    def body(oa_ref, ob_ref, oc_ref, ga_ref, gb_ref, gc_ref, mix_ref):
        off = 0
        for o_ref, g_ref, w in ((oa_ref, ga_ref, wa), (ob_ref, gb_ref, wb), (oc_ref, gc_ref, wc)):
            xv = o_ref[...]
            r = lax.rsqrt(jnp.mean(xv * xv, axis=-1, keepdims=True) + EPS)
            mix_ref[:, off:off + w] = ((xv * r) * g_ref[...]).astype(mix_ref.dtype)
            off += w

    def row(w):
        return pl.BlockSpec((tm, w), lambda i: (i, 0))

    def vec(w):
        return pl.BlockSpec((1, w), lambda i: (0, 0))

    return _call(body, name=name, grid=(s // tm,),
                 in_specs=[row(wa), row(wb), row(wc), vec(wa), vec(wb), vec(wc)],
                 out_specs=row(wa + wb + wc), out_shape=_sds((s, wa + wb + wc), MXU_DT),
                 sem=("parallel",))(oa, ob, oc, ga.reshape(1, wa), gb.reshape(1, wb), gc.reshape(1, wc))


def _groupnorm_bwd(dmix, oa, ob, oc, ga, gb, gc, ones, name):
    s = oa.shape[0]
    tm = _row_tile(s)
    wa, wb, wc = oa.shape[1], ob.shape[1], oc.shape[1]

    def body(dm_ref, oa_ref, ob_ref, oc_ref, ga_ref, gb_ref, gc_ref, ones_ref,
             doa_ref, dob_ref, doc_ref, docb_ref, dd_ref, dga_ref, dgb_ref, dgc_ref):
        i = pl.program_id(0)
        off = 0
        parts = []
        for o_ref, g_ref, do_ref, w in ((oa_ref, ga_ref, doa_ref, wa), (ob_ref, gb_ref, dob_ref, wb),
                                        (oc_ref, gc_ref, doc_ref, wc)):
            xv = o_ref[...]
            dh = dm_ref[:, off:off + w]
            r = lax.rsqrt(jnp.mean(xv * xv, axis=-1, keepdims=True) + EPS)
            xh = xv * r
            gd = dh * g_ref[...]
            c = jnp.mean(gd * xh, axis=-1, keepdims=True)
            dx = r * (gd - xh * c)
            do_ref[...] = dx
            parts.append(jnp.sum(dh * xh, axis=0, keepdims=True))
            if o_ref is oc_ref:
                docb_ref[...] = dx.astype(docb_ref.dtype)
                dd_ref[...] = _segsum64(dx * xv, ones_ref)
            off += w

        @pl.when(i == 0)
        def _():
            dga_ref[...], dgb_ref[...], dgc_ref[...] = parts

        @pl.when(i > 0)
        def _():
            dga_ref[...] += parts[0]
            dgb_ref[...] += parts[1]
            dgc_ref[...] += parts[2]

    def row(w):
        return pl.BlockSpec((tm, w), lambda i: (i, 0))

    def vec(w):
        return pl.BlockSpec((1, w), lambda i: (0, 0))

    return _call(body, name=name, grid=(s // tm,),
                 in_specs=[row(wa + wb + wc), row(wa), row(wb), row(wc), vec(wa), vec(wb), vec(wc),
                           pl.BlockSpec((LANES, LANES), lambda i: (0, 0))],
                 out_specs=[row(wa), row(wb), row(wc), row(wc), row(wc), vec(wa), vec(wb), vec(wc)],
                 out_shape=[_sds((s, wa)), _sds((s, wb)), _sds((s, wc)), _sds((s, wc), MXU_DT), _sds((s, wc)),
                            _sds((1, wa)), _sds((1, wb)), _sds((1, wc))],
                 sem=("arbitrary",))(dmix, oa, ob, oc, ga.reshape(1, wa), gb.reshape(1, wb), gc.reshape(1, wc), ones)


def _adam_math(w, g, m, v):
    m = ADAM_B1 * m + (1.0 - ADAM_B1) * g
    v = ADAM_B2 * v + (1.0 - ADAM_B2) * jnp.square(g)
    m_hat = m / (1.0 - ADAM_B1 ** ADAM_STEP)
    v_hat = v / (1.0 - ADAM_B2 ** ADAM_STEP)
    delta = -ADAM_LR * (m_hat / (jnp.sqrt(v_hat) + ADAM_EPS) + ADAM_WD * w)
    return delta, m, v


def _mesh_pos():
    return lax.axis_index("x"), lax.axis_index("y"), lax.axis_index("c")


def _peer_chips(x, y):
    return [(1 - x, y), (x, 1 - y), (1 - x, 1 - y)]


def _allreduce_small_adam(g, w, m, v):
    rows = g.shape[0]

    def body(g_ref, w_ref, m_ref, v_ref, gs_ref, d_ref, mo_ref, vo_ref, buf, send_sems, recv_sems):
        x, y, c = _mesh_pos()
        me = 4 * x + 2 * y + c
        buf[me] = g_ref[...]
        copies = []
        for k in range(1, 8):
            px = 1 - x if (k >> 2) & 1 else x
            py = 1 - y if (k >> 1) & 1 else y
            pc = 1 - c if k & 1 else c
            cp = pltpu.make_async_remote_copy(src_ref=g_ref, dst_ref=buf.at[me], send_sem=send_sems.at[k - 1],
                                              recv_sem=recv_sems.at[k - 1], device_id=(px, py, pc),
                                              device_id_type=MESH_ID)
            cp.start()
            copies.append(cp)
        for cp in copies:
            cp.wait()
        total = buf[0]
        for d in range(1, 8):
            total = total + buf[d]
        gs_ref[...] = total
        d_ref[...], mo_ref[...], vo_ref[...] = _adam_math(w_ref[...], total, m_ref[...], v_ref[...])

    vm = pl.BlockSpec(memory_space=pltpu.VMEM)
    return _call(body, name="allreduce_small_adam", in_specs=[vm] * 4, out_specs=[vm] * 4,
                 out_shape=[_sds((rows, LANES))] * 4,
                 scratch=[pltpu.VMEM((8, rows, LANES), F32), pltpu.SemaphoreType.DMA((7,)),
                          pltpu.SemaphoreType.DMA((7,))])(g, w, m, v)


HBM_SPEC = pl.BlockSpec(memory_space=pltpu.HBM)
SEM_SPEC = pl.BlockSpec(memory_space=pltpu.SEMAPHORE)
VMEM_SPEC = pl.BlockSpec(memory_space=pltpu.VMEM)
SIDE_EFFECT = pltpu.SideEffectType.DATAFLOW_SIDE_EFFECTING
N_PEERS = 7


def _in_hbm(a):
    return pltpu.with_memory_space_constraint(a, pltpu.HBM)


def _landing(shape, dtype):
    return _in_hbm(lax.empty(shape, dtype))


def _token_shape():
    return _sds((8, LANES))


def _gather_start(shards):
    n = len(shards)
    n_layers = shards[0].shape[0]
    jobs = [(l, t) for l in range(n_layers) for t in range(n)]
    nj = len(jobs)

    def body(*refs):
        sh = refs[:n]
        outs = refs[n + nj:]
        send, recv, land, token = outs[:nj], outs[nj:2 * nj], outs[2 * nj:3 * nj], outs[3 * nj]
        x, y, c = _mesh_pos()
        me = 2 * x + y
        for j, (l, t) in enumerate(jobs):
            for k, (px, py) in enumerate(_peer_chips(x, y)):
                pltpu.make_async_remote_copy(src_ref=sh[t].at[l], dst_ref=land[j].at[me], send_sem=send[j].at[k],
                                             recv_sem=recv[j].at[k], device_id=(px, py, c),
                                             device_id_type=MESH_ID).start()
        token[...] = jnp.zeros_like(token)

    lands = [_landing((4,) + shards[t].shape[1:], shards[t].dtype) for _, t in jobs]
    res = pl.pallas_call(
        body, name="gather_start",
        out_shape=tuple([pltpu.SemaphoreType.DMA((3,))] * (2 * nj)
                        + [pltpu.HBM(a.shape, a.dtype) for a in lands] + [_token_shape()]),
        in_specs=[HBM_SPEC] * (n + nj), out_specs=tuple([SEM_SPEC] * (2 * nj) + [HBM_SPEC] * nj + [VMEM_SPEC]),
        input_output_aliases={n + j: 2 * nj + j for j in range(nj)},
        compiler_params=pltpu.CompilerParams(has_side_effects=SIDE_EFFECT),
    )(*[_in_hbm(a) for a in shards], *lands)
    return jobs, res[:nj], res[nj:2 * nj], res[2 * nj:3 * nj], res[3 * nj]


def _gather_wait(shard, layer, land, send_sem, recv_sem, after, name):
    def body(sh_ref, land_ref, send_ref, recv_ref, after_ref, land_out):
        x, y, c = _mesh_pos()
        for k in range(3):
            cp = pltpu.make_async_remote_copy(src_ref=sh_ref.at[layer], dst_ref=land_ref.at[k],
                                              send_sem=send_ref.at[k], recv_sem=recv_ref.at[k],
                                              device_id=(x, y, 1 - c), device_id_type=MESH_ID)
            cp.wait_send()
            cp.wait_recv()

    return pl.pallas_call(
        body, name=name, out_shape=pltpu.HBM(land.shape, land.dtype),
        in_specs=[HBM_SPEC, HBM_SPEC, SEM_SPEC, SEM_SPEC, ANY], out_specs=HBM_SPEC,
        input_output_aliases={1: 0},
        compiler_params=pltpu.CompilerParams(has_side_effects=SIDE_EFFECT),
    )(shard, land, send_sem, recv_sem, after)


def _grad_start(g, name):
    def body(g_ref, land_in, send, recv, land, token):
        x, y, c = _mesh_pos()
        me = 2 * x + y
        pltpu.make_async_remote_copy(src_ref=g_ref.at[me], dst_ref=land.at[0], send_sem=send.at[0],
                                     recv_sem=recv.at[0], device_id=(x, y, 1 - c), device_id_type=MESH_ID).start()
        for k, (px, py) in enumerate(_peer_chips(x, y)):
            for c2 in range(2):
                pltpu.make_async_remote_copy(src_ref=g_ref.at[2 * px + py], dst_ref=land.at[1 + 2 * k + c],
                                             send_sem=send.at[1 + 2 * k + c2], recv_sem=recv.at[1 + 2 * k + c],
                                             device_id=(px, py, c2), device_id_type=MESH_ID).start()
        token[...] = jnp.zeros_like(token)

    land = _landing((N_PEERS,) + g.shape[1:], g.dtype)
    return pl.pallas_call(
        body, name=name,
        out_shape=(pltpu.SemaphoreType.DMA((N_PEERS,)), pltpu.SemaphoreType.DMA((N_PEERS,)),
                   pltpu.HBM(land.shape, land.dtype), _token_shape()),
        in_specs=[HBM_SPEC, HBM_SPEC], out_specs=(SEM_SPEC, SEM_SPEC, HBM_SPEC, VMEM_SPEC),
        input_output_aliases={1: 2},
        compiler_params=pltpu.CompilerParams(has_side_effects=SIDE_EFFECT),
    )(_in_hbm(g), land)


def _grad_wait(g, land, send_sem, recv_sem, after, name):
    def body(g_ref, land_ref, send_ref, recv_ref, after_ref, land_out):
        x, y, c = _mesh_pos()
        for k in range(N_PEERS):
            cp = pltpu.make_async_remote_copy(src_ref=g_ref.at[0], dst_ref=land_ref.at[k], send_sem=send_ref.at[k],
                                              recv_sem=recv_ref.at[k], device_id=(x, y, 1 - c),
                                              device_id_type=MESH_ID)
            cp.wait_send()
            cp.wait_recv()

    return pl.pallas_call(
        body, name=name, out_shape=pltpu.HBM(land.shape, land.dtype),
        in_specs=[HBM_SPEC, HBM_SPEC, SEM_SPEC, SEM_SPEC, ANY], out_specs=HBM_SPEC,
        input_output_aliases={1: 0},
        compiler_params=pltpu.CompilerParams(has_side_effects=SIDE_EFFECT),
    )(g, land, send_sem, recv_sem, after)


def _sum_adam(g, land, w, m, v, prev, layer, me_idx, name):
    _, r, cols = g.shape
    tr = min(128, r)

    def body(me_ref, g_ref, l0, l1, l2, l3, l4, l5, l6, w_ref, m_ref, v_ref, p0, p1, p2, p3,
             go_ref, d_ref, mo_ref, vo_ref):
        total = g_ref[...].astype(F32) + l0[...].astype(F32)
        for ref in (l1, l2, l3, l4, l5, l6):
            total = total + ref[...].astype(F32)
        go_ref[...] = total
        d_ref[...], mo_ref[...], vo_ref[...] = _adam_math(w_ref[...], total, m_ref[...], v_ref[...])

    def slot(k):
        return pl.BlockSpec((None, tr, cols), lambda i, me: (k, i, 0))

    lay = pl.BlockSpec((None, tr, cols), lambda i, me: (layer, i, 0))
    return _call(body, name=name, grid=(r // tr,), prefetch=1,
                 in_specs=[pl.BlockSpec((None, tr, cols), lambda i, me: (me[0], i, 0))]
                 + [slot(k) for k in range(N_PEERS)] + [lay, lay, lay] + [ANY] * 4,
                 out_specs=[lay] * 4, out_shape=[_sds(w.shape)] * 4,
                 aliases={12 + k: k for k in range(4)}, sem=("parallel",))(
                     me_idx, g, *([land] * N_PEERS), w, m, v, *prev)


def _t5_bucket(rel):
    nb = T5_BUCKETS // 2
    max_exact = nb // 2
    base = jnp.where(rel > 0, nb, 0)
    n = jnp.abs(rel)
    nf = jnp.maximum(n, 1).astype(F32)
    large = max_exact + (jnp.log(nf / max_exact) / math.log(T5_MAX_DIST / max_exact)
                         * (nb - max_exact)).astype(jnp.int32)
    large = jnp.minimum(large, nb - 1)
    return base + jnp.where(n < max_exact, n, large)


def _a_bias_maps():
    v = jnp.arange(3)[:, None, None]
    q = jnp.arange(128)[None, :, None]
    k = jnp.arange(384)[None, None, :]
    rel = k - 128 * v - q
    valid = jnp.abs(rel) <= 128
    onehot = (_t5_bucket(rel)[..., None] == jnp.arange(T5_BUCKETS)).astype(F32)
    return onehot * valid[..., None].astype(F32), valid


def _b_bias_maps():
    v = jnp.arange(8)[:, None]
    i = jnp.arange(NA_ROWS)[None, :]
    dr = jnp.where(v == 4, i + 3, i - v + 7)
    row_oh = (dr[..., None] == jnp.arange(2 * NA_ROWS - 1)).astype(F32)
    q = jnp.arange(GRID_W)[:, None]
    kc = jnp.arange(GRID_W)[None, :]
    cs = jnp.clip(q - 8, 0, GRID_W - 16)
    valid = (kc >= cs) & (kc < cs + 16)
    col_oh = ((kc - q + 15)[..., None] == jnp.arange(31)).astype(F32) * valid[..., None].astype(F32)
    return row_oh, col_oh, valid


def _rope_tables(s):
    t = jnp.arange(s)
    row = (t // GRID_W).astype(F32)
    col = (t % GRID_W).astype(F32)
    axis_dim = HEAD_DIM // 2
    freqs = ROPE_THETA ** (-jnp.arange(0, axis_dim, 2, dtype=F32) / axis_dim)
    ang_row = row[:, None] * freqs[None, :]
    ang_col = col[:, None] * freqs[None, :]
    cos = jnp.concatenate([jnp.cos(ang_row)] * 2 + [jnp.cos(ang_col)] * 2, axis=1)
    sin = jnp.concatenate([-jnp.sin(ang_row), jnp.sin(ang_row), -jnp.sin(ang_col), jnp.sin(ang_col)], axis=1)
    return jnp.tile(cos, (1, CW // HEAD_DIM)), jnp.tile(sin, (1, CW // HEAD_DIM))


def _pack(parts, rows):
    flat = jnp.concatenate([p.reshape(-1).astype(F32) for p in parts])
    return jnp.pad(flat, (0, rows * LANES - flat.shape[0])).reshape(rows, LANES)


def _unpack(buf, shapes):
    flat = buf.reshape(-1)
    out, off = [], 0
    for shp in shapes:
        size = math.prod(shp)
        out.append(flat[off:off + size].reshape(shp))
        off += size
    return out


def kernel(x, norm_mix, w_in, a_sink, t5_table, b_rpb, c_q_gain, c_k_gain, out_gain_a, out_gain_b, out_gain_c, w_o, norm_mlp, w_up, w_down, norm_final, loss_target, m_norm_mix, m_w_in, m_a_sink, m_t5_table, m_b_rpb, m_c_q_gain, m_c_k_gain, m_out_gain_a, m_out_gain_b, m_out_gain_c, m_w_o, m_norm_mlp, m_w_up, m_w_down, m_norm_final, v_norm_mix, v_w_in, v_a_sink, v_t5_table, v_b_rpb, v_c_q_gain, v_c_k_gain, v_out_gain_a, v_out_gain_b, v_out_gain_c, v_w_o, v_norm_mlp, v_w_up, v_w_down, v_norm_final):
    n_layers = w_in.shape[0]
    s, d = x.shape[1], x.shape[2]
    d_ff = 4 * w_up.shape[2]
    in_w = 4 * w_in.shape[2]
    xs = x.reshape(s, d)
    target = loss_target.reshape(s, d)
    cfg_a, cfg_b = _cfg_a(s), _cfg_b(s)

    x_i, y_i, _ = _mesh_pos()
    me_chip = 2 * x_i + y_i
    me_idx = me_chip.astype(jnp.int32).reshape(1)
    w_bf = [w_in.astype(MXU_DT), w_o.astype(MXU_DT), w_up.astype(MXU_DT), w_down.astype(MXU_DT)]
    jobs, gather_send, gather_recv, gather_land, gather_token = _gather_start(w_bf)
    job_of = {job: j for j, job in enumerate(jobs)}
    ff_shard = w_up.shape[2]

    def gathered(l, t, after):
        j = job_of[(l, t)]
        land = _gather_wait(w_bf[t], l, gather_land[j], gather_send[j], gather_recv[j], after,
                            "gather_wait_%d_%d" % (l, t))
        return lax.dynamic_update_slice(land, w_bf[t][l][None], (me_chip, 0, 0))

    ones = _pair_ones()
    cos_t, sin_t = _rope_tables(s)
    a_onehot, a_valid = _a_bias_maps()
    bias_a = jnp.where(a_valid[:, None], jnp.einsum("vqkb,bh->vhqk", a_onehot, t5_table, precision=HIGHEST),
                       MASK_VALUE)
    row_oh, col_oh, b_valid = _b_bias_maps()
    sink_b = jnp.full((4, 1, LANES), MASK_VALUE, F32)

    def b_bias(rpb):
        t = jnp.einsum("hrz,vir->vhiz", rpb, row_oh, precision=HIGHEST)
        t = jnp.einsum("vhiz,qcz->vhqic", t, col_oh, precision=HIGHEST)
        t = jnp.where(b_valid[None, None, :, None, :], t, MASK_VALUE)
        return t.reshape(8, 8, GRID_W, NA_ROWS * GRID_W)

    def tile_gain(gvec):
        return jnp.tile(gvec, CW // HEAD_DIM).reshape(1, CW)

    def pad_sink(svec):
        return jnp.pad(svec, (0, LANES - svec.shape[0])).reshape(1, 1, LANES)

    saved = []
    xc = xs
    for l in range(n_layers):
        h1 = _rms_fwd(xc, norm_mix[l] + gather_token[0, 0] if l == 0 else norm_mix[l], "rms_mix")
        wf_in = gathered(l, 0, h1).transpose(1, 0, 2).reshape(d, in_w)
        proj = _matmul(h1, wf_in, mode="nn", name="proj_in", tm=1024, tn=768, tk=2048)
        bias_b = b_bias(b_rpb[l])
        oa = _local_attn_fwd(proj, bias_a, pad_sink(a_sink[l]), cfg_a, "attn_a_fwd")
        ob = _local_attn_fwd(proj, bias_b, sink_b, cfg_b, "attn_b_fwd")
        gq, gk = tile_gain(c_q_gain[l]), tile_gain(c_k_gain[l])
        qh, kd, vd = _cprep_fwd(proj, gq, gk, cos_t, sin_t, ones, "cprep_fwd")
        kdt, vdt = kd.T, vd.T
        oct, lse = _flash_fwd(qh, kd, vdt, "attn_c_fwd")
        oc = oct.T
        mix = _groupnorm_fwd(oa, ob, oc, out_gain_a[l], out_gain_b[l], out_gain_c[l], "groupnorm_fwd")
        wf_o = gathered(l, 1, mix).reshape(d, d)
        x_mid = _matmul(mix, wf_o, mode="nn", name="proj_out", tm=1024, tn=1024, tk=2048, epi="res",
                        extra=(xc,))
        h2 = _rms_fwd(x_mid, norm_mlp[l], "rms_mlp")
        wg_up = gathered(l, 2, h2)
        nb_up = ff_shard // 1024
        u, uu = _matmul(h2, wg_up, mode="nn", name="mlp_up", tm=1024, tn=1024, tk=2048, epi="relu2",
                        out_dtypes=(F32, MXU_DT), mkn=(s, d, d_ff),
                        b_spec=pl.BlockSpec((None, 2048, 1024), lambda i, j, kk: (j // nb_up, kk, j % nb_up)))
        wf_down = gathered(l, 3, uu).reshape(d_ff, d)
        x_out = _matmul(uu, wf_down, mode="nn", name="mlp_down", tm=1024, tn=1024, tk=2048, epi="res",
                        extra=(x_mid,))
        saved.append((xc, h1, proj, bias_b, oa, ob, qh, kd, vd, kdt, oc, lse, mix, x_mid, h2, u, uu,
                      wf_in, wf_o, wg_up, wf_down))
        xc = x_out

    loss_part, dx, dxb, dg_final = _final_loss(xc, norm_final, target, "final_loss")

    small = {k: [] for k in ("norm_mix", "a_sink", "b_rpb", "cq", "ck", "oga", "ogb", "ogc", "norm_mlp")}
    dbias_a_total = jnp.zeros_like(bias_a)
    big_w = {"w_in": (w_in, m_w_in, v_w_in), "w_o": (w_o, m_w_o, v_w_o), "w_up": (w_up, m_w_up, v_w_up),
             "w_down": (w_down, m_w_down, v_w_down)}
    big = {nm: [lax.empty(wmv[0].shape, F32) for _ in range(4)] for nm, wmv in big_w.items()}

    def send_grad(nm, l, g):
        send, recv, land, token = _grad_start(g, "grad_start_%s_%d" % (nm, l))
        return (nm, l, g, send, recv, land), token[0, 0]

    def finish_grads(pending, after):
        for nm, l, g, send, recv, land in pending:
            land = _grad_wait(g, land, send, recv, after, "grad_wait_%s_%d" % (nm, l))
            wmv = big_w[nm]
            big[nm] = _sum_adam(g, land, wmv[0], wmv[1], wmv[2], big[nm], l, me_idx, "sum_adam_%s_%d" % (nm, l))

    pending = []
    for l in reversed(range(n_layers)):
        (xin, h1, proj, bias_b, oa, ob, qh, kd, vd, kdt, oc, lse, mix, x_mid, h2, u, uu,
         wf_in, wf_o, wg_up, wf_down) = saved[l]
        started = []
        du = _matmul(dxb, wf_down, mode="nt", name="mlp_down_dgrad", tm=1024, tn=1024, tk=2048, epi="mul2u",
                     extra=(u,), out_dtypes=(MXU_DT,))
        gw = _matmul(uu, dxb, mode="tn", name="mlp_down_wgrad", tm=1024, tn=1024, tk=1024, out_dtypes=(GRAD_DT,))
        rec, tok_down = send_grad("w_down", l, gw.reshape(4, d_ff // 4, d))
        started.append(rec)
        nbk = ff_shard // 2048
        dh2 = _matmul(du, wg_up, mode="nt", name="mlp_up_dgrad", tm=1024, tn=1024, tk=2048,
                      mkn=(s, d_ff, d),
                      b_spec=pl.BlockSpec((None, 1024, 2048), lambda i, j, kk: (kk // nbk, j, kk % nbk)))
        nbo = ff_shard // 1024
        gw = _matmul(h2, du, mode="tn", name="mlp_up_wgrad", tm=1024, tn=1024, tk=1024, out_dtypes=(GRAD_DT,),
                     out_spec=pl.BlockSpec((None, 1024, 1024), lambda i, j, kk: (j // nbo, i, j % nbo)),
                     out_shape=(4, d, ff_shard))
        rec, tok_up = send_grad("w_up", l, gw)
        started.append(rec)
        dx_mid, dxmb, dg = _rms_bwd(x_mid, norm_mlp[l] + (tok_down + tok_up), dh2, dx, "rms_mlp_bwd")
        small["norm_mlp"].append(dg)
        dmix = _matmul(dxmb, wf_o, mode="nt", name="proj_out_dgrad", tm=1024, tn=1024, tk=2048)
        gw = _matmul(mix, dxmb, mode="tn", name="proj_out_wgrad", tm=1024, tn=1024, tk=1024, out_dtypes=(GRAD_DT,))
        rec, tok_o = send_grad("w_o", l, gw.reshape(4, d // 4, d))
        started.append(rec)
        doa, dob, doc, docb, ddc, dga, dgb, dgc = _groupnorm_bwd(
            dmix, oa, ob, oc, out_gain_a[l] + tok_o, out_gain_b[l], out_gain_c[l], ones, "groupnorm_bwd")
        small["oga"].append(dga)
        small["ogb"].append(dgb)
        small["ogc"].append(dgc)
        dqa, dka, dva, dbias_a, dsink = _local_attn_bwd(proj, bias_a, pad_sink(a_sink[l]), doa, cfg_a, "attn_a_bwd")
        dbias_a_total = dbias_a_total + dbias_a
        small["a_sink"].append(dsink[0, 0, :a_sink.shape[1]])
        dqb, dkb, dvb, dbias_b, _ = _local_attn_bwd(proj, bias_b, sink_b, dob, cfg_b, "attn_b_bwd")
        db5 = jnp.where(b_valid[None, None, :, None, :], dbias_b.reshape(8, 8, GRID_W, NA_ROWS, GRID_W), 0.0)
        t = jnp.einsum("vhqic,qcz->vhiz", db5, col_oh, precision=HIGHEST)
        small["b_rpb"].append(jnp.einsum("vhiz,vir->hrz", t, row_oh, precision=HIGHEST))
        dd_rows = ddc.reshape(s, 16, HEAD_DIM)[:, :, 0].T.reshape(4, 4, s)
        dqht, dkd, dvd = _flash_bwd(qh, kd, vd, kdt, docb, lse, dd_rows, "attn_c_bwd")
        dqh = dqht.T
        gq, gk = tile_gain(c_q_gain[l]), tile_gain(c_k_gain[l])
        dqc, dkc, dvc, dgq, dgk = _cprep_bwd(proj, gq, gk, cos_t, sin_t, ones, dqh, dkd, dvd, "cprep_bwd")
        small["cq"].append(dgq.reshape(CW // HEAD_DIM, HEAD_DIM).sum(0))
        small["ck"].append(dgk.reshape(CW // HEAD_DIM, HEAD_DIM).sum(0))
        dproj = jnp.concatenate([dqa, dka, dva, dqb, dkb, dvb, dqc, dkc, dvc], axis=1).astype(MXU_DT)
        dh1 = _matmul(dproj, wf_in, mode="nt", name="proj_in_dgrad", tm=1024, tn=1024, tk=1920)
        gw = _matmul(h1, dproj, mode="tn", name="proj_in_wgrad", tm=1024, tn=768, tk=1024, out_dtypes=(GRAD_DT,))
        rec, tok_in = send_grad("w_in", l, gw.reshape(d, 4, in_w // 4).transpose(1, 0, 2))
        started.append(rec)
        dx, dxb, dg = _rms_bwd(xin, norm_mix[l] + tok_in, dh1, dx_mid, "rms_mix_bwd")
        small["norm_mix"].append(dg)
        finish_grads(pending, dx)
        pending = started
    finish_grads(pending, dx)

    for lst in small.values():
        lst.reverse()

    dt5 = jnp.einsum("vhqk,vqkb->bh", dbias_a_total, a_onehot, precision=HIGHEST)
    small_names = ["norm_mix", "a_sink", "t5_table", "b_rpb", "c_q_gain", "c_k_gain", "out_gain_a", "out_gain_b",
                   "out_gain_c", "norm_mlp", "norm_final"]
    small_w = [norm_mix, a_sink, t5_table, b_rpb, c_q_gain, c_k_gain, out_gain_a, out_gain_b, out_gain_c, norm_mlp,
               norm_final]
    small_m = [m_norm_mix, m_a_sink, m_t5_table, m_b_rpb, m_c_q_gain, m_c_k_gain, m_out_gain_a, m_out_gain_b,
               m_out_gain_c, m_norm_mlp, m_norm_final]
    small_v = [v_norm_mix, v_a_sink, v_t5_table, v_b_rpb, v_c_q_gain, v_c_k_gain, v_out_gain_a, v_out_gain_b,
               v_out_gain_c, v_norm_mlp, v_norm_final]
    small_g = [jnp.stack(small["norm_mix"]), jnp.stack(small["a_sink"]), dt5, jnp.stack(small["b_rpb"]),
               jnp.stack(small["cq"]), jnp.stack(small["ck"]), jnp.stack(small["oga"]), jnp.stack(small["ogb"]),
               jnp.stack(small["ogc"]), jnp.stack(small["norm_mlp"]), dg_final]
    shapes = [w.shape for w in small_w]
    total = sum(math.prod(shp) for shp in shapes) + 1
    rows = -(-total // (8 * LANES)) * 8
    one = [jnp.ones((1,), F32)]
    gs, dl, mo, vo = _allreduce_small_adam(_pack(small_g + [loss_part[0, :1]], rows), _pack(small_w + one, rows),
                                           _pack(small_m + one, rows), _pack(small_v + one, rows))
    sg = _unpack(gs, shapes + [(1,)])
    sd, sm, sv = _unpack(dl, shapes), _unpack(mo, shapes), _unpack(vo, shapes)
    loss = sg[-1].reshape(())

    by_name = {nm: (sg[i], sd[i], sm[i], sv[i]) for i, nm in enumerate(small_names)}
    by_name.update(big)
    order = ["norm_mix", "w_in", "a_sink", "t5_table", "b_rpb", "c_q_gain", "c_k_gain", "out_gain_a", "out_gain_b",
             "out_gain_c", "w_o", "norm_mlp", "w_up", "w_down", "norm_final"]
    outs = [loss, dx.reshape(x.shape)]
    for field in range(4):
        outs.extend(by_name[nm][field] for nm in order)
    return tuple(outs)
```

```python
import functools
import math

import jax
import jax.numpy as jnp
from jax import lax
from jax.experimental import pallas as pl
from jax.experimental.pallas import tpu as pltpu

F32 = jnp.float32
MXU_DT = jnp.bfloat16
GRAD_DT = jnp.bfloat16
HIGHEST = lax.Precision.HIGHEST

HEAD_DIM = 64
LANES = 128
EPS = 1e-6
MASK_VALUE = -1e30
GRID_W = 64
NA_ROWS = 8
T5_BUCKETS = 32
T5_MAX_DIST = 128
ROPE_THETA = 10000.0
ADAM_LR, ADAM_B1, ADAM_B2, ADAM_EPS, ADAM_WD, ADAM_STEP = 0.001, 0.9, 0.999, 1e-08, 0.01, 10
VMEM_LIMIT = 56 * 1024 * 1024

MESH_ID = pl.DeviceIdType.MESH
ANY = pl.BlockSpec(memory_space=pl.ANY)

NT_DIMS = (((1,), (1,)), ((), ()))
TN_DIMS = (((0,), (0,)), ((), ()))
NN_DIMS = (((1,), (0,)), ((), ()))


def _dot(a, b, dims=NN_DIMS):
    return lax.dot_general(a, b, dims, preferred_element_type=F32)


def _call(body, *, name, out_shape, grid=(), in_specs=None, out_specs=None, scratch=(), sem=None,
          prefetch=0, aliases=None):
    params = {"vmem_limit_bytes": VMEM_LIMIT}
    if sem is not None:
        params["dimension_semantics"] = sem
    kwargs = {}
    if aliases:
        kwargs["input_output_aliases"] = aliases
    if prefetch:
        spec = pltpu.PrefetchScalarGridSpec(num_scalar_prefetch=prefetch, grid=grid, in_specs=in_specs,
                                            out_specs=out_specs, scratch_shapes=list(scratch))
        return pl.pallas_call(body, grid_spec=spec, out_shape=out_shape, name=name,
                              compiler_params=pltpu.CompilerParams(**params), **kwargs)
    return pl.pallas_call(body, grid=grid, in_specs=in_specs, out_specs=out_specs, out_shape=out_shape,
                          scratch_shapes=list(scratch), name=name,
                          compiler_params=pltpu.CompilerParams(**params), **kwargs)


def _sds(shape, dtype=F32):
    return jax.ShapeDtypeStruct(tuple(shape), dtype)


def _matmul(a, b, *, mode, name, tm, tn, tk, epi="plain", extra=(), out_dtypes=(F32,), mkn=None,
            b_spec=None, out_spec=None, out_shape=None):
    if mkn is None:
        if mode == "nn":
            (m, k), n = a.shape, b.shape[1]
        elif mode == "nt":
            (m, k), n = a.shape, b.shape[0]
        else:
            (k, m), n = a.shape, b.shape[1]
    else:
        m, k, n = mkn
    tm, tn, tk = min(tm, m), min(tn, n), min(tk, k)
    assert m % tm == 0 and n % tn == 0 and k % tk == 0, (name, m, n, k, tm, tn, tk)
    nk = k // tk
    dims = {"nn": NN_DIMS, "nt": NT_DIMS, "tn": TN_DIMS}[mode]
    n_extra, n_out = len(extra), len(out_dtypes)

    def body(a_ref, b_ref, *rest):
        extra_refs = rest[:n_extra]
        out_refs = rest[n_extra:n_extra + n_out]
        acc_ref = rest[n_extra + n_out]
        kk = pl.program_id(2)

        @pl.when(kk == 0)
        def _():
            acc_ref[...] = jnp.zeros_like(acc_ref)

        acc_ref[...] += _dot(a_ref[...].astype(MXU_DT), b_ref[...].astype(MXU_DT), dims)

        @pl.when(kk == nk - 1)
        def _():
            acc = acc_ref[...]
            if epi == "plain":
                out_refs[0][...] = acc.astype(out_refs[0].dtype)
            elif epi == "res":
                out_refs[0][...] = (extra_refs[0][...] + acc).astype(out_refs[0].dtype)
            elif epi == "relu2":
                u = jnp.maximum(acc, 0.0)
                out_refs[0][...] = u.astype(out_refs[0].dtype)
                out_refs[1][...] = (u * u).astype(out_refs[1].dtype)
            elif epi == "mul2u":
                out_refs[0][...] = (2.0 * extra_refs[0][...] * acc).astype(out_refs[0].dtype)
            else:
                raise ValueError(epi)

    if mode == "tn":
        a_spec = pl.BlockSpec((tk, tm), lambda i, j, kk: (kk, i))
    else:
        a_spec = pl.BlockSpec((tm, tk), lambda i, j, kk: (i, kk))
    if b_spec is None:
        if mode == "nt":
            b_spec = pl.BlockSpec((tn, tk), lambda i, j, kk: (j, kk))
        else:
            b_spec = pl.BlockSpec((tk, tn), lambda i, j, kk: (kk, j))
    mn_spec = pl.BlockSpec((tm, tn), lambda i, j, kk: (i, j))
    if out_spec is None:
        out_spec = mn_spec
    if out_shape is None:
        out_shape = (m, n)
    res = _call(body, name=name, grid=(m // tm, n // tn, nk),
                in_specs=[a_spec, b_spec] + [mn_spec] * n_extra,
                out_specs=[out_spec] * n_out,
                out_shape=[_sds(out_shape, d) for d in out_dtypes],
                scratch=[pltpu.VMEM((tm, tn), F32)],
                sem=("parallel", "parallel", "arbitrary"))(a, b, *extra)
    return res if n_out > 1 else res[0]


def _row_tile(s):
    return min(512, s)


def _rms_fwd(x, g, name):
    s, d = x.shape
    tm = _row_tile(s)

    def body(x_ref, g_ref, h_ref):
        xv = x_ref[...]
        r = lax.rsqrt(jnp.mean(xv * xv, axis=-1, keepdims=True) + EPS)
        h_ref[...] = ((xv * r) * g_ref[...]).astype(h_ref.dtype)

    return _call(body, name=name, grid=(s // tm,),
                 in_specs=[pl.BlockSpec((tm, d), lambda i: (i, 0)), pl.BlockSpec((1, d), lambda i: (0, 0))],
                 out_specs=pl.BlockSpec((tm, d), lambda i: (i, 0)),
                 out_shape=_sds((s, d), MXU_DT), sem=("parallel",))(x, g.reshape(1, d))


def _rms_bwd(x, g, dh, dres, name):
    s, d = x.shape
    tm = _row_tile(s)

    def body(x_ref, g_ref, dh_ref, dres_ref, dx_ref, dxb_ref, dg_ref):
        i = pl.program_id(0)
        xv = x_ref[...]
        r = lax.rsqrt(jnp.mean(xv * xv, axis=-1, keepdims=True) + EPS)
        xh = xv * r
        dhv = dh_ref[...]
        gd = dhv * g_ref[...]
        c = jnp.mean(gd * xh, axis=-1, keepdims=True)
        dx = dres_ref[...] + r * (gd - xh * c)
        dx_ref[...] = dx
        dxb_ref[...] = dx.astype(dxb_ref.dtype)
        part = jnp.sum(dhv * xh, axis=0, keepdims=True)

        @pl.when(i == 0)
        def _():
            dg_ref[...] = part

        @pl.when(i > 0)
        def _():
            dg_ref[...] += part

    row = pl.BlockSpec((tm, d), lambda i: (i, 0))
    vec = pl.BlockSpec((1, d), lambda i: (0, 0))
    return _call(body, name=name, grid=(s // tm,), in_specs=[row, vec, row, row],
                 out_specs=[row, row, vec],
                 out_shape=[_sds((s, d)), _sds((s, d), MXU_DT), _sds((1, d))],
                 sem=("arbitrary",))(x, g.reshape(1, d), dh, dres)


def _final_loss(x, g, target, name):
    s, d = x.shape
    tm = _row_tile(s)

    def body(x_ref, g_ref, t_ref, loss_ref, dx_ref, dxb_ref, dg_ref):
        i = pl.program_id(0)
        xv = x_ref[...]
        gv = g_ref[...]
        r = lax.rsqrt(jnp.mean(xv * xv, axis=-1, keepdims=True) + EPS)
        xh = xv * r
        err = xh * gv - t_ref[...]
        part_loss = 0.5 * jnp.sum(jnp.mean(err * err, axis=-1, keepdims=True), axis=0, keepdims=True)
        dy = err * (1.0 / d)
        gd = dy * gv
        c = jnp.mean(gd * xh, axis=-1, keepdims=True)
        dx = r * (gd - xh * c)
        dx_ref[...] = dx
        dxb_ref[...] = dx.astype(dxb_ref.dtype)
        part_g = jnp.sum(dy * xh, axis=0, keepdims=True)
        part_l = jnp.broadcast_to(part_loss, (1, LANES))

        @pl.when(i == 0)
        def _():
            dg_ref[...] = part_g
            loss_ref[...] = part_l

        @pl.when(i > 0)
        def _():
            dg_ref[...] += part_g
            loss_ref[...] += part_l

    row = pl.BlockSpec((tm, d), lambda i: (i, 0))
    vec = pl.BlockSpec((1, d), lambda i: (0, 0))
    return _call(body, name=name, grid=(s // tm,), in_specs=[row, vec, row],
                 out_specs=[pl.BlockSpec((1, LANES), lambda i: (0, 0)), row, row, vec],
                 out_shape=[_sds((1, LANES)), _sds((s, d)), _sds((s, d), MXU_DT), _sds((1, d))],
                 sem=("arbitrary",))(x, g.reshape(1, d), target)


def _lane_iota(shape):
    return lax.broadcasted_iota(jnp.int32, shape, len(shape) - 1)


def _swap_halves(x):
    return pltpu.roll(x, HEAD_DIM, 1)


def _segsum64(x, ones_ref):
    ones = ones_ref[...]
    outs = []
    for c in range(x.shape[1] // LANES):
        xc = x[:, c * LANES:(c + 1) * LANES]
        hi = xc.astype(MXU_DT)
        r1 = xc - hi.astype(F32)
        mid = r1.astype(MXU_DT)
        lo = (r1 - mid.astype(F32)).astype(MXU_DT)
        outs.append(_dot(hi, ones) + _dot(mid, ones) + _dot(lo, ones))
    return outs[0] if len(outs) == 1 else jnp.concatenate(outs, axis=1)


def _pair_ones():
    i = jnp.arange(LANES)
    return (i[:, None] // HEAD_DIM == i[None, :] // HEAD_DIM).astype(MXU_DT)


def _col(x, lane):
    return jnp.sum(jnp.where(_lane_iota(x.shape) == lane, x, 0.0), axis=-1, keepdims=True)


class _LocalCfg:
    def __init__(self, *, groups, qb, kw, qw, sub, qcol, kcol, vcol, kvhalf, kstart, variant, variant_py):
        self.groups, self.qb, self.kw, self.qw = groups, qb, kw, qw
        self.sub = sub
        self.qcol, self.kcol, self.vcol = qcol, kcol, vcol
        self.kvhalf = kvhalf
        self.kstart, self.variant = kstart, variant
        self.variant_py = variant_py
        self.pairs = qw // LANES


def _cfg_a(s):
    nb = s // 128
    return _LocalCfg(groups=1, qb=128, kw=384, qw=512, sub=1, qcol=lambda g: 0, kcol=lambda g: 4,
                     vcol=lambda g: 5, kvhalf=lambda t, e: t // 2,
                     kstart=lambda n: 128 * jnp.clip(n - 1, 0, nb - 3),
                     variant=lambda n: jnp.where(n <= 0, 0, jnp.where(n == nb - 1, 2, 1)),
                     variant_py=lambda n: 0 if n <= 0 else (2 if n == nb - 1 else 1))


def _cfg_b(s):
    rows = s // GRID_W
    return _LocalCfg(groups=4, qb=64, kw=512, qw=128, sub=4, qcol=lambda g: 6 + g, kcol=lambda g: 10 + g,
                     vcol=lambda g: 14 + g, kvhalf=lambda t, e: e,
                     kstart=lambda n: GRID_W * jnp.clip(n - NA_ROWS // 2, 0, rows - NA_ROWS),
                     variant=lambda n: jnp.where(n < 4, jnp.maximum(n, 0),
                                                 jnp.where(n > rows - 4, n - (rows - 8), 4)),
                     variant_py=lambda n: max(n, 0) if n < 4 else (n - (rows - 8) if n > rows - 4 else 4))


def _sum_visited(parts, cfg, s):
    n_var = parts[0].shape[0]
    variants = [cfg.variant_py(n) for n in range(s // cfg.qb)]
    total = None
    for i, part in enumerate(parts):
        seen = jnp.array([v in variants[i::cfg.sub] for v in range(n_var)]).reshape(n_var, 1, 1, 1)
        term = jnp.where(seen, part, 0.0)
        total = term if total is None else total + term
    return total


def _local_head(cfg, t, e, qp, qp_sw, kb, bias, sink_row, left_q):
    kvh = cfg.kvhalf(t, e)
    qsrc = qp if e == kvh else qp_sw
    keep = left_q if kvh == 0 else jnp.logical_not(left_q)
    qm = jnp.where(keep, qsrc, 0.0).astype(MXU_DT)
    sc = _dot(qm, kb, NT_DIMS) + bias
    snk = _col(sink_row, 2 * t + e)
    m = jnp.maximum(jnp.max(sc, axis=-1, keepdims=True), snk)
    p = jnp.exp(sc - m)
    l = jnp.sum(p, axis=-1, keepdims=True) + jnp.exp(snk - m)
    p = p / l
    return qm, keep, p, m, l, snk


def _local_attn_fwd(proj, bias, sink, cfg, name):
    s = proj.shape[0]
    qb, kw, qw, g_n = cfg.qb, cfg.kw, cfg.qw, cfg.groups
    hq = 2 * cfg.pairs

    sub = cfg.sub

    def body(q_ref, k_ref, v_ref, *rest):
        b_refs, s_ref, o_ref = rest[:sub], rest[sub], rest[sub + 1]
        n = pl.program_id(1)
        left_q = _lane_iota((qb, LANES)) < HEAD_DIM
        left_k = _lane_iota((kw, LANES)) < HEAD_DIM
        sink_row = s_ref[...]
        for i in range(sub):
            ks = pl.multiple_of(cfg.kstart(sub * n + i), 64)
            kf = k_ref[pl.ds(ks, kw), :]
            vf = v_ref[pl.ds(ks, kw), :]
            kb = kf.astype(MXU_DT)
            vf_sw = _swap_halves(vf)
            rows = slice(i * qb, (i + 1) * qb)
            for t in range(cfg.pairs):
                qp = q_ref[rows, t * LANES:(t + 1) * LANES] * 0.125
                qp_sw = _swap_halves(qp)
                acc = jnp.zeros((qb, LANES), F32)
                for e in range(2):
                    _, _, p, _, _, _ = _local_head(cfg, t, e, qp, qp_sw, kb, b_refs[i][0, 2 * t + e], sink_row,
                                                   left_q)
                    vsrc = vf if e == cfg.kvhalf(t, e) else vf_sw
                    vsel = jnp.where(left_k if e == 0 else jnp.logical_not(left_k), vsrc, 0.0).astype(MXU_DT)
                    acc = acc + _dot(p.astype(MXU_DT), vsel)
                o_ref[rows, t * LANES:(t + 1) * LANES] = acc

    def bias_spec(i):
        return pl.BlockSpec((1, hq, qb, kw), lambda g, n: (cfg.variant(sub * n + i), g, 0, 0))

    return _call(
        body, name=name, grid=(g_n, s // (sub * qb)),
        in_specs=[pl.BlockSpec((sub * qb, qw), lambda g, n: (n, cfg.qcol(g))),
                  pl.BlockSpec((s, LANES), lambda g, n: (0, cfg.kcol(g))),
                  pl.BlockSpec((s, LANES), lambda g, n: (0, cfg.vcol(g)))]
        + [bias_spec(i) for i in range(sub)]
        + [pl.BlockSpec((None, 1, LANES), lambda g, n: (g, 0, 0))],
        out_specs=pl.BlockSpec((sub * qb, qw), lambda g, n: (n, g)),
        out_shape=_sds((s, g_n * qw)), sem=("parallel", "arbitrary"))(proj, proj, proj, *([bias] * sub), sink)


def _local_attn_bwd(proj, bias, sink, do, cfg, name):
    s = proj.shape[0]
    qb, kw, qw, g_n = cfg.qb, cfg.kw, cfg.qw, cfg.groups
    hq = 2 * cfg.pairs

    sub = cfg.sub

    def body(q_ref, k_ref, v_ref, *rest):
        b_refs, s_ref, do_ref = rest[:sub], rest[sub], rest[sub + 1]
        dq_ref, dk_ref, dv_ref = rest[sub + 2:sub + 5]
        db_refs, dsk_ref = rest[sub + 5:2 * sub + 5], rest[2 * sub + 5]
        n = pl.program_id(1)

        @pl.when(n == 0)
        def _():
            dk_ref[...] = jnp.zeros_like(dk_ref)
            dv_ref[...] = jnp.zeros_like(dv_ref)
            dsk_ref[...] = jnp.zeros_like(dsk_ref)

        left_q = _lane_iota((qb, LANES)) < HEAD_DIM
        left_k = _lane_iota((kw, LANES)) < HEAD_DIM
        sink_row = s_ref[...]
        row0 = lax.broadcasted_iota(jnp.int32, (8, LANES), 0) == 0
        lane8 = _lane_iota((8, LANES))
        dsk_acc = jnp.zeros((8, LANES), F32)
        for i in range(sub):
            blk = sub * n + i
            ks = pl.multiple_of(cfg.kstart(blk), 64)
            db_ref = db_refs[i]

            @pl.when(jnp.logical_or(n == 0, cfg.variant(blk) != cfg.variant(blk - sub)))
            def _():
                db_ref[...] = jnp.zeros_like(db_ref)

            kf = k_ref[pl.ds(ks, kw), :]
            vf = v_ref[pl.ds(ks, kw), :]
            kb = kf.astype(MXU_DT)
            vb = vf.astype(MXU_DT)
            kf_sw = _swap_halves(kf)
            rows = slice(i * qb, (i + 1) * qb)
            dk_acc = jnp.zeros((kw, LANES), F32)
            dv_acc = jnp.zeros((kw, LANES), F32)
            for t in range(cfg.pairs):
                qp = q_ref[rows, t * LANES:(t + 1) * LANES] * 0.125
                qp_sw = _swap_halves(qp)
                dop = do_ref[rows, t * LANES:(t + 1) * LANES]
                dop_sw = _swap_halves(dop)
                dq_t = jnp.zeros((qb, LANES), F32)
                for e in range(2):
                    h = 2 * t + e
                    qm, keep, p, m, l, snk = _local_head(cfg, t, e, qp, qp_sw, kb, b_refs[i][0, h], sink_row,
                                                         left_q)
                    kvh = cfg.kvhalf(t, e)
                    dom = jnp.where(keep, dop if e == kvh else dop_sw, 0.0).astype(MXU_DT)
                    dp = _dot(dom, vb, NT_DIMS)
                    dd = jnp.sum(p * dp, axis=-1, keepdims=True)
                    ds = p * (dp - dd)
                    p_sink = jnp.exp(snk - m) / l
                    dsink = jnp.sum(-p_sink * dd, axis=0, keepdims=True)
                    dsk_acc = dsk_acc + jnp.where(jnp.logical_and(row0, lane8 == h), dsink, 0.0)
                    dsb = ds.astype(MXU_DT)
                    dv_acc = dv_acc + _dot(p.astype(MXU_DT), dom, TN_DIMS)
                    dk_acc = dk_acc + _dot(dsb, qm, TN_DIMS)
                    ksrc = kf if e == kvh else kf_sw
                    ksel = jnp.where(left_k if e == 0 else jnp.logical_not(left_k), ksrc, 0.0).astype(MXU_DT)
                    dq_t = dq_t + _dot(dsb, ksel)
                    db_ref[0, h] += ds
                dq_ref[rows, t * LANES:(t + 1) * LANES] = dq_t * 0.125
            dk_ref[pl.ds(ks, kw), :] += dk_acc
            dv_ref[pl.ds(ks, kw), :] += dv_acc
        dsk_ref[...] += dsk_acc

    def bias_spec(i):
        return pl.BlockSpec((1, hq, qb, kw), lambda g, n: (cfg.variant(sub * n + i), g, 0, 0))

    n_var = bias.shape[0]
    res = _call(
        body, name=name, grid=(g_n, s // (sub * qb)),
        in_specs=[pl.BlockSpec((sub * qb, qw), lambda g, n: (n, cfg.qcol(g))),
                  pl.BlockSpec((s, LANES), lambda g, n: (0, cfg.kcol(g))),
                  pl.BlockSpec((s, LANES), lambda g, n: (0, cfg.vcol(g)))]
        + [bias_spec(i) for i in range(sub)]
        + [pl.BlockSpec((None, 1, LANES), lambda g, n: (g, 0, 0)),
           pl.BlockSpec((sub * qb, qw), lambda g, n: (n, g))],
        out_specs=[pl.BlockSpec((sub * qb, qw), lambda g, n: (n, g)),
                   pl.BlockSpec((s, LANES), lambda g, n: (0, g)),
                   pl.BlockSpec((s, LANES), lambda g, n: (0, g))]
        + [bias_spec(i) for i in range(sub)]
        + [pl.BlockSpec((None, 8, LANES), lambda g, n: (g, 0, 0))],
        out_shape=[_sds((s, g_n * qw)), _sds((s, g_n * LANES)), _sds((s, g_n * LANES))]
        + [_sds((n_var, g_n * hq, qb, kw))] * sub + [_sds((g_n, 8, LANES))],
        sem=("parallel", "arbitrary"))(proj, proj, proj, *([bias] * sub), sink, do)
    dq, dk, dv = res[:3]
    dbias = _sum_visited(res[3:3 + sub], cfg, s)
    return dq, dk, dv, dbias, res[3 + sub]


QC_COL, KC_COL, VC_COL = 9, 13, 14
CW = 256


def _swap16(x):
    w = x.shape[1]
    lane = _lane_iota(x.shape)
    return jnp.where(lane % 32 < 16, pltpu.roll(x, w - 16, 1), pltpu.roll(x, 16, 1))


def _dup_halves(x):
    left = _lane_iota(x.shape) < HEAD_DIM
    sw = _swap_halves(x)
    return jnp.where(left, x, sw), jnp.where(left, sw, x)


def _normrope(x, gain, cos, sin, ones_ref):
    ms = _segsum64(x * x, ones_ref) * (1.0 / HEAD_DIM)
    r = lax.rsqrt(ms + EPS)
    y = (x * r) * gain
    return y * cos + _swap16(y) * sin, r


def _cprep_fwd(proj, gq, gk, cos, sin, ones, name):
    s = proj.shape[0]
    tm = _row_tile(s)

    def body(q0, q1, q2, q3, k_ref, v_ref, gq_ref, gk_ref, cos_ref, sin_ref, ones_ref, qh_ref, kd_ref, vd_ref):
        cos_v, sin_v = cos_ref[...], sin_ref[...]
        for c, q_ref in enumerate((q0, q1, q2, q3)):
            y, _ = _normrope(q_ref[...], gq_ref[...], cos_v, sin_v, ones_ref)
            qh_ref[:, c * CW:(c + 1) * CW] = (y * 0.125).astype(qh_ref.dtype)
        yk, _ = _normrope(k_ref[...], gk_ref[...], cos_v, sin_v, ones_ref)
        vv = v_ref[...]
        for p in range(2):
            ka, kb_ = _dup_halves(yk[:, p * LANES:(p + 1) * LANES])
            va, vb_ = _dup_halves(vv[:, p * LANES:(p + 1) * LANES])
            kd_ref[:, (2 * p) * LANES:(2 * p + 1) * LANES] = ka.astype(kd_ref.dtype)
            kd_ref[:, (2 * p + 1) * LANES:(2 * p + 2) * LANES] = kb_.astype(kd_ref.dtype)
            vd_ref[:, (2 * p) * LANES:(2 * p + 1) * LANES] = va.astype(vd_ref.dtype)
            vd_ref[:, (2 * p + 1) * LANES:(2 * p + 2) * LANES] = vb_.astype(vd_ref.dtype)

    def chunk(col):
        return pl.BlockSpec((tm, CW), lambda i: (i, col))

    vec = pl.BlockSpec((1, CW), lambda i: (0, 0))
    tab = pl.BlockSpec((tm, CW), lambda i: (i, 0))
    return _call(body, name=name, grid=(s // tm,),
                 in_specs=[chunk(QC_COL), chunk(QC_COL + 1), chunk(QC_COL + 2), chunk(QC_COL + 3),
                           chunk(KC_COL), chunk(VC_COL), vec, vec, tab, tab,
                           pl.BlockSpec((LANES, LANES), lambda i: (0, 0))],
                 out_specs=[pl.BlockSpec((tm, 4 * CW), lambda i: (i, 0)),
                            pl.BlockSpec((tm, 2 * CW), lambda i: (i, 0)),
                            pl.BlockSpec((tm, 2 * CW), lambda i: (i, 0))],
                 out_shape=[_sds((s, 4 * CW), MXU_DT), _sds((s, 2 * CW), MXU_DT), _sds((s, 2 * CW), MXU_DT)],
                 sem=("parallel",))(proj, proj, proj, proj, proj, proj, gq, gk, cos, sin, ones)


def _cprep_bwd(proj, gq, gk, cos, sin, ones, dqh, dkd, dvd, name):
    s = proj.shape[0]
    tm = _row_tile(s)

    def fold(ref, p):
        a = ref[:, (2 * p) * LANES:(2 * p + 1) * LANES]
        b = ref[:, (2 * p + 1) * LANES:(2 * p + 2) * LANES]
        ta = a + _swap_halves(a)
        tb = b + _swap_halves(b)
        return jnp.where(_lane_iota(a.shape) < HEAD_DIM, ta, tb)

    def norm_bwd(x, gain, dyr, cos_v, sin_v, ones_ref):
        dy = dyr * cos_v + _swap16(dyr * sin_v)
        ms = _segsum64(x * x, ones_ref) * (1.0 / HEAD_DIM)
        r = lax.rsqrt(ms + EPS)
        xh = x * r
        gd = dy * gain
        c = _segsum64(gd * xh, ones_ref) * (1.0 / HEAD_DIM)
        return r * (gd - xh * c), jnp.sum(dy * xh, axis=0, keepdims=True)

    def body(q0, q1, q2, q3, k_ref, gq_ref, gk_ref, cos_ref, sin_ref, ones_ref, dqh_ref, dkd_ref, dvd_ref,
             dq_ref, dk_ref, dv_ref, dgq_ref, dgk_ref):
        i = pl.program_id(0)
        cos_v, sin_v = cos_ref[...], sin_ref[...]
        gq_part = jnp.zeros((1, CW), F32)
        for c, q_ref in enumerate((q0, q1, q2, q3)):
            dx, dg = norm_bwd(q_ref[...], gq_ref[...], dqh_ref[:, c * CW:(c + 1) * CW] * 0.125, cos_v, sin_v,
                              ones_ref)
            dq_ref[:, c * CW:(c + 1) * CW] = dx
            gq_part = gq_part + dg
        dkr = jnp.concatenate([fold(dkd_ref, 0), fold(dkd_ref, 1)], axis=1)
        dxk, gk_part = norm_bwd(k_ref[...], gk_ref[...], dkr, cos_v, sin_v, ones_ref)
        dk_ref[...] = dxk
        dv_ref[...] = jnp.concatenate([fold(dvd_ref, 0), fold(dvd_ref, 1)], axis=1)

        @pl.when(i == 0)
        def _():
            dgq_ref[...] = gq_part
            dgk_ref[...] = gk_part

        @pl.when(i > 0)
        def _():
            dgq_ref[...] += gq_part
            dgk_ref[...] += gk_part

    def chunk(col):
        return pl.BlockSpec((tm, CW), lambda i: (i, col))

    vec = pl.BlockSpec((1, CW), lambda i: (0, 0))
    tab = pl.BlockSpec((tm, CW), lambda i: (i, 0))
    return _call(body, name=name, grid=(s // tm,),
                 in_specs=[chunk(QC_COL), chunk(QC_COL + 1), chunk(QC_COL + 2), chunk(QC_COL + 3), chunk(KC_COL),
                           vec, vec, tab, tab, pl.BlockSpec((LANES, LANES), lambda i: (0, 0)),
                           pl.BlockSpec((tm, 4 * CW), lambda i: (i, 0)),
                           pl.BlockSpec((tm, 2 * CW), lambda i: (i, 0)),
                           pl.BlockSpec((tm, 2 * CW), lambda i: (i, 0))],
                 out_specs=[pl.BlockSpec((tm, 4 * CW), lambda i: (i, 0)), tab, tab, vec, vec],
                 out_shape=[_sds((s, 4 * CW)), _sds((s, CW)), _sds((s, CW)), _sds((1, CW)), _sds((1, CW))],
                 sem=("arbitrary",))(proj, proj, proj, proj, proj, gq, gk, cos, sin, ones, dqh, dkd, dvd)


def _flash_tiles(s):
    return min(512, s), min(1024, s)


def _row_iota(shape):
    return lax.broadcasted_iota(jnp.int32, shape, 0)


def _flash_fwd(qh, kd, vdt, name):
    s = qh.shape[0]
    tq, tk = _flash_tiles(s)
    nk = s // tk

    n_chunks = 1
    cw = tq // n_chunks
    units = [(t, c, e) for t in range(2) for c in range(n_chunks) for e in range(2)]

    def body(q_ref, k_ref, vt_ref, ot_ref, lse_ref, qm_ref, m_ref, lacc_ref, acc_ref):
        j = pl.program_id(2)

        @pl.when(j == 0)
        def _():
            m_ref[...] = jnp.full(m_ref.shape, MASK_VALUE, F32)
            lacc_ref[...] = jnp.zeros_like(lacc_ref)
            acc_ref[...] = jnp.zeros_like(acc_ref)
            left_q = _lane_iota((tq, LANES)) < HEAD_DIM
            for t in range(2):
                qp = q_ref[:, t * LANES:(t + 1) * LANES]
                qm_ref[2 * t] = jnp.where(left_q, qp, jnp.zeros_like(qp))
                qm_ref[2 * t + 1] = jnp.where(left_q, jnp.zeros_like(qp), qp)

        kb = k_ref[...]
        vt = vt_ref[...]
        top_k = _row_iota((LANES, tk)) < HEAD_DIM
        top_c = _row_iota((LANES, cw)) < HEAD_DIM
        vt_e = (jnp.where(top_k, vt, jnp.ones_like(vt)), jnp.where(top_k, jnp.ones_like(vt), vt))

        def scores(unit):
            t, c, e = unit
            return _dot(kb, qm_ref[2 * t + e, c * cw:(c + 1) * cw, :], NT_DIMS)

        nxt = scores(units[0])
        pv, alpha = [], []
        for n, (t, c, e) in enumerate(units):
            st = nxt
            if n + 1 < len(units):
                nxt = scores(units[n + 1])
            h = 2 * t + e
            cols = slice(c * cw, (c + 1) * cw)
            m_prev = m_ref[h, :, cols]
            m_new = jnp.maximum(m_prev, jnp.max(st, axis=0, keepdims=True))
            alpha.append(jnp.exp(m_prev - m_new))
            pt = jnp.exp(st - m_new)
            m_ref[h, :, cols] = m_new
            pv.append(_dot(vt_e[e], pt.astype(MXU_DT)))
            if e == 1:
                acc_ref[t, :, cols] = (acc_ref[t, :, cols] * jnp.where(top_c, alpha[0], alpha[1])
                                       + jnp.where(top_c, pv[0], pv[1]))
                lacc_ref[t, :, cols] = (lacc_ref[t, :, cols] * jnp.where(top_c, alpha[1], alpha[0])
                                        + jnp.where(top_c, pv[1], pv[0]))
                pv, alpha = [], []

        @pl.when(j == nk - 1)
        def _():
            for t in range(2):
                lacc = lacc_ref[t]
                l_sw = jnp.concatenate([lacc[HEAD_DIM:], lacc[:HEAD_DIM]], axis=0)
                ot_ref[t * LANES:(t + 1) * LANES, :] = acc_ref[t] / l_sw
                lse_ref[2 * t:2 * t + 1, :] = m_ref[2 * t] + jnp.log(lacc[HEAD_DIM:HEAD_DIM + 1])
                lse_ref[2 * t + 1:2 * t + 2, :] = m_ref[2 * t + 1] + jnp.log(lacc[0:1])

    return _call(body, name=name, grid=(4, s // tq, nk),
                 in_specs=[pl.BlockSpec((tq, CW), lambda g, i, j: (i, g)),
                           pl.BlockSpec((tk, LANES), lambda g, i, j: (j, g)),
                           pl.BlockSpec((LANES, tk), lambda g, i, j: (g, j))],
                 out_specs=[pl.BlockSpec((CW, tq), lambda g, i, j: (g, i)),
                            pl.BlockSpec((None, 4, tq), lambda g, i, j: (g, 0, i))],
                 out_shape=[_sds((4 * CW, s)), _sds((4, 4, s))],
                 scratch=[pltpu.VMEM((4, tq, LANES), MXU_DT), pltpu.VMEM((4, 1, tq), F32),
                          pltpu.VMEM((2, LANES, tq), F32), pltpu.VMEM((2, LANES, tq), F32)],
                 sem=("parallel", "parallel", "arbitrary"))(qh, kd, vdt)


def _flash_bwd(qh, kd, vd, kdt, do, lse, dd, name):
    s = qh.shape[0]
    tq, tk = _flash_tiles(s)
    ni = s // tq

    def body(q_ref, k_ref, v_ref, kt_ref, do_ref, lse_ref, dd_ref, dqt_ref, dk_ref, dv_ref, dk_acc, dv_acc):
        j = pl.program_id(1)
        i = pl.program_id(2)

        @pl.when(i == 0)
        def _():
            dk_acc[...] = jnp.zeros_like(dk_acc)
            dv_acc[...] = jnp.zeros_like(dv_acc)

        kb = k_ref[...]
        vb = v_ref[...]
        kt = kt_ref[...]
        left_q = _lane_iota((tq, LANES)) < HEAD_DIM
        top = _row_iota((LANES, tq)) < HEAD_DIM
        cols = pl.ds(pl.multiple_of(i * tq, tq), tq)

        def first_stage(h):
            t, e = divmod(h, 2)
            keep_q = left_q if e == 0 else jnp.logical_not(left_q)
            qp = q_ref[:, t * LANES:(t + 1) * LANES]
            dop = do_ref[:, t * LANES:(t + 1) * LANES]
            qm = jnp.where(keep_q, qp, jnp.zeros_like(qp))
            dom = jnp.where(keep_q, dop, jnp.zeros_like(dop))
            return qm, dom, _dot(kb, qm, NT_DIMS), _dot(vb, dom, NT_DIMS)

        nxt = first_stage(0)
        dqt = []
        for h in range(4):
            qm, dom, st, dpt = nxt
            if h < 3:
                nxt = first_stage(h + 1)
            pt = jnp.exp(st - lse_ref[h:h + 1, :])
            dsb = (pt * (dpt - dd_ref[h:h + 1, :])).astype(MXU_DT)
            dv_acc[...] += _dot(pt.astype(MXU_DT), dom)
            dk_acc[...] += _dot(dsb, qm)
            dqt.append(_dot(kt, dsb))
            if h % 2 == 1:
                t = h // 2
                dq_t = jnp.where(top, dqt[0], dqt[1])
                dqt = []

                @pl.when(j == 0)
                def _():
                    dqt_ref[t * LANES:(t + 1) * LANES, cols] = dq_t

                @pl.when(j > 0)
                def _():
                    dqt_ref[t * LANES:(t + 1) * LANES, cols] += dq_t

        @pl.when(i == ni - 1)
        def _():
            dk_ref[...] = dk_acc[...]
            dv_ref[...] = dv_acc[...]

    qspec = pl.BlockSpec((tq, CW), lambda g, j, i: (i, g))
    kspec = pl.BlockSpec((tk, LANES), lambda g, j, i: (j, g))
    rowspec = pl.BlockSpec((None, 4, tq), lambda g, j, i: (g, 0, i))
    return _call(body, name=name, grid=(4, s // tk, ni),
                 in_specs=[qspec, kspec, kspec, pl.BlockSpec((LANES, tk), lambda g, j, i: (g, j)), qspec,
                           rowspec, rowspec],
                 out_specs=[pl.BlockSpec((CW, s), lambda g, j, i: (g, 0)), kspec, kspec],
                 out_shape=[_sds((4 * CW, s)), _sds((s, 2 * CW)), _sds((s, 2 * CW))],
                 scratch=[pltpu.VMEM((tk, LANES), F32), pltpu.VMEM((tk, LANES), F32)],
                 sem=("parallel", "arbitrary", "arbitrary"))(qh, kd, vd, kdt, do, lse, dd)


def _groupnorm_fwd(oa, ob, oc, ga, gb, gc, name):
    s = oa.shape[0]
    tm = _row_tile(s)
    wa, wb, wc = oa.shape[1], ob.shape[1], oc.shape[1]

    def body(oa_ref, ob_ref, oc_ref, ga_ref, gb_ref, gc_ref, mix_ref):
        off = 0
        for o_ref, g_ref, w in ((oa_ref, ga_ref, wa), (ob_ref, gb_ref, wb), (oc_ref, gc_ref, wc)):
            xv = o_ref[...]
            r = lax.rsqrt(jnp.mean(xv * xv, axis=-1, keepdims=True) + EPS)
            mix_ref[:, off:off + w] = ((xv * r) * g_ref[...]).astype(mix_ref.dtype)
            off += w

    def row(w):
        return pl.BlockSpec((tm, w), lambda i: (i, 0))

    def vec(w):
        return pl.BlockSpec((1, w), lambda i: (0, 0))

    return _call(body, name=name, grid=(s // tm,),
                 in_specs=[row(wa), row(wb), row(wc), vec(wa), vec(wb), vec(wc)],
                 out_specs=row(wa + wb + wc), out_shape=_sds((s, wa + wb + wc), MXU_DT),
                 sem=("parallel",))(oa, ob, oc, ga.reshape(1, wa), gb.reshape(1, wb), gc.reshape(1, wc))


def _groupnorm_bwd(dmix, oa, ob, oc, ga, gb, gc, ones, name):
    s = oa.shape[0]
    tm = _row_tile(s)
    wa, wb, wc = oa.shape[1], ob.shape[1], oc.shape[1]

    def body(dm_ref, oa_ref, ob_ref, oc_ref, ga_ref, gb_ref, gc_ref, ones_ref,
             doa_ref, dob_ref, doc_ref, docb_ref, dd_ref, dga_ref, dgb_ref, dgc_ref):
        i = pl.program_id(0)
        off = 0
        parts = []
        for o_ref, g_ref, do_ref, w in ((oa_ref, ga_ref, doa_ref, wa), (ob_ref, gb_ref, dob_ref, wb),
                                        (oc_ref, gc_ref, doc_ref, wc)):
            xv = o_ref[...]
            dh = dm_ref[:, off:off + w]
            r = lax.rsqrt(jnp.mean(xv * xv, axis=-1, keepdims=True) + EPS)
            xh = xv * r
            gd = dh * g_ref[...]
            c = jnp.mean(gd * xh, axis=-1, keepdims=True)
            dx = r * (gd - xh * c)
            do_ref[...] = dx
            parts.append(jnp.sum(dh * xh, axis=0, keepdims=True))
            if o_ref is oc_ref:
                docb_ref[...] = dx.astype(docb_ref.dtype)
                dd_ref[...] = _segsum64(dx * xv, ones_ref)
            off += w

        @pl.when(i == 0)
        def _():
            dga_ref[...], dgb_ref[...], dgc_ref[...] = parts

        @pl.when(i > 0)
        def _():
            dga_ref[...] += parts[0]
            dgb_ref[...] += parts[1]
            dgc_ref[...] += parts[2]

    def row(w):
        return pl.BlockSpec((tm, w), lambda i: (i, 0))

    def vec(w):
        return pl.BlockSpec((1, w), lambda i: (0, 0))

    return _call(body, name=name, grid=(s // tm,),
                 in_specs=[row(wa + wb + wc), row(wa), row(wb), row(wc), vec(wa), vec(wb), vec(wc),
                           pl.BlockSpec((LANES, LANES), lambda i: (0, 0))],
                 out_specs=[row(wa), row(wb), row(wc), row(wc), row(wc), vec(wa), vec(wb), vec(wc)],
                 out_shape=[_sds((s, wa)), _sds((s, wb)), _sds((s, wc)), _sds((s, wc), MXU_DT), _sds((s, wc)),
                            _sds((1, wa)), _sds((1, wb)), _sds((1, wc))],
                 sem=("arbitrary",))(dmix, oa, ob, oc, ga.reshape(1, wa), gb.reshape(1, wb), gc.reshape(1, wc), ones)


def _adam_math(w, g, m, v):
    m = ADAM_B1 * m + (1.0 - ADAM_B1) * g
    v = ADAM_B2 * v + (1.0 - ADAM_B2) * jnp.square(g)
    m_hat = m / (1.0 - ADAM_B1 ** ADAM_STEP)
    v_hat = v / (1.0 - ADAM_B2 ** ADAM_STEP)
    delta = -ADAM_LR * (m_hat / (jnp.sqrt(v_hat) + ADAM_EPS) + ADAM_WD * w)
    return delta, m, v


def _mesh_pos():
    return lax.axis_index("x"), lax.axis_index("y"), lax.axis_index("c")


def _peer_chips(x, y):
    return [(1 - x, y), (x, 1 - y), (1 - x, 1 - y)]


def _allreduce_small_adam(g, w, m, v):
    rows = g.shape[0]

    def body(g_ref, w_ref, m_ref, v_ref, gs_ref, d_ref, mo_ref, vo_ref, buf, send_sems, recv_sems):
        x, y, c = _mesh_pos()
        me = 4 * x + 2 * y + c
        buf[me] = g_ref[...]
        copies = []
        for k in range(1, 8):
            px = 1 - x if (k >> 2) & 1 else x
            py = 1 - y if (k >> 1) & 1 else y
            pc = 1 - c if k & 1 else c
            cp = pltpu.make_async_remote_copy(src_ref=g_ref, dst_ref=buf.at[me], send_sem=send_sems.at[k - 1],
                                              recv_sem=recv_sems.at[k - 1], device_id=(px, py, pc),
                                              device_id_type=MESH_ID)
            cp.start()
            copies.append(cp)
        for cp in copies:
            cp.wait()
        total = buf[0]
        for d in range(1, 8):
            total = total + buf[d]
        gs_ref[...] = total
        d_ref[...], mo_ref[...], vo_ref[...] = _adam_math(w_ref[...], total, m_ref[...], v_ref[...])

    vm = pl.BlockSpec(memory_space=pltpu.VMEM)
    return _call(body, name="allreduce_small_adam", in_specs=[vm] * 4, out_specs=[vm] * 4,
                 out_shape=[_sds((rows, LANES))] * 4,
                 scratch=[pltpu.VMEM((8, rows, LANES), F32), pltpu.SemaphoreType.DMA((7,)),
                          pltpu.SemaphoreType.DMA((7,))])(g, w, m, v)


HBM_SPEC = pl.BlockSpec(memory_space=pltpu.HBM)
SEM_SPEC = pl.BlockSpec(memory_space=pltpu.SEMAPHORE)
VMEM_SPEC = pl.BlockSpec(memory_space=pltpu.VMEM)
SIDE_EFFECT = pltpu.SideEffectType.DATAFLOW_SIDE_EFFECTING
N_PEERS = 7


def _in_hbm(a):
    return pltpu.with_memory_space_constraint(a, pltpu.HBM)


def _landing(shape, dtype):
    return _in_hbm(lax.empty(shape, dtype))


def _token_shape():
    return _sds((8, LANES))


def _gather_start(shards):
    n = len(shards)
    n_layers = shards[0].shape[0]
    jobs = [(l, t) for l in range(n_layers) for t in range(n)]
    nj = len(jobs)

    def body(*refs):
        sh = refs[:n]
        outs = refs[n + nj:]
        send, recv, land, token = outs[:nj], outs[nj:2 * nj], outs[2 * nj:3 * nj], outs[3 * nj]
        x, y, c = _mesh_pos()
        me = 2 * x + y
        for j, (l, t) in enumerate(jobs):
            for k, (px, py) in enumerate(_peer_chips(x, y)):
                pltpu.make_async_remote_copy(src_ref=sh[t].at[l], dst_ref=land[j].at[me], send_sem=send[j].at[k],
                                             recv_sem=recv[j].at[k], device_id=(px, py, c),
                                             device_id_type=MESH_ID).start()
        token[...] = jnp.zeros_like(token)

    lands = [_landing((4,) + shards[t].shape[1:], shards[t].dtype) for _, t in jobs]
    res = pl.pallas_call(
        body, name="gather_start",
        out_shape=tuple([pltpu.SemaphoreType.DMA((3,))] * (2 * nj)
                        + [pltpu.HBM(a.shape, a.dtype) for a in lands] + [_token_shape()]),
        in_specs=[HBM_SPEC] * (n + nj), out_specs=tuple([SEM_SPEC] * (2 * nj) + [HBM_SPEC] * nj + [VMEM_SPEC]),
        input_output_aliases={n + j: 2 * nj + j for j in range(nj)},
        compiler_params=pltpu.CompilerParams(has_side_effects=SIDE_EFFECT),
    )(*[_in_hbm(a) for a in shards], *lands)
    return jobs, res[:nj], res[nj:2 * nj], res[2 * nj:3 * nj], res[3 * nj]


def _gather_wait(shard, layer, land, send_sem, recv_sem, after, name):
    def body(sh_ref, land_ref, send_ref, recv_ref, after_ref, land_out):
        x, y, c = _mesh_pos()
        for k in range(3):
            cp = pltpu.make_async_remote_copy(src_ref=sh_ref.at[layer], dst_ref=land_ref.at[k],
                                              send_sem=send_ref.at[k], recv_sem=recv_ref.at[k],
                                              device_id=(x, y, 1 - c), device_id_type=MESH_ID)
            cp.wait_send()
            cp.wait_recv()

    return pl.pallas_call(
        body, name=name, out_shape=pltpu.HBM(land.shape, land.dtype),
        in_specs=[HBM_SPEC, HBM_SPEC, SEM_SPEC, SEM_SPEC, ANY], out_specs=HBM_SPEC,
        input_output_aliases={1: 0},
        compiler_params=pltpu.CompilerParams(has_side_effects=SIDE_EFFECT),
    )(shard, land, send_sem, recv_sem, after)


def _grad_start(g, name):
    def body(g_ref, land_in, send, recv, land, token):
        x, y, c = _mesh_pos()
        me = 2 * x + y
        pltpu.make_async_remote_copy(src_ref=g_ref.at[me], dst_ref=land.at[0], send_sem=send.at[0],
                                     recv_sem=recv.at[0], device_id=(x, y, 1 - c), device_id_type=MESH_ID).start()
        for k, (px, py) in enumerate(_peer_chips(x, y)):
            for c2 in range(2):
                pltpu.make_async_remote_copy(src_ref=g_ref.at[2 * px + py], dst_ref=land.at[1 + 2 * k + c],
                                             send_sem=send.at[1 + 2 * k + c2], recv_sem=recv.at[1 + 2 * k + c],
                                             device_id=(px, py, c2), device_id_type=MESH_ID).start()
        token[...] = jnp.zeros_like(token)

    land = _landing((N_PEERS,) + g.shape[1:], g.dtype)
    return pl.pallas_call(
        body, name=name,
        out_shape=(pltpu.SemaphoreType.DMA((N_PEERS,)), pltpu.SemaphoreType.DMA((N_PEERS,)),
                   pltpu.HBM(land.shape, land.dtype), _token_shape()),
        in_specs=[HBM_SPEC, HBM_SPEC], out_specs=(SEM_SPEC, SEM_SPEC, HBM_SPEC, VMEM_SPEC),
        input_output_aliases={1: 2},
        compiler_params=pltpu.CompilerParams(has_side_effects=SIDE_EFFECT),
    )(_in_hbm(g), land)


def _grad_wait(g, land, send_sem, recv_sem, after, name):
    def body(g_ref, land_ref, send_ref, recv_ref, after_ref, land_out):
        x, y, c = _mesh_pos()
        for k in range(N_PEERS):
            cp = pltpu.make_async_remote_copy(src_ref=g_ref.at[0], dst_ref=land_ref.at[k], send_sem=send_ref.at[k],
                                              recv_sem=recv_ref.at[k], device_id=(x, y, 1 - c),
                                              device_id_type=MESH_ID)
            cp.wait_send()
            cp.wait_recv()

    return pl.pallas_call(
        body, name=name, out_shape=pltpu.HBM(land.shape, land.dtype),
        in_specs=[HBM_SPEC, HBM_SPEC, SEM_SPEC, SEM_SPEC, ANY], out_specs=HBM_SPEC,
        input_output_aliases={1: 0},
        compiler_params=pltpu.CompilerParams(has_side_effects=SIDE_EFFECT),
    )(g, land, send_sem, recv_sem, after)


def _sum_adam(g, land, w, m, v, prev, layer, me_idx, name):
    _, r, cols = g.shape
    tr = min(128, r)

    def body(me_ref, g_ref, l0, l1, l2, l3, l4, l5, l6, w_ref, m_ref, v_ref, p0, p1, p2, p3,
             go_ref, d_ref, mo_ref, vo_ref):
        total = g_ref[...].astype(F32) + l0[...].astype(F32)
        for ref in (l1, l2, l3, l4, l5, l6):
            total = total + ref[...].astype(F32)
        go_ref[...] = total
        d_ref[...], mo_ref[...], vo_ref[...] = _adam_math(w_ref[...], total, m_ref[...], v_ref[...])

    def slot(k):
        return pl.BlockSpec((None, tr, cols), lambda i, me: (k, i, 0))

    lay = pl.BlockSpec((None, tr, cols), lambda i, me: (layer, i, 0))
    return _call(body, name=name, grid=(r // tr,), prefetch=1,
                 in_specs=[pl.BlockSpec((None, tr, cols), lambda i, me: (me[0], i, 0))]
                 + [slot(k) for k in range(N_PEERS)] + [lay, lay, lay] + [ANY] * 4,
                 out_specs=[lay] * 4, out_shape=[_sds(w.shape)] * 4,
                 aliases={12 + k: k for k in range(4)}, sem=("parallel",))(
                     me_idx, g, *([land] * N_PEERS), w, m, v, *prev)


def _t5_bucket(rel):
    nb = T5_BUCKETS // 2
    max_exact = nb // 2
    base = jnp.where(rel > 0, nb, 0)
    n = jnp.abs(rel)
    nf = jnp.maximum(n, 1).astype(F32)
    large = max_exact + (jnp.log(nf / max_exact) / math.log(T5_MAX_DIST / max_exact)
                         * (nb - max_exact)).astype(jnp.int32)
    large = jnp.minimum(large, nb - 1)
    return base + jnp.where(n < max_exact, n, large)


def _a_bias_maps():
    v = jnp.arange(3)[:, None, None]
    q = jnp.arange(128)[None, :, None]
    k = jnp.arange(384)[None, None, :]
    rel = k - 128 * v - q
    valid = jnp.abs(rel) <= 128
    onehot = (_t5_bucket(rel)[..., None] == jnp.arange(T5_BUCKETS)).astype(F32)
    return onehot * valid[..., None].astype(F32), valid


def _b_bias_maps():
    v = jnp.arange(8)[:, None]
    i = jnp.arange(NA_ROWS)[None, :]
    dr = jnp.where(v == 4, i + 3, i - v + 7)
    row_oh = (dr[..., None] == jnp.arange(2 * NA_ROWS - 1)).astype(F32)
    q = jnp.arange(GRID_W)[:, None]
    kc = jnp.arange(GRID_W)[None, :]
    cs = jnp.clip(q - 8, 0, GRID_W - 16)
    valid = (kc >= cs) & (kc < cs + 16)
    col_oh = ((kc - q + 15)[..., None] == jnp.arange(31)).astype(F32) * valid[..., None].astype(F32)
    return row_oh, col_oh, valid


def _rope_tables(s):
    t = jnp.arange(s)
    row = (t // GRID_W).astype(F32)
    col = (t % GRID_W).astype(F32)
    axis_dim = HEAD_DIM // 2
    freqs = ROPE_THETA ** (-jnp.arange(0, axis_dim, 2, dtype=F32) / axis_dim)
    ang_row = row[:, None] * freqs[None, :]
    ang_col = col[:, None] * freqs[None, :]
    cos = jnp.concatenate([jnp.cos(ang_row)] * 2 + [jnp.cos(ang_col)] * 2, axis=1)
    sin = jnp.concatenate([-jnp.sin(ang_row), jnp.sin(ang_row), -jnp.sin(ang_col), jnp.sin(ang_col)], axis=1)
    return jnp.tile(cos, (1, CW // HEAD_DIM)), jnp.tile(sin, (1, CW // HEAD_DIM))


def _pack(parts, rows):
    flat = jnp.concatenate([p.reshape(-1).astype(F32) for p in parts])
    return jnp.pad(flat, (0, rows * LANES - flat.shape[0])).reshape(rows, LANES)


def _unpack(buf, shapes):
    flat = buf.reshape(-1)
    out, off = [], 0
    for shp in shapes:
        size = math.prod(shp)
        out.append(flat[off:off + size].reshape(shp))
        off += size
    return out


def kernel(x, norm_mix, w_in, a_sink, t5_table, b_rpb, c_q_gain, c_k_gain, out_gain_a, out_gain_b, out_gain_c, w_o, norm_mlp, w_up, w_down, norm_final, loss_target, m_norm_mix, m_w_in, m_a_sink, m_t5_table, m_b_rpb, m_c_q_gain, m_c_k_gain, m_out_gain_a, m_out_gain_b, m_out_gain_c, m_w_o, m_norm_mlp, m_w_up, m_w_down, m_norm_final, v_norm_mix, v_w_in, v_a_sink, v_t5_table, v_b_rpb, v_c_q_gain, v_c_k_gain, v_out_gain_a, v_out_gain_b, v_out_gain_c, v_w_o, v_norm_mlp, v_w_up, v_w_down, v_norm_final):
    n_layers = w_in.shape[0]
    s, d = x.shape[1], x.shape[2]
    d_ff = 4 * w_up.shape[2]
    in_w = 4 * w_in.shape[2]
    xs = x.reshape(s, d)
    target = loss_target.reshape(s, d)
    cfg_a, cfg_b = _cfg_a(s), _cfg_b(s)

    x_i, y_i, _ = _mesh_pos()
    me_chip = 2 * x_i + y_i
    me_idx = me_chip.astype(jnp.int32).reshape(1)
    w_bf = [w_in.astype(MXU_DT), w_o.astype(MXU_DT), w_up.astype(MXU_DT), w_down.astype(MXU_DT)]
    jobs, gather_send, gather_recv, gather_land, gather_token = _gather_start(w_bf)
    job_of = {job: j for j, job in enumerate(jobs)}
    ff_shard = w_up.shape[2]

    def gathered(l, t, after):
        j = job_of[(l, t)]
        land = _gather_wait(w_bf[t], l, gather_land[j], gather_send[j], gather_recv[j], after,
                            "gather_wait_%d_%d" % (l, t))
        return lax.dynamic_update_slice(land, w_bf[t][l][None], (me_chip, 0, 0))

    ones = _pair_ones()
    cos_t, sin_t = _rope_tables(s)
    a_onehot, a_valid = _a_bias_maps()
    bias_a = jnp.where(a_valid[:, None], jnp.einsum("vqkb,bh->vhqk", a_onehot, t5_table, precision=HIGHEST),
                       MASK_VALUE)
    row_oh, col_oh, b_valid = _b_bias_maps()
    sink_b = jnp.full((4, 1, LANES), MASK_VALUE, F32)

    def b_bias(rpb):
        t = jnp.einsum("hrz,vir->vhiz", rpb, row_oh, precision=HIGHEST)
        t = jnp.einsum("vhiz,qcz->vhqic", t, col_oh, precision=HIGHEST)
        t = jnp.where(b_valid[None, None, :, None, :], t, MASK_VALUE)
        return t.reshape(8, 8, GRID_W, NA_ROWS * GRID_W)

    def tile_gain(gvec):
        return jnp.tile(gvec, CW // HEAD_DIM).reshape(1, CW)

    def pad_sink(svec):
        return jnp.pad(svec, (0, LANES - svec.shape[0])).reshape(1, 1, LANES)

    saved = []
    xc = xs
    for l in range(n_layers):
        h1 = _rms_fwd(xc, norm_mix[l] + gather_token[0, 0] if l == 0 else norm_mix[l], "rms_mix")
        wf_in = gathered(l, 0, h1).transpose(1, 0, 2).reshape(d, in_w)
        proj = _matmul(h1, wf_in, mode="nn", name="proj_in", tm=1024, tn=768, tk=2048)
        bias_b = b_bias(b_rpb[l])
        oa = _local_attn_fwd(proj, bias_a, pad_sink(a_sink[l]), cfg_a, "attn_a_fwd")
        ob = _local_attn_fwd(proj, bias_b, sink_b, cfg_b, "attn_b_fwd")
        gq, gk = tile_gain(c_q_gain[l]), tile_gain(c_k_gain[l])
        qh, kd, vd = _cprep_fwd(proj, gq, gk, cos_t, sin_t, ones, "cprep_fwd")
        kdt, vdt = kd.T, vd.T
        oct, lse = _flash_fwd(qh, kd, vdt, "attn_c_fwd")
        oc = oct.T
        mix = _groupnorm_fwd(oa, ob, oc, out_gain_a[l], out_gain_b[l], out_gain_c[l], "groupnorm_fwd")
        wf_o = gathered(l, 1, mix).reshape(d, d)
        x_mid = _matmul(mix, wf_o, mode="nn", name="proj_out", tm=1024, tn=1024, tk=2048, epi="res",
                        extra=(xc,))
        h2 = _rms_fwd(x_mid, norm_mlp[l], "rms_mlp")
        wg_up = gathered(l, 2, h2)
        nb_up = ff_shard // 1024
        u, uu = _matmul(h2, wg_up, mode="nn", name="mlp_up", tm=1024, tn=1024, tk=2048, epi="relu2",
                        out_dtypes=(F32, MXU_DT), mkn=(s, d, d_ff),
                        b_spec=pl.BlockSpec((None, 2048, 1024), lambda i, j, kk: (j // nb_up, kk, j % nb_up)))
        wf_down = gathered(l, 3, uu).reshape(d_ff, d)
        x_out = _matmul(uu, wf_down, mode="nn", name="mlp_down", tm=1024, tn=1024, tk=2048, epi="res",
                        extra=(x_mid,))
        saved.append((xc, h1, proj, bias_b, oa, ob, qh, kd, vd, kdt, oc, lse, mix, x_mid, h2, u, uu,
                      wf_in, wf_o, wg_up, wf_down))
        xc = x_out

    loss_part, dx, dxb, dg_final = _final_loss(xc, norm_final, target, "final_loss")

    small = {k: [] for k in ("norm_mix", "a_sink", "b_rpb", "cq", "ck", "oga", "ogb", "ogc", "norm_mlp")}
    dbias_a_total = jnp.zeros_like(bias_a)
    big_w = {"w_in": (w_in, m_w_in, v_w_in), "w_o": (w_o, m_w_o, v_w_o), "w_up": (w_up, m_w_up, v_w_up),
             "w_down": (w_down, m_w_down, v_w_down)}
    big = {nm: [lax.empty(wmv[0].shape, F32) for _ in range(4)] for nm, wmv in big_w.items()}

    def send_grad(nm, l, g):
        send, recv, land, token = _grad_start(g, "grad_start_%s_%d" % (nm, l))
        return (nm, l, g, send, recv, land), token[0, 0]

    def finish_grads(pending, after):
        for nm, l, g, send, recv, land in pending:
            land = _grad_wait(g, land, send, recv, after, "grad_wait_%s_%d" % (nm, l))
            wmv = big_w[nm]
            big[nm] = _sum_adam(g, land, wmv[0], wmv[1], wmv[2], big[nm], l, me_idx, "sum_adam_%s_%d" % (nm, l))

    pending = []
    for l in reversed(range(n_layers)):
        (xin, h1, proj, bias_b, oa, ob, qh, kd, vd, kdt, oc, lse, mix, x_mid, h2, u, uu,
         wf_in, wf_o, wg_up, wf_down) = saved[l]
        started = []
        du = _matmul(dxb, wf_down, mode="nt", name="mlp_down_dgrad", tm=1024, tn=1024, tk=2048, epi="mul2u",
                     extra=(u,), out_dtypes=(MXU_DT,))
        gw = _matmul(uu, dxb, mode="tn", name="mlp_down_wgrad", tm=1024, tn=1024, tk=1024, out_dtypes=(GRAD_DT,))
        rec, tok_down = send_grad("w_down", l, gw.reshape(4, d_ff // 4, d))
        started.append(rec)
        nbk = ff_shard // 2048
        dh2 = _matmul(du, wg_up, mode="nt", name="mlp_up_dgrad", tm=1024, tn=1024, tk=2048,
                      mkn=(s, d_ff, d),
                      b_spec=pl.BlockSpec((None, 1024, 2048), lambda i, j, kk: (kk // nbk, j, kk % nbk)))
        nbo = ff_shard // 1024
        gw = _matmul(h2, du, mode="tn", name="mlp_up_wgrad", tm=1024, tn=1024, tk=1024, out_dtypes=(GRAD_DT,),
                     out_spec=pl.BlockSpec((None, 1024, 1024), lambda i, j, kk: (j // nbo, i, j % nbo)),
                     out_shape=(4, d, ff_shard))
        rec, tok_up = send_grad("w_up", l, gw)
        started.append(rec)
        dx_mid, dxmb, dg = _rms_bwd(x_mid, norm_mlp[l] + (tok_down + tok_up), dh2, dx, "rms_mlp_bwd")
        small["norm_mlp"].append(dg)
        dmix = _matmul(dxmb, wf_o, mode="nt", name="proj_out_dgrad", tm=1024, tn=1024, tk=2048)
        gw = _matmul(mix, dxmb, mode="tn", name="proj_out_wgrad", tm=1024, tn=1024, tk=1024, out_dtypes=(GRAD_DT,))
        rec, tok_o = send_grad("w_o", l, gw.reshape(4, d // 4, d))
        started.append(rec)
        doa, dob, doc, docb, ddc, dga, dgb, dgc = _groupnorm_bwd(
            dmix, oa, ob, oc, out_gain_a[l] + tok_o, out_gain_b[l], out_gain_c[l], ones, "groupnorm_bwd")
        small["oga"].append(dga)
        small["ogb"].append(dgb)
        small["ogc"].append(dgc)
        dqa, dka, dva, dbias_a, dsink = _local_attn_bwd(proj, bias_a, pad_sink(a_sink[l]), doa, cfg_a, "attn_a_bwd")
        dbias_a_total = dbias_a_total + dbias_a
        small["a_sink"].append(dsink[0, 0, :a_sink.shape[1]])
        dqb, dkb, dvb, dbias_b, _ = _local_attn_bwd(proj, bias_b, sink_b, dob, cfg_b, "attn_b_bwd")
        db5 = jnp.where(b_valid[None, None, :, None, :], dbias_b.reshape(8, 8, GRID_W, NA_ROWS, GRID_W), 0.0)
        t = jnp.einsum("vhqic,qcz->vhiz", db5, col_oh, precision=HIGHEST)
        small["b_rpb"].append(jnp.einsum("vhiz,vir->hrz", t, row_oh, precision=HIGHEST))
        dd_rows = ddc.reshape(s, 16, HEAD_DIM)[:, :, 0].T.reshape(4, 4, s)
        dqht, dkd, dvd = _flash_bwd(qh, kd, vd, kdt, docb, lse, dd_rows, "attn_c_bwd")
        dqh = dqht.T
        gq, gk = tile_gain(c_q_gain[l]), tile_gain(c_k_gain[l])
        dqc, dkc, dvc, dgq, dgk = _cprep_bwd(proj, gq, gk, cos_t, sin_t, ones, dqh, dkd, dvd, "cprep_bwd")
        small["cq"].append(dgq.reshape(CW // HEAD_DIM, HEAD_DIM).sum(0))
        small["ck"].append(dgk.reshape(CW // HEAD_DIM, HEAD_DIM).sum(0))
        dproj = jnp.concatenate([dqa, dka, dva, dqb, dkb, dvb, dqc, dkc, dvc], axis=1).astype(MXU_DT)
        dh1 = _matmul(dproj, wf_in, mode="nt", name="proj_in_dgrad", tm=1024, tn=1024, tk=1920)
        gw = _matmul(h1, dproj, mode="tn", name="proj_in_wgrad", tm=1024, tn=768, tk=1024, out_dtypes=(GRAD_DT,))
        rec, tok_in = send_grad("w_in", l, gw.reshape(d, 4, in_w // 4).transpose(1, 0, 2))
        started.append(rec)
        dx, dxb, dg = _rms_bwd(xin, norm_mix[l] + tok_in, dh1, dx_mid, "rms_mix_bwd")
        small["norm_mix"].append(dg)
        finish_grads(pending, dx)
        pending = started
    finish_grads(pending, dx)

    for lst in small.values():
        lst.reverse()

    dt5 = jnp.einsum("vhqk,vqkb->bh", dbias_a_total, a_onehot, precision=HIGHEST)
    small_names = ["norm_mix", "a_sink", "t5_table", "b_rpb", "c_q_gain", "c_k_gain", "out_gain_a", "out_gain_b",
                   "out_gain_c", "norm_mlp", "norm_final"]
    small_w = [norm_mix, a_sink, t5_table, b_rpb, c_q_gain, c_k_gain, out_gain_a, out_gain_b, out_gain_c, norm_mlp,
               norm_final]
    small_m = [m_norm_mix, m_a_sink, m_t5_table, m_b_rpb, m_c_q_gain, m_c_k_gain, m_out_gain_a, m_out_gain_b,
               m_out_gain_c, m_norm_mlp, m_norm_final]
    small_v = [v_norm_mix, v_a_sink, v_t5_table, v_b_rpb, v_c_q_gain, v_c_k_gain, v_out_gain_a, v_out_gain_b,
               v_out_gain_c, v_norm_mlp, v_norm_final]
    small_g = [jnp.stack(small["norm_mix"]), jnp.stack(small["a_sink"]), dt5, jnp.stack(small["b_rpb"]),
               jnp.stack(small["cq"]), jnp.stack(small["ck"]), jnp.stack(small["oga"]), jnp.stack(small["ogb"]),
               jnp.stack(small["ogc"]), jnp.stack(small["norm_mlp"]), dg_final]
    shapes = [w.shape for w in small_w]
    total = sum(math.prod(shp) for shp in shapes) + 1
    rows = -(-total // (8 * LANES)) * 8
    one = [jnp.ones((1,), F32)]
    gs, dl, mo, vo = _allreduce_small_adam(_pack(small_g + [loss_part[0, :1]], rows), _pack(small_w + one, rows),
                                           _pack(small_m + one, rows), _pack(small_v + one, rows))
    sg = _unpack(gs, shapes + [(1,)])
    sd, sm, sv = _unpack(dl, shapes), _unpack(mo, shapes), _unpack(vo, shapes)
    loss = sg[-1].reshape(())

    by_name = {nm: (sg[i], sd[i], sm[i], sv[i]) for i, nm in enumerate(small_names)}
    by_name.update(big)
    order = ["norm_mix", "w_in", "a_sink", "t5_table", "b_rpb", "c_q_gain", "c_k_gain", "out_gain_a", "out_gain_b",
             "out_gain_c", "w_o", "norm_mlp", "w_up", "w_down", "norm_final"]
    outs = [loss, dx.reshape(x.shape)]
    for field in range(4):
        outs.extend(by_name[nm][field] for nm in order)
    return tuple(outs)
```

```python
import functools
import math

import jax
import jax.numpy as jnp
from jax import lax
from jax.experimental import pallas as pl
from jax.experimental.pallas import tpu as pltpu

F32 = jnp.float32
MXU_DT = jnp.bfloat16
GRAD_DT = jnp.bfloat16
HIGHEST = lax.Precision.HIGHEST

HEAD_DIM = 64
LANES = 128
EPS = 1e-6
MASK_VALUE = -1e30
GRID_W = 64
NA_ROWS = 8
T5_BUCKETS = 32
T5_MAX_DIST = 128
ROPE_THETA = 10000.0
ADAM_LR, ADAM_B1, ADAM_B2, ADAM_EPS, ADAM_WD, ADAM_STEP = 0.001, 0.9, 0.999, 1e-08, 0.01, 10
VMEM_LIMIT = 56 * 1024 * 1024

MESH_ID = pl.DeviceIdType.MESH
ANY = pl.BlockSpec(memory_space=pl.ANY)

NT_DIMS = (((1,), (1,)), ((), ()))
TN_DIMS = (((0,), (0,)), ((), ()))
NN_DIMS = (((1,), (0,)), ((), ()))


def _dot(a, b, dims=NN_DIMS):
    return lax.dot_general(a, b, dims, preferred_element_type=F32)


def _call(body, *, name, out_shape, grid=(), in_specs=None, out_specs=None, scratch=(), sem=None,
          prefetch=0, aliases=None):
    params = {"vmem_limit_bytes": VMEM_LIMIT}
    if sem is not None:
        params["dimension_semantics"] = sem
    kwargs = {}
    if aliases:
        kwargs["input_output_aliases"] = aliases
    if prefetch:
        spec = pltpu.PrefetchScalarGridSpec(num_scalar_prefetch=prefetch, grid=grid, in_specs=in_specs,
                                            out_specs=out_specs, scratch_shapes=list(scratch))
        return pl.pallas_call(body, grid_spec=spec, out_shape=out_shape, name=name,
                              compiler_params=pltpu.CompilerParams(**params), **kwargs)
    return pl.pallas_call(body, grid=grid, in_specs=in_specs, out_specs=out_specs, out_shape=out_shape,
                          scratch_shapes=list(scratch), name=name,
                          compiler_params=pltpu.CompilerParams(**params), **kwargs)


def _sds(shape, dtype=F32):
    return jax.ShapeDtypeStruct(tuple(shape), dtype)


def _matmul(a, b, *, mode, name, tm, tn, tk, epi="plain", extra=(), out_dtypes=(F32,), mkn=None,
            b_spec=None, out_spec=None, out_shape=None):
    if mkn is None:
        if mode == "nn":
            (m, k), n = a.shape, b.shape[1]
        elif mode == "nt":
            (m, k), n = a.shape, b.shape[0]
        else:
            (k, m), n = a.shape, b.shape[1]
    else:
        m, k, n = mkn
    tm, tn, tk = min(tm, m), min(tn, n), min(tk, k)
    assert m % tm == 0 and n % tn == 0 and k % tk == 0, (name, m, n, k, tm, tn, tk)
    nk = k // tk
    dims = {"nn": NN_DIMS, "nt": NT_DIMS, "tn": TN_DIMS}[mode]
    n_extra, n_out = len(extra), len(out_dtypes)

    def body(a_ref, b_ref, *rest):
        extra_refs = rest[:n_extra]
        out_refs = rest[n_extra:n_extra + n_out]
        acc_ref = rest[n_extra + n_out]
        kk = pl.program_id(2)

        @pl.when(kk == 0)
        def _():
            acc_ref[...] = jnp.zeros_like(acc_ref)

        acc_ref[...] += _dot(a_ref[...].astype(MXU_DT), b_ref[...].astype(MXU_DT), dims)

        @pl.when(kk == nk - 1)
        def _():
            acc = acc_ref[...]
            if epi == "plain":
                out_refs[0][...] = acc.astype(out_refs[0].dtype)
            elif epi == "res":
                out_refs[0][...] = (extra_refs[0][...] + acc).astype(out_refs[0].dtype)
            elif epi == "relu2":
                u = jnp.maximum(acc, 0.0)
                out_refs[0][...] = u.astype(out_refs[0].dtype)
                out_refs[1][...] = (u * u).astype(out_refs[1].dtype)
            elif epi == "mul2u":
                out_refs[0][...] = (2.0 * extra_refs[0][...] * acc).astype(out_refs[0].dtype)
            else:
                raise ValueError(epi)

    if mode == "tn":
        a_spec = pl.BlockSpec((tk, tm), lambda i, j, kk: (kk, i))
    else:
        a_spec = pl.BlockSpec((tm, tk), lambda i, j, kk: (i, kk))
    if b_spec is None:
        if mode == "nt":
            b_spec = pl.BlockSpec((tn, tk), lambda i, j, kk: (j, kk))
        else:
            b_spec = pl.BlockSpec((tk, tn), lambda i, j, kk: (kk, j))
    mn_spec = pl.BlockSpec((tm, tn), lambda i, j, kk: (i, j))
    if out_spec is None:
        out_spec = mn_spec
    if out_shape is None:
        out_shape = (m, n)
    res = _call(body, name=name, grid=(m // tm, n // tn, nk),
                in_specs=[a_spec, b_spec] + [mn_spec] * n_extra,
                out_specs=[out_spec] * n_out,
                out_shape=[_sds(out_shape, d) for d in out_dtypes],
                scratch=[pltpu.VMEM((tm, tn), F32)],
                sem=("parallel", "parallel", "arbitrary"))(a, b, *extra)
    return res if n_out > 1 else res[0]


def _row_tile(s):
    return min(512, s)


def _rms_fwd(x, g, name):
    s, d = x.shape
    tm = _row_tile(s)

    def body(x_ref, g_ref, h_ref):
        xv = x_ref[...]
        r = lax.rsqrt(jnp.mean(xv * xv, axis=-1, keepdims=True) + EPS)
        h_ref[...] = ((xv * r) * g_ref[...]).astype(h_ref.dtype)

    return _call(body, name=name, grid=(s // tm,),
                 in_specs=[pl.BlockSpec((tm, d), lambda i: (i, 0)), pl.BlockSpec((1, d), lambda i: (0, 0))],
                 out_specs=pl.BlockSpec((tm, d), lambda i: (i, 0)),
                 out_shape=_sds((s, d), MXU_DT), sem=("parallel",))(x, g.reshape(1, d))


def _rms_bwd(x, g, dh, dres, name):
    s, d = x.shape
    tm = _row_tile(s)

    def body(x_ref, g_ref, dh_ref, dres_ref, dx_ref, dxb_ref, dg_ref):
        i = pl.program_id(0)
        xv = x_ref[...]
        r = lax.rsqrt(jnp.mean(xv * xv, axis=-1, keepdims=True) + EPS)
        xh = xv * r
        dhv = dh_ref[...]
        gd = dhv * g_ref[...]
        c = jnp.mean(gd * xh, axis=-1, keepdims=True)
        dx = dres_ref[...] + r * (gd - xh * c)
        dx_ref[...] = dx
        dxb_ref[...] = dx.astype(dxb_ref.dtype)
        part = jnp.sum(dhv * xh, axis=0, keepdims=True)

        @pl.when(i == 0)
        def _():
            dg_ref[...] = part

        @pl.when(i > 0)
        def _():
            dg_ref[...] += part

    row = pl.BlockSpec((tm, d), lambda i: (i, 0))
    vec = pl.BlockSpec((1, d), lambda i: (0, 0))
    return _call(body, name=name, grid=(s // tm,), in_specs=[row, vec, row, row],
                 out_specs=[row, row, vec],
                 out_shape=[_sds((s, d)), _sds((s, d), MXU_DT), _sds((1, d))],
                 sem=("arbitrary",))(x, g.reshape(1, d), dh, dres)


def _final_loss(x, g, target, name):
    s, d = x.shape
    tm = _row_tile(s)

    def body(x_ref, g_ref, t_ref, loss_ref, dx_ref, dxb_ref, dg_ref):
        i = pl.program_id(0)
        xv = x_ref[...]
        gv = g_ref[...]
        r = lax.rsqrt(jnp.mean(xv * xv, axis=-1, keepdims=True) + EPS)
        xh = xv * r
        err = xh * gv - t_ref[...]
        part_loss = 0.5 * jnp.sum(jnp.mean(err * err, axis=-1, keepdims=True), axis=0, keepdims=True)
        dy = err * (1.0 / d)
        gd = dy * gv
        c = jnp.mean(gd * xh, axis=-1, keepdims=True)
        dx = r * (gd - xh * c)
        dx_ref[...] = dx
        dxb_ref[...] = dx.astype(dxb_ref.dtype)
        part_g = jnp.sum(dy * xh, axis=0, keepdims=True)
        part_l = jnp.broadcast_to(part_loss, (1, LANES))

        @pl.when(i == 0)
        def _():
            dg_ref[...] = part_g
            loss_ref[...] = part_l

        @pl.when(i > 0)
        def _():
            dg_ref[...] += part_g
            loss_ref[...] += part_l

    row = pl.BlockSpec((tm, d), lambda i: (i, 0))
    vec = pl.BlockSpec((1, d), lambda i: (0, 0))
    return _call(body, name=name, grid=(s // tm,), in_specs=[row, vec, row],
                 out_specs=[pl.BlockSpec((1, LANES), lambda i: (0, 0)), row, row, vec],
                 out_shape=[_sds((1, LANES)), _sds((s, d)), _sds((s, d), MXU_DT), _sds((1, d))],
                 sem=("arbitrary",))(x, g.reshape(1, d), target)


def _lane_iota(shape):
    return lax.broadcasted_iota(jnp.int32, shape, len(shape) - 1)


def _swap_halves(x):
    return pltpu.roll(x, HEAD_DIM, 1)


def _segsum64(x, ones_ref):
    ones = ones_ref[...]
    outs = []
    for c in range(x.shape[1] // LANES):
        xc = x[:, c * LANES:(c + 1) * LANES]
        hi = xc.astype(MXU_DT)
        r1 = xc - hi.astype(F32)
        mid = r1.astype(MXU_DT)
        lo = (r1 - mid.astype(F32)).astype(MXU_DT)
        outs.append(_dot(hi, ones) + _dot(mid, ones) + _dot(lo, ones))
    return outs[0] if len(outs) == 1 else jnp.concatenate(outs, axis=1)


def _pair_ones():
    i = jnp.arange(LANES)
    return (i[:, None] // HEAD_DIM == i[None, :] // HEAD_DIM).astype(MXU_DT)


def _col(x, lane):
    return jnp.sum(jnp.where(_lane_iota(x.shape) == lane, x, 0.0), axis=-1, keepdims=True)


class _LocalCfg:
    def __init__(self, *, groups, qb, kw, qw, sub, qcol, kcol, vcol, kvhalf, kstart, variant, variant_py):
        self.groups, self.qb, self.kw, self.qw = groups, qb, kw, qw
        self.sub = sub
        self.qcol, self.kcol, self.vcol = qcol, kcol, vcol
        self.kvhalf = kvhalf
        self.kstart, self.variant = kstart, variant
        self.variant_py = variant_py
        self.pairs = qw // LANES


def _cfg_a(s):
    nb = s // 128
    return _LocalCfg(groups=1, qb=128, kw=384, qw=512, sub=1, qcol=lambda g: 0, kcol=lambda g: 4,
                     vcol=lambda g: 5, kvhalf=lambda t, e: t // 2,
                     kstart=lambda n: 128 * jnp.clip(n - 1, 0, nb - 3),
                     variant=lambda n: jnp.where(n <= 0, 0, jnp.where(n == nb - 1, 2, 1)),
                     variant_py=lambda n: 0 if n <= 0 else (2 if n == nb - 1 else 1))


def _cfg_b(s):
    rows = s // GRID_W
    return _LocalCfg(groups=4, qb=64, kw=512, qw=128, sub=4, qcol=lambda g: 6 + g, kcol=lambda g: 10 + g,
                     vcol=lambda g: 14 + g, kvhalf=lambda t, e: e,
                     kstart=lambda n: GRID_W * jnp.clip(n - NA_ROWS // 2, 0, rows - NA_ROWS),
                     variant=lambda n: jnp.where(n < 4, jnp.maximum(n, 0),
                                                 jnp.where(n > rows - 4, n - (rows - 8), 4)),
                     variant_py=lambda n: max(n, 0) if n < 4 else (n - (rows - 8) if n > rows - 4 else 4))


def _sum_visited(parts, cfg, s):
    n_var = parts[0].shape[0]
    variants = [cfg.variant_py(n) for n in range(s // cfg.qb)]
    total = None
    for i, part in enumerate(parts):
        seen = jnp.array([v in variants[i::cfg.sub] for v in range(n_var)]).reshape(n_var, 1, 1, 1)
        term = jnp.where(seen, part, 0.0)
        total = term if total is None else total + term
    return total


def _local_head(cfg, t, e, qp, qp_sw, kb, bias, sink_row, left_q):
    kvh = cfg.kvhalf(t, e)
    qsrc = qp if e == kvh else qp_sw
    keep = left_q if kvh == 0 else jnp.logical_not(left_q)
    qm = jnp.where(keep, qsrc, 0.0).astype(MXU_DT)
    sc = _dot(qm, kb, NT_DIMS) + bias
    snk = _col(sink_row, 2 * t + e)
    m = jnp.maximum(jnp.max(sc, axis=-1, keepdims=True), snk)
    p = jnp.exp(sc - m)
    l = jnp.sum(p, axis=-1, keepdims=True) + jnp.exp(snk - m)
    p = p / l
    return qm, keep, p, m, l, snk


def _local_attn_fwd(proj, bias, sink, cfg, name):
    s = proj.shape[0]
    qb, kw, qw, g_n = cfg.qb, cfg.kw, cfg.qw, cfg.groups
    hq = 2 * cfg.pairs

    sub = cfg.sub

    def body(q_ref, k_ref, v_ref, *rest):
        b_refs, s_ref, o_ref = rest[:sub], rest[sub], rest[sub + 1]
        n = pl.program_id(1)
        left_q = _lane_iota((qb, LANES)) < HEAD_DIM
        left_k = _lane_iota((kw, LANES)) < HEAD_DIM
        sink_row = s_ref[...]
        for i in range(sub):
            ks = pl.multiple_of(cfg.kstart(sub * n + i), 64)
            kf = k_ref[pl.ds(ks, kw), :]
            vf = v_ref[pl.ds(ks, kw), :]
            kb = kf.astype(MXU_DT)
            vf_sw = _swap_halves(vf)
            rows = slice(i * qb, (i + 1) * qb)
            for t in range(cfg.pairs):
                qp = q_ref[rows, t * LANES:(t + 1) * LANES] * 0.125
                qp_sw = _swap_halves(qp)
                acc = jnp.zeros((qb, LANES), F32)
                for e in range(2):
                    _, _, p, _, _, _ = _local_head(cfg, t, e, qp, qp_sw, kb, b_refs[i][0, 2 * t + e], sink_row,
                                                   left_q)
                    vsrc = vf if e == cfg.kvhalf(t, e) else vf_sw
                    vsel = jnp.where(left_k if e == 0 else jnp.logical_not(left_k), vsrc, 0.0).astype(MXU_DT)
                    acc = acc + _dot(p.astype(MXU_DT), vsel)
                o_ref[rows, t * LANES:(t + 1) * LANES] = acc

    def bias_spec(i):
        return pl.BlockSpec((1, hq, qb, kw), lambda g, n: (cfg.variant(sub * n + i), g, 0, 0))

    return _call(
        body, name=name, grid=(g_n, s // (sub * qb)),
        in_specs=[pl.BlockSpec((sub * qb, qw), lambda g, n: (n, cfg.qcol(g))),
                  pl.BlockSpec((s, LANES), lambda g, n: (0, cfg.kcol(g))),
                  pl.BlockSpec((s, LANES), lambda g, n: (0, cfg.vcol(g)))]
        + [bias_spec(i) for i in range(sub)]
        + [pl.BlockSpec((None, 1, LANES), lambda g, n: (g, 0, 0))],
        out_specs=pl.BlockSpec((sub * qb, qw), lambda g, n: (n, g)),
        out_shape=_sds((s, g_n * qw)), sem=("parallel", "arbitrary"))(proj, proj, proj, *([bias] * sub), sink)


def _local_attn_bwd(proj, bias, sink, do, cfg, name):
    s = proj.shape[0]
    qb, kw, qw, g_n = cfg.qb, cfg.kw, cfg.qw, cfg.groups
    hq = 2 * cfg.pairs

    sub = cfg.sub

    def body(q_ref, k_ref, v_ref, *rest):
        b_refs, s_ref, do_ref = rest[:sub], rest[sub], rest[sub + 1]
        dq_ref, dk_ref, dv_ref = rest[sub + 2:sub + 5]
        db_refs, dsk_ref = rest[sub + 5:2 * sub + 5], rest[2 * sub + 5]
        n = pl.program_id(1)

        @pl.when(n == 0)
        def _():
            dk_ref[...] = jnp.zeros_like(dk_ref)
            dv_ref[...] = jnp.zeros_like(dv_ref)
            dsk_ref[...] = jnp.zeros_like(dsk_ref)

        left_q = _lane_iota((qb, LANES)) < HEAD_DIM
        left_k = _lane_iota((kw, LANES)) < HEAD_DIM
        sink_row = s_ref[...]
        row0 = lax.broadcasted_iota(jnp.int32, (8, LANES), 0) == 0
        lane8 = _lane_iota((8, LANES))
        dsk_acc = jnp.zeros((8, LANES), F32)
        for i in range(sub):
            blk = sub * n + i
            ks = pl.multiple_of(cfg.kstart(blk), 64)
            db_ref = db_refs[i]

            @pl.when(jnp.logical_or(n == 0, cfg.variant(blk) != cfg.variant(blk - sub)))
            def _():
                db_ref[...] = jnp.zeros_like(db_ref)

            kf = k_ref[pl.ds(ks, kw), :]
            vf = v_ref[pl.ds(ks, kw), :]
            kb = kf.astype(MXU_DT)
            vb = vf.astype(MXU_DT)
            kf_sw = _swap_halves(kf)
            rows = slice(i * qb, (i + 1) * qb)
            dk_acc = jnp.zeros((kw, LANES), F32)
            dv_acc = jnp.zeros((kw, LANES), F32)
            for t in range(cfg.pairs):
                qp = q_ref[rows, t * LANES:(t + 1) * LANES] * 0.125
                qp_sw = _swap_halves(qp)
                dop = do_ref[rows, t * LANES:(t + 1) * LANES]
                dop_sw = _swap_halves(dop)
                dq_t = jnp.zeros((qb, LANES), F32)
                for e in range(2):
                    h = 2 * t + e
                    qm, keep, p, m, l, snk = _local_head(cfg, t, e, qp, qp_sw, kb, b_refs[i][0, h], sink_row,
                                                         left_q)
                    kvh = cfg.kvhalf(t, e)
                    dom = jnp.where(keep, dop if e == kvh else dop_sw, 0.0).astype(MXU_DT)
                    dp = _dot(dom, vb, NT_DIMS)
                    dd = jnp.sum(p * dp, axis=-1, keepdims=True)
                    ds = p * (dp - dd)
                    p_sink = jnp.exp(snk - m) / l
                    dsink = jnp.sum(-p_sink * dd, axis=0, keepdims=True)
                    dsk_acc = dsk_acc + jnp.where(jnp.logical_and(row0, lane8 == h), dsink, 0.0)
                    dsb = ds.astype(MXU_DT)
                    dv_acc = dv_acc + _dot(p.astype(MXU_DT), dom, TN_DIMS)
                    dk_acc = dk_acc + _dot(dsb, qm, TN_DIMS)
                    ksrc = kf if e == kvh else kf_sw
                    ksel = jnp.where(left_k if e == 0 else jnp.logical_not(left_k), ksrc, 0.0).astype(MXU_DT)
                    dq_t = dq_t + _dot(dsb, ksel)
                    db_ref[0, h] += ds
                dq_ref[rows, t * LANES:(t + 1) * LANES] = dq_t * 0.125
            dk_ref[pl.ds(ks, kw), :] += dk_acc
            dv_ref[pl.ds(ks, kw), :] += dv_acc
        dsk_ref[...] += dsk_acc

    def bias_spec(i):
        return pl.BlockSpec((1, hq, qb, kw), lambda g, n: (cfg.variant(sub * n + i), g, 0, 0))

    n_var = bias.shape[0]
    res = _call(
        body, name=name, grid=(g_n, s // (sub * qb)),
        in_specs=[pl.BlockSpec((sub * qb, qw), lambda g, n: (n, cfg.qcol(g))),
                  pl.BlockSpec((s, LANES), lambda g, n: (0, cfg.kcol(g))),
                  pl.BlockSpec((s, LANES), lambda g, n: (0, cfg.vcol(g)))]
        + [bias_spec(i) for i in range(sub)]
        + [pl.BlockSpec((None, 1, LANES), lambda g, n: (g, 0, 0)),
           pl.BlockSpec((sub * qb, qw), lambda g, n: (n, g))],
        out_specs=[pl.BlockSpec((sub * qb, qw), lambda g, n: (n, g)),
                   pl.BlockSpec((s, LANES), lambda g, n: (0, g)),
                   pl.BlockSpec((s, LANES), lambda g, n: (0, g))]
        + [bias_spec(i) for i in range(sub)]
        + [pl.BlockSpec((None, 8, LANES), lambda g, n: (g, 0, 0))],
        out_shape=[_sds((s, g_n * qw)), _sds((s, g_n * LANES)), _sds((s, g_n * LANES))]
        + [_sds((n_var, g_n * hq, qb, kw))] * sub + [_sds((g_n, 8, LANES))],
        sem=("parallel", "arbitrary"))(proj, proj, proj, *([bias] * sub), sink, do)
    dq, dk, dv = res[:3]
    dbias = _sum_visited(res[3:3 + sub], cfg, s)
    return dq, dk, dv, dbias, res[3 + sub]


QC_COL, KC_COL, VC_COL = 9, 13, 14
CW = 256


def _swap16(x):
    w = x.shape[1]
    lane = _lane_iota(x.shape)
    return jnp.where(lane % 32 < 16, pltpu.roll(x, w - 16, 1), pltpu.roll(x, 16, 1))


def _dup_halves(x):
    left = _lane_iota(x.shape) < HEAD_DIM
    sw = _swap_halves(x)
    return jnp.where(left, x, sw), jnp.where(left, sw, x)


def _normrope(x, gain, cos, sin, ones_ref):
    ms = _segsum64(x * x, ones_ref) * (1.0 / HEAD_DIM)
    r = lax.rsqrt(ms + EPS)
    y = (x * r) * gain
    return y * cos + _swap16(y) * sin, r


def _cprep_fwd(proj, gq, gk, cos, sin, ones, name):
    s = proj.shape[0]
    tm = _row_tile(s)

    def body(q0, q1, q2, q3, k_ref, v_ref, gq_ref, gk_ref, cos_ref, sin_ref, ones_ref, qh_ref, kd_ref, vd_ref):
        cos_v, sin_v = cos_ref[...], sin_ref[...]
        for c, q_ref in enumerate((q0, q1, q2, q3)):
            y, _ = _normrope(q_ref[...], gq_ref[...], cos_v, sin_v, ones_ref)
            qh_ref[:, c * CW:(c + 1) * CW] = (y * 0.125).astype(qh_ref.dtype)
        yk, _ = _normrope(k_ref[...], gk_ref[...], cos_v, sin_v, ones_ref)
        vv = v_ref[...]
        for p in range(2):
            ka, kb_ = _dup_halves(yk[:, p * LANES:(p + 1) * LANES])
            va, vb_ = _dup_halves(vv[:, p * LANES:(p + 1) * LANES])
            kd_ref[:, (2 * p) * LANES:(2 * p + 1) * LANES] = ka.astype(kd_ref.dtype)
            kd_ref[:, (2 * p + 1) * LANES:(2 * p + 2) * LANES] = kb_.astype(kd_ref.dtype)
            vd_ref[:, (2 * p) * LANES:(2 * p + 1) * LANES] = va.astype(vd_ref.dtype)
            vd_ref[:, (2 * p + 1) * LANES:(2 * p + 2) * LANES] = vb_.astype(vd_ref.dtype)

    def chunk(col):
        return pl.BlockSpec((tm, CW), lambda i: (i, col))

    vec = pl.BlockSpec((1, CW), lambda i: (0, 0))
    tab = pl.BlockSpec((tm, CW), lambda i: (i, 0))
    return _call(body, name=name, grid=(s // tm,),
                 in_specs=[chunk(QC_COL), chunk(QC_COL + 1), chunk(QC_COL + 2), chunk(QC_COL + 3),
                           chunk(KC_COL), chunk(VC_COL), vec, vec, tab, tab,
                           pl.BlockSpec((LANES, LANES), lambda i: (0, 0))],
                 out_specs=[pl.BlockSpec((tm, 4 * CW), lambda i: (i, 0)),
                            pl.BlockSpec((tm, 2 * CW), lambda i: (i, 0)),
                            pl.BlockSpec((tm, 2 * CW), lambda i: (i, 0))],
                 out_shape=[_sds((s, 4 * CW), MXU_DT), _sds((s, 2 * CW), MXU_DT), _sds((s, 2 * CW), MXU_DT)],
                 sem=("parallel",))(proj, proj, proj, proj, proj, proj, gq, gk, cos, sin, ones)


def _cprep_bwd(proj, gq, gk, cos, sin, ones, dqh, dkd, dvd, name):
    s = proj.shape[0]
    tm = _row_tile(s)

    def fold(ref, p):
        a = ref[:, (2 * p) * LANES:(2 * p + 1) * LANES]
        b = ref[:, (2 * p + 1) * LANES:(2 * p + 2) * LANES]
        ta = a + _swap_halves(a)
        tb = b + _swap_halves(b)
        return jnp.where(_lane_iota(a.shape) < HEAD_DIM, ta, tb)

    def norm_bwd(x, gain, dyr, cos_v, sin_v, ones_ref):
        dy = dyr * cos_v + _swap16(dyr * sin_v)
        ms = _segsum64(x * x, ones_ref) * (1.0 / HEAD_DIM)
        r = lax.rsqrt(ms + EPS)
        xh = x * r
        gd = dy * gain
        c = _segsum64(gd * xh, ones_ref) * (1.0 / HEAD_DIM)
        return r * (gd - xh * c), jnp.sum(dy * xh, axis=0, keepdims=True)

    def body(q0, q1, q2, q3, k_ref, gq_ref, gk_ref, cos_ref, sin_ref, ones_ref, dqh_ref, dkd_ref, dvd_ref,
             dq_ref, dk_ref, dv_ref, dgq_ref, dgk_ref):
        i = pl.program_id(0)
        cos_v, sin_v = cos_ref[...], sin_ref[...]
        gq_part = jnp.zeros((1, CW), F32)
        for c, q_ref in enumerate((q0, q1, q2, q3)):
            dx, dg = norm_bwd(q_ref[...], gq_ref[...], dqh_ref[:, c * CW:(c + 1) * CW] * 0.125, cos_v, sin_v,
                              ones_ref)
            dq_ref[:, c * CW:(c + 1) * CW] = dx
            gq_part = gq_part + dg
        dkr = jnp.concatenate([fold(dkd_ref, 0), fold(dkd_ref, 1)], axis=1)
        dxk, gk_part = norm_bwd(k_ref[...], gk_ref[...], dkr, cos_v, sin_v, ones_ref)
        dk_ref[...] = dxk
        dv_ref[...] = jnp.concatenate([fold(dvd_ref, 0), fold(dvd_ref, 1)], axis=1)

        @pl.when(i == 0)
        def _():
            dgq_ref[...] = gq_part
            dgk_ref[...] = gk_part

        @pl.when(i > 0)
        def _():
            dgq_ref[...] += gq_part
            dgk_ref[...] += gk_part

    def chunk(col):
        return pl.BlockSpec((tm, CW), lambda i: (i, col))

    vec = pl.BlockSpec((1, CW), lambda i: (0, 0))
    tab = pl.BlockSpec((tm, CW), lambda i: (i, 0))
    return _call(body, name=name, grid=(s // tm,),
                 in_specs=[chunk(QC_COL), chunk(QC_COL + 1), chunk(QC_COL + 2), chunk(QC_COL + 3), chunk(KC_COL),
                           vec, vec, tab, tab, pl.BlockSpec((LANES, LANES), lambda i: (0, 0)),
                           pl.BlockSpec((tm, 4 * CW), lambda i: (i, 0)),
                           pl.BlockSpec((tm, 2 * CW), lambda i: (i, 0)),
                           pl.BlockSpec((tm, 2 * CW), lambda i: (i, 0))],
                 out_specs=[pl.BlockSpec((tm, 4 * CW), lambda i: (i, 0)), tab, tab, vec, vec],
                 out_shape=[_sds((s, 4 * CW)), _sds((s, CW)), _sds((s, CW)), _sds((1, CW)), _sds((1, CW))],
                 sem=("arbitrary",))(proj, proj, proj, proj, proj, gq, gk, cos, sin, ones, dqh, dkd, dvd)


def _flash_tiles(s):
    return min(512, s), min(2048, s)


def _row_iota(shape):
    return lax.broadcasted_iota(jnp.int32, shape, 0)


def _flash_fwd(qh, kd, vdt, name):
    s = qh.shape[0]
    tq, tk = _flash_tiles(s)
    nk = s // tk

    n_chunks = 1
    cw = tq // n_chunks
    units = [(t, c, e) for t in range(2) for c in range(n_chunks) for e in range(2)]

    def body(q_ref, k_ref, vt_ref, ot_ref, lse_ref, qm_ref, m_ref, lacc_ref, acc_ref):
        j = pl.program_id(2)

        @pl.when(j == 0)
        def _():
            m_ref[...] = jnp.full(m_ref.shape, MASK_VALUE, F32)
            lacc_ref[...] = jnp.zeros_like(lacc_ref)
            acc_ref[...] = jnp.zeros_like(acc_ref)
            left_q = _lane_iota((tq, LANES)) < HEAD_DIM
            for t in range(2):
                qp = q_ref[:, t * LANES:(t + 1) * LANES]
                qm_ref[2 * t] = jnp.where(left_q, qp, jnp.zeros_like(qp))
                qm_ref[2 * t + 1] = jnp.where(left_q, jnp.zeros_like(qp), qp)

        kb = k_ref[...]
        vt = vt_ref[...]
        top_k = _row_iota((LANES, tk)) < HEAD_DIM
        top_c = _row_iota((LANES, cw)) < HEAD_DIM
        vt_e = (jnp.where(top_k, vt, jnp.ones_like(vt)), jnp.where(top_k, jnp.ones_like(vt), vt))

        def scores(unit):
            t, c, e = unit
            return _dot(kb, qm_ref[2 * t + e, c * cw:(c + 1) * cw, :], NT_DIMS)

        nxt = scores(units[0])
        pv, alpha = [], []
        for n, (t, c, e) in enumerate(units):
            st = nxt
            if n + 1 < len(units):
                nxt = scores(units[n + 1])
            h = 2 * t + e
            cols = slice(c * cw, (c + 1) * cw)
            m_prev = m_ref[h, :, cols]
            m_new = jnp.maximum(m_prev, jnp.max(st, axis=0, keepdims=True))
            alpha.append(jnp.exp(m_prev - m_new))
            pt = jnp.exp(st - m_new)
            m_ref[h, :, cols] = m_new
            pv.append(_dot(vt_e[e], pt.astype(MXU_DT)))
            if e == 1:
                acc_ref[t, :, cols] = (acc_ref[t, :, cols] * jnp.where(top_c, alpha[0], alpha[1])
                                       + jnp.where(top_c, pv[0], pv[1]))
                lacc_ref[t, :, cols] = (lacc_ref[t, :, cols] * jnp.where(top_c, alpha[1], alpha[0])
                                        + jnp.where(top_c, pv[1], pv[0]))
                pv, alpha = [], []

        @pl.when(j == nk - 1)
        def _():
            for t in range(2):
                lacc = lacc_ref[t]
                l_sw = jnp.concatenate([lacc[HEAD_DIM:], lacc[:HEAD_DIM]], axis=0)
                ot_ref[t * LANES:(t + 1) * LANES, :] = acc_ref[t] / l_sw
                lse_ref[2 * t:2 * t + 1, :] = m_ref[2 * t] + jnp.log(lacc[HEAD_DIM:HEAD_DIM + 1])
                lse_ref[2 * t + 1:2 * t + 2, :] = m_ref[2 * t + 1] + jnp.log(lacc[0:1])

    return _call(body, name=name, grid=(4, s // tq, nk),
                 in_specs=[pl.BlockSpec((tq, CW), lambda g, i, j: (i, g)),
                           pl.BlockSpec((tk, LANES), lambda g, i, j: (j, g)),
                           pl.BlockSpec((LANES, tk), lambda g, i, j: (g, j))],
                 out_specs=[pl.BlockSpec((CW, tq), lambda g, i, j: (g, i)),
                            pl.BlockSpec((None, 4, tq), lambda g, i, j: (g, 0, i))],
                 out_shape=[_sds((4 * CW, s)), _sds((4, 4, s))],
                 scratch=[pltpu.VMEM((4, tq, LANES), MXU_DT), pltpu.VMEM((4, 1, tq), F32),
                          pltpu.VMEM((2, LANES, tq), F32), pltpu.VMEM((2, LANES, tq), F32)],
                 sem=("parallel", "parallel", "arbitrary"))(qh, kd, vdt)


def _flash_bwd(qh, kd, vd, kdt, do, lse, dd, name):
    s = qh.shape[0]
    tq, tk = _flash_tiles(s)
    ni = s // tq

    def body(q_ref, k_ref, v_ref, kt_ref, do_ref, lse_ref, dd_ref, dqt_ref, dk_ref, dv_ref, dk_acc, dv_acc):
        j = pl.program_id(1)
        i = pl.program_id(2)

        @pl.when(i == 0)
        def _():
            dk_acc[...] = jnp.zeros_like(dk_acc)
            dv_acc[...] = jnp.zeros_like(dv_acc)

        kb = k_ref[...]
        vb = v_ref[...]
        kt = kt_ref[...]
        left_q = _lane_iota((tq, LANES)) < HEAD_DIM
        top = _row_iota((LANES, tq)) < HEAD_DIM
        cols = pl.ds(pl.multiple_of(i * tq, tq), tq)

        def first_stage(h):
            t, e = divmod(h, 2)
            keep_q = left_q if e == 0 else jnp.logical_not(left_q)
            qp = q_ref[:, t * LANES:(t + 1) * LANES]
            dop = do_ref[:, t * LANES:(t + 1) * LANES]
            qm = jnp.where(keep_q, qp, jnp.zeros_like(qp))
            dom = jnp.where(keep_q, dop, jnp.zeros_like(dop))
            return qm, dom, _dot(kb, qm, NT_DIMS), _dot(vb, dom, NT_DIMS)

        nxt = first_stage(0)
        dqt = []
        for h in range(4):
            qm, dom, st, dpt = nxt
            if h < 3:
                nxt = first_stage(h + 1)
            pt = jnp.exp(st - lse_ref[h:h + 1, :])
            dsb = (pt * (dpt - dd_ref[h:h + 1, :])).astype(MXU_DT)
            dv_acc[...] += _dot(pt.astype(MXU_DT), dom)
            dk_acc[...] += _dot(dsb, qm)
            dqt.append(_dot(kt, dsb))
            if h % 2 == 1:
                t = h // 2
                dq_t = jnp.where(top, dqt[0], dqt[1])
                dqt = []

                @pl.when(j == 0)
                def _():
                    dqt_ref[t * LANES:(t + 1) * LANES, cols] = dq_t

                @pl.when(j > 0)
                def _():
                    dqt_ref[t * LANES:(t + 1) * LANES, cols] += dq_t

        @pl.when(i == ni - 1)
        def _():
            dk_ref[...] = dk_acc[...]
            dv_ref[...] = dv_acc[...]

    qspec = pl.BlockSpec((tq, CW), lambda g, j, i: (i, g))
    kspec = pl.BlockSpec((tk, LANES), lambda g, j, i: (j, g))
    rowspec = pl.BlockSpec((None, 4, tq), lambda g, j, i: (g, 0, i))
    return _call(body, name=name, grid=(4, s // tk, ni),
                 in_specs=[qspec, kspec, kspec, pl.BlockSpec((LANES, tk), lambda g, j, i: (g, j)), qspec,
                           rowspec, rowspec],
                 out_specs=[pl.BlockSpec((CW, s), lambda g, j, i: (g, 0)), kspec, kspec],
                 out_shape=[_sds((4 * CW, s)), _sds((s, 2 * CW)), _sds((s, 2 * CW))],
                 scratch=[pltpu.VMEM((tk, LANES), F32), pltpu.VMEM((tk, LANES), F32)],
                 sem=("parallel", "arbitrary", "arbitrary"))(qh, kd, vd, kdt, do, lse, dd)


def _groupnorm_fwd(oa, ob, oc, ga, gb, gc, name):
    s = oa.shape[0]
    tm = _row_tile(s)
    wa, wb, wc = oa.shape[1], ob.shape[1], oc.shape[1]

    def body(oa_ref, ob_ref, oc_ref, ga_ref, gb_ref, gc_ref, mix_ref):
        off = 0
        for o_ref, g_ref, w in ((oa_ref, ga_ref, wa), (ob_ref, gb_ref, wb), (oc_ref, gc_ref, wc)):
            xv = o_ref[...]
            r = lax.rsqrt(jnp.mean(xv * xv, axis=-1, keepdims=True) + EPS)
            mix_ref[:, off:off + w] = ((xv * r) * g_ref[...]).astype(mix_ref.dtype)
            off += w

    def row(w):
        return pl.BlockSpec((tm, w), lambda i: (i, 0))

    def vec(w):
        return pl.BlockSpec((1, w), lambda i: (0, 0))

    return _call(body, name=name, grid=(s // tm,),
                 in_specs=[row(wa), row(wb), row(wc), vec(wa), vec(wb), vec(wc)],
                 out_specs=row(wa + wb + wc), out_shape=_sds((s, wa + wb + wc), MXU_DT),
                 sem=("parallel",))(oa, ob, oc, ga.reshape(1, wa), gb.reshape(1, wb), gc.reshape(1, wc))


def _groupnorm_bwd(dmix, oa, ob, oc, ga, gb, gc, ones, name):
    s = oa.shape[0]
    tm = _row_tile(s)
    wa, wb, wc = oa.shape[1], ob.shape[1], oc.shape[1]

    def body(dm_ref, oa_ref, ob_ref, oc_ref, ga_ref, gb_ref, gc_ref, ones_ref,
             doa_ref, dob_ref, doc_ref, docb_ref, dd_ref, dga_ref, dgb_ref, dgc_ref):
        i = pl.program_id(0)
        off = 0
        parts = []
        for o_ref, g_ref, do_ref, w in ((oa_ref, ga_ref, doa_ref, wa), (ob_ref, gb_ref, dob_ref, wb),
                                        (oc_ref, gc_ref, doc_ref, wc)):
            xv = o_ref[...]
            dh = dm_ref[:, off:off + w]
            r = lax.rsqrt(jnp.mean(xv * xv, axis=-1, keepdims=True) + EPS)
            xh = xv * r
            gd = dh * g_ref[...]
            c = jnp.mean(gd * xh, axis=-1, keepdims=True)
            dx = r * (gd - xh * c)
            do_ref[...] = dx
            parts.append(jnp.sum(dh * xh, axis=0, keepdims=True))
            if o_ref is oc_ref:
                docb_ref[...] = dx.astype(docb_ref.dtype)
                dd_ref[...] = _segsum64(dx * xv, ones_ref)
            off += w

        @pl.when(i == 0)
        def _():
            dga_ref[...], dgb_ref[...], dgc_ref[...] = parts

        @pl.when(i > 0)
        def _():
            dga_ref[...] += parts[0]
            dgb_ref[...] += parts[1]
            dgc_ref[...] += parts[2]

    def row(w):
        return pl.BlockSpec((tm, w), lambda i: (i, 0))

    def vec(w):
        return pl.BlockSpec((1, w), lambda i: (0, 0))

    return _call(body, name=name, grid=(s // tm,),
                 in_specs=[row(wa + wb + wc), row(wa), row(wb), row(wc), vec(wa), vec(wb), vec(wc),
                           pl.BlockSpec((LANES, LANES), lambda i: (0, 0))],
                 out_specs=[row(wa), row(wb), row(wc), row(wc), row(wc), vec(wa), vec(wb), vec(wc)],
                 out_shape=[_sds((s, wa)), _sds((s, wb)), _sds((s, wc)), _sds((s, wc), MXU_DT), _sds((s, wc)),
                            _sds((1, wa)), _sds((1, wb)), _sds((1, wc))],
                 sem=("arbitrary",))(dmix, oa, ob, oc, ga.reshape(1, wa), gb.reshape(1, wb), gc.reshape(1, wc), ones)


def _adam_math(w, g, m, v):
    m = ADAM_B1 * m + (1.0 - ADAM_B1) * g
    v = ADAM_B2 * v + (1.0 - ADAM_B2) * jnp.square(g)
    m_hat = m / (1.0 - ADAM_B1 ** ADAM_STEP)
    v_hat = v / (1.0 - ADAM_B2 ** ADAM_STEP)
    delta = -ADAM_LR * (m_hat / (jnp.sqrt(v_hat) + ADAM_EPS) + ADAM_WD * w)
    return delta, m, v


def _mesh_pos():
    return lax.axis_index("x"), lax.axis_index("y"), lax.axis_index("c")


def _peer_chips(x, y):
    return [(1 - x, y), (x, 1 - y), (1 - x, 1 - y)]


def _allreduce_small_adam(g, w, m, v):
    rows = g.shape[0]

    def body(g_ref, w_ref, m_ref, v_ref, gs_ref, d_ref, mo_ref, vo_ref, buf, send_sems, recv_sems):
        x, y, c = _mesh_pos()
        me = 4 * x + 2 * y + c
        buf[me] = g_ref[...]
        copies = []
        for k in range(1, 8):
            px = 1 - x if (k >> 2) & 1 else x
            py = 1 - y if (k >> 1) & 1 else y
            pc = 1 - c if k & 1 else c
            cp = pltpu.make_async_remote_copy(src_ref=g_ref, dst_ref=buf.at[me], send_sem=send_sems.at[k - 1],
                                              recv_sem=recv_sems.at[k - 1], device_id=(px, py, pc),
                                              device_id_type=MESH_ID)
            cp.start()
            copies.append(cp)
        for cp in copies:
            cp.wait()
        total = buf[0]
        for d in range(1, 8):
            total = total + buf[d]
        gs_ref[...] = total
        d_ref[...], mo_ref[...], vo_ref[...] = _adam_math(w_ref[...], total, m_ref[...], v_ref[...])

    vm = pl.BlockSpec(memory_space=pltpu.VMEM)
    return _call(body, name="allreduce_small_adam", in_specs=[vm] * 4, out_specs=[vm] * 4,
                 out_shape=[_sds((rows, LANES))] * 4,
                 scratch=[pltpu.VMEM((8, rows, LANES), F32), pltpu.SemaphoreType.DMA((7,)),
                          pltpu.SemaphoreType.DMA((7,))])(g, w, m, v)


HBM_SPEC = pl.BlockSpec(memory_space=pltpu.HBM)
SEM_SPEC = pl.BlockSpec(memory_space=pltpu.SEMAPHORE)
VMEM_SPEC = pl.BlockSpec(memory_space=pltpu.VMEM)
SIDE_EFFECT = pltpu.SideEffectType.DATAFLOW_SIDE_EFFECTING
N_PEERS = 7


def _in_hbm(a):
    return pltpu.with_memory_space_constraint(a, pltpu.HBM)


def _landing(shape, dtype):
    return _in_hbm(lax.empty(shape, dtype))


def _token_shape():
    return _sds((8, LANES))


def _gather_start(shards):
    n = len(shards)
    n_layers = shards[0].shape[0]
    jobs = [(l, t) for l in range(n_layers) for t in range(n)]
    nj = len(jobs)

    def body(*refs):
        sh = refs[:n]
        outs = refs[n + nj:]
        send, recv, land, token = outs[:nj], outs[nj:2 * nj], outs[2 * nj:3 * nj], outs[3 * nj]
        x, y, c = _mesh_pos()
        me = 2 * x + y
        for j, (l, t) in enumerate(jobs):
            for k, (px, py) in enumerate(_peer_chips(x, y)):
                pltpu.make_async_remote_copy(src_ref=sh[t].at[l], dst_ref=land[j].at[me], send_sem=send[j].at[k],
                                             recv_sem=recv[j].at[k], device_id=(px, py, c),
                                             device_id_type=MESH_ID).start()
        token[...] = jnp.zeros_like(token)

    lands = [_landing((4,) + shards[t].shape[1:], shards[t].dtype) for _, t in jobs]
    res = pl.pallas_call(
        body, name="gather_start",
        out_shape=tuple([pltpu.SemaphoreType.DMA((3,))] * (2 * nj)
                        + [pltpu.HBM(a.shape, a.dtype) for a in lands] + [_token_shape()]),
        in_specs=[HBM_SPEC] * (n + nj), out_specs=tuple([SEM_SPEC] * (2 * nj) + [HBM_SPEC] * nj + [VMEM_SPEC]),
        input_output_aliases={n + j: 2 * nj + j for j in range(nj)},
        compiler_params=pltpu.CompilerParams(has_side_effects=SIDE_EFFECT),
    )(*[_in_hbm(a) for a in shards], *lands)
    return jobs, res[:nj], res[nj:2 * nj], res[2 * nj:3 * nj], res[3 * nj]


def _gather_wait(shard, layer, land, send_sem, recv_sem, after, name):
    def body(sh_ref, land_ref, send_ref, recv_ref, after_ref, land_out):
        x, y, c = _mesh_pos()
        for k in range(3):
            cp = pltpu.make_async_remote_copy(src_ref=sh_ref.at[layer], dst_ref=land_ref.at[k],
                                              send_sem=send_ref.at[k], recv_sem=recv_ref.at[k],
                                              device_id=(x, y, 1 - c), device_id_type=MESH_ID)
            cp.wait_send()
            cp.wait_recv()

    return pl.pallas_call(
        body, name=name, out_shape=pltpu.HBM(land.shape, land.dtype),
        in_specs=[HBM_SPEC, HBM_SPEC, SEM_SPEC, SEM_SPEC, ANY], out_specs=HBM_SPEC,
        input_output_aliases={1: 0},
        compiler_params=pltpu.CompilerParams(has_side_effects=SIDE_EFFECT),
    )(shard, land, send_sem, recv_sem, after)


def _grad_start(g, name):
    def body(g_ref, land_in, send, recv, land, token):
        x, y, c = _mesh_pos()
        me = 2 * x + y
        pltpu.make_async_remote_copy(src_ref=g_ref.at[me], dst_ref=land.at[0], send_sem=send.at[0],
                                     recv_sem=recv.at[0], device_id=(x, y, 1 - c), device_id_type=MESH_ID).start()
        for k, (px, py) in enumerate(_peer_chips(x, y)):
            for c2 in range(2):
                pltpu.make_async_remote_copy(src_ref=g_ref.at[2 * px + py], dst_ref=land.at[1 + 2 * k + c],
                                             send_sem=send.at[1 + 2 * k + c2], recv_sem=recv.at[1 + 2 * k + c],
                                             device_id=(px, py, c2), device_id_type=MESH_ID).start()
        token[...] = jnp.zeros_like(token)

    land = _landing((N_PEERS,) + g.shape[1:], g.dtype)
    return pl.pallas_call(
        body, name=name,
        out_shape=(pltpu.SemaphoreType.DMA((N_PEERS,)), pltpu.SemaphoreType.DMA((N_PEERS,)),
                   pltpu.HBM(land.shape, land.dtype), _token_shape()),
        in_specs=[HBM_SPEC, HBM_SPEC], out_specs=(SEM_SPEC, SEM_SPEC, HBM_SPEC, VMEM_SPEC),
        input_output_aliases={1: 2},
        compiler_params=pltpu.CompilerParams(has_side_effects=SIDE_EFFECT),
    )(_in_hbm(g), land)


def _grad_wait(g, land, send_sem, recv_sem, after, name):
    def body(g_ref, land_ref, send_ref, recv_ref, after_ref, land_out):
        x, y, c = _mesh_pos()
        for k in range(N_PEERS):
            cp = pltpu.make_async_remote_copy(src_ref=g_ref.at[0], dst_ref=land_ref.at[k], send_sem=send_ref.at[k],
                                              recv_sem=recv_ref.at[k], device_id=(x, y, 1 - c),
                                              device_id_type=MESH_ID)
            cp.wait_send()
            cp.wait_recv()

    return pl.pallas_call(
        body, name=name, out_shape=pltpu.HBM(land.shape, land.dtype),
        in_specs=[HBM_SPEC, HBM_SPEC, SEM_SPEC, SEM_SPEC, ANY], out_specs=HBM_SPEC,
        input_output_aliases={1: 0},
        compiler_params=pltpu.CompilerParams(has_side_effects=SIDE_EFFECT),
    )(g, land, send_sem, recv_sem, after)


def _sum_adam(g, land, w, m, v, prev, layer, me_idx, name):
    _, r, cols = g.shape
    tr = min(128, r)

    def body(me_ref, g_ref, l0, l1, l2, l3, l4, l5, l6, w_ref, m_ref, v_ref, p0, p1, p2, p3,
             go_ref, d_ref, mo_ref, vo_ref):
        total = g_ref[...].astype(F32) + l0[...].astype(F32)
        for ref in (l1, l2, l3, l4, l5, l6):
            total = total + ref[...].astype(F32)
        go_ref[...] = total
        d_ref[...], mo_ref[...], vo_ref[...] = _adam_math(w_ref[...], total, m_ref[...], v_ref[...])

    def slot(k):
        return pl.BlockSpec((None, tr, cols), lambda i, me: (k, i, 0))

    lay = pl.BlockSpec((None, tr, cols), lambda i, me: (layer, i, 0))
    return _call(body, name=name, grid=(r // tr,), prefetch=1,
                 in_specs=[pl.BlockSpec((None, tr, cols), lambda i, me: (me[0], i, 0))]
                 + [slot(k) for k in range(N_PEERS)] + [lay, lay, lay] + [ANY] * 4,
                 out_specs=[lay] * 4, out_shape=[_sds(w.shape)] * 4,
                 aliases={12 + k: k for k in range(4)}, sem=("parallel",))(
                     me_idx, g, *([land] * N_PEERS), w, m, v, *prev)


def _t5_bucket(rel):
    nb = T5_BUCKETS // 2
    max_exact = nb // 2
    base = jnp.where(rel > 0, nb, 0)
    n = jnp.abs(rel)
    nf = jnp.maximum(n, 1).astype(F32)
    large = max_exact + (jnp.log(nf / max_exact) / math.log(T5_MAX_DIST / max_exact)
                         * (nb - max_exact)).astype(jnp.int32)
    large = jnp.minimum(large, nb - 1)
    return base + jnp.where(n < max_exact, n, large)


def _a_bias_maps():
    v = jnp.arange(3)[:, None, None]
    q = jnp.arange(128)[None, :, None]
    k = jnp.arange(384)[None, None, :]
    rel = k - 128 * v - q
    valid = jnp.abs(rel) <= 128
    onehot = (_t5_bucket(rel)[..., None] == jnp.arange(T5_BUCKETS)).astype(F32)
    return onehot * valid[..., None].astype(F32), valid


def _b_bias_maps():
    v = jnp.arange(8)[:, None]
    i = jnp.arange(NA_ROWS)[None, :]
    dr = jnp.where(v == 4, i + 3, i - v + 7)
    row_oh = (dr[..., None] == jnp.arange(2 * NA_ROWS - 1)).astype(F32)
    q = jnp.arange(GRID_W)[:, None]
    kc = jnp.arange(GRID_W)[None, :]
    cs = jnp.clip(q - 8, 0, GRID_W - 16)
    valid = (kc >= cs) & (kc < cs + 16)
    col_oh = ((kc - q + 15)[..., None] == jnp.arange(31)).astype(F32) * valid[..., None].astype(F32)
    return row_oh, col_oh, valid


def _rope_tables(s):
    t = jnp.arange(s)
    row = (t // GRID_W).astype(F32)
    col = (t % GRID_W).astype(F32)
    axis_dim = HEAD_DIM // 2
    freqs = ROPE_THETA ** (-jnp.arange(0, axis_dim, 2, dtype=F32) / axis_dim)
    ang_row = row[:, None] * freqs[None, :]
    ang_col = col[:, None] * freqs[None, :]
    cos = jnp.concatenate([jnp.cos(ang_row)] * 2 + [jnp.cos(ang_col)] * 2, axis=1)
    sin = jnp.concatenate([-jnp.sin(ang_row), jnp.sin(ang_row), -jnp.sin(ang_col), jnp.sin(ang_col)], axis=1)
    return jnp.tile(cos, (1, CW // HEAD_DIM)), jnp.tile(sin, (1, CW // HEAD_DIM))


def _pack(parts, rows):
    flat = jnp.concatenate([p.reshape(-1).astype(F32) for p in parts])
    return jnp.pad(flat, (0, rows * LANES - flat.shape[0])).reshape(rows, LANES)


def _unpack(buf, shapes):
    flat = buf.reshape(-1)
    out, off = [], 0
    for shp in shapes:
        size = math.prod(shp)
        out.append(flat[off:off + size].reshape(shp))
        off += size
    return out


def kernel(x, norm_mix, w_in, a_sink, t5_table, b_rpb, c_q_gain, c_k_gain, out_gain_a, out_gain_b, out_gain_c, w_o, norm_mlp, w_up, w_down, norm_final, loss_target, m_norm_mix, m_w_in, m_a_sink, m_t5_table, m_b_rpb, m_c_q_gain, m_c_k_gain, m_out_gain_a, m_out_gain_b, m_out_gain_c, m_w_o, m_norm_mlp, m_w_up, m_w_down, m_norm_final, v_norm_mix, v_w_in, v_a_sink, v_t5_table, v_b_rpb, v_c_q_gain, v_c_k_gain, v_out_gain_a, v_out_gain_b, v_out_gain_c, v_w_o, v_norm_mlp, v_w_up, v_w_down, v_norm_final):
    n_layers = w_in.shape[0]
    s, d = x.shape[1], x.shape[2]
    d_ff = 4 * w_up.shape[2]
    in_w = 4 * w_in.shape[2]
    xs = x.reshape(s, d)
    target = loss_target.reshape(s, d)
    cfg_a, cfg_b = _cfg_a(s), _cfg_b(s)

    x_i, y_i, _ = _mesh_pos()
    me_chip = 2 * x_i + y_i
    me_idx = me_chip.astype(jnp.int32).reshape(1)
    w_bf = [w_in.astype(MXU_DT), w_o.astype(MXU_DT), w_up.astype(MXU_DT), w_down.astype(MXU_DT)]
    jobs, gather_send, gather_recv, gather_land, gather_token = _gather_start(w_bf)
    job_of = {job: j for j, job in enumerate(jobs)}
    ff_shard = w_up.shape[2]

    def gathered(l, t, after):
        j = job_of[(l, t)]
        land = _gather_wait(w_bf[t], l, gather_land[j], gather_send[j], gather_recv[j], after,
                            "gather_wait_%d_%d" % (l, t))
        return lax.dynamic_update_slice(land, w_bf[t][l][None], (me_chip, 0, 0))

    ones = _pair_ones()
    cos_t, sin_t = _rope_tables(s)
    a_onehot, a_valid = _a_bias_maps()
    bias_a = jnp.where(a_valid[:, None], jnp.einsum("vqkb,bh->vhqk", a_onehot, t5_table, precision=HIGHEST),
                       MASK_VALUE)
    row_oh, col_oh, b_valid = _b_bias_maps()
    sink_b = jnp.full((4, 1, LANES), MASK_VALUE, F32)

    def b_bias(rpb):
        t = jnp.einsum("hrz,vir->vhiz", rpb, row_oh, precision=HIGHEST)
        t = jnp.einsum("vhiz,qcz->vhqic", t, col_oh, precision=HIGHEST)
        t = jnp.where(b_valid[None, None, :, None, :], t, MASK_VALUE)
        return t.reshape(8, 8, GRID_W, NA_ROWS * GRID_W)

    def tile_gain(gvec):
        return jnp.tile(gvec, CW // HEAD_DIM).reshape(1, CW)

    def pad_sink(svec):
        return jnp.pad(svec, (0, LANES - svec.shape[0])).reshape(1, 1, LANES)

    saved = []
    xc = xs
    for l in range(n_layers):
        h1 = _rms_fwd(xc, norm_mix[l] + gather_token[0, 0] if l == 0 else norm_mix[l], "rms_mix")
        wf_in = gathered(l, 0, h1).transpose(1, 0, 2).reshape(d, in_w)
        proj = _matmul(h1, wf_in, mode="nn", name="proj_in", tm=1024, tn=768, tk=2048)
        bias_b = b_bias(b_rpb[l])
        oa = _local_attn_fwd(proj, bias_a, pad_sink(a_sink[l]), cfg_a, "attn_a_fwd")
        ob = _local_attn_fwd(proj, bias_b, sink_b, cfg_b, "attn_b_fwd")
        gq, gk = tile_gain(c_q_gain[l]), tile_gain(c_k_gain[l])
        qh, kd, vd = _cprep_fwd(proj, gq, gk, cos_t, sin_t, ones, "cprep_fwd")
        kdt, vdt = kd.T, vd.T
        oct, lse = _flash_fwd(qh, kd, vdt, "attn_c_fwd")
        oc = oct.T
        mix = _groupnorm_fwd(oa, ob, oc, out_gain_a[l], out_gain_b[l], out_gain_c[l], "groupnorm_fwd")
        wf_o = gathered(l, 1, mix).reshape(d, d)
        x_mid = _matmul(mix, wf_o, mode="nn", name="proj_out", tm=1024, tn=1024, tk=2048, epi="res",
                        extra=(xc,))
        h2 = _rms_fwd(x_mid, norm_mlp[l], "rms_mlp")
        wg_up = gathered(l, 2, h2)
        nb_up = ff_shard // 1024
        u, uu = _matmul(h2, wg_up, mode="nn", name="mlp_up", tm=1024, tn=1024, tk=2048, epi="relu2",
                        out_dtypes=(F32, MXU_DT), mkn=(s, d, d_ff),
                        b_spec=pl.BlockSpec((None, 2048, 1024), lambda i, j, kk: (j // nb_up, kk, j % nb_up)))
        wf_down = gathered(l, 3, uu).reshape(d_ff, d)
        x_out = _matmul(uu, wf_down, mode="nn", name="mlp_down", tm=1024, tn=1024, tk=2048, epi="res",
                        extra=(x_mid,))
        saved.append((xc, h1, proj, bias_b, oa, ob, qh, kd, vd, kdt, oc, lse, mix, x_mid, h2, u, uu,
                      wf_in, wf_o, wg_up, wf_down))
        xc = x_out

    loss_part, dx, dxb, dg_final = _final_loss(xc, norm_final, target, "final_loss")

    small = {k: [] for k in ("norm_mix", "a_sink", "b_rpb", "cq", "ck", "oga", "ogb", "ogc", "norm_mlp")}
    dbias_a_total = jnp.zeros_like(bias_a)
    big_w = {"w_in": (w_in, m_w_in, v_w_in), "w_o": (w_o, m_w_o, v_w_o), "w_up": (w_up, m_w_up, v_w_up),
             "w_down": (w_down, m_w_down, v_w_down)}
    big = {nm: [lax.empty(wmv[0].shape, F32) for _ in range(4)] for nm, wmv in big_w.items()}

    def send_grad(nm, l, g):
        send, recv, land, token = _grad_start(g, "grad_start_%s_%d" % (nm, l))
        return (nm, l, g, send, recv, land), token[0, 0]

    def finish_grads(pending, after):
        for nm, l, g, send, recv, land in pending:
            land = _grad_wait(g, land, send, recv, after, "grad_wait_%s_%d" % (nm, l))
            wmv = big_w[nm]
            big[nm] = _sum_adam(g, land, wmv[0], wmv[1], wmv[2], big[nm], l, me_idx, "sum_adam_%s_%d" % (nm, l))

    pending = []
    for l in reversed(range(n_layers)):
        (xin, h1, proj, bias_b, oa, ob, qh, kd, vd, kdt, oc, lse, mix, x_mid, h2, u, uu,
         wf_in, wf_o, wg_up, wf_down) = saved[l]
        started = []
        du = _matmul(dxb, wf_down, mode="nt", name="mlp_down_dgrad", tm=1024, tn=1024, tk=2048, epi="mul2u",
                     extra=(u,), out_dtypes=(MXU_DT,))
        gw = _matmul(uu, dxb, mode="tn", name="mlp_down_wgrad", tm=1024, tn=1024, tk=1024, out_dtypes=(GRAD_DT,))
        rec, tok_down = send_grad("w_down", l, gw.reshape(4, d_ff // 4, d))
        started.append(rec)
        nbk = ff_shard // 2048
        dh2 = _matmul(du, wg_up, mode="nt", name="mlp_up_dgrad", tm=1024, tn=1024, tk=2048,
                      mkn=(s, d_ff, d),
                      b_spec=pl.BlockSpec((None, 1024, 2048), lambda i, j, kk: (kk // nbk, j, kk % nbk)))
        nbo = ff_shard // 1024
        gw = _matmul(h2, du, mode="tn", name="mlp_up_wgrad", tm=1024, tn=1024, tk=1024, out_dtypes=(GRAD_DT,),
                     out_spec=pl.BlockSpec((None, 1024, 1024), lambda i, j, kk: (j // nbo, i, j % nbo)),
                     out_shape=(4, d, ff_shard))
        rec, tok_up = send_grad("w_up", l, gw)
        started.append(rec)
        dx_mid, dxmb, dg = _rms_bwd(x_mid, norm_mlp[l] + (tok_down + tok_up), dh2, dx, "rms_mlp_bwd")
        small["norm_mlp"].append(dg)
        dmix = _matmul(dxmb, wf_o, mode="nt", name="proj_out_dgrad", tm=1024, tn=1024, tk=2048)
        gw = _matmul(mix, dxmb, mode="tn", name="proj_out_wgrad", tm=1024, tn=1024, tk=1024, out_dtypes=(GRAD_DT,))
        rec, tok_o = send_grad("w_o", l, gw.reshape(4, d // 4, d))
        started.append(rec)
        doa, dob, doc, docb, ddc, dga, dgb, dgc = _groupnorm_bwd(
            dmix, oa, ob, oc, out_gain_a[l] + tok_o, out_gain_b[l], out_gain_c[l], ones, "groupnorm_bwd")
        small["oga"].append(dga)
        small["ogb"].append(dgb)
        small["ogc"].append(dgc)
        dqa, dka, dva, dbias_a, dsink = _local_attn_bwd(proj, bias_a, pad_sink(a_sink[l]), doa, cfg_a, "attn_a_bwd")
        dbias_a_total = dbias_a_total + dbias_a
        small["a_sink"].append(dsink[0, 0, :a_sink.shape[1]])
        dqb, dkb, dvb, dbias_b, _ = _local_attn_bwd(proj, bias_b, sink_b, dob, cfg_b, "attn_b_bwd")
        db5 = jnp.where(b_valid[None, None, :, None, :], dbias_b.reshape(8, 8, GRID_W, NA_ROWS, GRID_W), 0.0)
        t = jnp.einsum("vhqic,qcz->vhiz", db5, col_oh, precision=HIGHEST)
        small["b_rpb"].append(jnp.einsum("vhiz,vir->hrz", t, row_oh, precision=HIGHEST))
        dd_rows = ddc.reshape(s, 16, HEAD_DIM)[:, :, 0].T.reshape(4, 4, s)
        dqht, dkd, dvd = _flash_bwd(qh, kd, vd, kdt, docb, lse, dd_rows, "attn_c_bwd")
        dqh = dqht.T
        gq, gk = tile_gain(c_q_gain[l]), tile_gain(c_k_gain[l])
        dqc, dkc, dvc, dgq, dgk = _cprep_bwd(proj, gq, gk, cos_t, sin_t, ones, dqh, dkd, dvd, "cprep_bwd")
        small["cq"].append(dgq.reshape(CW // HEAD_DIM, HEAD_DIM).sum(0))
        small["ck"].append(dgk.reshape(CW // HEAD_DIM, HEAD_DIM).sum(0))
        dproj = jnp.concatenate([dqa, dka, dva, dqb, dkb, dvb, dqc, dkc, dvc], axis=1).astype(MXU_DT)
        dh1 = _matmul(dproj, wf_in, mode="nt", name="proj_in_dgrad", tm=1024, tn=1024, tk=1920)
        gw = _matmul(h1, dproj, mode="tn", name="proj_in_wgrad", tm=1024, tn=768, tk=1024, out_dtypes=(GRAD_DT,))
        rec, tok_in = send_grad("w_in", l, gw.reshape(d, 4, in_w // 4).transpose(1, 0, 2))
        started.append(rec)
        dx, dxb, dg = _rms_bwd(xin, norm_mix[l] + tok_in, dh1, dx_mid, "rms_mix_bwd")
        small["norm_mix"].append(dg)
        finish_grads(pending, dx)
        pending = started
    finish_grads(pending, dx)

    for lst in small.values():
        lst.reverse()

    dt5 = jnp.einsum("vhqk,vqkb->bh", dbias_a_total, a_onehot, precision=HIGHEST)
    small_names = ["norm_mix", "a_sink", "t5_table", "b_rpb", "c_q_gain", "c_k_gain", "out_gain_a", "out_gain_b",
                   "out_gain_c", "norm_mlp", "norm_final"]
    small_w = [norm_mix, a_sink, t5_table, b_rpb, c_q_gain, c_k_gain, out_gain_a, out_gain_b, out_gain_c, norm_mlp,
               norm_final]
    small_m = [m_norm_mix, m_a_sink, m_t5_table, m_b_rpb, m_c_q_gain, m_c_k_gain, m_out_gain_a, m_out_gain_b,
               m_out_gain_c, m_norm_mlp, m_norm_final]
    small_v = [v_norm_mix, v_a_sink, v_t5_table, v_b_rpb, v_c_q_gain, v_c_k_gain, v_out_gain_a, v_out_gain_b,
               v_out_gain_c, v_norm_mlp, v_norm_final]
    small_g = [jnp.stack(small["norm_mix"]), jnp.stack(small["a_sink"]), dt5, jnp.stack(small["b_rpb"]),
               jnp.stack(small["cq"]), jnp.stack(small["ck"]), jnp.stack(small["oga"]), jnp.stack(small["ogb"]),
               jnp.stack(small["ogc"]), jnp.stack(small["norm_mlp"]), dg_final]
    shapes = [w.shape for w in small_w]
    total = sum(math.prod(shp) for shp in shapes) + 1
    rows = -(-total // (8 * LANES)) * 8
    one = [jnp.ones((1,), F32)]
    gs, dl, mo, vo = _allreduce_small_adam(_pack(small_g + [loss_part[0, :1]], rows), _pack(small_w + one, rows),
                                           _pack(small_m + one, rows), _pack(small_v + one, rows))
    sg = _unpack(gs, shapes + [(1,)])
    sd, sm, sv = _unpack(dl, shapes), _unpack(mo, shapes), _unpack(vo, shapes)
    loss = sg[-1].reshape(())

    by_name = {nm: (sg[i], sd[i], sm[i], sv[i]) for i, nm in enumerate(small_names)}
    by_name.update(big)
    order = ["norm_mix", "w_in", "a_sink", "t5_table", "b_rpb", "c_q_gain", "c_k_gain", "out_gain_a", "out_gain_b",
             "out_gain_c", "w_o", "norm_mlp", "w_up", "w_down", "norm_final"]
    outs = [loss, dx.reshape(x.shape)]
    for field in range(4):
        outs.extend(by_name[nm][field] for nm in order)
    return tuple(outs)
```

```python
import functools
import math

import jax
import jax.numpy as jnp
from jax import lax
from jax.experimental import pallas as pl
from jax.experimental.pallas import tpu as pltpu

F32 = jnp.float32
MXU_DT = jnp.bfloat16
GRAD_DT = jnp.bfloat16
HIGHEST = lax.Precision.HIGHEST

HEAD_DIM = 64
LANES = 128
EPS = 1e-6
MASK_VALUE = -1e30
GRID_W = 64
NA_ROWS = 8
T5_BUCKETS = 32
T5_MAX_DIST = 128
ROPE_THETA = 10000.0
ADAM_LR, ADAM_B1, ADAM_B2, ADAM_EPS, ADAM_WD, ADAM_STEP = 0.001, 0.9, 0.999, 1e-08, 0.01, 10
VMEM_LIMIT = 56 * 1024 * 1024

MESH_ID = pl.DeviceIdType.MESH
ANY = pl.BlockSpec(memory_space=pl.ANY)

NT_DIMS = (((1,), (1,)), ((), ()))
TN_DIMS = (((0,), (0,)), ((), ()))
NN_DIMS = (((1,), (0,)), ((), ()))


def _dot(a, b, dims=NN_DIMS):
    return lax.dot_general(a, b, dims, preferred_element_type=F32)


def _call(body, *, name, out_shape, grid=(), in_specs=None, out_specs=None, scratch=(), sem=None,
          prefetch=0, aliases=None):
    params = {"vmem_limit_bytes": VMEM_LIMIT}
    if sem is not None:
        params["dimension_semantics"] = sem
    kwargs = {}
    if aliases:
        kwargs["input_output_aliases"] = aliases
    if prefetch:
        spec = pltpu.PrefetchScalarGridSpec(num_scalar_prefetch=prefetch, grid=grid, in_specs=in_specs,
                                            out_specs=out_specs, scratch_shapes=list(scratch))
        return pl.pallas_call(body, grid_spec=spec, out_shape=out_shape, name=name,
                              compiler_params=pltpu.CompilerParams(**params), **kwargs)
    return pl.pallas_call(body, grid=grid, in_specs=in_specs, out_specs=out_specs, out_shape=out_shape,
                          scratch_shapes=list(scratch), name=name,
                          compiler_params=pltpu.CompilerParams(**params), **kwargs)


def _sds(shape, dtype=F32):
    return jax.ShapeDtypeStruct(tuple(shape), dtype)


def _matmul(a, b, *, mode, name, tm, tn, tk, epi="plain", extra=(), out_dtypes=(F32,), mkn=None,
            b_spec=None, out_spec=None, out_shape=None):
    if mkn is None:
        if mode == "nn":
            (m, k), n = a.shape, b.shape[1]
        elif mode == "nt":
            (m, k), n = a.shape, b.shape[0]
        else:
            (k, m), n = a.shape, b.shape[1]
    else:
        m, k, n = mkn
    tm, tn, tk = min(tm, m), min(tn, n), min(tk, k)
    assert m % tm == 0 and n % tn == 0 and k % tk == 0, (name, m, n, k, tm, tn, tk)
    nk = k // tk
    dims = {"nn": NN_DIMS, "nt": NT_DIMS, "tn": TN_DIMS}[mode]
    n_extra, n_out = len(extra), len(out_dtypes)

    def body(a_ref, b_ref, *rest):
        extra_refs = rest[:n_extra]
        out_refs = rest[n_extra:n_extra + n_out]
        acc_ref = rest[n_extra + n_out]
        kk = pl.program_id(2)

        @pl.when(kk == 0)
        def _():
            acc_ref[...] = jnp.zeros_like(acc_ref)

        acc_ref[...] += _dot(a_ref[...].astype(MXU_DT), b_ref[...].astype(MXU_DT), dims)

        @pl.when(kk == nk - 1)
        def _():
            acc = acc_ref[...]
            if epi == "plain":
                out_refs[0][...] = acc.astype(out_refs[0].dtype)
            elif epi == "res":
                out_refs[0][...] = (extra_refs[0][...] + acc).astype(out_refs[0].dtype)
            elif epi == "relu2":
                u = jnp.maximum(acc, 0.0)
                out_refs[0][...] = u.astype(out_refs[0].dtype)
                out_refs[1][...] = (u * u).astype(out_refs[1].dtype)
            elif epi == "mul2u":
                out_refs[0][...] = (2.0 * extra_refs[0][...] * acc).astype(out_refs[0].dtype)
            else:
                raise ValueError(epi)

    if mode == "tn":
        a_spec = pl.BlockSpec((tk, tm), lambda i, j, kk: (kk, i))
    else:
        a_spec = pl.BlockSpec((tm, tk), lambda i, j, kk: (i, kk))
    if b_spec is None:
        if mode == "nt":
            b_spec = pl.BlockSpec((tn, tk), lambda i, j, kk: (j, kk))
        else:
            b_spec = pl.BlockSpec((tk, tn), lambda i, j, kk: (kk, j))
    mn_spec = pl.BlockSpec((tm, tn), lambda i, j, kk: (i, j))
    if out_spec is None:
        out_spec = mn_spec
    if out_shape is None:
        out_shape = (m, n)
    res = _call(body, name=name, grid=(m // tm, n // tn, nk),
                in_specs=[a_spec, b_spec] + [mn_spec] * n_extra,
                out_specs=[out_spec] * n_out,
                out_shape=[_sds(out_shape, d) for d in out_dtypes],
                scratch=[pltpu.VMEM((tm, tn), F32)],
                sem=("parallel", "parallel", "arbitrary"))(a, b, *extra)
    return res if n_out > 1 else res[0]


def _row_tile(s):
    return min(512, s)


def _rms_fwd(x, g, name):
    s, d = x.shape
    tm = _row_tile(s)

    def body(x_ref, g_ref, h_ref):
        xv = x_ref[...]
        r = lax.rsqrt(jnp.mean(xv * xv, axis=-1, keepdims=True) + EPS)
        h_ref[...] = ((xv * r) * g_ref[...]).astype(h_ref.dtype)

    return _call(body, name=name, grid=(s // tm,),
                 in_specs=[pl.BlockSpec((tm, d), lambda i: (i, 0)), pl.BlockSpec((1, d), lambda i: (0, 0))],
                 out_specs=pl.BlockSpec((tm, d), lambda i: (i, 0)),
                 out_shape=_sds((s, d), MXU_DT), sem=("parallel",))(x, g.reshape(1, d))


def _rms_bwd(x, g, dh, dres, name):
    s, d = x.shape
    tm = _row_tile(s)

    def body(x_ref, g_ref, dh_ref, dres_ref, dx_ref, dxb_ref, dg_ref):
        i = pl.program_id(0)
        xv = x_ref[...]
        r = lax.rsqrt(jnp.mean(xv * xv, axis=-1, keepdims=True) + EPS)
        xh = xv * r
        dhv = dh_ref[...]
        gd = dhv * g_ref[...]
        c = jnp.mean(gd * xh, axis=-1, keepdims=True)
        dx = dres_ref[...] + r * (gd - xh * c)
        dx_ref[...] = dx
        dxb_ref[...] = dx.astype(dxb_ref.dtype)
        part = jnp.sum(dhv * xh, axis=0, keepdims=True)

        @pl.when(i == 0)
        def _():
            dg_ref[...] = part

        @pl.when(i > 0)
        def _():
            dg_ref[...] += part

    row = pl.BlockSpec((tm, d), lambda i: (i, 0))
    vec = pl.BlockSpec((1, d), lambda i: (0, 0))
    return _call(body, name=name, grid=(s // tm,), in_specs=[row, vec, row, row],
                 out_specs=[row, row, vec],
                 out_shape=[_sds((s, d)), _sds((s, d), MXU_DT), _sds((1, d))],
                 sem=("arbitrary",))(x, g.reshape(1, d), dh, dres)


def _final_loss(x, g, target, name):
    s, d = x.shape
    tm = _row_tile(s)

    def body(x_ref, g_ref, t_ref, loss_ref, dx_ref, dxb_ref, dg_ref):
        i = pl.program_id(0)
        xv = x_ref[...]
        gv = g_ref[...]
        r = lax.rsqrt(jnp.mean(xv * xv, axis=-1, keepdims=True) + EPS)
        xh = xv * r
        err = xh * gv - t_ref[...]
        part_loss = 0.5 * jnp.sum(jnp.mean(err * err, axis=-1, keepdims=True), axis=0, keepdims=True)
        dy = err * (1.0 / d)
        gd = dy * gv
        c = jnp.mean(gd * xh, axis=-1, keepdims=True)
        dx = r * (gd - xh * c)
        dx_ref[...] = dx
        dxb_ref[...] = dx.astype(dxb_ref.dtype)
        part_g = jnp.sum(dy * xh, axis=0, keepdims=True)
        part_l = jnp.broadcast_to(part_loss, (1, LANES))

        @pl.when(i == 0)
        def _():
            dg_ref[...] = part_g
            loss_ref[...] = part_l

        @pl.when(i > 0)
        def _():
            dg_ref[...] += part_g
            loss_ref[...] += part_l

    row = pl.BlockSpec((tm, d), lambda i: (i, 0))
    vec = pl.BlockSpec((1, d), lambda i: (0, 0))
    return _call(body, name=name, grid=(s // tm,), in_specs=[row, vec, row],
                 out_specs=[pl.BlockSpec((1, LANES), lambda i: (0, 0)), row, row, vec],
                 out_shape=[_sds((1, LANES)), _sds((s, d)), _sds((s, d), MXU_DT), _sds((1, d))],
                 sem=("arbitrary",))(x, g.reshape(1, d), target)


def _lane_iota(shape):
    return lax.broadcasted_iota(jnp.int32, shape, len(shape) - 1)


def _swap_halves(x):
    return pltpu.roll(x, HEAD_DIM, 1)


def _segsum64(x, ones_ref):
    ones = ones_ref[...]
    outs = []
    for c in range(x.shape[1] // LANES):
        xc = x[:, c * LANES:(c + 1) * LANES]
        hi = xc.astype(MXU_DT)
        r1 = xc - hi.astype(F32)
        mid = r1.astype(MXU_DT)
        lo = (r1 - mid.astype(F32)).astype(MXU_DT)
        outs.append(_dot(hi, ones) + _dot(mid, ones) + _dot(lo, ones))
    return outs[0] if len(outs) == 1 else jnp.concatenate(outs, axis=1)


def _pair_ones():
    i = jnp.arange(LANES)
    return (i[:, None] // HEAD_DIM == i[None, :] // HEAD_DIM).astype(MXU_DT)


def _col(x, lane):
    return jnp.sum(jnp.where(_lane_iota(x.shape) == lane, x, 0.0), axis=-1, keepdims=True)


class _LocalCfg:
    def __init__(self, *, groups, qb, kw, qw, sub, qcol, kcol, vcol, kvhalf, kstart, variant, variant_py):
        self.groups, self.qb, self.kw, self.qw = groups, qb, kw, qw
        self.sub = sub
        self.qcol, self.kcol, self.vcol = qcol, kcol, vcol
        self.kvhalf = kvhalf
        self.kstart, self.variant = kstart, variant
        self.variant_py = variant_py
        self.pairs = qw // LANES


def _cfg_a(s):
    nb = s // 128
    return _LocalCfg(groups=1, qb=128, kw=384, qw=512, sub=1, qcol=lambda g: 0, kcol=lambda g: 4,
                     vcol=lambda g: 5, kvhalf=lambda t, e: t // 2,
                     kstart=lambda n: 128 * jnp.clip(n - 1, 0, nb - 3),
                     variant=lambda n: jnp.where(n <= 0, 0, jnp.where(n == nb - 1, 2, 1)),
                     variant_py=lambda n: 0 if n <= 0 else (2 if n == nb - 1 else 1))


def _cfg_b(s):
    rows = s // GRID_W
    return _LocalCfg(groups=4, qb=64, kw=512, qw=128, sub=4, qcol=lambda g: 6 + g, kcol=lambda g: 10 + g,
                     vcol=lambda g: 14 + g, kvhalf=lambda t, e: e,
                     kstart=lambda n: GRID_W * jnp.clip(n - NA_ROWS // 2, 0, rows - NA_ROWS),
                     variant=lambda n: jnp.where(n < 4, jnp.maximum(n, 0),
                                                 jnp.where(n > rows - 4, n - (rows - 8), 4)),
                     variant_py=lambda n: max(n, 0) if n < 4 else (n - (rows - 8) if n > rows - 4 else 4))


def _sum_visited(parts, cfg, s):
    n_var = parts[0].shape[0]
    variants = [cfg.variant_py(n) for n in range(s // cfg.qb)]
    total = None
    for i, part in enumerate(parts):
        seen = jnp.array([v in variants[i::cfg.sub] for v in range(n_var)]).reshape(n_var, 1, 1, 1)
        term = jnp.where(seen, part, 0.0)
        total = term if total is None else total + term
    return total


def _local_head(cfg, t, e, qp, qp_sw, kb, bias, sink_row, left_q):
    kvh = cfg.kvhalf(t, e)
    qsrc = qp if e == kvh else qp_sw
    keep = left_q if kvh == 0 else jnp.logical_not(left_q)
    qm = jnp.where(keep, qsrc, 0.0).astype(MXU_DT)
    sc = _dot(qm, kb, NT_DIMS) + bias
    snk = _col(sink_row, 2 * t + e)
    m = jnp.maximum(jnp.max(sc, axis=-1, keepdims=True), snk)
    p = jnp.exp(sc - m)
    l = jnp.sum(p, axis=-1, keepdims=True) + jnp.exp(snk - m)
    p = p / l
    return qm, keep, p, m, l, snk


def _local_attn_fwd(proj, bias, sink, cfg, name):
    s = proj.shape[0]
    qb, kw, qw, g_n = cfg.qb, cfg.kw, cfg.qw, cfg.groups
    hq = 2 * cfg.pairs

    sub = cfg.sub

    def body(q_ref, k_ref, v_ref, *rest):
        b_refs, s_ref, o_ref = rest[:sub], rest[sub], rest[sub + 1]
        n = pl.program_id(1)
        left_q = _lane_iota((qb, LANES)) < HEAD_DIM
        left_k = _lane_iota((kw, LANES)) < HEAD_DIM
        sink_row = s_ref[...]
        for i in range(sub):
            ks = pl.multiple_of(cfg.kstart(sub * n + i), 64)
            kf = k_ref[pl.ds(ks, kw), :]
            vf = v_ref[pl.ds(ks, kw), :]
            kb = kf.astype(MXU_DT)
            vf_sw = _swap_halves(vf)
            rows = slice(i * qb, (i + 1) * qb)
            for t in range(cfg.pairs):
                qp = q_ref[rows, t * LANES:(t + 1) * LANES] * 0.125
                qp_sw = _swap_halves(qp)
                acc = jnp.zeros((qb, LANES), F32)
                for e in range(2):
                    _, _, p, _, _, _ = _local_head(cfg, t, e, qp, qp_sw, kb, b_refs[i][0, 2 * t + e], sink_row,
                                                   left_q)
                    vsrc = vf if e == cfg.kvhalf(t, e) else vf_sw
                    vsel = jnp.where(left_k if e == 0 else jnp.logical_not(left_k), vsrc, 0.0).astype(MXU_DT)
                    acc = acc + _dot(p.astype(MXU_DT), vsel)
                o_ref[rows, t * LANES:(t + 1) * LANES] = acc

    def bias_spec(i):
        return pl.BlockSpec((1, hq, qb, kw), lambda g, n: (cfg.variant(sub * n + i), g, 0, 0))

    return _call(
        body, name=name, grid=(g_n, s // (sub * qb)),
        in_specs=[pl.BlockSpec((sub * qb, qw), lambda g, n: (n, cfg.qcol(g))),
                  pl.BlockSpec((s, LANES), lambda g, n: (0, cfg.kcol(g))),
                  pl.BlockSpec((s, LANES), lambda g, n: (0, cfg.vcol(g)))]
        + [bias_spec(i) for i in range(sub)]
        + [pl.BlockSpec((None, 1, LANES), lambda g, n: (g, 0, 0))],
        out_specs=pl.BlockSpec((sub * qb, qw), lambda g, n: (n, g)),
        out_shape=_sds((s, g_n * qw)), sem=("parallel", "arbitrary"))(proj, proj, proj, *([bias] * sub), sink)


def _local_attn_bwd(proj, bias, sink, do, cfg, name):
    s = proj.shape[0]
    qb, kw, qw, g_n = cfg.qb, cfg.kw, cfg.qw, cfg.groups
    hq = 2 * cfg.pairs

    sub = cfg.sub

    def body(q_ref, k_ref, v_ref, *rest):
        b_refs, s_ref, do_ref = rest[:sub], rest[sub], rest[sub + 1]
        dq_ref, dk_ref, dv_ref = rest[sub + 2:sub + 5]
        db_refs, dsk_ref = rest[sub + 5:2 * sub + 5], rest[2 * sub + 5]
        n = pl.program_id(1)

        @pl.when(n == 0)
        def _():
            dk_ref[...] = jnp.zeros_like(dk_ref)
            dv_ref[...] = jnp.zeros_like(dv_ref)
            dsk_ref[...] = jnp.zeros_like(dsk_ref)

        left_q = _lane_iota((qb, LANES)) < HEAD_DIM
        left_k = _lane_iota((kw, LANES)) < HEAD_DIM
        sink_row = s_ref[...]
        row0 = lax.broadcasted_iota(jnp.int32, (8, LANES), 0) == 0
        lane8 = _lane_iota((8, LANES))
        dsk_acc = jnp.zeros((8, LANES), F32)
        for i in range(sub):
            blk = sub * n + i
            ks = pl.multiple_of(cfg.kstart(blk), 64)
            db_ref = db_refs[i]

            @pl.when(jnp.logical_or(n == 0, cfg.variant(blk) != cfg.variant(blk - sub)))
            def _():
                db_ref[...] = jnp.zeros_like(db_ref)

            kf = k_ref[pl.ds(ks, kw), :]
            vf = v_ref[pl.ds(ks, kw), :]
            kb = kf.astype(MXU_DT)
            vb = vf.astype(MXU_DT)
            kf_sw = _swap_halves(kf)
            rows = slice(i * qb, (i + 1) * qb)
            dk_acc = jnp.zeros((kw, LANES), F32)
            dv_acc = jnp.zeros((kw, LANES), F32)
            for t in range(cfg.pairs):
                qp = q_ref[rows, t * LANES:(t + 1) * LANES] * 0.125
                qp_sw = _swap_halves(qp)
                dop = do_ref[rows, t * LANES:(t + 1) * LANES]
                dop_sw = _swap_halves(dop)
                dq_t = jnp.zeros((qb, LANES), F32)
                for e in range(2):
                    h = 2 * t + e
                    qm, keep, p, m, l, snk = _local_head(cfg, t, e, qp, qp_sw, kb, b_refs[i][0, h], sink_row,
                                                         left_q)
                    kvh = cfg.kvhalf(t, e)
                    dom = jnp.where(keep, dop if e == kvh else dop_sw, 0.0).astype(MXU_DT)
                    dp = _dot(dom, vb, NT_DIMS)
                    dd = jnp.sum(p * dp, axis=-1, keepdims=True)
                    ds = p * (dp - dd)
                    p_sink = jnp.exp(snk - m) / l
                    dsink = jnp.sum(-p_sink * dd, axis=0, keepdims=True)
                    dsk_acc = dsk_acc + jnp.where(jnp.logical_and(row0, lane8 == h), dsink, 0.0)
                    dsb = ds.astype(MXU_DT)
                    dv_acc = dv_acc + _dot(p.astype(MXU_DT), dom, TN_DIMS)
                    dk_acc = dk_acc + _dot(dsb, qm, TN_DIMS)
                    ksrc = kf if e == kvh else kf_sw
                    ksel = jnp.where(left_k if e == 0 else jnp.logical_not(left_k), ksrc, 0.0).astype(MXU_DT)
                    dq_t = dq_t + _dot(dsb, ksel)
                    db_ref[0, h] += ds
                dq_ref[rows, t * LANES:(t + 1) * LANES] = dq_t * 0.125
            dk_ref[pl.ds(ks, kw), :] += dk_acc
            dv_ref[pl.ds(ks, kw), :] += dv_acc
        dsk_ref[...] += dsk_acc

    def bias_spec(i):
        return pl.BlockSpec((1, hq, qb, kw), lambda g, n: (cfg.variant(sub * n + i), g, 0, 0))

    n_var = bias.shape[0]
    res = _call(
        body, name=name, grid=(g_n, s // (sub * qb)),
        in_specs=[pl.BlockSpec((sub * qb, qw), lambda g, n: (n, cfg.qcol(g))),
                  pl.BlockSpec((s, LANES), lambda g, n: (0, cfg.kcol(g))),
                  pl.BlockSpec((s, LANES), lambda g, n: (0, cfg.vcol(g)))]
        + [bias_spec(i) for i in range(sub)]
        + [pl.BlockSpec((None, 1, LANES), lambda g, n: (g, 0, 0)),
           pl.BlockSpec((sub * qb, qw), lambda g, n: (n, g))],
        out_specs=[pl.BlockSpec((sub * qb, qw), lambda g, n: (n, g)),
                   pl.BlockSpec((s, LANES), lambda g, n: (0, g)),
                   pl.BlockSpec((s, LANES), lambda g, n: (0, g))]
        + [bias_spec(i) for i in range(sub)]
        + [pl.BlockSpec((None, 8, LANES), lambda g, n: (g, 0, 0))],
        out_shape=[_sds((s, g_n * qw)), _sds((s, g_n * LANES)), _sds((s, g_n * LANES))]
        + [_sds((n_var, g_n * hq, qb, kw))] * sub + [_sds((g_n, 8, LANES))],
        sem=("parallel", "arbitrary"))(proj, proj, proj, *([bias] * sub), sink, do)
    dq, dk, dv = res[:3]
    dbias = _sum_visited(res[3:3 + sub], cfg, s)
    return dq, dk, dv, dbias, res[3 + sub]


QC_COL, KC_COL, VC_COL = 9, 13, 14
CW = 256


def _swap16(x):
    w = x.shape[1]
    lane = _lane_iota(x.shape)
    return jnp.where(lane % 32 < 16, pltpu.roll(x, w - 16, 1), pltpu.roll(x, 16, 1))


def _dup_halves(x):
    left = _lane_iota(x.shape) < HEAD_DIM
    sw = _swap_halves(x)
    return jnp.where(left, x, sw), jnp.where(left, sw, x)


def _normrope(x, gain, cos, sin, ones_ref):
    ms = _segsum64(x * x, ones_ref) * (1.0 / HEAD_DIM)
    r = lax.rsqrt(ms + EPS)
    y = (x * r) * gain
    return y * cos + _swap16(y) * sin, r


def _cprep_fwd(proj, gq, gk, cos, sin, ones, name):
    s = proj.shape[0]
    tm = _row_tile(s)

    def body(q0, q1, q2, q3, k_ref, v_ref, gq_ref, gk_ref, cos_ref, sin_ref, ones_ref, qh_ref, kd_ref, vd_ref):
        cos_v, sin_v = cos_ref[...], sin_ref[...]
        for c, q_ref in enumerate((q0, q1, q2, q3)):
            y, _ = _normrope(q_ref[...], gq_ref[...], cos_v, sin_v, ones_ref)
            qh_ref[:, c * CW:(c + 1) * CW] = (y * 0.125).astype(qh_ref.dtype)
        yk, _ = _normrope(k_ref[...], gk_ref[...], cos_v, sin_v, ones_ref)
        vv = v_ref[...]
        for p in range(2):
            ka, kb_ = _dup_halves(yk[:, p * LANES:(p + 1) * LANES])
            va, vb_ = _dup_halves(vv[:, p * LANES:(p + 1) * LANES])
            kd_ref[:, (2 * p) * LANES:(2 * p + 1) * LANES] = ka.astype(kd_ref.dtype)
            kd_ref[:, (2 * p + 1) * LANES:(2 * p + 2) * LANES] = kb_.astype(kd_ref.dtype)
            vd_ref[:, (2 * p) * LANES:(2 * p + 1) * LANES] = va.astype(vd_ref.dtype)
            vd_ref[:, (2 * p + 1) * LANES:(2 * p + 2) * LANES] = vb_.astype(vd_ref.dtype)

    def chunk(col):
        return pl.BlockSpec((tm, CW), lambda i: (i, col))

    vec = pl.BlockSpec((1, CW), lambda i: (0, 0))
    tab = pl.BlockSpec((tm, CW), lambda i: (i, 0))
    return _call(body, name=name, grid=(s // tm,),
                 in_specs=[chunk(QC_COL), chunk(QC_COL + 1), chunk(QC_COL + 2), chunk(QC_COL + 3),
                           chunk(KC_COL), chunk(VC_COL), vec, vec, tab, tab,
                           pl.BlockSpec((LANES, LANES), lambda i: (0, 0))],
                 out_specs=[pl.BlockSpec((tm, 4 * CW), lambda i: (i, 0)),
                            pl.BlockSpec((tm, 2 * CW), lambda i: (i, 0)),
                            pl.BlockSpec((tm, 2 * CW), lambda i: (i, 0))],
                 out_shape=[_sds((s, 4 * CW), MXU_DT), _sds((s, 2 * CW), MXU_DT), _sds((s, 2 * CW), MXU_DT)],
                 sem=("parallel",))(proj, proj, proj, proj, proj, proj, gq, gk, cos, sin, ones)


def _cprep_bwd(proj, gq, gk, cos, sin, ones, dqh, dkd, dvd, name):
    s = proj.shape[0]
    tm = _row_tile(s)

    def fold(ref, p):
        a = ref[:, (2 * p) * LANES:(2 * p + 1) * LANES]
        b = ref[:, (2 * p + 1) * LANES:(2 * p + 2) * LANES]
        ta = a + _swap_halves(a)
        tb = b + _swap_halves(b)
        return jnp.where(_lane_iota(a.shape) < HEAD_DIM, ta, tb)

    def norm_bwd(x, gain, dyr, cos_v, sin_v, ones_ref):
        dy = dyr * cos_v + _swap16(dyr * sin_v)
        ms = _segsum64(x * x, ones_ref) * (1.0 / HEAD_DIM)
        r = lax.rsqrt(ms + EPS)
        xh = x * r
        gd = dy * gain
        c = _segsum64(gd * xh, ones_ref) * (1.0 / HEAD_DIM)
        return r * (gd - xh * c), jnp.sum(dy * xh, axis=0, keepdims=True)

    def body(q0, q1, q2, q3, k_ref, gq_ref, gk_ref, cos_ref, sin_ref, ones_ref, dqh_ref, dkd_ref, dvd_ref,
             dq_ref, dk_ref, dv_ref, dgq_ref, dgk_ref):
        i = pl.program_id(0)
        cos_v, sin_v = cos_ref[...], sin_ref[...]
        gq_part = jnp.zeros((1, CW), F32)
        for c, q_ref in enumerate((q0, q1, q2, q3)):
            dx, dg = norm_bwd(q_ref[...], gq_ref[...], dqh_ref[:, c * CW:(c + 1) * CW] * 0.125, cos_v, sin_v,
                              ones_ref)
            dq_ref[:, c * CW:(c + 1) * CW] = dx
            gq_part = gq_part + dg
        dkr = jnp.concatenate([fold(dkd_ref, 0), fold(dkd_ref, 1)], axis=1)
        dxk, gk_part = norm_bwd(k_ref[...], gk_ref[...], dkr, cos_v, sin_v, ones_ref)
        dk_ref[...] = dxk
        dv_ref[...] = jnp.concatenate([fold(dvd_ref, 0), fold(dvd_ref, 1)], axis=1)

        @pl.when(i == 0)
        def _():
            dgq_ref[...] = gq_part
            dgk_ref[...] = gk_part

        @pl.when(i > 0)
        def _():
            dgq_ref[...] += gq_part
            dgk_ref[...] += gk_part

    def chunk(col):
        return pl.BlockSpec((tm, CW), lambda i: (i, col))

    vec = pl.BlockSpec((1, CW), lambda i: (0, 0))
    tab = pl.BlockSpec((tm, CW), lambda i: (i, 0))
    return _call(body, name=name, grid=(s // tm,),
                 in_specs=[chunk(QC_COL), chunk(QC_COL + 1), chunk(QC_COL + 2), chunk(QC_COL + 3), chunk(KC_COL),
                           vec, vec, tab, tab, pl.BlockSpec((LANES, LANES), lambda i: (0, 0)),
                           pl.BlockSpec((tm, 4 * CW), lambda i: (i, 0)),
                           pl.BlockSpec((tm, 2 * CW), lambda i: (i, 0)),
                           pl.BlockSpec((tm, 2 * CW), lambda i: (i, 0))],
                 out_specs=[pl.BlockSpec((tm, 4 * CW), lambda i: (i, 0)), tab, tab, vec, vec],
                 out_shape=[_sds((s, 4 * CW)), _sds((s, CW)), _sds((s, CW)), _sds((1, CW)), _sds((1, CW))],
                 sem=("arbitrary",))(proj, proj, proj, proj, proj, gq, gk, cos, sin, ones, dqh, dkd, dvd)


def _flash_tiles(s):
    return min(512, s), min(4096, s)


def _row_iota(shape):
    return lax.broadcasted_iota(jnp.int32, shape, 0)


def _flash_fwd(qh, kd, vdt, name):
    s = qh.shape[0]
    tq, tk = _flash_tiles(s)
    nk = s // tk

    n_chunks = 1
    cw = tq // n_chunks
    units = [(t, c, e) for t in range(2) for c in range(n_chunks) for e in range(2)]

    def body(q_ref, k_ref, vt_ref, ot_ref, lse_ref, qm_ref, m_ref, lacc_ref, acc_ref):
        j = pl.program_id(2)

        @pl.when(j == 0)
        def _():
            m_ref[...] = jnp.full(m_ref.shape, MASK_VALUE, F32)
            lacc_ref[...] = jnp.zeros_like(lacc_ref)
            acc_ref[...] = jnp.zeros_like(acc_ref)
            left_q = _lane_iota((tq, LANES)) < HEAD_DIM
            for t in range(2):
                qp = q_ref[:, t * LANES:(t + 1) * LANES]
                qm_ref[2 * t] = jnp.where(left_q, qp, jnp.zeros_like(qp))
                qm_ref[2 * t + 1] = jnp.where(left_q, jnp.zeros_like(qp), qp)

        kb = k_ref[...]
        vt = vt_ref[...]
        top_k = _row_iota((LANES, tk)) < HEAD_DIM
        top_c = _row_iota((LANES, cw)) < HEAD_DIM
        vt_e = (jnp.where(top_k, vt, jnp.ones_like(vt)), jnp.where(top_k, jnp.ones_like(vt), vt))

        def scores(unit):
            t, c, e = unit
            return _dot(kb, qm_ref[2 * t + e, c * cw:(c + 1) * cw, :], NT_DIMS)

        nxt = scores(units[0])
        pv, alpha = [], []
        for n, (t, c, e) in enumerate(units):
            st = nxt
            if n + 1 < len(units):
                nxt = scores(units[n + 1])
            h = 2 * t + e
            cols = slice(c * cw, (c + 1) * cw)
            m_prev = m_ref[h, :, cols]
            m_new = jnp.maximum(m_prev, jnp.max(st, axis=0, keepdims=True))
            alpha.append(jnp.exp(m_prev - m_new))
            pt = jnp.exp(st - m_new)
            m_ref[h, :, cols] = m_new
            pv.append(_dot(vt_e[e], pt.astype(MXU_DT)))
            if e == 1:
                acc_ref[t, :, cols] = (acc_ref[t, :, cols] * jnp.where(top_c, alpha[0], alpha[1])
                                       + jnp.where(top_c, pv[0], pv[1]))
                lacc_ref[t, :, cols] = (lacc_ref[t, :, cols] * jnp.where(top_c, alpha[1], alpha[0])
                                        + jnp.where(top_c, pv[1], pv[0]))
                pv, alpha = [], []

        @pl.when(j == nk - 1)
        def _():
            for t in range(2):
                lacc = lacc_ref[t]
                l_sw = jnp.concatenate([lacc[HEAD_DIM:], lacc[:HEAD_DIM]], axis=0)
                ot_ref[t * LANES:(t + 1) * LANES, :] = acc_ref[t] / l_sw
                lse_ref[2 * t:2 * t + 1, :] = m_ref[2 * t] + jnp.log(lacc[HEAD_DIM:HEAD_DIM + 1])
                lse_ref[2 * t + 1:2 * t + 2, :] = m_ref[2 * t + 1] + jnp.log(lacc[0:1])

    return _call(body, name=name, grid=(4, s // tq, nk),
                 in_specs=[pl.BlockSpec((tq, CW), lambda g, i, j: (i, g)),
                           pl.BlockSpec((tk, LANES), lambda g, i, j: (j, g)),
                           pl.BlockSpec((LANES, tk), lambda g, i, j: (g, j))],
                 out_specs=[pl.BlockSpec((CW, tq), lambda g, i, j: (g, i)),
                            pl.BlockSpec((None, 4, tq), lambda g, i, j: (g, 0, i))],
                 out_shape=[_sds((4 * CW, s)), _sds((4, 4, s))],
                 scratch=[pltpu.VMEM((4, tq, LANES), MXU_DT), pltpu.VMEM((4, 1, tq), F32),
                          pltpu.VMEM((2, LANES, tq), F32), pltpu.VMEM((2, LANES, tq), F32)],
                 sem=("parallel", "parallel", "arbitrary"))(qh, kd, vdt)


def _flash_bwd(qh, kd, vd, kdt, do, lse, dd, name):
    s = qh.shape[0]
    tq, tk = _flash_tiles(s)[0], min(2048, s)
    ni = s // tq

    def body(q_ref, k_ref, v_ref, kt_ref, do_ref, lse_ref, dd_ref, dqt_ref, dk_ref, dv_ref, dk_acc, dv_acc):
        j = pl.program_id(1)
        i = pl.program_id(2)

        @pl.when(i == 0)
        def _():
            dk_acc[...] = jnp.zeros_like(dk_acc)
            dv_acc[...] = jnp.zeros_like(dv_acc)

        kb = k_ref[...]
        vb = v_ref[...]
        kt = kt_ref[...]
        left_q = _lane_iota((tq, LANES)) < HEAD_DIM
        top = _row_iota((LANES, tq)) < HEAD_DIM
        cols = pl.ds(pl.multiple_of(i * tq, tq), tq)

        def first_stage(h):
            t, e = divmod(h, 2)
            keep_q = left_q if e == 0 else jnp.logical_not(left_q)
            qp = q_ref[:, t * LANES:(t + 1) * LANES]
            dop = do_ref[:, t * LANES:(t + 1) * LANES]
            qm = jnp.where(keep_q, qp, jnp.zeros_like(qp))
            dom = jnp.where(keep_q, dop, jnp.zeros_like(dop))
            return qm, dom, _dot(kb, qm, NT_DIMS), _dot(vb, dom, NT_DIMS)

        nxt = first_stage(0)
        dqt = []
        for h in range(4):
            qm, dom, st, dpt = nxt
            if h < 3:
                nxt = first_stage(h + 1)
            pt = jnp.exp(st - lse_ref[h:h + 1, :])
            dsb = (pt * (dpt - dd_ref[h:h + 1, :])).astype(MXU_DT)
            dv_acc[...] += _dot(pt.astype(MXU_DT), dom)
            dk_acc[...] += _dot(dsb, qm)
            dqt.append(_dot(kt, dsb))
            if h % 2 == 1:
                t = h // 2
                dq_t = jnp.where(top, dqt[0], dqt[1])
                dqt = []

                @pl.when(j == 0)
                def _():
                    dqt_ref[t * LANES:(t + 1) * LANES, cols] = dq_t

                @pl.when(j > 0)
                def _():
                    dqt_ref[t * LANES:(t + 1) * LANES, cols] += dq_t

        @pl.when(i == ni - 1)
        def _():
            dk_ref[...] = dk_acc[...]
            dv_ref[...] = dv_acc[...]

    qspec = pl.BlockSpec((tq, CW), lambda g, j, i: (i, g))
    kspec = pl.BlockSpec((tk, LANES), lambda g, j, i: (j, g))
    rowspec = pl.BlockSpec((None, 4, tq), lambda g, j, i: (g, 0, i))
    return _call(body, name=name, grid=(4, s // tk, ni),
                 in_specs=[qspec, kspec, kspec, pl.BlockSpec((LANES, tk), lambda g, j, i: (g, j)), qspec,
                           rowspec, rowspec],
                 out_specs=[pl.BlockSpec((CW, s), lambda g, j, i: (g, 0)), kspec, kspec],
                 out_shape=[_sds((4 * CW, s)), _sds((s, 2 * CW)), _sds((s, 2 * CW))],
                 scratch=[pltpu.VMEM((tk, LANES), F32), pltpu.VMEM((tk, LANES), F32)],
                 sem=("parallel", "arbitrary", "arbitrary"))(qh, kd, vd, kdt, do, lse, dd)


def _groupnorm_fwd(oa, ob, oc, ga, gb, gc, name):
    s = oa.shape[0]
    tm = _row_tile(s)
    wa, wb, wc = oa.shape[1], ob.shape[1], oc.shape[1]

    def body(oa_ref, ob_ref, oc_ref, ga_ref, gb_ref, gc_ref, mix_ref):
        off = 0
        for o_ref, g_ref, w in ((oa_ref, ga_ref, wa), (ob_ref, gb_ref, wb), (oc_ref, gc_ref, wc)):
            xv = o_ref[...]
            r = lax.rsqrt(jnp.mean(xv * xv, axis=-1, keepdims=True) + EPS)
            mix_ref[:, off:off + w] = ((xv * r) * g_ref[...]).astype(mix_ref.dtype)
            off += w

    def row(w):
        return pl.BlockSpec((tm, w), lambda i: (i, 0))

    def vec(w):
        return pl.BlockSpec((1, w), lambda i: (0, 0))

    return _call(body, name=name, grid=(s // tm,),
                 in_specs=[row(wa), row(wb), row(wc), vec(wa), vec(wb), vec(wc)],
                 out_specs=row(wa + wb + wc), out_shape=_sds((s, wa + wb + wc), MXU_DT),
                 sem=("parallel",))(oa, ob, oc, ga.reshape(1, wa), gb.reshape(1, wb), gc.reshape(1, wc))


def _groupnorm_bwd(dmix, oa, ob, oc, ga, gb, gc, ones, name):
    s = oa.shape[0]
    tm = _row_tile(s)
    wa, wb, wc = oa.shape[1], ob.shape[1], oc.shape[1]

    def body(dm_ref, oa_ref, ob_ref, oc_ref, ga_ref, gb_ref, gc_ref, ones_ref,
             doa_ref, dob_ref, doc_ref, docb_ref, dd_ref, dga_ref, dgb_ref, dgc_ref):
        i = pl.program_id(0)
        off = 0
        parts = []
        for o_ref, g_ref, do_ref, w in ((oa_ref, ga_ref, doa_ref, wa), (ob_ref, gb_ref, dob_ref, wb),
                                        (oc_ref, gc_ref, doc_ref, wc)):
            xv = o_ref[...]
            dh = dm_ref[:, off:off + w]
            r = lax.rsqrt(jnp.mean(xv * xv, axis=-1, keepdims=True) + EPS)
            xh = xv * r
            gd = dh * g_ref[...]
            c = jnp.mean(gd * xh, axis=-1, keepdims=True)
            dx = r * (gd - xh * c)
            do_ref[...] = dx
            parts.append(jnp.sum(dh * xh, axis=0, keepdims=True))
            if o_ref is oc_ref:
                docb_ref[...] = dx.astype(docb_ref.dtype)
                dd_ref[...] = _segsum64(dx * xv, ones_ref)
            off += w

        @pl.when(i == 0)
        def _():
            dga_ref[...], dgb_ref[...], dgc_ref[...] = parts

        @pl.when(i > 0)
        def _():
            dga_ref[...] += parts[0]
            dgb_ref[...] += parts[1]
            dgc_ref[...] += parts[2]

    def row(w):
        return pl.BlockSpec((tm, w), lambda i: (i, 0))

    def vec(w):
        return pl.BlockSpec((1, w), lambda i: (0, 0))

    return _call(body, name=name, grid=(s // tm,),
                 in_specs=[row(wa + wb + wc), row(wa), row(wb), row(wc), vec(wa), vec(wb), vec(wc),
                           pl.BlockSpec((LANES, LANES), lambda i: (0, 0))],
                 out_specs=[row(wa), row(wb), row(wc), row(wc), row(wc), vec(wa), vec(wb), vec(wc)],
                 out_shape=[_sds((s, wa)), _sds((s, wb)), _sds((s, wc)), _sds((s, wc), MXU_DT), _sds((s, wc)),
                            _sds((1, wa)), _sds((1, wb)), _sds((1, wc))],
                 sem=("arbitrary",))(dmix, oa, ob, oc, ga.reshape(1, wa), gb.reshape(1, wb), gc.reshape(1, wc), ones)


def _adam_math(w, g, m, v):
    m = ADAM_B1 * m + (1.0 - ADAM_B1) * g
    v = ADAM_B2 * v + (1.0 - ADAM_B2) * jnp.square(g)
    m_hat = m / (1.0 - ADAM_B1 ** ADAM_STEP)
    v_hat = v / (1.0 - ADAM_B2 ** ADAM_STEP)
    delta = -ADAM_LR * (m_hat / (jnp.sqrt(v_hat) + ADAM_EPS) + ADAM_WD * w)
    return delta, m, v


def _mesh_pos():
    return lax.axis_index("x"), lax.axis_index("y"), lax.axis_index("c")


def _peer_chips(x, y):
    return [(1 - x, y), (x, 1 - y), (1 - x, 1 - y)]


def _allreduce_small_adam(g, w, m, v):
    rows = g.shape[0]

    def body(g_ref, w_ref, m_ref, v_ref, gs_ref, d_ref, mo_ref, vo_ref, buf, send_sems, recv_sems):
        x, y, c = _mesh_pos()
        me = 4 * x + 2 * y + c
        buf[me] = g_ref[...]
        copies = []
        for k in range(1, 8):
            px = 1 - x if (k >> 2) & 1 else x
            py = 1 - y if (k >> 1) & 1 else y
            pc = 1 - c if k & 1 else c
            cp = pltpu.make_async_remote_copy(src_ref=g_ref, dst_ref=buf.at[me], send_sem=send_sems.at[k - 1],
                                              recv_sem=recv_sems.at[k - 1], device_id=(px, py, pc),
                                              device_id_type=MESH_ID)
            cp.start()
            copies.append(cp)
        for cp in copies:
            cp.wait()
        total = buf[0]
        for d in range(1, 8):
            total = total + buf[d]
        gs_ref[...] = total
        d_ref[...], mo_ref[...], vo_ref[...] = _adam_math(w_ref[...], total, m_ref[...], v_ref[...])

    vm = pl.BlockSpec(memory_space=pltpu.VMEM)
    return _call(body, name="allreduce_small_adam", in_specs=[vm] * 4, out_specs=[vm] * 4,
                 out_shape=[_sds((rows, LANES))] * 4,
                 scratch=[pltpu.VMEM((8, rows, LANES), F32), pltpu.SemaphoreType.DMA((7,)),
                          pltpu.SemaphoreType.DMA((7,))])(g, w, m, v)


HBM_SPEC = pl.BlockSpec(memory_space=pltpu.HBM)
SEM_SPEC = pl.BlockSpec(memory_space=pltpu.SEMAPHORE)
VMEM_SPEC = pl.BlockSpec(memory_space=pltpu.VMEM)
SIDE_EFFECT = pltpu.SideEffectType.DATAFLOW_SIDE_EFFECTING
N_PEERS = 7


def _in_hbm(a):
    return pltpu.with_memory_space_constraint(a, pltpu.HBM)


def _landing(shape, dtype):
    return _in_hbm(lax.empty(shape, dtype))


def _token_shape():
    return _sds((8, LANES))


def _gather_start(shards):
    n = len(shards)
    n_layers = shards[0].shape[0]
    jobs = [(l, t) for l in range(n_layers) for t in range(n)]
    nj = len(jobs)

    def body(*refs):
        sh = refs[:n]
        outs = refs[n + nj:]
        send, recv, land, token = outs[:nj], outs[nj:2 * nj], outs[2 * nj:3 * nj], outs[3 * nj]
        x, y, c = _mesh_pos()
        me = 2 * x + y
        for j, (l, t) in enumerate(jobs):
            for k, (px, py) in enumerate(_peer_chips(x, y)):
                pltpu.make_async_remote_copy(src_ref=sh[t].at[l], dst_ref=land[j].at[me], send_sem=send[j].at[k],
                                             recv_sem=recv[j].at[k], device_id=(px, py, c),
                                             device_id_type=MESH_ID).start()
        token[...] = jnp.zeros_like(token)

    lands = [_landing((4,) + shards[t].shape[1:], shards[t].dtype) for _, t in jobs]
    res = pl.pallas_call(
        body, name="gather_start",
        out_shape=tuple([pltpu.SemaphoreType.DMA((3,))] * (2 * nj)
                        + [pltpu.HBM(a.shape, a.dtype) for a in lands] + [_token_shape()]),
        in_specs=[HBM_SPEC] * (n + nj), out_specs=tuple([SEM_SPEC] * (2 * nj) + [HBM_SPEC] * nj + [VMEM_SPEC]),
        input_output_aliases={n + j: 2 * nj + j for j in range(nj)},
        compiler_params=pltpu.CompilerParams(has_side_effects=SIDE_EFFECT),
    )(*[_in_hbm(a) for a in shards], *lands)
    return jobs, res[:nj], res[nj:2 * nj], res[2 * nj:3 * nj], res[3 * nj]


def _gather_wait(shard, layer, land, send_sem, recv_sem, after, name):
    def body(sh_ref, land_ref, send_ref, recv_ref, after_ref, land_out):
        x, y, c = _mesh_pos()
        for k in range(3):
            cp = pltpu.make_async_remote_copy(src_ref=sh_ref.at[layer], dst_ref=land_ref.at[k],
                                              send_sem=send_ref.at[k], recv_sem=recv_ref.at[k],
                                              device_id=(x, y, 1 - c), device_id_type=MESH_ID)
            cp.wait_send()
            cp.wait_recv()

    return pl.pallas_call(
        body, name=name, out_shape=pltpu.HBM(land.shape, land.dtype),
        in_specs=[HBM_SPEC, HBM_SPEC, SEM_SPEC, SEM_SPEC, ANY], out_specs=HBM_SPEC,
        input_output_aliases={1: 0},
        compiler_params=pltpu.CompilerParams(has_side_effects=SIDE_EFFECT),
    )(shard, land, send_sem, recv_sem, after)


def _grad_start(g, name):
    def body(g_ref, land_in, send, recv, land, token):
        x, y, c = _mesh_pos()
        me = 2 * x + y
        pltpu.make_async_remote_copy(src_ref=g_ref.at[me], dst_ref=land.at[0], send_sem=send.at[0],
                                     recv_sem=recv.at[0], device_id=(x, y, 1 - c), device_id_type=MESH_ID).start()
        for k, (px, py) in enumerate(_peer_chips(x, y)):
            for c2 in range(2):
                pltpu.make_async_remote_copy(src_ref=g_ref.at[2 * px + py], dst_ref=land.at[1 + 2 * k + c],
                                             send_sem=send.at[1 + 2 * k + c2], recv_sem=recv.at[1 + 2 * k + c],
                                             device_id=(px, py, c2), device_id_type=MESH_ID).start()
        token[...] = jnp.zeros_like(token)

    land = _landing((N_PEERS,) + g.shape[1:], g.dtype)
    return pl.pallas_call(
        body, name=name,
        out_shape=(pltpu.SemaphoreType.DMA((N_PEERS,)), pltpu.SemaphoreType.DMA((N_PEERS,)),
                   pltpu.HBM(land.shape, land.dtype), _token_shape()),
        in_specs=[HBM_SPEC, HBM_SPEC], out_specs=(SEM_SPEC, SEM_SPEC, HBM_SPEC, VMEM_SPEC),
        input_output_aliases={1: 2},
        compiler_params=pltpu.CompilerParams(has_side_effects=SIDE_EFFECT),
    )(_in_hbm(g), land)


def _grad_wait(g, land, send_sem, recv_sem, after, name):
    def body(g_ref, land_ref, send_ref, recv_ref, after_ref, land_out):
        x, y, c = _mesh_pos()
        for k in range(N_PEERS):
            cp = pltpu.make_async_remote_copy(src_ref=g_ref.at[0], dst_ref=land_ref.at[k], send_sem=send_ref.at[k],
                                              recv_sem=recv_ref.at[k], device_id=(x, y, 1 - c),
                                              device_id_type=MESH_ID)
            cp.wait_send()
            cp.wait_recv()

    return pl.pallas_call(
        body, name=name, out_shape=pltpu.HBM(land.shape, land.dtype),
        in_specs=[HBM_SPEC, HBM_SPEC, SEM_SPEC, SEM_SPEC, ANY], out_specs=HBM_SPEC,
        input_output_aliases={1: 0},
        compiler_params=pltpu.CompilerParams(has_side_effects=SIDE_EFFECT),
    )(g, land, send_sem, recv_sem, after)


def _sum_adam(g, land, w, m, v, prev, layer, me_idx, name):
    _, r, cols = g.shape
    tr = min(128, r)

    def body(me_ref, g_ref, l0, l1, l2, l3, l4, l5, l6, w_ref, m_ref, v_ref, p0, p1, p2, p3,
             go_ref, d_ref, mo_ref, vo_ref):
        total = g_ref[...].astype(F32) + l0[...].astype(F32)
        for ref in (l1, l2, l3, l4, l5, l6):
            total = total + ref[...].astype(F32)
        go_ref[...] = total
        d_ref[...], mo_ref[...], vo_ref[...] = _adam_math(w_ref[...], total, m_ref[...], v_ref[...])

    def slot(k):
        return pl.BlockSpec((None, tr, cols), lambda i, me: (k, i, 0))

    lay = pl.BlockSpec((None, tr, cols), lambda i, me: (layer, i, 0))
    return _call(body, name=name, grid=(r // tr,), prefetch=1,
                 in_specs=[pl.BlockSpec((None, tr, cols), lambda i, me: (me[0], i, 0))]
                 + [slot(k) for k in range(N_PEERS)] + [lay, lay, lay] + [ANY] * 4,
                 out_specs=[lay] * 4, out_shape=[_sds(w.shape)] * 4,
                 aliases={12 + k: k for k in range(4)}, sem=("parallel",))(
                     me_idx, g, *([land] * N_PEERS), w, m, v, *prev)


def _t5_bucket(rel):
    nb = T5_BUCKETS // 2
    max_exact = nb // 2
    base = jnp.where(rel > 0, nb, 0)
    n = jnp.abs(rel)
    nf = jnp.maximum(n, 1).astype(F32)
    large = max_exact + (jnp.log(nf / max_exact) / math.log(T5_MAX_DIST / max_exact)
                         * (nb - max_exact)).astype(jnp.int32)
    large = jnp.minimum(large, nb - 1)
    return base + jnp.where(n < max_exact, n, large)


def _a_bias_maps():
    v = jnp.arange(3)[:, None, None]
    q = jnp.arange(128)[None, :, None]
    k = jnp.arange(384)[None, None, :]
    rel = k - 128 * v - q
    valid = jnp.abs(rel) <= 128
    onehot = (_t5_bucket(rel)[..., None] == jnp.arange(T5_BUCKETS)).astype(F32)
    return onehot * valid[..., None].astype(F32), valid


def _b_bias_maps():
    v = jnp.arange(8)[:, None]
    i = jnp.arange(NA_ROWS)[None, :]
    dr = jnp.where(v == 4, i + 3, i - v + 7)
    row_oh = (dr[..., None] == jnp.arange(2 * NA_ROWS - 1)).astype(F32)
    q = jnp.arange(GRID_W)[:, None]
    kc = jnp.arange(GRID_W)[None, :]
    cs = jnp.clip(q - 8, 0, GRID_W - 16)
    valid = (kc >= cs) & (kc < cs + 16)
    col_oh = ((kc - q + 15)[..., None] == jnp.arange(31)).astype(F32) * valid[..., None].astype(F32)
    return row_oh, col_oh, valid


def _rope_tables(s):
    t = jnp.arange(s)
    row = (t // GRID_W).astype(F32)
    col = (t % GRID_W).astype(F32)
    axis_dim = HEAD_DIM // 2
    freqs = ROPE_THETA ** (-jnp.arange(0, axis_dim, 2, dtype=F32) / axis_dim)
    ang_row = row[:, None] * freqs[None, :]
    ang_col = col[:, None] * freqs[None, :]
    cos = jnp.concatenate([jnp.cos(ang_row)] * 2 + [jnp.cos(ang_col)] * 2, axis=1)
    sin = jnp.concatenate([-jnp.sin(ang_row), jnp.sin(ang_row), -jnp.sin(ang_col), jnp.sin(ang_col)], axis=1)
    return jnp.tile(cos, (1, CW // HEAD_DIM)), jnp.tile(sin, (1, CW // HEAD_DIM))


def _pack(parts, rows):
    flat = jnp.concatenate([p.reshape(-1).astype(F32) for p in parts])
    return jnp.pad(flat, (0, rows * LANES - flat.shape[0])).reshape(rows, LANES)


def _unpack(buf, shapes):
    flat = buf.reshape(-1)
    out, off = [], 0
    for shp in shapes:
        size = math.prod(shp)
        out.append(flat[off:off + size].reshape(shp))
        off += size
    return out


def kernel(x, norm_mix, w_in, a_sink, t5_table, b_rpb, c_q_gain, c_k_gain, out_gain_a, out_gain_b, out_gain_c, w_o, norm_mlp, w_up, w_down, norm_final, loss_target, m_norm_mix, m_w_in, m_a_sink, m_t5_table, m_b_rpb, m_c_q_gain, m_c_k_gain, m_out_gain_a, m_out_gain_b, m_out_gain_c, m_w_o, m_norm_mlp, m_w_up, m_w_down, m_norm_final, v_norm_mix, v_w_in, v_a_sink, v_t5_table, v_b_rpb, v_c_q_gain, v_c_k_gain, v_out_gain_a, v_out_gain_b, v_out_gain_c, v_w_o, v_norm_mlp, v_w_up, v_w_down, v_norm_final):
    n_layers = w_in.shape[0]
    s, d = x.shape[1], x.shape[2]
    d_ff = 4 * w_up.shape[2]
    in_w = 4 * w_in.shape[2]
    xs = x.reshape(s, d)
    target = loss_target.reshape(s, d)
    cfg_a, cfg_b = _cfg_a(s), _cfg_b(s)

    x_i, y_i, _ = _mesh_pos()
    me_chip = 2 * x_i + y_i
    me_idx = me_chip.astype(jnp.int32).reshape(1)
    w_bf = [w_in.astype(MXU_DT), w_o.astype(MXU_DT), w_up.astype(MXU_DT), w_down.astype(MXU_DT)]
    jobs, gather_send, gather_recv, gather_land, gather_token = _gather_start(w_bf)
    job_of = {job: j for j, job in enumerate(jobs)}
    ff_shard = w_up.shape[2]

    def gathered(l, t, after):
        j = job_of[(l, t)]
        land = _gather_wait(w_bf[t], l, gather_land[j], gather_send[j], gather_recv[j], after,
                            "gather_wait_%d_%d" % (l, t))
        return lax.dynamic_update_slice(land, w_bf[t][l][None], (me_chip, 0, 0))

    ones = _pair_ones()
    cos_t, sin_t = _rope_tables(s)
    a_onehot, a_valid = _a_bias_maps()
    bias_a = jnp.where(a_valid[:, None], jnp.einsum("vqkb,bh->vhqk", a_onehot, t5_table, precision=HIGHEST),
                       MASK_VALUE)
    row_oh, col_oh, b_valid = _b_bias_maps()
    sink_b = jnp.full((4, 1, LANES), MASK_VALUE, F32)

    def b_bias(rpb):
        t = jnp.einsum("hrz,vir->vhiz", rpb, row_oh, precision=HIGHEST)
        t = jnp.einsum("vhiz,qcz->vhqic", t, col_oh, precision=HIGHEST)
        t = jnp.where(b_valid[None, None, :, None, :], t, MASK_VALUE)
        return t.reshape(8, 8, GRID_W, NA_ROWS * GRID_W)

    def tile_gain(gvec):
        return jnp.tile(gvec, CW // HEAD_DIM).reshape(1, CW)

    def pad_sink(svec):
        return jnp.pad(svec, (0, LANES - svec.shape[0])).reshape(1, 1, LANES)

    saved = []
    xc = xs
    for l in range(n_layers):
        h1 = _rms_fwd(xc, norm_mix[l] + gather_token[0, 0] if l == 0 else norm_mix[l], "rms_mix")
        wf_in = gathered(l, 0, h1).transpose(1, 0, 2).reshape(d, in_w)
        proj = _matmul(h1, wf_in, mode="nn", name="proj_in", tm=1024, tn=768, tk=2048)
        bias_b = b_bias(b_rpb[l])
        oa = _local_attn_fwd(proj, bias_a, pad_sink(a_sink[l]), cfg_a, "attn_a_fwd")
        ob = _local_attn_fwd(proj, bias_b, sink_b, cfg_b, "attn_b_fwd")
        gq, gk = tile_gain(c_q_gain[l]), tile_gain(c_k_gain[l])
        qh, kd, vd = _cprep_fwd(proj, gq, gk, cos_t, sin_t, ones, "cprep_fwd")
        kdt, vdt = kd.T, vd.T
        oct, lse = _flash_fwd(qh, kd, vdt, "attn_c_fwd")
        oc = oct.T
        mix = _groupnorm_fwd(oa, ob, oc, out_gain_a[l], out_gain_b[l], out_gain_c[l], "groupnorm_fwd")
        wf_o = gathered(l, 1, mix).reshape(d, d)
        x_mid = _matmul(mix, wf_o, mode="nn", name="proj_out", tm=1024, tn=1024, tk=2048, epi="res",
                        extra=(xc,))
        h2 = _rms_fwd(x_mid, norm_mlp[l], "rms_mlp")
        wg_up = gathered(l, 2, h2)
        nb_up = ff_shard // 1024
        u, uu = _matmul(h2, wg_up, mode="nn", name="mlp_up", tm=1024, tn=1024, tk=2048, epi="relu2",
                        out_dtypes=(F32, MXU_DT), mkn=(s, d, d_ff),
                        b_spec=pl.BlockSpec((None, 2048, 1024), lambda i, j, kk: (j // nb_up, kk, j % nb_up)))
        wf_down = gathered(l, 3, uu).reshape(d_ff, d)
        x_out = _matmul(uu, wf_down, mode="nn", name="mlp_down", tm=1024, tn=1024, tk=2048, epi="res",
                        extra=(x_mid,))
        saved.append((xc, h1, proj, bias_b, oa, ob, qh, kd, vd, kdt, oc, lse, mix, x_mid, h2, u, uu,
                      wf_in, wf_o, wg_up, wf_down))
        xc = x_out

    loss_part, dx, dxb, dg_final = _final_loss(xc, norm_final, target, "final_loss")

    small = {k: [] for k in ("norm_mix", "a_sink", "b_rpb", "cq", "ck", "oga", "ogb", "ogc", "norm_mlp")}
    dbias_a_total = jnp.zeros_like(bias_a)
    big_w = {"w_in": (w_in, m_w_in, v_w_in), "w_o": (w_o, m_w_o, v_w_o), "w_up": (w_up, m_w_up, v_w_up),
             "w_down": (w_down, m_w_down, v_w_down)}
    big = {nm: [lax.empty(wmv[0].shape, F32) for _ in range(4)] for nm, wmv in big_w.items()}

    def send_grad(nm, l, g):
        send, recv, land, token = _grad_start(g, "grad_start_%s_%d" % (nm, l))
        return (nm, l, g, send, recv, land), token[0, 0]

    def finish_grads(pending, after):
        for nm, l, g, send, recv, land in pending:
            land = _grad_wait(g, land, send, recv, after, "grad_wait_%s_%d" % (nm, l))
            wmv = big_w[nm]
            big[nm] = _sum_adam(g, land, wmv[0], wmv[1], wmv[2], big[nm], l, me_idx, "sum_adam_%s_%d" % (nm, l))

    pending = []
    for l in reversed(range(n_layers)):
        (xin, h1, proj, bias_b, oa, ob, qh, kd, vd, kdt, oc, lse, mix, x_mid, h2, u, uu,
         wf_in, wf_o, wg_up, wf_down) = saved[l]
        started = []
        du = _matmul(dxb, wf_down, mode="nt", name="mlp_down_dgrad", tm=1024, tn=1024, tk=2048, epi="mul2u",
                     extra=(u,), out_dtypes=(MXU_DT,))
        gw = _matmul(uu, dxb, mode="tn", name="mlp_down_wgrad", tm=1024, tn=1024, tk=1024, out_dtypes=(GRAD_DT,))
        rec, tok_down = send_grad("w_down", l, gw.reshape(4, d_ff // 4, d))
        started.append(rec)
        nbk = ff_shard // 2048
        dh2 = _matmul(du, wg_up, mode="nt", name="mlp_up_dgrad", tm=1024, tn=1024, tk=2048,
                      mkn=(s, d_ff, d),
                      b_spec=pl.BlockSpec((None, 1024, 2048), lambda i, j, kk: (kk // nbk, j, kk % nbk)))
        nbo = ff_shard // 1024
        gw = _matmul(h2, du, mode="tn", name="mlp_up_wgrad", tm=1024, tn=1024, tk=1024, out_dtypes=(GRAD_DT,),
                     out_spec=pl.BlockSpec((None, 1024, 1024), lambda i, j, kk: (j // nbo, i, j % nbo)),
                     out_shape=(4, d, ff_shard))
        rec, tok_up = send_grad("w_up", l, gw)
        started.append(rec)
        dx_mid, dxmb, dg = _rms_bwd(x_mid, norm_mlp[l] + (tok_down + tok_up), dh2, dx, "rms_mlp_bwd")
        small["norm_mlp"].append(dg)
        dmix = _matmul(dxmb, wf_o, mode="nt", name="proj_out_dgrad", tm=1024, tn=1024, tk=2048)
        gw = _matmul(mix, dxmb, mode="tn", name="proj_out_wgrad", tm=1024, tn=1024, tk=1024, out_dtypes=(GRAD_DT,))
        rec, tok_o = send_grad("w_o", l, gw.reshape(4, d // 4, d))
        started.append(rec)
        doa, dob, doc, docb, ddc, dga, dgb, dgc = _groupnorm_bwd(
            dmix, oa, ob, oc, out_gain_a[l] + tok_o, out_gain_b[l], out_gain_c[l], ones, "groupnorm_bwd")
        small["oga"].append(dga)
        small["ogb"].append(dgb)
        small["ogc"].append(dgc)
        dqa, dka, dva, dbias_a, dsink = _local_attn_bwd(proj, bias_a, pad_sink(a_sink[l]), doa, cfg_a, "attn_a_bwd")
        dbias_a_total = dbias_a_total + dbias_a
        small["a_sink"].append(dsink[0, 0, :a_sink.shape[1]])
        dqb, dkb, dvb, dbias_b, _ = _local_attn_bwd(proj, bias_b, sink_b, dob, cfg_b, "attn_b_bwd")
        db5 = jnp.where(b_valid[None, None, :, None, :], dbias_b.reshape(8, 8, GRID_W, NA_ROWS, GRID_W), 0.0)
        t = jnp.einsum("vhqic,qcz->vhiz", db5, col_oh, precision=HIGHEST)
        small["b_rpb"].append(jnp.einsum("vhiz,vir->hrz", t, row_oh, precision=HIGHEST))
        dd_rows = ddc.reshape(s, 16, HEAD_DIM)[:, :, 0].T.reshape(4, 4, s)
        dqht, dkd, dvd = _flash_bwd(qh, kd, vd, kdt, docb, lse, dd_rows, "attn_c_bwd")
        dqh = dqht.T
        gq, gk = tile_gain(c_q_gain[l]), tile_gain(c_k_gain[l])
        dqc, dkc, dvc, dgq, dgk = _cprep_bwd(proj, gq, gk, cos_t, sin_t, ones, dqh, dkd, dvd, "cprep_bwd")
        small["cq"].append(dgq.reshape(CW // HEAD_DIM, HEAD_DIM).sum(0))
        small["ck"].append(dgk.reshape(CW // HEAD_DIM, HEAD_DIM).sum(0))
        dproj = jnp.concatenate([dqa, dka, dva, dqb, dkb, dvb, dqc, dkc, dvc], axis=1).astype(MXU_DT)
        dh1 = _matmul(dproj, wf_in, mode="nt", name="proj_in_dgrad", tm=1024, tn=1024, tk=1920)
        gw = _matmul(h1, dproj, mode="tn", name="proj_in_wgrad", tm=1024, tn=768, tk=1024, out_dtypes=(GRAD_DT,))
        rec, tok_in = send_grad("w_in", l, gw.reshape(d, 4, in_w // 4).transpose(1, 0, 2))
        started.append(rec)
        dx, dxb, dg = _rms_bwd(xin, norm_mix[l] + tok_in, dh1, dx_mid, "rms_mix_bwd")
        small["norm_mix"].append(dg)
        finish_grads(pending, dx)
        pending = started
    finish_grads(pending, dx)

    for lst in small.values():
        lst.reverse()

    dt5 = jnp.einsum("vhqk,vqkb->bh", dbias_a_total, a_onehot, precision=HIGHEST)
    small_names = ["norm_mix", "a_sink", "t5_table", "b_rpb", "c_q_gain", "c_k_gain", "out_gain_a", "out_gain_b",
                   "out_gain_c", "norm_mlp", "norm_final"]
    small_w = [norm_mix, a_sink, t5_table, b_rpb, c_q_gain, c_k_gain, out_gain_a, out_gain_b, out_gain_c, norm_mlp,
               norm_final]
    small_m = [m_norm_mix, m_a_sink, m_t5_table, m_b_rpb, m_c_q_gain, m_c_k_gain, m_out_gain_a, m_out_gain_b,
               m_out_gain_c, m_norm_mlp, m_norm_final]
    small_v = [v_norm_mix, v_a_sink, v_t5_table, v_b_rpb, v_c_q_gain, v_c_k_gain, v_out_gain_a, v_out_gain_b,
               v_out_gain_c, v_norm_mlp, v_norm_final]
    small_g = [jnp.stack(small["norm_mix"]), jnp.stack(small["a_sink"]), dt5, jnp.stack(small["b_rpb"]),
               jnp.stack(small["cq"]), jnp.stack(small["ck"]), jnp.stack(small["oga"]), jnp.stack(small["ogb"]),
               jnp.stack(small["ogc"]), jnp.stack(small["norm_mlp"]), dg_final]
    shapes = [w.shape for w in small_w]
    total = sum(math.prod(shp) for shp in shapes) + 1
    rows = -(-total // (8 * LANES)) * 8
    one = [jnp.ones((1,), F32)]
    gs, dl, mo, vo = _allreduce_small_adam(_pack(small_g + [loss_part[0, :1]], rows), _pack(small_w + one, rows),
                                           _pack(small_m + one, rows), _pack(small_v + one, rows))
    sg = _unpack(gs, shapes + [(1,)])
    sd, sm, sv = _unpack(dl, shapes), _unpack(mo, shapes), _unpack(vo, shapes)
    loss = sg[-1].reshape(())

    by_name = {nm: (sg[i], sd[i], sm[i], sv[i]) for i, nm in enumerate(small_names)}
    by_name.update(big)
    order = ["norm_mix", "w_in", "a_sink", "t5_table", "b_rpb", "c_q_gain", "c_k_gain", "out_gain_a", "out_gain_b",
             "out_gain_c", "w_o", "norm_mlp", "w_up", "w_down", "norm_final"]
    outs = [loss, dx.reshape(x.shape)]
    for field in range(4):
        outs.extend(by_name[nm][field] for nm in order)
    return tuple(outs)
```
